```python
import jax
import jax.numpy as jnp
from jax import lax
import numpy as np


D_MODEL = 2048
BATCH = 4
SEQ = 2048
DEPTH = 1

CHUNK = 64
SPATIAL_BLOCK = 128
A_WIDTH = D_MODEL
A_GROUPS = 8
A_GROUP_DIM = A_WIDTH // A_GROUPS
GLA_HEADS = 4
GLA_DK = D_MODEL // 2
GLA_DV = D_MODEL
GLA_HEAD_K = GLA_DK // GLA_HEADS
GLA_HEAD_V = GLA_DV // GLA_HEADS
GLA_GATE_RANK = 16
GLA_GATE_TAU = 16.0
N_EXPERTS = 32
TOP_K = 4
D_FF = D_MODEL
SWIGLU_LIMIT = 7.0
SWIGLU_ALPHA = 1.702
EXPERT_BLOCK = 256
NORM_EPS = 1e-6
IN_SPLITS = (A_WIDTH, A_WIDTH, GLA_DK, GLA_DK, GLA_DV, GLA_DV, GLA_GATE_RANK, D_MODEL, D_MODEL)
IN_WIDTH = 2 * A_WIDTH + 2 * GLA_DK + 2 * GLA_DV + GLA_GATE_RANK + 2 * D_MODEL

kernel_name = 'hybrid_gmlp_gla_moe_adaln_block'


def _split_points():
    pts, acc = [], 0
    for w in IN_SPLITS[:-1]:
        acc += w
        pts.append(acc)
    return pts


def rms_norm(x, g):
    xf = x.astype(jnp.float32)
    y = xf * lax.rsqrt(jnp.mean(xf * xf, axis=-1, keepdims=True) + NORM_EPS)
    return (y * g.astype(jnp.float32)).astype(x.dtype)


def layer_norm(x, g, b):
    xf = x.astype(jnp.float32)
    mu = jnp.mean(xf, axis=-1, keepdims=True)
    xc = xf - mu
    var = jnp.mean(xc * xc, axis=-1, keepdims=True)
    return (xc * lax.rsqrt(var + NORM_EPS) * g.astype(jnp.float32) + b.astype(jnp.float32)).astype(x.dtype)


def spatial_gating(u, v, ln_g, ln_b, w_s, b_s):
    B, S, _ = v.shape
    nb = S // SPATIAL_BLOCK
    vn = layer_norm(v, ln_g, ln_b).reshape(B, nb, SPATIAL_BLOCK, A_GROUPS, A_GROUP_DIM)
    chunk_id = jnp.arange(SPATIAL_BLOCK) // CHUNK
    mask = chunk_id[None, :] <= chunk_id[:, None]
    w = jnp.where(mask[None], w_s, 0.0)
    mixed = jnp.einsum('gts,bnsgc->bntgc', w, vn) + b_s.T[None, None, :, :, None]
    return u * mixed.reshape(B, S, A_WIDTH)


def gated_linear_attention(q, k, v, log_a):
    B, S, H, dk = q.shape
    dv = v.shape[-1]
    n = S // CHUNK
    f32 = jnp.float32
    q = q.astype(f32).reshape(B, n, CHUNK, H, dk) * (dk ** -0.5)
    k = k.astype(f32).reshape(B, n, CHUNK, H, dk)
    v = v.astype(f32).reshape(B, n, CHUNK, H, dv)
    b = jnp.cumsum(log_a.astype(f32).reshape(B, n, CHUNK, H, dk), axis=2)
    b_last = b[:, :, -1:]
    q_dec = q * jnp.exp(b)
    k_intra = k * jnp.exp(-b)
    k_state = k * jnp.exp(b_last - b)
    causal = jnp.tril(jnp.ones((CHUNK, CHUNK), dtype=bool))
    att = jnp.einsum('bnthk,bnshk->bnhts', q_dec, k_intra)
    att = jnp.where(causal, att, 0.0)
    o_intra = jnp.einsum('bnhts,bnshv->bnthv', att, v)

    def step(state, inp):
        qc, kc, vc, dc = inp
        o = jnp.einsum('blhk,bhkv->blhv', qc, state)
        state = dc[..., None] * state + jnp.einsum('blhk,blhv->bhkv', kc, vc)
        return state, o

    xs = (jnp.moveaxis(q_dec, 1, 0), jnp.moveaxis(k_state, 1, 0), jnp.moveaxis(v, 1, 0),
          jnp.moveaxis(jnp.exp(b_last[:, :, 0]), 1, 0))
    s0 = jnp.zeros((B, H, dk, dv), f32)
    _, o_inter = lax.scan(step, s0, xs)
    o = o_intra + jnp.moveaxis(o_inter, 0, 1)
    return o.reshape(B, S, H, dv)


def routed_experts(h, router_w, router_b, w_gate, b_gate, w_up, b_up, w_down, b_down):
    B, S, D = h.shape
    T = B * S
    n_rows = T * TOP_K
    n_blocks = (n_rows + N_EXPERTS * (EXPERT_BLOCK - 1) + EXPERT_BLOCK - 1) // EXPERT_BLOCK
    hf = h.reshape(T, D)
    logits = (hf @ router_w + router_b).astype(jnp.float32)
    top_logit, top_idx = lax.top_k(logits, TOP_K)
    top_w = jax.nn.softmax(top_logit, axis=-1)
    flat_e = top_idx.reshape(n_rows)
    flat_tok = jnp.repeat(jnp.arange(T, dtype=jnp.int32), TOP_K)
    flat_w = top_w.reshape(n_rows)
    order = jnp.argsort(flat_e)
    sorted_e = flat_e[order]
    counts = jnp.bincount(flat_e, length=N_EXPERTS)
    starts = jnp.cumsum(counts) - counts
    padded = (counts + EXPERT_BLOCK - 1) // EXPERT_BLOCK * EXPERT_BLOCK
    padded_end = jnp.cumsum(padded)
    padded_start = padded_end - padded
    dest = padded_start[sorted_e] + jnp.arange(n_rows) - starts[sorted_e]
    row_tok = jnp.zeros((n_blocks * EXPERT_BLOCK,), jnp.int32).at[dest].set(flat_tok[order])
    row_w = jnp.zeros((n_blocks * EXPERT_BLOCK,), jnp.float32).at[dest].set(flat_w[order])
    block_e = jnp.minimum(jnp.searchsorted(padded_end, jnp.arange(n_blocks) * EXPERT_BLOCK, side='right'),
                          N_EXPERTS - 1)

    def step(acc, inp):
        e, tok, w = inp
        xb = hf[tok]
        g = jnp.minimum(xb @ w_gate[e] + b_gate[e], SWIGLU_LIMIT)
        u = jnp.clip(xb @ w_up[e] + b_up[e], -SWIGLU_LIMIT, SWIGLU_LIMIT)
        y = ((u + 1.0) * (g * jax.nn.sigmoid(SWIGLU_ALPHA * g))) @ w_down[e] + b_down[e]
        return acc.at[tok].add(y.astype(jnp.float32) * w[:, None]), None

    acc, _ = lax.scan(step, jnp.zeros((T, D), jnp.float32),
                      (block_e, row_tok.reshape(n_blocks, EXPERT_BLOCK), row_w.reshape(n_blocks, EXPERT_BLOCK)))
    return acc.reshape(B, S, D).astype(h.dtype)


def setup_inputs(seed: int = 0) -> dict:
    key = jax.random.key(seed)
    ks = jax.random.split(key, 26)
    f32 = jnp.float32
    L, D = DEPTH, D_MODEL

    def nrm(k, shape, scale):
        return jax.random.normal(k, shape, f32) * scale

    return {
        'x': nrm(ks[0], (BATCH, SEQ, D), 1.0),
        'c': nrm(ks[1], (BATCH, D), 1.0),
        'ada_w': nrm(ks[2], (L, D, 6 * D), 0.5 * D ** -0.5),
        'ada_b': nrm(ks[3], (L, 6 * D), 0.02),
        'norm1_g': 1.0 + nrm(ks[4], (L, D), 0.02),
        'w_in': nrm(ks[5], (L, D, IN_WIDTH), D ** -0.5),
        'gla_gate_w2': nrm(ks[6], (L, GLA_GATE_RANK, GLA_DK), GLA_GATE_RANK ** -0.5),
        'gla_gate_b': nrm(ks[7], (L, GLA_DK), 0.1),
        'sgu_ln_g': 1.0 + nrm(ks[8], (L, A_WIDTH), 0.02),
        'sgu_ln_b': nrm(ks[9], (L, A_WIDTH), 0.02),
        'sgu_w': nrm(ks[10], (L, A_GROUPS, SPATIAL_BLOCK, SPATIAL_BLOCK), SPATIAL_BLOCK ** -0.5),
        'sgu_b': 1.0 + nrm(ks[11], (L, A_GROUPS, SPATIAL_BLOCK), 0.02),
        'gla_norm_g': 1.0 + nrm(ks[12], (L, GLA_HEAD_V), 0.02),
        'w_branch_a': nrm(ks[13], (L, A_WIDTH, D), A_WIDTH ** -0.5),
        'w_branch_b': nrm(ks[14], (L, GLA_DV, D), GLA_DV ** -0.5),
        'w_out': nrm(ks[15], (L, D, D), D ** -0.5),
        'norm2_g': 1.0 + nrm(ks[16], (L, D), 0.02),
        'router_w': nrm(ks[17], (L, D, N_EXPERTS), D ** -0.5),
        'router_b': nrm(ks[18], (L, N_EXPERTS), 0.01),
        'exp_w_gate': nrm(ks[19], (L, N_EXPERTS, D, D_FF), D ** -0.5),
        'exp_b_gate': nrm(ks[20], (L, N_EXPERTS, D_FF), 0.01),
        'exp_w_up': nrm(ks[21], (L, N_EXPERTS, D, D_FF), D ** -0.5),
        'exp_b_up': nrm(ks[22], (L, N_EXPERTS, D_FF), 0.01),
        'exp_w_down': nrm(ks[23], (L, N_EXPERTS, D_FF, D), D_FF ** -0.5),
        'exp_b_down': nrm(ks[24], (L, N_EXPERTS, D), 0.01),
        'final_g': 1.0 + nrm(ks[25], (D,), 0.02),
    }


def reference(x, c, ada_w, ada_b, norm1_g, w_in, gla_gate_w2, gla_gate_b, sgu_ln_g, sgu_ln_b, sgu_w, sgu_b,
              gla_norm_g, w_branch_a, w_branch_b, w_out, norm2_g, router_w, router_b, exp_w_gate, exp_b_gate,
              exp_w_up, exp_b_up, exp_w_down, exp_b_down, final_g):
    B, S, D = x.shape
    cond = jax.nn.silu(c)
    split_points = _split_points()
    for l in range(DEPTH):
        mod = cond @ ada_w[l] + ada_b[l]
        sh1, sc1, gt1, sh2, sc2, gt2 = [m[:, None, :] for m in jnp.split(mod, 6, axis=-1)]

        h = rms_norm(x, norm1_g[l]) * (1.0 + sc1) + sh1
        proj = h @ w_in[l]
        a_u, a_v, q, k, v, r, g_lr, gate_a, gate_b = jnp.split(proj, split_points, axis=-1)
        y_a = spatial_gating(jax.nn.gelu(a_u), jax.nn.gelu(a_v), sgu_ln_g[l], sgu_ln_b[l], sgu_w[l], sgu_b[l])
        log_a = jax.nn.log_sigmoid((g_lr @ gla_gate_w2[l] + gla_gate_b[l]).astype(jnp.float32)) / GLA_GATE_TAU
        o = gated_linear_attention(q.reshape(B, S, GLA_HEADS, GLA_HEAD_K), k.reshape(B, S, GLA_HEADS, GLA_HEAD_K),
                                   v.reshape(B, S, GLA_HEADS, GLA_HEAD_V), log_a.reshape(B, S, GLA_HEADS, GLA_HEAD_K))
        o = rms_norm(o, gla_norm_g[l]).astype(x.dtype)
        y_b = o.reshape(B, S, GLA_DV) * jax.nn.silu(r)
        merged = jax.nn.sigmoid(gate_a) * (y_a @ w_branch_a[l]) + jax.nn.sigmoid(gate_b) * (y_b @ w_branch_b[l])
        x = x + gt1 * (merged @ w_out[l])

        h2 = rms_norm(x, norm2_g[l]) * (1.0 + sc2) + sh2
        x = x + gt2 * routed_experts(h2, router_w[l], router_b[l], exp_w_gate[l], exp_b_gate[l], exp_w_up[l],
                                     exp_b_up[l], exp_w_down[l], exp_b_down[l])
    return rms_norm(x, final_g)
```

```python
import functools

import jax
import jax.numpy as jnp
from jax import lax
from jax.experimental import pallas as pl
from jax.experimental.pallas import tpu as pltpu

F32 = jnp.float32
BF16 = jnp.bfloat16
I32 = jnp.int32

D_MODEL = 2048
BATCH = 4
SEQ = 2048
TOKENS = BATCH * SEQ
CHUNK = 64
SPATIAL_BLOCK = 128
A_GROUPS = 8
A_GROUP_DIM = D_MODEL // A_GROUPS
GLA_HEADS = 4
GLA_DK = D_MODEL // 2
GLA_HEAD_K = GLA_DK // GLA_HEADS
GLA_HEAD_V = D_MODEL // GLA_HEADS
GLA_GATE_RANK = 16
GLA_GATE_TAU = 16.0
N_EXPERTS = 32
TOP_K = 4
D_FF = D_MODEL
SWIGLU_LIMIT = 7.0
SWIGLU_ALPHA = 1.702
NORM_EPS = 1e-6

LANES = 128
VMEM_LIMIT = 56 * 1024 * 1024

COL_U, COL_V, COL_Q, COL_K, COL_VV, COL_R = 0, 2048, 4096, 5120, 6144, 8192
COL_GA, COL_GB = 10240, 12288
PROJ_W = 14336
GLR_SRC = 10240

ROW_TILE = 256
SUPER_TILES = 6
SUPER_ROWS = ROW_TILE * SUPER_TILES
N_ROWS = TOKENS * TOP_K
N_ROW_TILES = (N_ROWS + N_EXPERTS * (ROW_TILE - 1)) // ROW_TILE
PADDED_ROWS = N_ROW_TILES * ROW_TILE
MAX_SUPER = -(-(N_ROW_TILES + N_EXPERTS * (SUPER_TILES - 1)) // SUPER_TILES)
FF_TILE = 256
N_FF = D_FF // FF_TILE


def _cparams(*sem):
    return pltpu.CompilerParams(dimension_semantics=sem, vmem_limit_bytes=VMEM_LIMIT)


def _rms(x):
    return x * lax.rsqrt(jnp.mean(x * x, axis=-1, keepdims=True) + NORM_EPS)


def _ada_body(c_ref, w_ref, b_ref, o_ref):
    c = c_ref[...]
    cond = c * jax.nn.sigmoid(c)
    o_ref[...] = jnp.dot(cond.astype(BF16), w_ref[...].astype(BF16),
                         preferred_element_type=F32) + b_ref[...]


def _ada(c, ada_w, ada_b):
    tn = 1024
    cp = jnp.zeros((8, D_MODEL), F32).at[:BATCH].set(c)
    out = pl.pallas_call(
        _ada_body,
        grid=(6 * D_MODEL // tn,),
        in_specs=[pl.BlockSpec((8, D_MODEL), lambda j: (0, 0)),
                  pl.BlockSpec((D_MODEL, tn), lambda j: (0, j)),
                  pl.BlockSpec((1, tn), lambda j: (0, j))],
        out_specs=pl.BlockSpec((8, tn), lambda j: (0, j)),
        out_shape=jax.ShapeDtypeStruct((8, 6 * D_MODEL), F32),
        compiler_params=_cparams("arbitrary"),
        name="ada",
    )(cp, ada_w, ada_b.reshape(1, 6 * D_MODEL))
    return out[:BATCH].reshape(BATCH, 6, D_MODEL)


def _inproj_body(x_ref, mod_ref, g_ref, w_ref, wglr_ref, proj_ref, glr_ref, h_ref, *, tn):
    j = pl.program_id(1)

    @pl.when(j == 0)
    def _():
        h = _rms(x_ref[...]) * g_ref[...] * (1.0 + mod_ref[1:2, :]) + mod_ref[0:1, :]
        hb = h.astype(BF16)
        h_ref[...] = hb
        glr_ref[...] = jnp.dot(hb, wglr_ref[...], preferred_element_type=F32)

    acc = jnp.dot(h_ref[...], w_ref[...], preferred_element_type=F32)
    col = j * tn

    @pl.when(col < COL_Q)
    def _():
        proj_ref[...] = jax.nn.gelu(acc).astype(BF16)

    @pl.when((col >= COL_Q) & (col < COL_R))
    def _():
        proj_ref[...] = acc.astype(BF16)

    @pl.when((col >= COL_R) & (col < COL_GA))
    def _():
        proj_ref[...] = (acc * jax.nn.sigmoid(acc)).astype(BF16)

    @pl.when(col >= COL_GA)
    def _():
        proj_ref[...] = jax.nn.sigmoid(acc).astype(BF16)


def _inproj(x2d, mod, norm1_g, w_main, w_glr):
    tm, tn = 512, 1024
    per_batch = SEQ // tm
    return pl.pallas_call(
        functools.partial(_inproj_body, tn=tn),
        grid=(TOKENS // tm, PROJ_W // tn),
        in_specs=[pl.BlockSpec((tm, D_MODEL), lambda i, j: (i, 0)),
                  pl.BlockSpec((None, 6, D_MODEL), lambda i, j: (i // per_batch, 0, 0)),
                  pl.BlockSpec((1, D_MODEL), lambda i, j: (0, 0)),
                  pl.BlockSpec((D_MODEL, tn), lambda i, j: (0, j)),
                  pl.BlockSpec((D_MODEL, LANES), lambda i, j: (0, 0))],
        out_specs=[pl.BlockSpec((tm, tn), lambda i, j: (i, j)),
                   pl.BlockSpec((tm, LANES), lambda i, j: (i, 0))],
        out_shape=[jax.ShapeDtypeStruct((TOKENS, PROJ_W), BF16),
                   jax.ShapeDtypeStruct((TOKENS, LANES), F32)],
        scratch_shapes=[pltpu.VMEM((tm, D_MODEL), BF16)],
        compiler_params=_cparams("arbitrary", "arbitrary"),
        name="inproj",
    )(x2d, mod, norm1_g, w_main, w_glr)


def _sgu_body(u_ref, v_ref, lg_ref, lb_ref, ws_ref, bs_ref, o_ref):
    v = v_ref[...].astype(F32)
    mu = jnp.mean(v, axis=-1, keepdims=True)
    xc = v - mu
    var = jnp.mean(xc * xc, axis=-1, keepdims=True)
    vn = (xc * lax.rsqrt(var + NORM_EPS) * lg_ref[...] + lb_ref[...]).astype(BF16)
    t_chunk = lax.broadcasted_iota(I32, (SPATIAL_BLOCK, SPATIAL_BLOCK), 0) // CHUNK
    s_chunk = lax.broadcasted_iota(I32, (SPATIAL_BLOCK, SPATIAL_BLOCK), 1) // CHUNK
    mask = s_chunk <= t_chunk
    for g in range(A_GROUPS):
        cols = slice(g * A_GROUP_DIM, (g + 1) * A_GROUP_DIM)
        w = jnp.where(mask, ws_ref[g], 0.0).astype(BF16)
        mixed = jnp.dot(w, vn[:, cols], preferred_element_type=F32) + bs_ref[:, g:g + 1]
        o_ref[:, cols] = (u_ref[:, cols].astype(F32) * mixed).astype(BF16)


def _sgu(proj, ln_g, ln_b, w_s, b_s_t):
    nblk = TOKENS // SPATIAL_BLOCK
    wb = D_MODEL
    return pl.pallas_call(
        _sgu_body,
        grid=(nblk,),
        in_specs=[pl.BlockSpec((SPATIAL_BLOCK, wb), lambda i: (i, COL_U // wb)),
                  pl.BlockSpec((SPATIAL_BLOCK, wb), lambda i: (i, COL_V // wb)),
                  pl.BlockSpec((1, wb), lambda i: (0, 0)),
                  pl.BlockSpec((1, wb), lambda i: (0, 0)),
                  pl.BlockSpec((A_GROUPS, SPATIAL_BLOCK, SPATIAL_BLOCK), lambda i: (0, 0, 0)),
                  pl.BlockSpec((SPATIAL_BLOCK, A_GROUPS), lambda i: (0, 0))],
        out_specs=pl.BlockSpec((SPATIAL_BLOCK, wb), lambda i: (i, 0)),
        out_shape=jax.ShapeDtypeStruct((TOKENS, wb), BF16),
        compiler_params=_cparams("arbitrary"),
        name="sgu",
    )(proj, proj, ln_g, ln_b, w_s, b_s_t)


def _dot_nt(a, b):
    return lax.dot_general(a, b, (((1,), (1,)), ((), ())), preferred_element_type=F32)


def _dot_tn(a, b):
    return lax.dot_general(a, b, (((0,), (0,)), ((), ())), preferred_element_type=F32)


def _gla_body(q_ref, k_ref, v_ref, r_ref, glr_ref, w2_ref, gb_ref, ng_ref, o_ref, st_ref, la_ref, *, rows_blk):
    @pl.when(pl.program_id(2) == 0)
    def _():
        st_ref[...] = jnp.zeros_like(st_ref)

    z = jnp.dot(glr_ref[...].astype(BF16), w2_ref[...], preferred_element_type=F32) + gb_ref[...]
    la_ref[...] = jax.nn.log_sigmoid(z) / GLA_GATE_TAU

    r_i = lax.broadcasted_iota(I32, (CHUNK, CHUNK), 0)
    c_i = lax.broadcasted_iota(I32, (CHUNK, CHUNK), 1)
    causal = c_i <= r_i
    tril = causal.astype(BF16)
    scale = GLA_HEAD_K ** -0.5

    def chunk(c, carry):
        rows = pl.ds(pl.multiple_of(c * CHUNK, CHUNK), CHUNK)
        la = la_ref[rows, :]
        hi = la.astype(BF16)
        r1 = la - hi.astype(F32)
        mid = r1.astype(BF16)
        lo = (r1 - mid.astype(F32)).astype(BF16)
        b = (jnp.dot(tril, hi, preferred_element_type=F32)
             + jnp.dot(tril, mid, preferred_element_type=F32)
             + jnp.dot(tril, lo, preferred_element_type=F32))
        b_last = b[CHUNK - 1:CHUNK, :]
        q = q_ref[rows, :].astype(F32) * scale
        k = k_ref[rows, :].astype(F32)
        v = v_ref[rows, :]
        q_dec = (q * jnp.exp(b)).astype(BF16)
        k_intra = (k * jnp.exp(-b)).astype(BF16)
        k_state = (k * jnp.exp(b_last - b)).astype(BF16)
        att = jnp.where(causal, _dot_nt(q_dec, k_intra), 0.0).astype(BF16)
        st = st_ref[...]
        o = jnp.dot(att, v, preferred_element_type=F32) + _dot_nt(q_dec, st.astype(BF16))
        st_ref[...] = st * jnp.exp(b_last) + _dot_tn(v, k_state)
        on = _rms(o) * ng_ref[...]
        o_ref[rows, :] = (on * r_ref[rows, :].astype(F32)).astype(BF16)
        return carry

    lax.fori_loop(0, rows_blk // CHUNK, chunk, 0)


def _gla(proj, glr, w2p, gate_b, norm_g):
    rows_blk = 512
    nblk = SEQ // rows_blk
    dk, dv = GLA_HEAD_K, GLA_HEAD_V

    def row(b, h, n):
        return b * nblk + n

    return pl.pallas_call(
        functools.partial(_gla_body, rows_blk=rows_blk),
        grid=(BATCH, GLA_HEADS, nblk),
        in_specs=[pl.BlockSpec((rows_blk, dk), lambda b, h, n: (row(b, h, n), COL_Q // dk + h)),
                  pl.BlockSpec((rows_blk, dk), lambda b, h, n: (row(b, h, n), COL_K // dk + h)),
                  pl.BlockSpec((rows_blk, dv), lambda b, h, n: (row(b, h, n), COL_VV // dv + h)),
                  pl.BlockSpec((rows_blk, dv), lambda b, h, n: (row(b, h, n), COL_R // dv + h)),
                  pl.BlockSpec((rows_blk, LANES), lambda b, h, n: (row(b, h, n), 0)),
                  pl.BlockSpec((LANES, dk), lambda b, h, n: (0, h)),
                  pl.BlockSpec((1, dk), lambda b, h, n: (0, h)),
                  pl.BlockSpec((1, dv), lambda b, h, n: (0, 0))],
        out_specs=pl.BlockSpec((rows_blk, dv), lambda b, h, n: (row(b, h, n), h)),
        out_shape=jax.ShapeDtypeStruct((TOKENS, D_MODEL), BF16),
        scratch_shapes=[pltpu.VMEM((dv, dk), F32), pltpu.VMEM((rows_blk, dk), F32)],
        compiler_params=_cparams("arbitrary", "arbitrary", "arbitrary"),
        name="gla",
    )(proj, proj, proj, proj, glr, w2p, gate_b, norm_g)


def _merge_body(ya_ref, yb_ref, wa_ref, wb_ref, ga_ref, gb_ref, o_ref):
    a = jnp.dot(ya_ref[...], wa_ref[...], preferred_element_type=F32)
    b = jnp.dot(yb_ref[...], wb_ref[...], preferred_element_type=F32)
    o_ref[...] = (ga_ref[...].astype(F32) * a + gb_ref[...].astype(F32) * b).astype(BF16)


def _merge(y_a, y_b, wa, wb, proj):
    tm, tn = 512, 1024
    return pl.pallas_call(
        _merge_body,
        grid=(TOKENS // tm, D_MODEL // tn),
        in_specs=[pl.BlockSpec((tm, D_MODEL), lambda i, j: (i, 0)),
                  pl.BlockSpec((tm, D_MODEL), lambda i, j: (i, 0)),
                  pl.BlockSpec((D_MODEL, tn), lambda i, j: (0, j)),
                  pl.BlockSpec((D_MODEL, tn), lambda i, j: (0, j)),
                  pl.BlockSpec((tm, tn), lambda i, j: (i, COL_GA // tn + j)),
                  pl.BlockSpec((tm, tn), lambda i, j: (i, COL_GB // tn + j))],
        out_specs=pl.BlockSpec((tm, tn), lambda i, j: (i, j)),
        out_shape=jax.ShapeDtypeStruct((TOKENS, D_MODEL), BF16),
        compiler_params=_cparams("arbitrary", "arbitrary"),
        name="merge",
    )(y_a, y_b, wa, wb, proj, proj)


def _outproj_body(m_ref, x_ref, mod_ref, wo_ref, g2_ref, rwh_ref, rwl_ref, rb_ref,
                  x1_ref, hp_ref, idx_ref, w_ref, rank_ref, cnt_ref, run_ref, *, tm):
    @pl.when(pl.program_id(0) == 0)
    def _():
        run_ref[...] = jnp.zeros_like(run_ref)

    y = jnp.dot(m_ref[...], wo_ref[...], preferred_element_type=F32)
    x1 = x_ref[...] + mod_ref[2:3, :] * y
    x1_ref[...] = x1
    h2 = _rms(x1) * g2_ref[...] * (1.0 + mod_ref[4:5, :]) + mod_ref[3:4, :]
    hp_ref[...] = h2
    hb = h2.astype(BF16)
    hf = hb.astype(F32)

    h_lo = (h2 - hf).astype(BF16)
    logits = (_dot_nt(rwh_ref[...], hb) + _dot_nt(rwh_ref[...], h_lo) + _dot_nt(rwl_ref[...], hb)
              + rb_ref[...])
    e_i = lax.broadcasted_iota(I32, (N_EXPERTS, tm), 0).astype(F32)
    vals, idxs = [], []
    l = logits
    for _ in range(TOP_K):
        m = jnp.max(l, axis=0, keepdims=True)
        i = jnp.min(jnp.where(l == m, e_i, float(N_EXPERTS)), axis=0, keepdims=True)
        vals.append(m)
        idxs.append(i)
        l = jnp.where(e_i == i, -jnp.inf, l)
    ex = [jnp.exp(v - vals[0]) for v in vals]
    den = ex[0] + ex[1] + ex[2] + ex[3]
    before = (lax.broadcasted_iota(I32, (tm, tm), 0) < lax.broadcasted_iota(I32, (tm, tm), 1)).astype(BF16)
    run = run_ref[...]
    for k in range(TOP_K):
        onehot = e_i == idxs[k]
        pref = jnp.dot(onehot.astype(BF16), before, preferred_element_type=F32) + run[:, 0:1]
        rank_ref[k:k + 1, :] = jnp.sum(jnp.where(onehot, pref, 0.0), axis=0, keepdims=True).astype(I32)
        run = run + jnp.sum(onehot.astype(F32), axis=1, keepdims=True)
        idx_ref[k:k + 1, :] = idxs[k].astype(I32)
        w_ref[k:k + 1, :] = ex[k] / den
    run_ref[...] = run
    cnt_ref[...] = run.astype(I32)


def _outproj(merged, x2d, mod, w_out, norm2_g, rw_hi, rw_lo, router_b):
    tm = 512
    per_batch = SEQ // tm
    body = functools.partial(_outproj_body, tm=tm)
    return pl.pallas_call(
        body,
        grid=(TOKENS // tm,),
        in_specs=[pl.BlockSpec((tm, D_MODEL), lambda i: (i, 0)),
                  pl.BlockSpec((tm, D_MODEL), lambda i: (i, 0)),
                  pl.BlockSpec((None, 6, D_MODEL), lambda i: (i // per_batch, 0, 0)),
                  pl.BlockSpec((D_MODEL, D_MODEL), lambda i: (0, 0)),
                  pl.BlockSpec((1, D_MODEL), lambda i: (0, 0)),
                  pl.BlockSpec((N_EXPERTS, D_MODEL), lambda i: (0, 0)),
                  pl.BlockSpec((N_EXPERTS, D_MODEL), lambda i: (0, 0)),
                  pl.BlockSpec((N_EXPERTS, 1), lambda i: (0, 0))],
        out_specs=[pl.BlockSpec((tm, D_MODEL), lambda i: (i, 0)),
                   pl.BlockSpec((tm, D_MODEL), lambda i: (i, 0)),
                   pl.BlockSpec((TOP_K, tm), lambda i: (0, i)),
                   pl.BlockSpec((TOP_K, tm), lambda i: (0, i)),
                   pl.BlockSpec((TOP_K, tm), lambda i: (0, i)),
                   pl.BlockSpec((N_EXPERTS, LANES), lambda i: (0, 0))],
        out_shape=[jax.ShapeDtypeStruct((TOKENS, D_MODEL), F32),
                   jax.ShapeDtypeStruct((TOKENS, D_MODEL), F32),
                   jax.ShapeDtypeStruct((TOP_K, TOKENS), I32),
                   jax.ShapeDtypeStruct((TOP_K, TOKENS), F32),
                   jax.ShapeDtypeStruct((TOP_K, TOKENS), I32),
                   jax.ShapeDtypeStruct((N_EXPERTS, LANES), I32)],
        scratch_shapes=[pltpu.VMEM((N_EXPERTS, LANES), F32)],
        compiler_params=_cparams("arbitrary"),
        name="outproj",
    )(merged, x2d, mod, w_out, norm2_g, rw_hi, rw_lo, router_b)


DISPATCH_CHUNK = 512


def _dispatch_body(dest_ref, pend_ref, padded_ref, h_hbm, xs_hbm, zero_ref, zsem, sems):
    zero_ref[...] = jnp.zeros_like(zero_ref)

    def zero_copy(e):
        return pltpu.make_async_copy(
            zero_ref, xs_hbm.at[pl.ds(pl.multiple_of(pend_ref[e] - ROW_TILE, ROW_TILE), ROW_TILE)], zsem)

    def zstart(e, c):
        @pl.when(padded_ref[e] > 0)
        def _():
            zero_copy(e).start()
        return c

    def zwait(e, c):
        @pl.when(padded_ref[e] > 0)
        def _():
            zero_copy(e).wait()
        return c

    def tail_copy(t):
        return pltpu.make_async_copy(
            zero_ref, xs_hbm.at[pl.ds(pl.multiple_of(t * ROW_TILE, ROW_TILE), ROW_TILE)], zsem)

    def tstart(t, c):
        tail_copy(t).start()
        return c

    def twait(t, c):
        tail_copy(t).wait()
        return c

    first_tail = pend_ref[N_EXPERTS - 1] // ROW_TILE
    lax.fori_loop(0, N_EXPERTS, zstart, 0)
    lax.fori_loop(first_tail, N_ROW_TILES, tstart, 0)
    lax.fori_loop(0, N_EXPERTS, zwait, 0)
    lax.fori_loop(first_tail, N_ROW_TILES, twait, 0)

    n_chunks = N_ROWS // DISPATCH_CHUNK

    def issue(c, slot):
        def one(i, carry):
            r = c * DISPATCH_CHUNK + i
            t = r & (TOKENS - 1)
            pltpu.make_async_copy(h_hbm.at[pl.ds(t, 1)], xs_hbm.at[pl.ds(dest_ref[r], 1)],
                                  sems.at[slot]).start()
            return carry
        lax.fori_loop(0, DISPATCH_CHUNK, one, 0)

    def drain(slot):
        pltpu.make_async_copy(h_hbm.at[pl.ds(0, DISPATCH_CHUNK)], xs_hbm.at[pl.ds(0, DISPATCH_CHUNK)],
                              sems.at[slot]).wait()

    issue(0, 0)

    def pair(p, carry):
        issue(2 * p + 1, 1)
        drain(0)

        @pl.when(2 * p + 2 < n_chunks)
        def _():
            issue(2 * p + 2, 0)
        drain(1)
        return carry

    lax.fori_loop(0, n_chunks // 2, pair, 0)


def _dispatch(dest_flat, padded_end, padded, h_rows):
    return pl.pallas_call(
        _dispatch_body,
        grid_spec=pltpu.PrefetchScalarGridSpec(
            num_scalar_prefetch=3,
            grid=(1,),
            in_specs=[pl.BlockSpec(memory_space=pl.ANY)],
            out_specs=pl.BlockSpec(memory_space=pl.ANY),
            scratch_shapes=[pltpu.VMEM((ROW_TILE, D_MODEL), F32),
                            pltpu.SemaphoreType.DMA(()),
                            pltpu.SemaphoreType.DMA((2,))]),
        out_shape=jax.ShapeDtypeStruct((PADDED_ROWS, D_MODEL), F32),
        compiler_params=_cparams("arbitrary"),
        name="dispatch",
    )(dest_flat, padded_end, padded, h_rows)


def _expert_body(se_ref, srow_ref, snsub_ref, xs_hbm, wg_ref, wu_ref, wd_ref, bg_ref, bu_ref, bd_ref,
                 y_hbm, xraw, xb, acc, wgb, wub, wdb, sem_in, sem_out):
    s = pl.program_id(0)
    f = pl.program_id(1)
    nsub = snsub_ref[s]
    row0 = srow_ref[s]

    def tile_rows(t):
        return pl.ds(pl.multiple_of(t * ROW_TILE, ROW_TILE), ROW_TILE)

    def in_copy(t):
        src = xs_hbm.at[pl.ds(pl.multiple_of(row0 + t * ROW_TILE, ROW_TILE), ROW_TILE)]
        return pltpu.make_async_copy(src, xraw.at[tile_rows(t)], sem_in)

    def out_copy(t):
        dst = y_hbm.at[pl.ds(pl.multiple_of(row0 + t * ROW_TILE, ROW_TILE), ROW_TILE)]
        return pltpu.make_async_copy(acc.at[tile_rows(t)], dst, sem_out)

    def for_tiles(fn):
        def body(t, c):
            fn(t)
            return c
        lax.fori_loop(0, nsub, body, 0)

    @pl.when(nsub > 0)
    def _():
        @pl.when(f == 0)
        def _():
            for_tiles(lambda t: in_copy(t).start())
            for_tiles(lambda t: in_copy(t).wait())

            def unpack(t):
                rows = tile_rows(t)
                xb[rows, :] = xraw[rows, :].astype(BF16)
                acc[rows, :] = jnp.broadcast_to(bd_ref[...], (ROW_TILE, D_MODEL))
            for_tiles(unpack)

        wgb[...] = wg_ref[...].astype(BF16)
        wub[...] = wu_ref[...].astype(BF16)
        wdb[...] = wd_ref[...].astype(BF16)

        def compute(t):
            rows = tile_rows(t)
            x = xb[rows, :]
            g = jnp.minimum(jnp.dot(x, wgb[...], preferred_element_type=F32) + bg_ref[...], SWIGLU_LIMIT)
            u = jnp.clip(jnp.dot(x, wub[...], preferred_element_type=F32) + bu_ref[...],
                         -SWIGLU_LIMIT, SWIGLU_LIMIT)
            a = ((u + 1.0) * (g * jax.nn.sigmoid(SWIGLU_ALPHA * g))).astype(BF16)
            acc[rows, :] += jnp.dot(a, wdb[...], preferred_element_type=F32)
        for_tiles(compute)

        @pl.when(f == N_FF - 1)
        def _():
            for_tiles(lambda t: out_copy(t).start())
            for_tiles(lambda t: out_copy(t).wait())

    @pl.when((s == MAX_SUPER - 1) & (f == N_FF - 1))
    def _():
        acc[0:ROW_TILE, :] = jnp.zeros((ROW_TILE, D_MODEL), F32)

        def tail_copy(t):
            dst = y_hbm.at[pl.ds(pl.multiple_of(t * ROW_TILE, ROW_TILE), ROW_TILE)]
            return pltpu.make_async_copy(acc.at[0:ROW_TILE], dst, sem_out)

        def tstart(t, c):
            tail_copy(t).start()
            return c

        def twait(t, c):
            tail_copy(t).wait()
            return c

        first_tail = snsub_ref[MAX_SUPER]
        lax.fori_loop(first_tail, N_ROW_TILES, tstart, 0)
        lax.fori_loop(first_tail, N_ROW_TILES, twait, 0)


def _experts(se, srow, snsub, xs, w_gate, b_gate, w_up, b_up, w_down, b_down):
    def f_eff(s, f, nsub):
        return jnp.where(nsub[s] > 0, f, N_FF - 1)

    return pl.pallas_call(
        _expert_body,
        grid_spec=pltpu.PrefetchScalarGridSpec(
            num_scalar_prefetch=3,
            grid=(MAX_SUPER, N_FF),
            in_specs=[pl.BlockSpec(memory_space=pl.ANY),
                      pl.BlockSpec((None, D_MODEL, FF_TILE), lambda s, f, se, sr, sn: (se[s], 0, f_eff(s, f, sn))),
                      pl.BlockSpec((None, D_MODEL, FF_TILE), lambda s, f, se, sr, sn: (se[s], 0, f_eff(s, f, sn))),
                      pl.BlockSpec((None, FF_TILE, D_MODEL), lambda s, f, se, sr, sn: (se[s], f_eff(s, f, sn), 0)),
                      pl.BlockSpec((None, 1, FF_TILE), lambda s, f, se, sr, sn: (se[s], 0, f_eff(s, f, sn))),
                      pl.BlockSpec((None, 1, FF_TILE), lambda s, f, se, sr, sn: (se[s], 0, f_eff(s, f, sn))),
                      pl.BlockSpec((None, 1, D_MODEL), lambda s, f, se, sr, sn: (se[s], 0, 0))],
            out_specs=pl.BlockSpec(memory_space=pl.ANY),
            scratch_shapes=[pltpu.VMEM((SUPER_ROWS, D_MODEL), F32),
                            pltpu.VMEM((SUPER_ROWS, D_MODEL), BF16),
                            pltpu.VMEM((SUPER_ROWS, D_MODEL), F32),
                            pltpu.VMEM((D_MODEL, FF_TILE), BF16),
                            pltpu.VMEM((D_MODEL, FF_TILE), BF16),
                            pltpu.VMEM((FF_TILE, D_MODEL), BF16),
                            pltpu.SemaphoreType.DMA(()),
                            pltpu.SemaphoreType.DMA(())]),
        out_shape=jax.ShapeDtypeStruct((PADDED_ROWS, D_MODEL), F32),
        compiler_params=_cparams("arbitrary", "arbitrary"),
        name="experts",
    )(se, srow, snsub, xs, w_gate, w_up, w_down,
      b_gate.reshape(N_EXPERTS, 1, D_FF), b_up.reshape(N_EXPERTS, 1, D_FF),
      b_down.reshape(N_EXPERTS, 1, D_MODEL))


def _combine_body(dest_ref, y_hbm, x1_ref, w_ref, mod_ref, fg_ref, o_ref, buf, sem, *, tm):
    i = pl.program_id(0)
    for k in range(TOP_K):
        def one(j, carry, k=k):
            d = dest_ref[k * TOKENS + i * tm + j]
            pltpu.make_async_copy(y_hbm.at[pl.ds(d, 1)], buf.at[k, pl.ds(j, 1)], sem).start()
            return carry
        lax.fori_loop(0, tm, one, 0)
    for k in range(TOP_K):
        pltpu.make_async_copy(y_hbm.at[pl.ds(0, tm)], buf.at[k], sem).wait()
    moe = buf[0] * w_ref[:, 0:1]
    for k in range(1, TOP_K):
        moe = moe + buf[k] * w_ref[:, k:k + 1]
    x2 = x1_ref[...] + mod_ref[5:6, :] * moe
    o_ref[...] = _rms(x2) * fg_ref[...]


def _combine(dest_flat, y, x1, w_t, mod, final_g):
    tm = 256
    per_batch = SEQ // tm
    return pl.pallas_call(
        functools.partial(_combine_body, tm=tm),
        grid_spec=pltpu.PrefetchScalarGridSpec(
            num_scalar_prefetch=1,
            grid=(TOKENS // tm,),
            in_specs=[pl.BlockSpec(memory_space=pl.ANY),
                      pl.BlockSpec((tm, D_MODEL), lambda i, d: (i, 0)),
                      pl.BlockSpec((tm, TOP_K), lambda i, d: (i, 0)),
                      pl.BlockSpec((None, 6, D_MODEL), lambda i, d: (i // per_batch, 0, 0)),
                      pl.BlockSpec((1, D_MODEL), lambda i, d: (0, 0))],
            out_specs=pl.BlockSpec((tm, D_MODEL), lambda i, d: (i, 0)),
            scratch_shapes=[pltpu.VMEM((TOP_K, tm, D_MODEL), F32),
                            pltpu.SemaphoreType.DMA(())]),
        out_shape=jax.ShapeDtypeStruct((TOKENS, D_MODEL), F32),
        compiler_params=_cparams("arbitrary"),
        name="combine",
    )(dest_flat, y, x1, w_t, mod, final_g)


def _routing_tables(idx, rank, counts):
    padded = (counts + ROW_TILE - 1) // ROW_TILE * ROW_TILE
    padded_end = jnp.cumsum(padded)
    padded_start = padded_end - padded
    e_i = jnp.arange(N_EXPERTS, dtype=I32)[:, None, None]
    start_of = jnp.sum(jnp.where(idx[None] == e_i, padded_start[:, None, None], 0), axis=0)
    dest = (start_of + rank).reshape(N_ROWS).astype(I32)

    tiles = padded // ROW_TILE
    n_super = (tiles + SUPER_TILES - 1) // SUPER_TILES
    super_end = jnp.cumsum(n_super)
    super_start = super_end - n_super
    s_i = jnp.arange(MAX_SUPER, dtype=I32)
    total = super_end[-1]
    valid = s_i < total
    e_of = jnp.minimum(jnp.sum((s_i[:, None] >= super_end[None, :]).astype(I32), axis=1), N_EXPERTS - 1)
    last_e = jnp.minimum(jnp.sum((total - 1 >= super_end).astype(I32)), N_EXPERTS - 1)
    local = s_i - super_start[e_of]
    srow = jnp.where(valid, padded_start[e_of] + local * SUPER_ROWS, 0).astype(I32)
    snsub = jnp.where(valid, jnp.minimum(tiles[e_of] - local * SUPER_TILES, SUPER_TILES), 0).astype(I32)
    snsub = jnp.concatenate([snsub, (padded_end[-1:] // ROW_TILE).astype(I32)])
    se = jnp.where(valid, e_of, last_e).astype(I32)
    return dest, padded_end.astype(I32), padded.astype(I32), se, srow, snsub


def kernel(x, c, ada_w, ada_b, norm1_g, w_in, gla_gate_w2, gla_gate_b, sgu_ln_g, sgu_ln_b, sgu_w, sgu_b,
           gla_norm_g, w_branch_a, w_branch_b, w_out, norm2_g, router_w, router_b, exp_w_gate, exp_b_gate,
           exp_w_up, exp_b_up, exp_w_down, exp_b_down, final_g):
    x2d = x.reshape(TOKENS, D_MODEL)
    mod = _ada(c, ada_w[0], ada_b[0])

    w_in0 = w_in[0]
    w_main = jnp.concatenate([w_in0[:, :GLR_SRC], w_in0[:, GLR_SRC + GLA_GATE_RANK:]], axis=1).astype(BF16)
    w_glr = jnp.zeros((D_MODEL, LANES), BF16).at[:, :GLA_GATE_RANK].set(
        w_in0[:, GLR_SRC:GLR_SRC + GLA_GATE_RANK].astype(BF16))
    proj, glr = _inproj(x2d, mod, norm1_g, w_main, w_glr)

    y_a = _sgu(proj, sgu_ln_g, sgu_ln_b, sgu_w[0], sgu_b[0].T)

    w2p = jnp.zeros((LANES, GLA_DK), BF16).at[:GLA_GATE_RANK].set(gla_gate_w2[0].astype(BF16))
    y_b = _gla(proj, glr, w2p, gla_gate_b, gla_norm_g)

    merged = _merge(y_a, y_b, w_branch_a[0].astype(BF16), w_branch_b[0].astype(BF16), proj)

    rw_t = router_w[0].T
    rw_hi = rw_t.astype(BF16)
    rw_lo = (rw_t - rw_hi.astype(F32)).astype(BF16)
    x1, h_rows, idx, top_w, rank, cnt = _outproj(
        merged, x2d, mod, w_out[0].astype(BF16), norm2_g, rw_hi, rw_lo, router_b[0].reshape(N_EXPERTS, 1))

    dest, padded_end, padded, se, srow, snsub = _routing_tables(idx, rank, cnt[:, 0])
    xs = _dispatch(dest, padded_end, padded, h_rows)
    y = _experts(se, srow, snsub, xs, exp_w_gate[0], exp_b_gate[0], exp_w_up[0], exp_b_up[0],
                 exp_w_down[0], exp_b_down[0])
    out = _combine(dest, y, x1, top_w.T, mod, final_g.reshape(1, D_MODEL))
    return out.reshape(BATCH, SEQ, D_MODEL)
```

```python
import functools

import jax
import jax.numpy as jnp
from jax import lax
from jax.experimental import pallas as pl
from jax.experimental.pallas import tpu as pltpu

F32 = jnp.float32
BF16 = jnp.bfloat16
I32 = jnp.int32

D_MODEL = 2048
BATCH = 4
SEQ = 2048
TOKENS = BATCH * SEQ
CHUNK = 64
SPATIAL_BLOCK = 128
A_GROUPS = 8
A_GROUP_DIM = D_MODEL // A_GROUPS
GLA_HEADS = 4
GLA_DK = D_MODEL // 2
GLA_HEAD_K = GLA_DK // GLA_HEADS
GLA_HEAD_V = D_MODEL // GLA_HEADS
GLA_GATE_RANK = 16
GLA_GATE_TAU = 16.0
N_EXPERTS = 32
TOP_K = 4
D_FF = D_MODEL
SWIGLU_LIMIT = 7.0
SWIGLU_ALPHA = 1.702
NORM_EPS = 1e-6

LANES = 128
VMEM_LIMIT = 56 * 1024 * 1024

COL_U, COL_V, COL_Q, COL_K, COL_VV, COL_R = 0, 2048, 4096, 5120, 6144, 8192
COL_GA, COL_GB = 10240, 12288
PROJ_W = 14336
GLR_SRC = 10240

ROW_TILE = 256
SUPER_TILES = 6
SUPER_ROWS = ROW_TILE * SUPER_TILES
N_ROWS = TOKENS * TOP_K
N_ROW_TILES = (N_ROWS + N_EXPERTS * (ROW_TILE - 1)) // ROW_TILE
PADDED_ROWS = N_ROW_TILES * ROW_TILE
MAX_SUPER = -(-(N_ROW_TILES + N_EXPERTS * (SUPER_TILES - 1)) // SUPER_TILES)
FF_TILE = 256
N_FF = D_FF // FF_TILE


def _cparams(*sem):
    return pltpu.CompilerParams(dimension_semantics=sem, vmem_limit_bytes=VMEM_LIMIT)


def _rms(x):
    return x * lax.rsqrt(jnp.mean(x * x, axis=-1, keepdims=True) + NORM_EPS)


def _ada_body(c_ref, w_ref, b_ref, o_ref):
    c = c_ref[...]
    cond = c * jax.nn.sigmoid(c)
    o_ref[...] = jnp.dot(cond.astype(BF16), w_ref[...].astype(BF16),
                         preferred_element_type=F32) + b_ref[...]


def _ada(c, ada_w, ada_b):
    tn = 1024
    cp = jnp.zeros((8, D_MODEL), F32).at[:BATCH].set(c)
    out = pl.pallas_call(
        _ada_body,
        grid=(6 * D_MODEL // tn,),
        in_specs=[pl.BlockSpec((8, D_MODEL), lambda j: (0, 0)),
                  pl.BlockSpec((D_MODEL, tn), lambda j: (0, j)),
                  pl.BlockSpec((1, tn), lambda j: (0, j))],
        out_specs=pl.BlockSpec((8, tn), lambda j: (0, j)),
        out_shape=jax.ShapeDtypeStruct((8, 6 * D_MODEL), F32),
        compiler_params=_cparams("arbitrary"),
        name="ada",
    )(cp, ada_w, ada_b.reshape(1, 6 * D_MODEL))
    return out[:BATCH].reshape(BATCH, 6, D_MODEL)


def _inproj_body(x_ref, mod_ref, g_ref, w_ref, wglr_ref, proj_ref, glr_ref, h_ref, *, tn):
    j = pl.program_id(1)

    @pl.when(j == 0)
    def _():
        h = _rms(x_ref[...]) * g_ref[...] * (1.0 + mod_ref[1:2, :]) + mod_ref[0:1, :]
        hb = h.astype(BF16)
        h_ref[...] = hb
        glr_ref[...] = jnp.dot(hb, wglr_ref[...], preferred_element_type=F32)

    acc = jnp.dot(h_ref[...], w_ref[...], preferred_element_type=F32)
    col = j * tn

    @pl.when(col < COL_Q)
    def _():
        proj_ref[...] = jax.nn.gelu(acc).astype(BF16)

    @pl.when((col >= COL_Q) & (col < COL_R))
    def _():
        proj_ref[...] = acc.astype(BF16)

    @pl.when((col >= COL_R) & (col < COL_GA))
    def _():
        proj_ref[...] = (acc * jax.nn.sigmoid(acc)).astype(BF16)

    @pl.when(col >= COL_GA)
    def _():
        proj_ref[...] = jax.nn.sigmoid(acc).astype(BF16)


def _inproj(x2d, mod, norm1_g, w_main, w_glr):
    tm, tn = 512, 1024
    per_batch = SEQ // tm
    return pl.pallas_call(
        functools.partial(_inproj_body, tn=tn),
        grid=(TOKENS // tm, PROJ_W // tn),
        in_specs=[pl.BlockSpec((tm, D_MODEL), lambda i, j: (i, 0)),
                  pl.BlockSpec((None, 6, D_MODEL), lambda i, j: (i // per_batch, 0, 0)),
                  pl.BlockSpec((1, D_MODEL), lambda i, j: (0, 0)),
                  pl.BlockSpec((D_MODEL, tn), lambda i, j: (0, j)),
                  pl.BlockSpec((D_MODEL, LANES), lambda i, j: (0, 0))],
        out_specs=[pl.BlockSpec((tm, tn), lambda i, j: (i, j)),
                   pl.BlockSpec((tm, LANES), lambda i, j: (i, 0))],
        out_shape=[jax.ShapeDtypeStruct((TOKENS, PROJ_W), BF16),
                   jax.ShapeDtypeStruct((TOKENS, LANES), F32)],
        scratch_shapes=[pltpu.VMEM((tm, D_MODEL), BF16)],
        compiler_params=_cparams("arbitrary", "arbitrary"),
        name="inproj",
    )(x2d, mod, norm1_g, w_main, w_glr)


def _sgu_body(u_ref, v_ref, lg_ref, lb_ref, ws_ref, bs_ref, o_ref):
    v = v_ref[...].astype(F32)
    mu = jnp.mean(v, axis=-1, keepdims=True)
    xc = v - mu
    var = jnp.mean(xc * xc, axis=-1, keepdims=True)
    vn = (xc * lax.rsqrt(var + NORM_EPS) * lg_ref[...] + lb_ref[...]).astype(BF16)
    t_chunk = lax.broadcasted_iota(I32, (SPATIAL_BLOCK, SPATIAL_BLOCK), 0) // CHUNK
    s_chunk = lax.broadcasted_iota(I32, (SPATIAL_BLOCK, SPATIAL_BLOCK), 1) // CHUNK
    mask = s_chunk <= t_chunk
    for g in range(A_GROUPS):
        cols = slice(g * A_GROUP_DIM, (g + 1) * A_GROUP_DIM)
        w = jnp.where(mask, ws_ref[g], 0.0).astype(BF16)
        mixed = jnp.dot(w, vn[:, cols], preferred_element_type=F32) + bs_ref[:, g:g + 1]
        o_ref[:, cols] = (u_ref[:, cols].astype(F32) * mixed).astype(BF16)


def _sgu(proj, ln_g, ln_b, w_s, b_s_t):
    nblk = TOKENS // SPATIAL_BLOCK
    wb = D_MODEL
    return pl.pallas_call(
        _sgu_body,
        grid=(nblk,),
        in_specs=[pl.BlockSpec((SPATIAL_BLOCK, wb), lambda i: (i, COL_U // wb)),
                  pl.BlockSpec((SPATIAL_BLOCK, wb), lambda i: (i, COL_V // wb)),
                  pl.BlockSpec((1, wb), lambda i: (0, 0)),
                  pl.BlockSpec((1, wb), lambda i: (0, 0)),
                  pl.BlockSpec((A_GROUPS, SPATIAL_BLOCK, SPATIAL_BLOCK), lambda i: (0, 0, 0)),
                  pl.BlockSpec((SPATIAL_BLOCK, A_GROUPS), lambda i: (0, 0))],
        out_specs=pl.BlockSpec((SPATIAL_BLOCK, wb), lambda i: (i, 0)),
        out_shape=jax.ShapeDtypeStruct((TOKENS, wb), BF16),
        compiler_params=_cparams("arbitrary"),
        name="sgu",
    )(proj, proj, ln_g, ln_b, w_s, b_s_t)


def _dot_nt(a, b):
    return lax.dot_general(a, b, (((1,), (1,)), ((), ())), preferred_element_type=F32)


def _dot_tn(a, b):
    return lax.dot_general(a, b, (((0,), (0,)), ((), ())), preferred_element_type=F32)


def _gla_body(q_ref, k_ref, v_ref, r_ref, glr_ref, w2_ref, gb_ref, ng_ref, o_ref, st_ref, la_ref, *, rows_blk):
    @pl.when(pl.program_id(2) == 0)
    def _():
        st_ref[...] = jnp.zeros_like(st_ref)

    z = jnp.dot(glr_ref[...].astype(BF16), w2_ref[...], preferred_element_type=F32) + gb_ref[...]
    la_ref[...] = jax.nn.log_sigmoid(z) / GLA_GATE_TAU

    r_i = lax.broadcasted_iota(I32, (CHUNK, CHUNK), 0)
    c_i = lax.broadcasted_iota(I32, (CHUNK, CHUNK), 1)
    causal = c_i <= r_i
    tril = causal.astype(BF16)
    scale = GLA_HEAD_K ** -0.5

    def chunk(c, carry):
        rows = pl.ds(pl.multiple_of(c * CHUNK, CHUNK), CHUNK)
        la = la_ref[rows, :]
        hi = la.astype(BF16)
        r1 = la - hi.astype(F32)
        mid = r1.astype(BF16)
        lo = (r1 - mid.astype(F32)).astype(BF16)
        b = (jnp.dot(tril, hi, preferred_element_type=F32)
             + jnp.dot(tril, mid, preferred_element_type=F32)
             + jnp.dot(tril, lo, preferred_element_type=F32))
        b_last = b[CHUNK - 1:CHUNK, :]
        q = q_ref[rows, :].astype(F32) * scale
        k = k_ref[rows, :].astype(F32)
        v = v_ref[rows, :]
        q_dec = (q * jnp.exp(b)).astype(BF16)
        k_intra = (k * jnp.exp(-b)).astype(BF16)
        k_state = (k * jnp.exp(b_last - b)).astype(BF16)
        att = jnp.where(causal, _dot_nt(q_dec, k_intra), 0.0).astype(BF16)
        st = st_ref[...]
        o = jnp.dot(att, v, preferred_element_type=F32) + _dot_nt(q_dec, st.astype(BF16))
        st_ref[...] = st * jnp.exp(b_last) + _dot_tn(v, k_state)
        on = _rms(o) * ng_ref[...]
        o_ref[rows, :] = (on * r_ref[rows, :].astype(F32)).astype(BF16)
        return carry

    lax.fori_loop(0, rows_blk // CHUNK, chunk, 0)


def _gla(proj, glr, w2p, gate_b, norm_g):
    rows_blk = 512
    nblk = SEQ // rows_blk
    dk, dv = GLA_HEAD_K, GLA_HEAD_V

    def row(b, h, n):
        return b * nblk + n

    return pl.pallas_call(
        functools.partial(_gla_body, rows_blk=rows_blk),
        grid=(BATCH, GLA_HEADS, nblk),
        in_specs=[pl.BlockSpec((rows_blk, dk), lambda b, h, n: (row(b, h, n), COL_Q // dk + h)),
                  pl.BlockSpec((rows_blk, dk), lambda b, h, n: (row(b, h, n), COL_K // dk + h)),
                  pl.BlockSpec((rows_blk, dv), lambda b, h, n: (row(b, h, n), COL_VV // dv + h)),
                  pl.BlockSpec((rows_blk, dv), lambda b, h, n: (row(b, h, n), COL_R // dv + h)),
                  pl.BlockSpec((rows_blk, LANES), lambda b, h, n: (row(b, h, n), 0)),
                  pl.BlockSpec((LANES, dk), lambda b, h, n: (0, h)),
                  pl.BlockSpec((1, dk), lambda b, h, n: (0, h)),
                  pl.BlockSpec((1, dv), lambda b, h, n: (0, 0))],
        out_specs=pl.BlockSpec((rows_blk, dv), lambda b, h, n: (row(b, h, n), h)),
        out_shape=jax.ShapeDtypeStruct((TOKENS, D_MODEL), BF16),
        scratch_shapes=[pltpu.VMEM((dv, dk), F32), pltpu.VMEM((rows_blk, dk), F32)],
        compiler_params=_cparams("arbitrary", "arbitrary", "arbitrary"),
        name="gla",
    )(proj, proj, proj, proj, glr, w2p, gate_b, norm_g)


def _merge_body(ya_ref, yb_ref, wa_ref, wb_ref, ga_ref, gb_ref, o_ref):
    a = jnp.dot(ya_ref[...], wa_ref[...], preferred_element_type=F32)
    b = jnp.dot(yb_ref[...], wb_ref[...], preferred_element_type=F32)
    o_ref[...] = (ga_ref[...].astype(F32) * a + gb_ref[...].astype(F32) * b).astype(BF16)


def _merge(y_a, y_b, wa, wb, proj):
    tm, tn = 512, 1024
    return pl.pallas_call(
        _merge_body,
        grid=(TOKENS // tm, D_MODEL // tn),
        in_specs=[pl.BlockSpec((tm, D_MODEL), lambda i, j: (i, 0)),
                  pl.BlockSpec((tm, D_MODEL), lambda i, j: (i, 0)),
                  pl.BlockSpec((D_MODEL, tn), lambda i, j: (0, j)),
                  pl.BlockSpec((D_MODEL, tn), lambda i, j: (0, j)),
                  pl.BlockSpec((tm, tn), lambda i, j: (i, COL_GA // tn + j)),
                  pl.BlockSpec((tm, tn), lambda i, j: (i, COL_GB // tn + j))],
        out_specs=pl.BlockSpec((tm, tn), lambda i, j: (i, j)),
        out_shape=jax.ShapeDtypeStruct((TOKENS, D_MODEL), BF16),
        compiler_params=_cparams("arbitrary", "arbitrary"),
        name="merge",
    )(y_a, y_b, wa, wb, proj, proj)


def _outproj_body(m_ref, x_ref, mod_ref, wo_ref, g2_ref, rwh_ref, rwl_ref, rb_ref,
                  x1_ref, hp_ref, idx_ref, w_ref, rank_ref, cnt_ref, run_ref, *, tm):
    @pl.when(pl.program_id(0) == 0)
    def _():
        run_ref[...] = jnp.zeros_like(run_ref)

    y = jnp.dot(m_ref[...], wo_ref[...], preferred_element_type=F32)
    x1 = x_ref[...] + mod_ref[2:3, :] * y
    x1_ref[...] = x1
    h2 = _rms(x1) * g2_ref[...] * (1.0 + mod_ref[4:5, :]) + mod_ref[3:4, :]
    hp_ref[...] = h2
    hb = h2.astype(BF16)
    hf = hb.astype(F32)

    h_lo = (h2 - hf).astype(BF16)
    logits = (_dot_nt(rwh_ref[...], hb) + _dot_nt(rwh_ref[...], h_lo) + _dot_nt(rwl_ref[...], hb)
              + rb_ref[...])
    e_i = lax.broadcasted_iota(I32, (N_EXPERTS, tm), 0).astype(F32)
    vals, idxs = [], []
    l = logits
    for _ in range(TOP_K):
        m = jnp.max(l, axis=0, keepdims=True)
        i = jnp.min(jnp.where(l == m, e_i, float(N_EXPERTS)), axis=0, keepdims=True)
        vals.append(m)
        idxs.append(i)
        l = jnp.where(e_i == i, -jnp.inf, l)
    ex = [jnp.exp(v - vals[0]) for v in vals]
    den = ex[0] + ex[1] + ex[2] + ex[3]
    before = (lax.broadcasted_iota(I32, (tm, tm), 0) < lax.broadcasted_iota(I32, (tm, tm), 1)).astype(BF16)
    run = run_ref[...]
    for k in range(TOP_K):
        onehot = e_i == idxs[k]
        pref = jnp.dot(onehot.astype(BF16), before, preferred_element_type=F32) + run[:, 0:1]
        rank_ref[k:k + 1, :] = jnp.sum(jnp.where(onehot, pref, 0.0), axis=0, keepdims=True).astype(I32)
        run = run + jnp.sum(onehot.astype(F32), axis=1, keepdims=True)
        idx_ref[k:k + 1, :] = idxs[k].astype(I32)
        w_ref[k:k + 1, :] = ex[k] / den
    run_ref[...] = run
    cnt_ref[...] = run.astype(I32)


def _outproj(merged, x2d, mod, w_out, norm2_g, rw_hi, rw_lo, router_b):
    tm = 512
    per_batch = SEQ // tm
    body = functools.partial(_outproj_body, tm=tm)
    return pl.pallas_call(
        body,
        grid=(TOKENS // tm,),
        in_specs=[pl.BlockSpec((tm, D_MODEL), lambda i: (i, 0)),
                  pl.BlockSpec((tm, D_MODEL), lambda i: (i, 0)),
                  pl.BlockSpec((None, 6, D_MODEL), lambda i: (i // per_batch, 0, 0)),
                  pl.BlockSpec((D_MODEL, D_MODEL), lambda i: (0, 0)),
                  pl.BlockSpec((1, D_MODEL), lambda i: (0, 0)),
                  pl.BlockSpec((N_EXPERTS, D_MODEL), lambda i: (0, 0)),
                  pl.BlockSpec((N_EXPERTS, D_MODEL), lambda i: (0, 0)),
                  pl.BlockSpec((N_EXPERTS, 1), lambda i: (0, 0))],
        out_specs=[pl.BlockSpec((tm, D_MODEL), lambda i: (i, 0)),
                   pl.BlockSpec((tm, D_MODEL), lambda i: (i, 0)),
                   pl.BlockSpec((TOP_K, tm), lambda i: (0, i)),
                   pl.BlockSpec((TOP_K, tm), lambda i: (0, i)),
                   pl.BlockSpec((TOP_K, tm), lambda i: (0, i)),
                   pl.BlockSpec((N_EXPERTS, LANES), lambda i: (0, 0))],
        out_shape=[jax.ShapeDtypeStruct((TOKENS, D_MODEL), F32),
                   jax.ShapeDtypeStruct((TOKENS, D_MODEL), F32),
                   jax.ShapeDtypeStruct((TOP_K, TOKENS), I32),
                   jax.ShapeDtypeStruct((TOP_K, TOKENS), F32),
                   jax.ShapeDtypeStruct((TOP_K, TOKENS), I32),
                   jax.ShapeDtypeStruct((N_EXPERTS, LANES), I32)],
        scratch_shapes=[pltpu.VMEM((N_EXPERTS, LANES), F32)],
        compiler_params=_cparams("arbitrary"),
        name="outproj",
    )(merged, x2d, mod, w_out, norm2_g, rw_hi, rw_lo, router_b)


def _dispatch_body(dest_ref, pend_ref, padded_ref, h_ref, xs_hbm, zero_ref, zsem, sem, *, tm):
    i = pl.program_id(0)

    @pl.when(i == 0)
    def _():
        _dispatch_clear(pend_ref, padded_ref, xs_hbm, zero_ref, zsem)

    for k in range(TOP_K):
        def one(j, carry, k=k):
            d = dest_ref[k * TOKENS + i * tm + j]
            pltpu.make_async_copy(h_ref.at[pl.ds(j, 1)], xs_hbm.at[pl.ds(d, 1)], sem).start()
            return carry
        lax.fori_loop(0, tm, one, 0)
    for k in range(TOP_K):
        pltpu.make_async_copy(h_ref, xs_hbm.at[pl.ds(0, tm)], sem).wait()


def _dispatch_clear(pend_ref, padded_ref, xs_hbm, zero_ref, zsem):
    zero_ref[...] = jnp.zeros_like(zero_ref)

    def zero_copy(e):
        return pltpu.make_async_copy(
            zero_ref, xs_hbm.at[pl.ds(pl.multiple_of(pend_ref[e] - ROW_TILE, ROW_TILE), ROW_TILE)], zsem)

    def zstart(e, c):
        @pl.when(padded_ref[e] > 0)
        def _():
            zero_copy(e).start()
        return c

    def zwait(e, c):
        @pl.when(padded_ref[e] > 0)
        def _():
            zero_copy(e).wait()
        return c

    def tail_copy(t):
        return pltpu.make_async_copy(
            zero_ref, xs_hbm.at[pl.ds(pl.multiple_of(t * ROW_TILE, ROW_TILE), ROW_TILE)], zsem)

    def tstart(t, c):
        tail_copy(t).start()
        return c

    def twait(t, c):
        tail_copy(t).wait()
        return c

    first_tail = pend_ref[N_EXPERTS - 1] // ROW_TILE
    lax.fori_loop(0, N_EXPERTS, zstart, 0)
    lax.fori_loop(first_tail, N_ROW_TILES, tstart, 0)
    lax.fori_loop(0, N_EXPERTS, zwait, 0)
    lax.fori_loop(first_tail, N_ROW_TILES, twait, 0)


def _dispatch(dest_flat, padded_end, padded, h_rows):
    tm = 256
    return pl.pallas_call(
        functools.partial(_dispatch_body, tm=tm),
        grid_spec=pltpu.PrefetchScalarGridSpec(
            num_scalar_prefetch=3,
            grid=(TOKENS // tm,),
            in_specs=[pl.BlockSpec((tm, D_MODEL), lambda i, d, pe, pd: (i, 0))],
            out_specs=pl.BlockSpec(memory_space=pl.ANY),
            scratch_shapes=[pltpu.VMEM((ROW_TILE, D_MODEL), F32),
                            pltpu.SemaphoreType.DMA(()),
                            pltpu.SemaphoreType.DMA(())]),
        out_shape=jax.ShapeDtypeStruct((PADDED_ROWS, D_MODEL), F32),
        compiler_params=_cparams("arbitrary"),
        name="dispatch",
    )(dest_flat, padded_end, padded, h_rows)


def _expert_body(se_ref, srow_ref, snsub_ref, xs_hbm, wg_ref, wu_ref, wd_ref, bg_ref, bu_ref, bd_ref,
                 y_hbm, xraw, xb, acc, wgb, wub, wdb, sem_in, sem_out):
    s = pl.program_id(0)
    f = pl.program_id(1)
    nsub = snsub_ref[s]
    row0 = srow_ref[s]

    def tile_rows(t):
        return pl.ds(pl.multiple_of(t * ROW_TILE, ROW_TILE), ROW_TILE)

    def in_copy(t):
        src = xs_hbm.at[pl.ds(pl.multiple_of(row0 + t * ROW_TILE, ROW_TILE), ROW_TILE)]
        return pltpu.make_async_copy(src, xraw.at[tile_rows(t)], sem_in)

    def out_copy(t):
        dst = y_hbm.at[pl.ds(pl.multiple_of(row0 + t * ROW_TILE, ROW_TILE), ROW_TILE)]
        return pltpu.make_async_copy(acc.at[tile_rows(t)], dst, sem_out)

    def for_tiles(fn):
        def body(t, c):
            fn(t)
            return c
        lax.fori_loop(0, nsub, body, 0)

    @pl.when(nsub > 0)
    def _():
        @pl.when(f == 0)
        def _():
            for_tiles(lambda t: in_copy(t).start())
            for_tiles(lambda t: in_copy(t).wait())

            def unpack(t):
                rows = tile_rows(t)
                xb[rows, :] = xraw[rows, :].astype(BF16)
                acc[rows, :] = jnp.broadcast_to(bd_ref[...], (ROW_TILE, D_MODEL))
            for_tiles(unpack)

        wgb[...] = wg_ref[...].astype(BF16)
        wub[...] = wu_ref[...].astype(BF16)
        wdb[...] = wd_ref[...].astype(BF16)

        def compute(t):
            rows = tile_rows(t)
            x = xb[rows, :]
            g = jnp.minimum(jnp.dot(x, wgb[...], preferred_element_type=F32) + bg_ref[...], SWIGLU_LIMIT)
            u = jnp.clip(jnp.dot(x, wub[...], preferred_element_type=F32) + bu_ref[...],
                         -SWIGLU_LIMIT, SWIGLU_LIMIT)
            a = ((u + 1.0) * (g * jax.nn.sigmoid(SWIGLU_ALPHA * g))).astype(BF16)
            acc[rows, :] += jnp.dot(a, wdb[...], preferred_element_type=F32)
        for_tiles(compute)

        @pl.when(f == N_FF - 1)
        def _():
            for_tiles(lambda t: out_copy(t).start())
            for_tiles(lambda t: out_copy(t).wait())

    @pl.when((s == MAX_SUPER - 1) & (f == N_FF - 1))
    def _():
        acc[0:ROW_TILE, :] = jnp.zeros((ROW_TILE, D_MODEL), F32)

        def tail_copy(t):
            dst = y_hbm.at[pl.ds(pl.multiple_of(t * ROW_TILE, ROW_TILE), ROW_TILE)]
            return pltpu.make_async_copy(acc.at[0:ROW_TILE], dst, sem_out)

        def tstart(t, c):
            tail_copy(t).start()
            return c

        def twait(t, c):
            tail_copy(t).wait()
            return c

        first_tail = snsub_ref[MAX_SUPER]
        lax.fori_loop(first_tail, N_ROW_TILES, tstart, 0)
        lax.fori_loop(first_tail, N_ROW_TILES, twait, 0)


def _experts(se, srow, snsub, xs, w_gate, b_gate, w_up, b_up, w_down, b_down):
    def f_eff(s, f, nsub):
        return jnp.where(nsub[s] > 0, f, N_FF - 1)

    return pl.pallas_call(
        _expert_body,
        grid_spec=pltpu.PrefetchScalarGridSpec(
            num_scalar_prefetch=3,
            grid=(MAX_SUPER, N_FF),
            in_specs=[pl.BlockSpec(memory_space=pl.ANY),
                      pl.BlockSpec((None, D_MODEL, FF_TILE), lambda s, f, se, sr, sn: (se[s], 0, f_eff(s, f, sn))),
                      pl.BlockSpec((None, D_MODEL, FF_TILE), lambda s, f, se, sr, sn: (se[s], 0, f_eff(s, f, sn))),
                      pl.BlockSpec((None, FF_TILE, D_MODEL), lambda s, f, se, sr, sn: (se[s], f_eff(s, f, sn), 0)),
                      pl.BlockSpec((None, 1, FF_TILE), lambda s, f, se, sr, sn: (se[s], 0, f_eff(s, f, sn))),
                      pl.BlockSpec((None, 1, FF_TILE), lambda s, f, se, sr, sn: (se[s], 0, f_eff(s, f, sn))),
                      pl.BlockSpec((None, 1, D_MODEL), lambda s, f, se, sr, sn: (se[s], 0, 0))],
            out_specs=pl.BlockSpec(memory_space=pl.ANY),
            scratch_shapes=[pltpu.VMEM((SUPER_ROWS, D_MODEL), F32),
                            pltpu.VMEM((SUPER_ROWS, D_MODEL), BF16),
                            pltpu.VMEM((SUPER_ROWS, D_MODEL), F32),
                            pltpu.VMEM((D_MODEL, FF_TILE), BF16),
                            pltpu.VMEM((D_MODEL, FF_TILE), BF16),
                            pltpu.VMEM((FF_TILE, D_MODEL), BF16),
                            pltpu.SemaphoreType.DMA(()),
                            pltpu.SemaphoreType.DMA(())]),
        out_shape=jax.ShapeDtypeStruct((PADDED_ROWS, D_MODEL), F32),
        compiler_params=_cparams("arbitrary", "arbitrary"),
        name="experts",
    )(se, srow, snsub, xs, w_gate, w_up, w_down,
      b_gate.reshape(N_EXPERTS, 1, D_FF), b_up.reshape(N_EXPERTS, 1, D_FF),
      b_down.reshape(N_EXPERTS, 1, D_MODEL))


def _combine_body(dest_ref, y_hbm, x1_ref, w_ref, mod_ref, fg_ref, o_ref, buf, sem, *, tm):
    i = pl.program_id(0)
    for k in range(TOP_K):
        def one(j, carry, k=k):
            d = dest_ref[k * TOKENS + i * tm + j]
            pltpu.make_async_copy(y_hbm.at[pl.ds(d, 1)], buf.at[k, pl.ds(j, 1)], sem).start()
            return carry
        lax.fori_loop(0, tm, one, 0)
    for k in range(TOP_K):
        pltpu.make_async_copy(y_hbm.at[pl.ds(0, tm)], buf.at[k], sem).wait()
    moe = buf[0] * w_ref[:, 0:1]
    for k in range(1, TOP_K):
        moe = moe + buf[k] * w_ref[:, k:k + 1]
    x2 = x1_ref[...] + mod_ref[5:6, :] * moe
    o_ref[...] = _rms(x2) * fg_ref[...]


def _combine(dest_flat, y, x1, w_t, mod, final_g):
    tm = 256
    per_batch = SEQ // tm
    return pl.pallas_call(
        functools.partial(_combine_body, tm=tm),
        grid_spec=pltpu.PrefetchScalarGridSpec(
            num_scalar_prefetch=1,
            grid=(TOKENS // tm,),
            in_specs=[pl.BlockSpec(memory_space=pl.ANY),
                      pl.BlockSpec((tm, D_MODEL), lambda i, d: (i, 0)),
                      pl.BlockSpec((tm, TOP_K), lambda i, d: (i, 0)),
                      pl.BlockSpec((None, 6, D_MODEL), lambda i, d: (i // per_batch, 0, 0)),
                      pl.BlockSpec((1, D_MODEL), lambda i, d: (0, 0))],
            out_specs=pl.BlockSpec((tm, D_MODEL), lambda i, d: (i, 0)),
            scratch_shapes=[pltpu.VMEM((TOP_K, tm, D_MODEL), F32),
                            pltpu.SemaphoreType.DMA(())]),
        out_shape=jax.ShapeDtypeStruct((TOKENS, D_MODEL), F32),
        compiler_params=_cparams("arbitrary"),
        name="combine",
    )(dest_flat, y, x1, w_t, mod, final_g)


def _routing_tables(idx, rank, counts):
    padded = (counts + ROW_TILE - 1) // ROW_TILE * ROW_TILE
    padded_end = jnp.cumsum(padded)
    padded_start = padded_end - padded
    e_i = jnp.arange(N_EXPERTS, dtype=I32)[:, None, None]
    start_of = jnp.sum(jnp.where(idx[None] == e_i, padded_start[:, None, None], 0), axis=0)
    dest = (start_of + rank).reshape(N_ROWS).astype(I32)

    tiles = padded // ROW_TILE
    n_super = (tiles + SUPER_TILES - 1) // SUPER_TILES
    super_end = jnp.cumsum(n_super)
    super_start = super_end - n_super
    s_i = jnp.arange(MAX_SUPER, dtype=I32)
    total = super_end[-1]
    valid = s_i < total
    e_of = jnp.minimum(jnp.sum((s_i[:, None] >= super_end[None, :]).astype(I32), axis=1), N_EXPERTS - 1)
    last_e = jnp.minimum(jnp.sum((total - 1 >= super_end).astype(I32)), N_EXPERTS - 1)
    local = s_i - super_start[e_of]
    srow = jnp.where(valid, padded_start[e_of] + local * SUPER_ROWS, 0).astype(I32)
    snsub = jnp.where(valid, jnp.minimum(tiles[e_of] - local * SUPER_TILES, SUPER_TILES), 0).astype(I32)
    snsub = jnp.concatenate([snsub, (padded_end[-1:] // ROW_TILE).astype(I32)])
    se = jnp.where(valid, e_of, last_e).astype(I32)
    return dest, padded_end.astype(I32), padded.astype(I32), se, srow, snsub


def kernel(x, c, ada_w, ada_b, norm1_g, w_in, gla_gate_w2, gla_gate_b, sgu_ln_g, sgu_ln_b, sgu_w, sgu_b,
           gla_norm_g, w_branch_a, w_branch_b, w_out, norm2_g, router_w, router_b, exp_w_gate, exp_b_gate,
           exp_w_up, exp_b_up, exp_w_down, exp_b_down, final_g):
    x2d = x.reshape(TOKENS, D_MODEL)
    mod = _ada(c, ada_w[0], ada_b[0])

    w_in0 = w_in[0]
    w_main = jnp.concatenate([w_in0[:, :GLR_SRC], w_in0[:, GLR_SRC + GLA_GATE_RANK:]], axis=1).astype(BF16)
    w_glr = jnp.zeros((D_MODEL, LANES), BF16).at[:, :GLA_GATE_RANK].set(
        w_in0[:, GLR_SRC:GLR_SRC + GLA_GATE_RANK].astype(BF16))
    proj, glr = _inproj(x2d, mod, norm1_g, w_main, w_glr)

    y_a = _sgu(proj, sgu_ln_g, sgu_ln_b, sgu_w[0], sgu_b[0].T)

    w2p = jnp.zeros((LANES, GLA_DK), BF16).at[:GLA_GATE_RANK].set(gla_gate_w2[0].astype(BF16))
    y_b = _gla(proj, glr, w2p, gla_gate_b, gla_norm_g)

    merged = _merge(y_a, y_b, w_branch_a[0].astype(BF16), w_branch_b[0].astype(BF16), proj)

    rw_t = router_w[0].T
    rw_hi = rw_t.astype(BF16)
    rw_lo = (rw_t - rw_hi.astype(F32)).astype(BF16)
    x1, h_rows, idx, top_w, rank, cnt = _outproj(
        merged, x2d, mod, w_out[0].astype(BF16), norm2_g, rw_hi, rw_lo, router_b[0].reshape(N_EXPERTS, 1))

    dest, padded_end, padded, se, srow, snsub = _routing_tables(idx, rank, cnt[:, 0])
    xs = _dispatch(dest, padded_end, padded, h_rows)
    y = _experts(se, srow, snsub, xs, exp_w_gate[0], exp_b_gate[0], exp_w_up[0], exp_b_up[0],
                 exp_w_down[0], exp_b_down[0])
    out = _combine(dest, y, x1, top_w.T, mod, final_g.reshape(1, D_MODEL))
    return out.reshape(BATCH, SEQ, D_MODEL)
```

```python
import functools

import jax
import jax.numpy as jnp
from jax import lax
from jax.experimental import pallas as pl
from jax.experimental.pallas import tpu as pltpu

F32 = jnp.float32
BF16 = jnp.bfloat16
I32 = jnp.int32

D_MODEL = 2048
BATCH = 4
SEQ = 2048
TOKENS = BATCH * SEQ
CHUNK = 64
SPATIAL_BLOCK = 128
A_GROUPS = 8
A_GROUP_DIM = D_MODEL // A_GROUPS
GLA_HEADS = 4
GLA_DK = D_MODEL // 2
GLA_HEAD_K = GLA_DK // GLA_HEADS
GLA_HEAD_V = D_MODEL // GLA_HEADS
GLA_GATE_RANK = 16
GLA_GATE_TAU = 16.0
N_EXPERTS = 32
TOP_K = 4
D_FF = D_MODEL
SWIGLU_LIMIT = 7.0
SWIGLU_ALPHA = 1.702
NORM_EPS = 1e-6

LANES = 128
VMEM_LIMIT = 56 * 1024 * 1024

COL_U, COL_V, COL_Q, COL_K, COL_VV, COL_R = 0, 2048, 4096, 5120, 6144, 8192
COL_GA, COL_GB = 10240, 12288
PROJ_W = 14336
GLR_SRC = 10240

ROW_TILE = 256
SUPER_TILES = 6
SUPER_ROWS = ROW_TILE * SUPER_TILES
N_ROWS = TOKENS * TOP_K
N_ROW_TILES = (N_ROWS + N_EXPERTS * (ROW_TILE - 1)) // ROW_TILE
PADDED_ROWS = N_ROW_TILES * ROW_TILE
MAX_SUPER = -(-(N_ROW_TILES + N_EXPERTS * (SUPER_TILES - 1)) // SUPER_TILES)
FF_TILE = 256
N_FF = D_FF // FF_TILE
ROW_DMA_UNROLL = 8


def _cparams(*sem):
    return pltpu.CompilerParams(dimension_semantics=sem, vmem_limit_bytes=VMEM_LIMIT)


def _rms(x):
    return x * lax.rsqrt(jnp.mean(x * x, axis=-1, keepdims=True) + NORM_EPS)


def _ada_body(c_ref, w_ref, b_ref, o_ref):
    c = c_ref[...]
    cond = c * jax.nn.sigmoid(c)
    o_ref[...] = jnp.dot(cond.astype(BF16), w_ref[...].astype(BF16),
                         preferred_element_type=F32) + b_ref[...]


def _ada(c, ada_w, ada_b):
    tn = 1024
    cp = jnp.zeros((8, D_MODEL), F32).at[:BATCH].set(c)
    out = pl.pallas_call(
        _ada_body,
        grid=(6 * D_MODEL // tn,),
        in_specs=[pl.BlockSpec((8, D_MODEL), lambda j: (0, 0)),
                  pl.BlockSpec((D_MODEL, tn), lambda j: (0, j)),
                  pl.BlockSpec((1, tn), lambda j: (0, j))],
        out_specs=pl.BlockSpec((8, tn), lambda j: (0, j)),
        out_shape=jax.ShapeDtypeStruct((8, 6 * D_MODEL), F32),
        compiler_params=_cparams("arbitrary"),
        name="ada",
    )(cp, ada_w, ada_b.reshape(1, 6 * D_MODEL))
    return out[:BATCH].reshape(BATCH, 6, D_MODEL)


def _norm1_body(x_ref, mod_ref, g_ref, wglr_ref, h_ref, glr_ref):
    h = _rms(x_ref[...]) * g_ref[...] * (1.0 + mod_ref[1:2, :]) + mod_ref[0:1, :]
    hb = h.astype(BF16)
    h_ref[...] = hb
    glr_ref[...] = jnp.dot(hb, wglr_ref[...], preferred_element_type=F32)


def _norm1(x2d, mod, norm1_g, w_glr):
    tm = 512
    per_batch = SEQ // tm
    return pl.pallas_call(
        _norm1_body,
        grid=(TOKENS // tm,),
        in_specs=[pl.BlockSpec((tm, D_MODEL), lambda i: (i, 0)),
                  pl.BlockSpec((None, 6, D_MODEL), lambda i: (i // per_batch, 0, 0)),
                  pl.BlockSpec((1, D_MODEL), lambda i: (0, 0)),
                  pl.BlockSpec((D_MODEL, LANES), lambda i: (0, 0))],
        out_specs=[pl.BlockSpec((tm, D_MODEL), lambda i: (i, 0)),
                   pl.BlockSpec((tm, LANES), lambda i: (i, 0))],
        out_shape=[jax.ShapeDtypeStruct((TOKENS, D_MODEL), BF16),
                   jax.ShapeDtypeStruct((TOKENS, LANES), F32)],
        compiler_params=_cparams("arbitrary"),
        name="norm1",
    )(x2d, mod, norm1_g, w_glr)


INPROJ_TN = 512
INPROJ_MAIN_TILES = COL_GA // INPROJ_TN
MXU_N = 256


def _inproj_body(h_ref, w_ref, wgate_ref, proj_ref, wb_ref):
    j = pl.program_id(0)
    first_rows = pl.program_id(1) == 0

    @pl.when(first_rows & (j < INPROJ_MAIN_TILES))
    def _():
        wb_ref[...] = w_ref[...].astype(BF16)

    @pl.when(first_rows & (j >= INPROJ_MAIN_TILES))
    def _():
        wb_ref[...] = wgate_ref[...].astype(BF16)

    def run(act):
        for n in range(INPROJ_TN // MXU_N):
            cols = slice(n * MXU_N, (n + 1) * MXU_N)
            acc = jnp.dot(h_ref[...], wb_ref[:, cols], preferred_element_type=F32)
            proj_ref[:, cols] = act(acc).astype(BF16)

    col = j * INPROJ_TN

    @pl.when(col < COL_Q)
    def _():
        run(jax.nn.gelu)

    @pl.when((col >= COL_Q) & (col < COL_R))
    def _():
        run(lambda a: a)

    @pl.when((col >= COL_R) & (col < COL_GA))
    def _():
        run(lambda a: a * jax.nn.sigmoid(a))

    @pl.when(col >= COL_GA)
    def _():
        run(jax.nn.sigmoid)


def _inproj(h, w_in0, w_gates):
    tm, tn = 1024, INPROJ_TN
    return pl.pallas_call(
        _inproj_body,
        grid=(PROJ_W // tn, TOKENS // tm),
        in_specs=[pl.BlockSpec((tm, D_MODEL), lambda j, i: (i, 0)),
                  pl.BlockSpec((D_MODEL, tn), lambda j, i: (0, jnp.minimum(j, INPROJ_MAIN_TILES - 1))),
                  pl.BlockSpec((D_MODEL, tn), lambda j, i: (0, jnp.maximum(j - INPROJ_MAIN_TILES, 0)))],
        out_specs=pl.BlockSpec((tm, tn), lambda j, i: (i, j)),
        out_shape=jax.ShapeDtypeStruct((TOKENS, PROJ_W), BF16),
        scratch_shapes=[pltpu.VMEM((D_MODEL, tn), BF16)],
        compiler_params=_cparams("arbitrary", "arbitrary"),
        name="inproj",
    )(h, w_in0, w_gates)


def _sgu_body(u_ref, v_ref, lg_ref, lb_ref, ws_ref, bs_ref, o_ref):
    v = v_ref[...].astype(F32)
    mu = jnp.mean(v, axis=-1, keepdims=True)
    xc = v - mu
    var = jnp.mean(xc * xc, axis=-1, keepdims=True)
    vn = (xc * lax.rsqrt(var + NORM_EPS) * lg_ref[...] + lb_ref[...]).astype(BF16)
    t_chunk = lax.broadcasted_iota(I32, (SPATIAL_BLOCK, SPATIAL_BLOCK), 0) // CHUNK
    s_chunk = lax.broadcasted_iota(I32, (SPATIAL_BLOCK, SPATIAL_BLOCK), 1) // CHUNK
    mask = s_chunk <= t_chunk
    for g in range(A_GROUPS):
        cols = slice(g * A_GROUP_DIM, (g + 1) * A_GROUP_DIM)
        w = jnp.where(mask, ws_ref[g], 0.0).astype(BF16)
        mixed = jnp.dot(w, vn[:, cols], preferred_element_type=F32) + bs_ref[:, g:g + 1]
        o_ref[:, cols] = (u_ref[:, cols].astype(F32) * mixed).astype(BF16)


def _sgu(proj, ln_g, ln_b, w_s, b_s_t):
    nblk = TOKENS // SPATIAL_BLOCK
    wb = D_MODEL
    return pl.pallas_call(
        _sgu_body,
        grid=(nblk,),
        in_specs=[pl.BlockSpec((SPATIAL_BLOCK, wb), lambda i: (i, COL_U // wb)),
                  pl.BlockSpec((SPATIAL_BLOCK, wb), lambda i: (i, COL_V // wb)),
                  pl.BlockSpec((1, wb), lambda i: (0, 0)),
                  pl.BlockSpec((1, wb), lambda i: (0, 0)),
                  pl.BlockSpec((A_GROUPS, SPATIAL_BLOCK, SPATIAL_BLOCK), lambda i: (0, 0, 0)),
                  pl.BlockSpec((SPATIAL_BLOCK, A_GROUPS), lambda i: (0, 0))],
        out_specs=pl.BlockSpec((SPATIAL_BLOCK, wb), lambda i: (i, 0)),
        out_shape=jax.ShapeDtypeStruct((TOKENS, wb), BF16),
        compiler_params=_cparams("arbitrary"),
        name="sgu",
    )(proj, proj, ln_g, ln_b, w_s, b_s_t)


def _dot_nt(a, b):
    return lax.dot_general(a, b, (((1,), (1,)), ((), ())), preferred_element_type=F32)


def _dot_tn(a, b):
    return lax.dot_general(a, b, (((0,), (0,)), ((), ())), preferred_element_type=F32)


def _gla_body(q_ref, k_ref, v_ref, r_ref, glr_ref, w2_ref, gb_ref, ng_ref, o_ref, st_ref, la_ref, *, rows_blk):
    @pl.when(pl.program_id(1) == 0)
    def _():
        st_ref[...] = jnp.zeros_like(st_ref)

    z = jnp.dot(glr_ref[...].astype(BF16), w2_ref[...], preferred_element_type=F32) + gb_ref[...]
    la_ref[...] = jax.nn.log_sigmoid(z) / GLA_GATE_TAU

    r_i = lax.broadcasted_iota(I32, (CHUNK, CHUNK), 0)
    c_i = lax.broadcasted_iota(I32, (CHUNK, CHUNK), 1)
    causal = c_i <= r_i
    tril = causal.astype(BF16)
    scale = GLA_HEAD_K ** -0.5

    def chunk(c, carry):
        rows = pl.ds(pl.multiple_of(c * CHUNK, CHUNK), CHUNK)
        for h in range(GLA_HEADS):
            kc = slice(h * GLA_HEAD_K, (h + 1) * GLA_HEAD_K)
            vc = slice(h * GLA_HEAD_V, (h + 1) * GLA_HEAD_V)
            la = la_ref[rows, kc]
            hi = la.astype(BF16)
            r1 = la - hi.astype(F32)
            mid = r1.astype(BF16)
            lo = (r1 - mid.astype(F32)).astype(BF16)
            b = (jnp.dot(tril, hi, preferred_element_type=F32)
                 + jnp.dot(tril, mid, preferred_element_type=F32)
                 + jnp.dot(tril, lo, preferred_element_type=F32))
            b_last = b[CHUNK - 1:CHUNK, :]
            q = q_ref[rows, kc].astype(F32) * scale
            k = k_ref[rows, kc].astype(F32)
            v = v_ref[rows, vc]
            q_dec = (q * jnp.exp(b)).astype(BF16)
            k_intra = (k * jnp.exp(-b)).astype(BF16)
            k_state = (k * jnp.exp(b_last - b)).astype(BF16)
            att = jnp.where(causal, _dot_nt(q_dec, k_intra), 0.0).astype(BF16)
            st = st_ref[h]
            o = jnp.dot(att, v, preferred_element_type=F32) + _dot_nt(q_dec, st.astype(BF16))
            st_ref[h] = st * jnp.exp(b_last) + _dot_tn(v, k_state)
            on = _rms(o) * ng_ref[...]
            o_ref[rows, vc] = (on * r_ref[rows, vc].astype(F32)).astype(BF16)
        return carry

    lax.fori_loop(0, rows_blk // CHUNK, chunk, 0)


def _gla(proj, glr, w2p, gate_b, norm_g):
    rows_blk = 512
    nblk = SEQ // rows_blk
    dk, dv = GLA_DK, D_MODEL

    def row(b, n):
        return b * nblk + n

    return pl.pallas_call(
        functools.partial(_gla_body, rows_blk=rows_blk),
        grid=(BATCH, nblk),
        in_specs=[pl.BlockSpec((rows_blk, dk), lambda b, n: (row(b, n), COL_Q // dk)),
                  pl.BlockSpec((rows_blk, dk), lambda b, n: (row(b, n), COL_K // dk)),
                  pl.BlockSpec((rows_blk, dv), lambda b, n: (row(b, n), COL_VV // dv)),
                  pl.BlockSpec((rows_blk, dv), lambda b, n: (row(b, n), COL_R // dv)),
                  pl.BlockSpec((rows_blk, LANES), lambda b, n: (row(b, n), 0)),
                  pl.BlockSpec((LANES, dk), lambda b, n: (0, 0)),
                  pl.BlockSpec((1, dk), lambda b, n: (0, 0)),
                  pl.BlockSpec((1, GLA_HEAD_V), lambda b, n: (0, 0))],
        out_specs=pl.BlockSpec((rows_blk, dv), lambda b, n: (row(b, n), 0)),
        out_shape=jax.ShapeDtypeStruct((TOKENS, D_MODEL), BF16),
        scratch_shapes=[pltpu.VMEM((GLA_HEADS, GLA_HEAD_V, GLA_HEAD_K), F32),
                        pltpu.VMEM((rows_blk, dk), F32)],
        compiler_params=_cparams("arbitrary", "arbitrary"),
        name="gla",
    )(proj, proj, proj, proj, glr, w2p, gate_b, norm_g)


def _merge_body(ya_ref, yb_ref, wa_ref, wb_ref, ga_ref, gb_ref, o_ref):
    a = jnp.dot(ya_ref[...], wa_ref[...], preferred_element_type=F32)
    b = jnp.dot(yb_ref[...], wb_ref[...], preferred_element_type=F32)
    o_ref[...] = (ga_ref[...].astype(F32) * a + gb_ref[...].astype(F32) * b).astype(BF16)


def _merge(y_a, y_b, wa, wb, proj):
    tm, tn = 512, 1024
    return pl.pallas_call(
        _merge_body,
        grid=(TOKENS // tm, D_MODEL // tn),
        in_specs=[pl.BlockSpec((tm, D_MODEL), lambda i, j: (i, 0)),
                  pl.BlockSpec((tm, D_MODEL), lambda i, j: (i, 0)),
                  pl.BlockSpec((D_MODEL, tn), lambda i, j: (0, j)),
                  pl.BlockSpec((D_MODEL, tn), lambda i, j: (0, j)),
                  pl.BlockSpec((tm, tn), lambda i, j: (i, COL_GA // tn + j)),
                  pl.BlockSpec((tm, tn), lambda i, j: (i, COL_GB // tn + j))],
        out_specs=pl.BlockSpec((tm, tn), lambda i, j: (i, j)),
        out_shape=jax.ShapeDtypeStruct((TOKENS, D_MODEL), BF16),
        compiler_params=_cparams("arbitrary", "arbitrary"),
        name="merge",
    )(y_a, y_b, wa, wb, proj, proj)


def _outproj_body(m_ref, x_ref, mod_ref, wo_ref, g2_ref, rwh_ref, rwl_ref, rb_ref,
                  x1_ref, hp_ref, idx_ref, w_ref, rank_ref, cnt_ref, run_ref, *, tm):
    @pl.when(pl.program_id(0) == 0)
    def _():
        run_ref[...] = jnp.zeros_like(run_ref)

    y = jnp.dot(m_ref[...], wo_ref[...], preferred_element_type=F32)
    x1 = x_ref[...] + mod_ref[2:3, :] * y
    x1_ref[...] = x1
    h2 = _rms(x1) * g2_ref[...] * (1.0 + mod_ref[4:5, :]) + mod_ref[3:4, :]
    hp_ref[...] = h2
    hb = h2.astype(BF16)
    hf = hb.astype(F32)

    h_lo = (h2 - hf).astype(BF16)
    logits = (_dot_nt(rwh_ref[...], hb) + _dot_nt(rwh_ref[...], h_lo) + _dot_nt(rwl_ref[...], hb)
              + rb_ref[...])
    e_i = lax.broadcasted_iota(I32, (N_EXPERTS, tm), 0).astype(F32)
    vals, idxs = [], []
    l = logits
    for _ in range(TOP_K):
        m = jnp.max(l, axis=0, keepdims=True)
        i = jnp.min(jnp.where(l == m, e_i, float(N_EXPERTS)), axis=0, keepdims=True)
        vals.append(m)
        idxs.append(i)
        l = jnp.where(e_i == i, -jnp.inf, l)
    ex = [jnp.exp(v - vals[0]) for v in vals]
    den = ex[0] + ex[1] + ex[2] + ex[3]
    before = (lax.broadcasted_iota(I32, (tm, tm), 0) < lax.broadcasted_iota(I32, (tm, tm), 1)).astype(BF16)
    run = run_ref[...]
    for k in range(TOP_K):
        onehot = e_i == idxs[k]
        pref = jnp.dot(onehot.astype(BF16), before, preferred_element_type=F32) + run[:, 0:1]
        rank_ref[k:k + 1, :] = jnp.sum(jnp.where(onehot, pref, 0.0), axis=0, keepdims=True).astype(I32)
        run = run + jnp.sum(onehot.astype(F32), axis=1, keepdims=True)
        idx_ref[k:k + 1, :] = idxs[k].astype(I32)
        w_ref[k:k + 1, :] = ex[k] / den
    run_ref[...] = run
    cnt_ref[...] = run.astype(I32)


def _outproj(merged, x2d, mod, w_out, norm2_g, rw_hi, rw_lo, router_b):
    tm = 512
    per_batch = SEQ // tm
    body = functools.partial(_outproj_body, tm=tm)
    return pl.pallas_call(
        body,
        grid=(TOKENS // tm,),
        in_specs=[pl.BlockSpec((tm, D_MODEL), lambda i: (i, 0)),
                  pl.BlockSpec((tm, D_MODEL), lambda i: (i, 0)),
                  pl.BlockSpec((None, 6, D_MODEL), lambda i: (i // per_batch, 0, 0)),
                  pl.BlockSpec((D_MODEL, D_MODEL), lambda i: (0, 0)),
                  pl.BlockSpec((1, D_MODEL), lambda i: (0, 0)),
                  pl.BlockSpec((N_EXPERTS, D_MODEL), lambda i: (0, 0)),
                  pl.BlockSpec((N_EXPERTS, D_MODEL), lambda i: (0, 0)),
                  pl.BlockSpec((N_EXPERTS, 1), lambda i: (0, 0))],
        out_specs=[pl.BlockSpec((tm, D_MODEL), lambda i: (i, 0)),
                   pl.BlockSpec((tm, D_MODEL), lambda i: (i, 0)),
                   pl.BlockSpec((TOP_K, tm), lambda i: (0, i)),
                   pl.BlockSpec((TOP_K, tm), lambda i: (0, i)),
                   pl.BlockSpec((TOP_K, tm), lambda i: (0, i)),
                   pl.BlockSpec((N_EXPERTS, LANES), lambda i: (0, 0))],
        out_shape=[jax.ShapeDtypeStruct((TOKENS, D_MODEL), F32),
                   jax.ShapeDtypeStruct((TOKENS, D_MODEL), F32),
                   jax.ShapeDtypeStruct((TOP_K, TOKENS), I32),
                   jax.ShapeDtypeStruct((TOP_K, TOKENS), F32),
                   jax.ShapeDtypeStruct((TOP_K, TOKENS), I32),
                   jax.ShapeDtypeStruct((N_EXPERTS, LANES), I32)],
        scratch_shapes=[pltpu.VMEM((N_EXPERTS, LANES), F32)],
        compiler_params=_cparams("arbitrary"),
        name="outproj",
    )(merged, x2d, mod, w_out, norm2_g, rw_hi, rw_lo, router_b)


def _dispatch_body(dest_ref, pend_ref, padded_ref, h_ref, xs_hbm, zero_ref, zsem, sem, *, tm):
    i = pl.program_id(0)

    @pl.when(i == 0)
    def _():
        _dispatch_clear(pend_ref, padded_ref, xs_hbm, zero_ref, zsem)

    for k in range(TOP_K):
        def one(j, carry, k=k):
            d = dest_ref[k * TOKENS + i * tm + j]
            pltpu.make_async_copy(h_ref.at[pl.ds(j, 1)], xs_hbm.at[pl.ds(d, 1)], sem).start()
            return carry
        lax.fori_loop(0, tm, one, 0, unroll=ROW_DMA_UNROLL)
    for k in range(TOP_K):
        pltpu.make_async_copy(h_ref, xs_hbm.at[pl.ds(0, tm)], sem).wait()


def _dispatch_clear(pend_ref, padded_ref, xs_hbm, zero_ref, zsem):
    zero_ref[...] = jnp.zeros_like(zero_ref)

    def zero_copy(e):
        return pltpu.make_async_copy(
            zero_ref, xs_hbm.at[pl.ds(pl.multiple_of(pend_ref[e] - ROW_TILE, ROW_TILE), ROW_TILE)], zsem)

    def zstart(e, c):
        @pl.when(padded_ref[e] > 0)
        def _():
            zero_copy(e).start()
        return c

    def zwait(e, c):
        @pl.when(padded_ref[e] > 0)
        def _():
            zero_copy(e).wait()
        return c

    def tail_copy(t):
        return pltpu.make_async_copy(
            zero_ref, xs_hbm.at[pl.ds(pl.multiple_of(t * ROW_TILE, ROW_TILE), ROW_TILE)], zsem)

    def tstart(t, c):
        tail_copy(t).start()
        return c

    def twait(t, c):
        tail_copy(t).wait()
        return c

    first_tail = pend_ref[N_EXPERTS - 1] // ROW_TILE
    lax.fori_loop(0, N_EXPERTS, zstart, 0)
    lax.fori_loop(first_tail, N_ROW_TILES, tstart, 0)
    lax.fori_loop(0, N_EXPERTS, zwait, 0)
    lax.fori_loop(first_tail, N_ROW_TILES, twait, 0)


def _dispatch(dest_flat, padded_end, padded, h_rows):
    tm = 256
    return pl.pallas_call(
        functools.partial(_dispatch_body, tm=tm),
        grid_spec=pltpu.PrefetchScalarGridSpec(
            num_scalar_prefetch=3,
            grid=(TOKENS // tm,),
            in_specs=[pl.BlockSpec((tm, D_MODEL), lambda i, d, pe, pd: (i, 0))],
            out_specs=pl.BlockSpec(memory_space=pl.ANY),
            scratch_shapes=[pltpu.VMEM((ROW_TILE, D_MODEL), F32),
                            pltpu.SemaphoreType.DMA(()),
                            pltpu.SemaphoreType.DMA(())]),
        out_shape=jax.ShapeDtypeStruct((PADDED_ROWS, D_MODEL), F32),
        compiler_params=_cparams("arbitrary"),
        name="dispatch",
    )(dest_flat, padded_end, padded, h_rows)


def _expert_body(se_ref, srow_ref, snsub_ref, xs_hbm, wg_ref, wu_ref, wd_ref, bg_ref, bu_ref, bd_ref,
                 y_hbm, xstage, xb, acc, wgb, wub, wdb, sem_in, sem_out):
    s = pl.program_id(0)
    f = pl.program_id(1)
    nsub = snsub_ref[s]
    row0 = srow_ref[s]

    last_f = f == N_FF - 1
    n_pair = nsub // 2
    odd = nsub - 2 * n_pair
    pair_rows = 2 * ROW_TILE

    def in_copy(t, slot):
        src = xs_hbm.at[pl.ds(pl.multiple_of(row0 + t * ROW_TILE, ROW_TILE), ROW_TILE)]
        return pltpu.make_async_copy(src, xstage.at[slot], sem_in.at[slot])

    def out_copy(off, m):
        dst = y_hbm.at[pl.ds(pl.multiple_of(row0 + off, ROW_TILE), m)]
        return pltpu.make_async_copy(acc.at[pl.ds(off, m)], dst, sem_out)

    def compute(off, m):
        rows = pl.ds(off, m)
        x = xb[rows, :]
        g = jnp.minimum(jnp.dot(x, wgb[...], preferred_element_type=F32) + bg_ref[...], SWIGLU_LIMIT)
        u = jnp.clip(jnp.dot(x, wub[...], preferred_element_type=F32) + bu_ref[...],
                     -SWIGLU_LIMIT, SWIGLU_LIMIT)
        a = ((u + 1.0) * (g * jax.nn.sigmoid(SWIGLU_ALPHA * g))).astype(BF16)
        acc[rows, :] += jnp.dot(a, wdb[...], preferred_element_type=F32)

        @pl.when(last_f)
        def _():
            out_copy(off, m).start()

    odd_off = pl.multiple_of((nsub - 1) * ROW_TILE, ROW_TILE)

    @pl.when(nsub > 0)
    def _():
        @pl.when(f == 0)
        def _():
            in_copy(0, 0).start()

            def load(t, c):
                slot = t % 2

                @pl.when(t + 1 < nsub)
                def _():
                    in_copy(t + 1, 1 - slot).start()
                in_copy(t, slot).wait()
                rows = pl.ds(pl.multiple_of(t * ROW_TILE, ROW_TILE), ROW_TILE)
                xb[rows, :] = xstage[slot].astype(BF16)
                acc[rows, :] = jnp.broadcast_to(bd_ref[...], (ROW_TILE, D_MODEL))
                return c
            lax.fori_loop(0, nsub, load, 0)

        wgb[...] = wg_ref[...].astype(BF16)
        wub[...] = wu_ref[...].astype(BF16)
        wdb[...] = wd_ref[...].astype(BF16)

        def pair(p, c):
            compute(pl.multiple_of(p * pair_rows, pair_rows), pair_rows)
            return c
        lax.fori_loop(0, n_pair, pair, 0)

        @pl.when(odd == 1)
        def _():
            compute(odd_off, ROW_TILE)

        @pl.when(last_f)
        def _():
            def drain(p, c):
                out_copy(pl.multiple_of(p * pair_rows, pair_rows), pair_rows).wait()
                return c
            lax.fori_loop(0, n_pair, drain, 0)

            @pl.when(odd == 1)
            def _():
                out_copy(odd_off, ROW_TILE).wait()

    @pl.when((s == MAX_SUPER - 1) & (f == N_FF - 1))
    def _():
        acc[0:ROW_TILE, :] = jnp.zeros((ROW_TILE, D_MODEL), F32)

        def tail_copy(t):
            dst = y_hbm.at[pl.ds(pl.multiple_of(t * ROW_TILE, ROW_TILE), ROW_TILE)]
            return pltpu.make_async_copy(acc.at[0:ROW_TILE], dst, sem_out)

        def tstart(t, c):
            tail_copy(t).start()
            return c

        def twait(t, c):
            tail_copy(t).wait()
            return c

        first_tail = snsub_ref[MAX_SUPER]
        lax.fori_loop(first_tail, N_ROW_TILES, tstart, 0)
        lax.fori_loop(first_tail, N_ROW_TILES, twait, 0)


def _experts(se, srow, snsub, xs, w_gate, b_gate, w_up, b_up, w_down, b_down):
    def f_eff(s, f, nsub):
        return jnp.where(nsub[s] > 0, f, N_FF - 1)

    return pl.pallas_call(
        _expert_body,
        grid_spec=pltpu.PrefetchScalarGridSpec(
            num_scalar_prefetch=3,
            grid=(MAX_SUPER, N_FF),
            in_specs=[pl.BlockSpec(memory_space=pl.ANY),
                      pl.BlockSpec((None, D_MODEL, FF_TILE), lambda s, f, se, sr, sn: (se[s], 0, f_eff(s, f, sn))),
                      pl.BlockSpec((None, D_MODEL, FF_TILE), lambda s, f, se, sr, sn: (se[s], 0, f_eff(s, f, sn))),
                      pl.BlockSpec((None, FF_TILE, D_MODEL), lambda s, f, se, sr, sn: (se[s], f_eff(s, f, sn), 0)),
                      pl.BlockSpec((None, 1, FF_TILE), lambda s, f, se, sr, sn: (se[s], 0, f_eff(s, f, sn))),
                      pl.BlockSpec((None, 1, FF_TILE), lambda s, f, se, sr, sn: (se[s], 0, f_eff(s, f, sn))),
                      pl.BlockSpec((None, 1, D_MODEL), lambda s, f, se, sr, sn: (se[s], 0, 0))],
            out_specs=pl.BlockSpec(memory_space=pl.ANY),
            scratch_shapes=[pltpu.VMEM((2, ROW_TILE, D_MODEL), F32),
                            pltpu.VMEM((SUPER_ROWS, D_MODEL), BF16),
                            pltpu.VMEM((SUPER_ROWS, D_MODEL), F32),
                            pltpu.VMEM((D_MODEL, FF_TILE), BF16),
                            pltpu.VMEM((D_MODEL, FF_TILE), BF16),
                            pltpu.VMEM((FF_TILE, D_MODEL), BF16),
                            pltpu.SemaphoreType.DMA((2,)),
                            pltpu.SemaphoreType.DMA(())]),
        out_shape=jax.ShapeDtypeStruct((PADDED_ROWS, D_MODEL), F32),
        compiler_params=_cparams("arbitrary", "arbitrary"),
        name="experts",
    )(se, srow, snsub, xs, w_gate, w_up, w_down,
      b_gate.reshape(N_EXPERTS, 1, D_FF), b_up.reshape(N_EXPERTS, 1, D_FF),
      b_down.reshape(N_EXPERTS, 1, D_MODEL))


def _combine_body(dest_ref, y_hbm, x1_ref, w_ref, mod_ref, fg_ref, o_ref, buf, sem, *, tm):
    i = pl.program_id(0)
    for k in range(TOP_K):
        def one(j, carry, k=k):
            d = dest_ref[k * TOKENS + i * tm + j]
            pltpu.make_async_copy(y_hbm.at[pl.ds(d, 1)], buf.at[k, pl.ds(j, 1)], sem).start()
            return carry
        lax.fori_loop(0, tm, one, 0, unroll=ROW_DMA_UNROLL)
    for k in range(TOP_K):
        pltpu.make_async_copy(y_hbm.at[pl.ds(0, tm)], buf.at[k], sem).wait()
    moe = buf[0] * w_ref[:, 0:1]
    for k in range(1, TOP_K):
        moe = moe + buf[k] * w_ref[:, k:k + 1]
    x2 = x1_ref[...] + mod_ref[5:6, :] * moe
    o_ref[...] = _rms(x2) * fg_ref[...]


def _combine(dest_flat, y, x1, w_t, mod, final_g):
    tm = 256
    per_batch = SEQ // tm
    return pl.pallas_call(
        functools.partial(_combine_body, tm=tm),
        grid_spec=pltpu.PrefetchScalarGridSpec(
            num_scalar_prefetch=1,
            grid=(TOKENS // tm,),
            in_specs=[pl.BlockSpec(memory_space=pl.ANY),
                      pl.BlockSpec((tm, D_MODEL), lambda i, d: (i, 0)),
                      pl.BlockSpec((tm, TOP_K), lambda i, d: (i, 0)),
                      pl.BlockSpec((None, 6, D_MODEL), lambda i, d: (i // per_batch, 0, 0)),
                      pl.BlockSpec((1, D_MODEL), lambda i, d: (0, 0))],
            out_specs=pl.BlockSpec((tm, D_MODEL), lambda i, d: (i, 0)),
            scratch_shapes=[pltpu.VMEM((TOP_K, tm, D_MODEL), F32),
                            pltpu.SemaphoreType.DMA(())]),
        out_shape=jax.ShapeDtypeStruct((TOKENS, D_MODEL), F32),
        compiler_params=_cparams("arbitrary"),
        name="combine",
    )(dest_flat, y, x1, w_t, mod, final_g)


def _routing_tables(idx, rank, counts):
    padded = (counts + ROW_TILE - 1) // ROW_TILE * ROW_TILE
    padded_end = jnp.cumsum(padded)
    padded_start = padded_end - padded
    e_i = jnp.arange(N_EXPERTS, dtype=I32)[:, None, None]
    start_of = jnp.sum(jnp.where(idx[None] == e_i, padded_start[:, None, None], 0), axis=0)
    dest = (start_of + rank).reshape(N_ROWS).astype(I32)

    tiles = padded // ROW_TILE
    n_super = (tiles + SUPER_TILES - 1) // SUPER_TILES
    super_end = jnp.cumsum(n_super)
    super_start = super_end - n_super
    s_i = jnp.arange(MAX_SUPER, dtype=I32)
    total = super_end[-1]
    valid = s_i < total
    e_of = jnp.minimum(jnp.sum((s_i[:, None] >= super_end[None, :]).astype(I32), axis=1), N_EXPERTS - 1)
    last_e = jnp.minimum(jnp.sum((total - 1 >= super_end).astype(I32)), N_EXPERTS - 1)
    local = s_i - super_start[e_of]
    srow = jnp.where(valid, padded_start[e_of] + local * SUPER_ROWS, 0).astype(I32)
    snsub = jnp.where(valid, jnp.minimum(tiles[e_of] - local * SUPER_TILES, SUPER_TILES), 0).astype(I32)
    snsub = jnp.concatenate([snsub, (padded_end[-1:] // ROW_TILE).astype(I32)])
    se = jnp.where(valid, e_of, last_e).astype(I32)
    return dest, padded_end.astype(I32), padded.astype(I32), se, srow, snsub


def kernel(x, c, ada_w, ada_b, norm1_g, w_in, gla_gate_w2, gla_gate_b, sgu_ln_g, sgu_ln_b, sgu_w, sgu_b,
           gla_norm_g, w_branch_a, w_branch_b, w_out, norm2_g, router_w, router_b, exp_w_gate, exp_b_gate,
           exp_w_up, exp_b_up, exp_w_down, exp_b_down, final_g):
    x2d = x.reshape(TOKENS, D_MODEL)
    mod = _ada(c, ada_w[0], ada_b[0])

    w_in0 = w_in[0]
    w_glr = jnp.zeros((D_MODEL, LANES), BF16).at[:, :GLA_GATE_RANK].set(
        w_in0[:, GLR_SRC:GLR_SRC + GLA_GATE_RANK].astype(BF16))
    h, glr = _norm1(x2d, mod, norm1_g, w_glr)
    proj = _inproj(h, w_in0, w_in0[:, GLR_SRC + GLA_GATE_RANK:])

    y_a = _sgu(proj, sgu_ln_g, sgu_ln_b, sgu_w[0], sgu_b[0].T)

    w2p = jnp.zeros((LANES, GLA_DK), BF16).at[:GLA_GATE_RANK].set(gla_gate_w2[0].astype(BF16))
    y_b = _gla(proj, glr, w2p, gla_gate_b, gla_norm_g)

    merged = _merge(y_a, y_b, w_branch_a[0].astype(BF16), w_branch_b[0].astype(BF16), proj)

    rw_t = router_w[0].T
    rw_hi = rw_t.astype(BF16)
    rw_lo = (rw_t - rw_hi.astype(F32)).astype(BF16)
    x1, h_rows, idx, top_w, rank, cnt = _outproj(
        merged, x2d, mod, w_out[0].astype(BF16), norm2_g, rw_hi, rw_lo, router_b[0].reshape(N_EXPERTS, 1))

    dest, padded_end, padded, se, srow, snsub = _routing_tables(idx, rank, cnt[:, 0])
    xs = _dispatch(dest, padded_end, padded, h_rows)
    y = _experts(se, srow, snsub, xs, exp_w_gate[0], exp_b_gate[0], exp_w_up[0], exp_b_up[0],
                 exp_w_down[0], exp_b_down[0])
    out = _combine(dest, y, x1, top_w.T, mod, final_g.reshape(1, D_MODEL))
    return out.reshape(BATCH, SEQ, D_MODEL)
```

```python
import functools

import jax
import jax.numpy as jnp
from jax import lax
from jax.experimental import pallas as pl
from jax.experimental.pallas import tpu as pltpu

F32 = jnp.float32
BF16 = jnp.bfloat16
I32 = jnp.int32

D_MODEL = 2048
BATCH = 4
SEQ = 2048
TOKENS = BATCH * SEQ
CHUNK = 64
SPATIAL_BLOCK = 128
A_GROUPS = 8
A_GROUP_DIM = D_MODEL // A_GROUPS
GLA_HEADS = 4
GLA_DK = D_MODEL // 2
GLA_HEAD_K = GLA_DK // GLA_HEADS
GLA_HEAD_V = D_MODEL // GLA_HEADS
GLA_GATE_RANK = 16
GLA_GATE_TAU = 16.0
N_EXPERTS = 32
TOP_K = 4
D_FF = D_MODEL
SWIGLU_LIMIT = 7.0
SWIGLU_ALPHA = 1.702
NORM_EPS = 1e-6

LANES = 128
VMEM_LIMIT = 56 * 1024 * 1024

COL_U, COL_V, COL_Q, COL_K, COL_VV, COL_R = 0, 2048, 4096, 5120, 6144, 8192
COL_GA, COL_GB = 10240, 12288
PROJ_W = 14336
GLR_SRC = 10240

ROW_TILE = 256
SUPER_TILES = 6
SUPER_ROWS = ROW_TILE * SUPER_TILES
N_ROWS = TOKENS * TOP_K
N_ROW_TILES = (N_ROWS + N_EXPERTS * (ROW_TILE - 1)) // ROW_TILE
PADDED_ROWS = N_ROW_TILES * ROW_TILE
MAX_SUPER = -(-(N_ROW_TILES + N_EXPERTS * (SUPER_TILES - 1)) // SUPER_TILES)
FF_TILE = 256
N_FF = D_FF // FF_TILE
ROW_DMA_GROUP = 8


def _cparams(*sem):
    return pltpu.CompilerParams(dimension_semantics=sem, vmem_limit_bytes=VMEM_LIMIT)


def _rms(x):
    return x * lax.rsqrt(jnp.mean(x * x, axis=-1, keepdims=True) + NORM_EPS)


def _ada_body(c_ref, w_ref, b_ref, o_ref):
    c = c_ref[...]
    cond = c * jax.nn.sigmoid(c)
    o_ref[...] = jnp.dot(cond.astype(BF16), w_ref[...].astype(BF16),
                         preferred_element_type=F32) + b_ref[...]


def _ada(c, ada_w, ada_b):
    tn = 1024
    cp = jnp.zeros((8, D_MODEL), F32).at[:BATCH].set(c)
    out = pl.pallas_call(
        _ada_body,
        grid=(6 * D_MODEL // tn,),
        in_specs=[pl.BlockSpec((8, D_MODEL), lambda j: (0, 0)),
                  pl.BlockSpec((D_MODEL, tn), lambda j: (0, j)),
                  pl.BlockSpec((1, tn), lambda j: (0, j))],
        out_specs=pl.BlockSpec((8, tn), lambda j: (0, j)),
        out_shape=jax.ShapeDtypeStruct((8, 6 * D_MODEL), F32),
        compiler_params=_cparams("arbitrary"),
        name="ada",
    )(cp, ada_w, ada_b.reshape(1, 6 * D_MODEL))
    return out[:BATCH].reshape(BATCH, 6, D_MODEL)


def _norm1_body(x_ref, mod_ref, g_ref, wglr_ref, h_ref, glr_ref):
    h = _rms(x_ref[...]) * g_ref[...] * (1.0 + mod_ref[1:2, :]) + mod_ref[0:1, :]
    hb = h.astype(BF16)
    h_ref[...] = hb
    glr_ref[...] = _dot_nt(hb, wglr_ref[...])


def _norm1(x2d, mod, norm1_g, w_glr):
    tm = 512
    per_batch = SEQ // tm
    return pl.pallas_call(
        _norm1_body,
        grid=(TOKENS // tm,),
        in_specs=[pl.BlockSpec((tm, D_MODEL), lambda i: (i, 0)),
                  pl.BlockSpec((None, 6, D_MODEL), lambda i: (i // per_batch, 0, 0)),
                  pl.BlockSpec((1, D_MODEL), lambda i: (0, 0)),
                  pl.BlockSpec((LANES, D_MODEL), lambda i: (0, 0))],
        out_specs=[pl.BlockSpec((tm, D_MODEL), lambda i: (i, 0)),
                   pl.BlockSpec((tm, LANES), lambda i: (i, 0))],
        out_shape=[jax.ShapeDtypeStruct((TOKENS, D_MODEL), BF16),
                   jax.ShapeDtypeStruct((TOKENS, LANES), F32)],
        compiler_params=_cparams("arbitrary"),
        name="norm1",
    )(x2d, mod, norm1_g, w_glr)


INPROJ_TN = 1024
INPROJ_MAIN_TILES = COL_GA // INPROJ_TN
MXU_N = 256


def _inproj_body(h_ref, wt_ref, proj_ref, wb_ref):
    j = pl.program_id(0)

    @pl.when(pl.program_id(1) == 0)
    def _():
        wb_ref[...] = wt_ref[...].astype(BF16)

    def run(act):
        for n in range(INPROJ_TN // MXU_N):
            cols = slice(n * MXU_N, (n + 1) * MXU_N)
            acc = _dot_nt(h_ref[...], wb_ref[cols, :])
            proj_ref[:, cols] = act(acc).astype(BF16)

    col = j * INPROJ_TN

    @pl.when(col < COL_Q)
    def _():
        run(jax.nn.gelu)

    @pl.when((col >= COL_Q) & (col < COL_R))
    def _():
        run(lambda a: a)

    @pl.when((col >= COL_R) & (col < COL_GA))
    def _():
        run(lambda a: a * jax.nn.sigmoid(a))

    @pl.when(col >= COL_GA)
    def _():
        run(jax.nn.sigmoid)


def _inproj(h, w_in_t):
    tm, tn = 1024, INPROJ_TN

    def w_row(j, i):
        return (pl.multiple_of(j * tn + jnp.where(j >= INPROJ_MAIN_TILES, GLA_GATE_RANK, 0), GLA_GATE_RANK), 0)

    return pl.pallas_call(
        _inproj_body,
        grid=(PROJ_W // tn, TOKENS // tm),
        in_specs=[pl.BlockSpec((tm, D_MODEL), lambda j, i: (i, 0)),
                  pl.BlockSpec((pl.Element(tn), pl.Element(D_MODEL)), w_row)],
        out_specs=pl.BlockSpec((tm, tn), lambda j, i: (i, j)),
        out_shape=jax.ShapeDtypeStruct((TOKENS, PROJ_W), BF16),
        scratch_shapes=[pltpu.VMEM((tn, D_MODEL), BF16)],
        compiler_params=_cparams("arbitrary", "arbitrary"),
        name="inproj",
    )(h, w_in_t)


def _sgu_body(u_ref, v_ref, lg_ref, lb_ref, ws_ref, bs_ref, o_ref):
    v = v_ref[...].astype(F32)
    mu = jnp.mean(v, axis=-1, keepdims=True)
    xc = v - mu
    var = jnp.mean(xc * xc, axis=-1, keepdims=True)
    vn = (xc * lax.rsqrt(var + NORM_EPS) * lg_ref[...] + lb_ref[...]).astype(BF16)
    t_chunk = lax.broadcasted_iota(I32, (SPATIAL_BLOCK, SPATIAL_BLOCK), 0) // CHUNK
    s_chunk = lax.broadcasted_iota(I32, (SPATIAL_BLOCK, SPATIAL_BLOCK), 1) // CHUNK
    mask = s_chunk <= t_chunk
    for g in range(A_GROUPS):
        cols = slice(g * A_GROUP_DIM, (g + 1) * A_GROUP_DIM)
        w = jnp.where(mask, ws_ref[g], 0.0).astype(BF16)
        mixed = jnp.dot(w, vn[:, cols], preferred_element_type=F32) + bs_ref[:, g:g + 1]
        o_ref[:, cols] = (u_ref[:, cols].astype(F32) * mixed).astype(BF16)


def _sgu(proj, ln_g, ln_b, w_s, b_s_t):
    nblk = TOKENS // SPATIAL_BLOCK
    wb = D_MODEL
    return pl.pallas_call(
        _sgu_body,
        grid=(nblk,),
        in_specs=[pl.BlockSpec((SPATIAL_BLOCK, wb), lambda i: (i, COL_U // wb)),
                  pl.BlockSpec((SPATIAL_BLOCK, wb), lambda i: (i, COL_V // wb)),
                  pl.BlockSpec((1, wb), lambda i: (0, 0)),
                  pl.BlockSpec((1, wb), lambda i: (0, 0)),
                  pl.BlockSpec((A_GROUPS, SPATIAL_BLOCK, SPATIAL_BLOCK), lambda i: (0, 0, 0)),
                  pl.BlockSpec((SPATIAL_BLOCK, A_GROUPS), lambda i: (0, 0))],
        out_specs=pl.BlockSpec((SPATIAL_BLOCK, wb), lambda i: (i, 0)),
        out_shape=jax.ShapeDtypeStruct((TOKENS, wb), BF16),
        compiler_params=_cparams("arbitrary"),
        name="sgu",
    )(proj, proj, ln_g, ln_b, w_s, b_s_t)


def _dot_nt(a, b):
    return lax.dot_general(a, b, (((1,), (1,)), ((), ())), preferred_element_type=F32)


def _dot_tn(a, b):
    return lax.dot_general(a, b, (((0,), (0,)), ((), ())), preferred_element_type=F32)


def _gla_body(q_ref, k_ref, v_ref, r_ref, glr_ref, w2_ref, gb_ref, ng_ref, o_ref, st_ref, la_ref, *, rows_blk):
    @pl.when(pl.program_id(1) == 0)
    def _():
        st_ref[...] = jnp.zeros_like(st_ref)

    z = jnp.dot(glr_ref[...].astype(BF16), w2_ref[...], preferred_element_type=F32) + gb_ref[...]
    la_ref[...] = jax.nn.log_sigmoid(z) / GLA_GATE_TAU

    r_i = lax.broadcasted_iota(I32, (CHUNK, CHUNK), 0)
    c_i = lax.broadcasted_iota(I32, (CHUNK, CHUNK), 1)
    causal = c_i <= r_i
    tril = causal.astype(BF16)
    scale = GLA_HEAD_K ** -0.5

    def chunk(c, carry):
        rows = pl.ds(pl.multiple_of(c * CHUNK, CHUNK), CHUNK)
        for h in range(GLA_HEADS):
            kc = slice(h * GLA_HEAD_K, (h + 1) * GLA_HEAD_K)
            vc = slice(h * GLA_HEAD_V, (h + 1) * GLA_HEAD_V)
            la = la_ref[rows, kc]
            hi = la.astype(BF16)
            r1 = la - hi.astype(F32)
            mid = r1.astype(BF16)
            lo = (r1 - mid.astype(F32)).astype(BF16)
            b = (jnp.dot(tril, hi, preferred_element_type=F32)
                 + jnp.dot(tril, mid, preferred_element_type=F32)
                 + jnp.dot(tril, lo, preferred_element_type=F32))
            b_last = b[CHUNK - 1:CHUNK, :]
            q = q_ref[rows, kc].astype(F32) * scale
            k = k_ref[rows, kc].astype(F32)
            v = v_ref[rows, vc]
            q_dec = (q * jnp.exp(b)).astype(BF16)
            k_intra = (k * jnp.exp(-b)).astype(BF16)
            k_state = (k * jnp.exp(b_last - b)).astype(BF16)
            att = jnp.where(causal, _dot_nt(q_dec, k_intra), 0.0).astype(BF16)
            st = st_ref[h]
            o = jnp.dot(att, v, preferred_element_type=F32) + _dot_nt(q_dec, st.astype(BF16))
            st_ref[h] = st * jnp.exp(b_last) + _dot_tn(v, k_state)
            on = _rms(o) * ng_ref[...]
            o_ref[rows, vc] = (on * r_ref[rows, vc].astype(F32)).astype(BF16)
        return carry

    lax.fori_loop(0, rows_blk // CHUNK, chunk, 0)


def _gla(proj, glr, w2p, gate_b, norm_g):
    rows_blk = 512
    nblk = SEQ // rows_blk
    dk, dv = GLA_DK, D_MODEL

    def row(b, n):
        return b * nblk + n

    return pl.pallas_call(
        functools.partial(_gla_body, rows_blk=rows_blk),
        grid=(BATCH, nblk),
        in_specs=[pl.BlockSpec((rows_blk, dk), lambda b, n: (row(b, n), COL_Q // dk)),
                  pl.BlockSpec((rows_blk, dk), lambda b, n: (row(b, n), COL_K // dk)),
                  pl.BlockSpec((rows_blk, dv), lambda b, n: (row(b, n), COL_VV // dv)),
                  pl.BlockSpec((rows_blk, dv), lambda b, n: (row(b, n), COL_R // dv)),
                  pl.BlockSpec((rows_blk, LANES), lambda b, n: (row(b, n), 0)),
                  pl.BlockSpec((LANES, dk), lambda b, n: (0, 0)),
                  pl.BlockSpec((1, dk), lambda b, n: (0, 0)),
                  pl.BlockSpec((1, GLA_HEAD_V), lambda b, n: (0, 0))],
        out_specs=pl.BlockSpec((rows_blk, dv), lambda b, n: (row(b, n), 0)),
        out_shape=jax.ShapeDtypeStruct((TOKENS, D_MODEL), BF16),
        scratch_shapes=[pltpu.VMEM((GLA_HEADS, GLA_HEAD_V, GLA_HEAD_K), F32),
                        pltpu.VMEM((rows_blk, dk), F32)],
        compiler_params=_cparams("arbitrary", "arbitrary"),
        name="gla",
    )(proj, proj, proj, proj, glr, w2p, gate_b, norm_g)


def _merge_body(ya_ref, yb_ref, wa_ref, wb_ref, ga_ref, gb_ref, o_ref):
    a = jnp.dot(ya_ref[...], wa_ref[...], preferred_element_type=F32)
    b = jnp.dot(yb_ref[...], wb_ref[...], preferred_element_type=F32)
    o_ref[...] = (ga_ref[...].astype(F32) * a + gb_ref[...].astype(F32) * b).astype(BF16)


def _merge(y_a, y_b, wa, wb, proj):
    tm, tn = 512, 1024
    return pl.pallas_call(
        _merge_body,
        grid=(TOKENS // tm, D_MODEL // tn),
        in_specs=[pl.BlockSpec((tm, D_MODEL), lambda i, j: (i, 0)),
                  pl.BlockSpec((tm, D_MODEL), lambda i, j: (i, 0)),
                  pl.BlockSpec((D_MODEL, tn), lambda i, j: (0, j)),
                  pl.BlockSpec((D_MODEL, tn), lambda i, j: (0, j)),
                  pl.BlockSpec((tm, tn), lambda i, j: (i, COL_GA // tn + j)),
                  pl.BlockSpec((tm, tn), lambda i, j: (i, COL_GB // tn + j))],
        out_specs=pl.BlockSpec((tm, tn), lambda i, j: (i, j)),
        out_shape=jax.ShapeDtypeStruct((TOKENS, D_MODEL), BF16),
        compiler_params=_cparams("arbitrary", "arbitrary"),
        name="merge",
    )(y_a, y_b, wa, wb, proj, proj)


def _outproj_body(m_ref, x_ref, mod_ref, wo_ref, g2_ref, rwh_ref, rwl_ref, rb_ref,
                  x1_ref, hp_ref, idx_ref, w_ref, rank_ref, cnt_ref, run_ref, *, tm):
    @pl.when(pl.program_id(0) == 0)
    def _():
        run_ref[...] = jnp.zeros_like(run_ref)

    y = jnp.dot(m_ref[...], wo_ref[...], preferred_element_type=F32)
    x1 = x_ref[...] + mod_ref[2:3, :] * y
    x1_ref[...] = x1
    h2 = _rms(x1) * g2_ref[...] * (1.0 + mod_ref[4:5, :]) + mod_ref[3:4, :]
    hp_ref[...] = h2
    hb = h2.astype(BF16)
    hf = hb.astype(F32)

    h_lo = (h2 - hf).astype(BF16)
    logits = (_dot_nt(rwh_ref[...], hb) + _dot_nt(rwh_ref[...], h_lo) + _dot_nt(rwl_ref[...], hb)
              + rb_ref[...])
    e_i = lax.broadcasted_iota(I32, (N_EXPERTS, tm), 0).astype(F32)
    vals, idxs = [], []
    l = logits
    for _ in range(TOP_K):
        m = jnp.max(l, axis=0, keepdims=True)
        i = jnp.min(jnp.where(l == m, e_i, float(N_EXPERTS)), axis=0, keepdims=True)
        vals.append(m)
        idxs.append(i)
        l = jnp.where(e_i == i, -jnp.inf, l)
    ex = [jnp.exp(v - vals[0]) for v in vals]
    den = ex[0] + ex[1] + ex[2] + ex[3]
    before = (lax.broadcasted_iota(I32, (tm, tm), 0) < lax.broadcasted_iota(I32, (tm, tm), 1)).astype(BF16)
    run = run_ref[...]
    for k in range(TOP_K):
        onehot = e_i == idxs[k]
        pref = jnp.dot(onehot.astype(BF16), before, preferred_element_type=F32) + run[:, 0:1]
        rank_ref[k:k + 1, :] = jnp.sum(jnp.where(onehot, pref, 0.0), axis=0, keepdims=True).astype(I32)
        run = run + jnp.sum(onehot.astype(F32), axis=1, keepdims=True)
        idx_ref[k:k + 1, :] = idxs[k].astype(I32)
        w_ref[k:k + 1, :] = ex[k] / den
    run_ref[...] = run
    cnt_ref[...] = run.astype(I32)


def _outproj(merged, x2d, mod, w_out, norm2_g, rw_hi, rw_lo, router_b):
    tm = 512
    per_batch = SEQ // tm
    body = functools.partial(_outproj_body, tm=tm)
    return pl.pallas_call(
        body,
        grid=(TOKENS // tm,),
        in_specs=[pl.BlockSpec((tm, D_MODEL), lambda i: (i, 0)),
                  pl.BlockSpec((tm, D_MODEL), lambda i: (i, 0)),
                  pl.BlockSpec((None, 6, D_MODEL), lambda i: (i // per_batch, 0, 0)),
                  pl.BlockSpec((D_MODEL, D_MODEL), lambda i: (0, 0)),
                  pl.BlockSpec((1, D_MODEL), lambda i: (0, 0)),
                  pl.BlockSpec((N_EXPERTS, D_MODEL), lambda i: (0, 0)),
                  pl.BlockSpec((N_EXPERTS, D_MODEL), lambda i: (0, 0)),
                  pl.BlockSpec((N_EXPERTS, 1), lambda i: (0, 0))],
        out_specs=[pl.BlockSpec((tm, D_MODEL), lambda i: (i, 0)),
                   pl.BlockSpec((tm, D_MODEL), lambda i: (i, 0)),
                   pl.BlockSpec((TOP_K, tm), lambda i: (0, i)),
                   pl.BlockSpec((TOP_K, tm), lambda i: (0, i)),
                   pl.BlockSpec((TOP_K, tm), lambda i: (0, i)),
                   pl.BlockSpec((N_EXPERTS, LANES), lambda i: (0, 0))],
        out_shape=[jax.ShapeDtypeStruct((TOKENS, D_MODEL), F32),
                   jax.ShapeDtypeStruct((TOKENS, D_MODEL), F32),
                   jax.ShapeDtypeStruct((TOP_K, TOKENS), I32),
                   jax.ShapeDtypeStruct((TOP_K, TOKENS), F32),
                   jax.ShapeDtypeStruct((TOP_K, TOKENS), I32),
                   jax.ShapeDtypeStruct((N_EXPERTS, LANES), I32)],
        scratch_shapes=[pltpu.VMEM((N_EXPERTS, LANES), F32)],
        compiler_params=_cparams("arbitrary"),
        name="outproj",
    )(merged, x2d, mod, w_out, norm2_g, rw_hi, rw_lo, router_b)


def _dispatch_body(dest_ref, pend_ref, padded_ref, h_ref, xs_hbm, zero_ref, zsem, sem, *, tm):
    i = pl.program_id(0)

    @pl.when(i == 0)
    def _():
        _dispatch_clear(pend_ref, padded_ref, xs_hbm, zero_ref, zsem)

    def group(g, carry):
        base = pl.multiple_of(g * ROW_DMA_GROUP, ROW_DMA_GROUP)
        for u in range(ROW_DMA_GROUP):
            src = h_ref.at[pl.ds(base + u, 1)]
            for k in range(TOP_K):
                d = dest_ref[(i * tm + base + u) * TOP_K + k]
                pltpu.make_async_copy(src, xs_hbm.at[pl.ds(d, 1)], sem).start()
        return carry
    lax.fori_loop(0, tm // ROW_DMA_GROUP, group, 0)
    for k in range(TOP_K):
        pltpu.make_async_copy(h_ref, xs_hbm.at[pl.ds(0, tm)], sem).wait()


def _dispatch_clear(pend_ref, padded_ref, xs_hbm, zero_ref, zsem):
    zero_ref[...] = jnp.zeros_like(zero_ref)

    def zero_copy(e):
        return pltpu.make_async_copy(
            zero_ref, xs_hbm.at[pl.ds(pl.multiple_of(pend_ref[e] - ROW_TILE, ROW_TILE), ROW_TILE)], zsem)

    def zstart(e, c):
        @pl.when(padded_ref[e] > 0)
        def _():
            zero_copy(e).start()
        return c

    def zwait(e, c):
        @pl.when(padded_ref[e] > 0)
        def _():
            zero_copy(e).wait()
        return c

    def tail_copy(t):
        return pltpu.make_async_copy(
            zero_ref, xs_hbm.at[pl.ds(pl.multiple_of(t * ROW_TILE, ROW_TILE), ROW_TILE)], zsem)

    def tstart(t, c):
        tail_copy(t).start()
        return c

    def twait(t, c):
        tail_copy(t).wait()
        return c

    first_tail = pend_ref[N_EXPERTS - 1] // ROW_TILE
    lax.fori_loop(0, N_EXPERTS, zstart, 0)
    lax.fori_loop(first_tail, N_ROW_TILES, tstart, 0)
    lax.fori_loop(0, N_EXPERTS, zwait, 0)
    lax.fori_loop(first_tail, N_ROW_TILES, twait, 0)


def _dispatch(dest_flat, padded_end, padded, h_rows):
    tm = 256
    return pl.pallas_call(
        functools.partial(_dispatch_body, tm=tm),
        grid_spec=pltpu.PrefetchScalarGridSpec(
            num_scalar_prefetch=3,
            grid=(TOKENS // tm,),
            in_specs=[pl.BlockSpec((tm, D_MODEL), lambda i, d, pe, pd: (i, 0))],
            out_specs=pl.BlockSpec(memory_space=pl.ANY),
            scratch_shapes=[pltpu.VMEM((ROW_TILE, D_MODEL), F32),
                            pltpu.SemaphoreType.DMA(()),
                            pltpu.SemaphoreType.DMA(())]),
        out_shape=jax.ShapeDtypeStruct((PADDED_ROWS, D_MODEL), F32),
        compiler_params=_cparams("arbitrary"),
        name="dispatch",
    )(dest_flat, padded_end, padded, h_rows)


def _expert_body(se_ref, srow_ref, snsub_ref, xs_hbm, wg_ref, wu_ref, wd_ref, bg_ref, bu_ref, bd_ref,
                 y_hbm, xstage, xb, acc, wgb, wub, wdb, sem_in, sem_out):
    s = pl.program_id(0)
    f = pl.program_id(1)
    nsub = snsub_ref[s]
    row0 = srow_ref[s]

    last_f = f == N_FF - 1
    n_pair = nsub // 2
    odd = nsub - 2 * n_pair
    pair_rows = 2 * ROW_TILE

    def in_copy(t, slot):
        src = xs_hbm.at[pl.ds(pl.multiple_of(row0 + t * ROW_TILE, ROW_TILE), ROW_TILE)]
        return pltpu.make_async_copy(src, xstage.at[slot], sem_in.at[slot])

    def out_copy(off, m):
        dst = y_hbm.at[pl.ds(pl.multiple_of(row0 + off, ROW_TILE), m)]
        return pltpu.make_async_copy(acc.at[pl.ds(off, m)], dst, sem_out)

    def compute(off, m):
        rows = pl.ds(off, m)
        x = xb[rows, :]
        g = jnp.minimum(jnp.dot(x, wgb[...], preferred_element_type=F32) + bg_ref[...], SWIGLU_LIMIT)
        u = jnp.clip(jnp.dot(x, wub[...], preferred_element_type=F32) + bu_ref[...],
                     -SWIGLU_LIMIT, SWIGLU_LIMIT)
        a = ((u + 1.0) * (g * jax.nn.sigmoid(SWIGLU_ALPHA * g))).astype(BF16)
        acc[rows, :] += jnp.dot(a, wdb[...], preferred_element_type=F32)

        @pl.when(last_f)
        def _():
            out_copy(off, m).start()

    odd_off = pl.multiple_of((nsub - 1) * ROW_TILE, ROW_TILE)

    @pl.when(nsub > 0)
    def _():
        @pl.when(f == 0)
        def _():
            in_copy(0, 0).start()

            def load(t, c):
                slot = t % 2

                @pl.when(t + 1 < nsub)
                def _():
                    in_copy(t + 1, 1 - slot).start()
                in_copy(t, slot).wait()
                rows = pl.ds(pl.multiple_of(t * ROW_TILE, ROW_TILE), ROW_TILE)
                xb[rows, :] = xstage[slot].astype(BF16)
                acc[rows, :] = jnp.broadcast_to(bd_ref[...], (ROW_TILE, D_MODEL))
                return c
            lax.fori_loop(0, nsub, load, 0)

        wgb[...] = wg_ref[...].astype(BF16)
        wub[...] = wu_ref[...].astype(BF16)
        wdb[...] = wd_ref[...].astype(BF16)

        def pair(p, c):
            compute(pl.multiple_of(p * pair_rows, pair_rows), pair_rows)
            return c
        lax.fori_loop(0, n_pair, pair, 0)

        @pl.when(odd == 1)
        def _():
            compute(odd_off, ROW_TILE)

        @pl.when(last_f)
        def _():
            def drain(p, c):
                out_copy(pl.multiple_of(p * pair_rows, pair_rows), pair_rows).wait()
                return c
            lax.fori_loop(0, n_pair, drain, 0)

            @pl.when(odd == 1)
            def _():
                out_copy(odd_off, ROW_TILE).wait()

    @pl.when((s == MAX_SUPER - 1) & (f == N_FF - 1))
    def _():
        acc[0:ROW_TILE, :] = jnp.zeros((ROW_TILE, D_MODEL), F32)

        def tail_copy(t):
            dst = y_hbm.at[pl.ds(pl.multiple_of(t * ROW_TILE, ROW_TILE), ROW_TILE)]
            return pltpu.make_async_copy(acc.at[0:ROW_TILE], dst, sem_out)

        def tstart(t, c):
            tail_copy(t).start()
            return c

        def twait(t, c):
            tail_copy(t).wait()
            return c

        first_tail = snsub_ref[MAX_SUPER]
        lax.fori_loop(first_tail, N_ROW_TILES, tstart, 0)
        lax.fori_loop(first_tail, N_ROW_TILES, twait, 0)


def _experts(se, srow, snsub, xs, w_gate, b_gate, w_up, b_up, w_down, b_down):
    def f_eff(s, f, nsub):
        return jnp.where(nsub[s] > 0, f, N_FF - 1)

    return pl.pallas_call(
        _expert_body,
        grid_spec=pltpu.PrefetchScalarGridSpec(
            num_scalar_prefetch=3,
            grid=(MAX_SUPER, N_FF),
            in_specs=[pl.BlockSpec(memory_space=pl.ANY),
                      pl.BlockSpec((None, D_MODEL, FF_TILE), lambda s, f, se, sr, sn: (se[s], 0, f_eff(s, f, sn))),
                      pl.BlockSpec((None, D_MODEL, FF_TILE), lambda s, f, se, sr, sn: (se[s], 0, f_eff(s, f, sn))),
                      pl.BlockSpec((None, FF_TILE, D_MODEL), lambda s, f, se, sr, sn: (se[s], f_eff(s, f, sn), 0)),
                      pl.BlockSpec((None, 1, FF_TILE), lambda s, f, se, sr, sn: (se[s], 0, f_eff(s, f, sn))),
                      pl.BlockSpec((None, 1, FF_TILE), lambda s, f, se, sr, sn: (se[s], 0, f_eff(s, f, sn))),
                      pl.BlockSpec((None, 1, D_MODEL), lambda s, f, se, sr, sn: (se[s], 0, 0))],
            out_specs=pl.BlockSpec(memory_space=pl.ANY),
            scratch_shapes=[pltpu.VMEM((2, ROW_TILE, D_MODEL), F32),
                            pltpu.VMEM((SUPER_ROWS, D_MODEL), BF16),
                            pltpu.VMEM((SUPER_ROWS, D_MODEL), F32),
                            pltpu.VMEM((D_MODEL, FF_TILE), BF16),
                            pltpu.VMEM((D_MODEL, FF_TILE), BF16),
                            pltpu.VMEM((FF_TILE, D_MODEL), BF16),
                            pltpu.SemaphoreType.DMA((2,)),
                            pltpu.SemaphoreType.DMA(())]),
        out_shape=jax.ShapeDtypeStruct((PADDED_ROWS, D_MODEL), F32),
        compiler_params=_cparams("arbitrary", "arbitrary"),
        name="experts",
    )(se, srow, snsub, xs, w_gate, w_up, w_down,
      b_gate.reshape(N_EXPERTS, 1, D_FF), b_up.reshape(N_EXPERTS, 1, D_FF),
      b_down.reshape(N_EXPERTS, 1, D_MODEL))


def _combine_body(dest_ref, y_hbm, x1_ref, w_ref, mod_ref, fg_ref, o_ref, buf, sem, *, tm):
    i = pl.program_id(0)
    def group(g, carry):
        base = pl.multiple_of(g * ROW_DMA_GROUP, ROW_DMA_GROUP)
        for u in range(ROW_DMA_GROUP):
            for k in range(TOP_K):
                d = dest_ref[(i * tm + base + u) * TOP_K + k]
                pltpu.make_async_copy(y_hbm.at[pl.ds(d, 1)], buf.at[k, pl.ds(base + u, 1)], sem).start()
        return carry
    lax.fori_loop(0, tm // ROW_DMA_GROUP, group, 0)
    for k in range(TOP_K):
        pltpu.make_async_copy(y_hbm.at[pl.ds(0, tm)], buf.at[k], sem).wait()
    moe = buf[0] * w_ref[:, 0:1]
    for k in range(1, TOP_K):
        moe = moe + buf[k] * w_ref[:, k:k + 1]
    x2 = x1_ref[...] + mod_ref[5:6, :] * moe
    o_ref[...] = _rms(x2) * fg_ref[...]


def _combine(dest_flat, y, x1, w_t, mod, final_g):
    tm = 256
    per_batch = SEQ // tm
    return pl.pallas_call(
        functools.partial(_combine_body, tm=tm),
        grid_spec=pltpu.PrefetchScalarGridSpec(
            num_scalar_prefetch=1,
            grid=(TOKENS // tm,),
            in_specs=[pl.BlockSpec(memory_space=pl.ANY),
                      pl.BlockSpec((tm, D_MODEL), lambda i, d: (i, 0)),
                      pl.BlockSpec((tm, TOP_K), lambda i, d: (i, 0)),
                      pl.BlockSpec((None, 6, D_MODEL), lambda i, d: (i // per_batch, 0, 0)),
                      pl.BlockSpec((1, D_MODEL), lambda i, d: (0, 0))],
            out_specs=pl.BlockSpec((tm, D_MODEL), lambda i, d: (i, 0)),
            scratch_shapes=[pltpu.VMEM((TOP_K, tm, D_MODEL), F32),
                            pltpu.SemaphoreType.DMA(())]),
        out_shape=jax.ShapeDtypeStruct((TOKENS, D_MODEL), F32),
        compiler_params=_cparams("arbitrary"),
        name="combine",
    )(dest_flat, y, x1, w_t, mod, final_g)


def _routing_tables(idx, rank, counts):
    padded = (counts + ROW_TILE - 1) // ROW_TILE * ROW_TILE
    padded_end = jnp.cumsum(padded)
    padded_start = padded_end - padded
    e_i = jnp.arange(N_EXPERTS, dtype=I32)[:, None, None]
    start_of = jnp.sum(jnp.where(idx[None] == e_i, padded_start[:, None, None], 0), axis=0)
    dest = (start_of + rank).T.reshape(N_ROWS).astype(I32)

    tiles = padded // ROW_TILE
    n_super = (tiles + SUPER_TILES - 1) // SUPER_TILES
    super_end = jnp.cumsum(n_super)
    super_start = super_end - n_super
    s_i = jnp.arange(MAX_SUPER, dtype=I32)
    total = super_end[-1]
    valid = s_i < total
    e_of = jnp.minimum(jnp.sum((s_i[:, None] >= super_end[None, :]).astype(I32), axis=1), N_EXPERTS - 1)
    last_e = jnp.minimum(jnp.sum((total - 1 >= super_end).astype(I32)), N_EXPERTS - 1)
    local = s_i - super_start[e_of]
    srow = jnp.where(valid, padded_start[e_of] + local * SUPER_ROWS, 0).astype(I32)
    snsub = jnp.where(valid, jnp.minimum(tiles[e_of] - local * SUPER_TILES, SUPER_TILES), 0).astype(I32)
    snsub = jnp.concatenate([snsub, (padded_end[-1:] // ROW_TILE).astype(I32)])
    se = jnp.where(valid, e_of, last_e).astype(I32)
    return dest, padded_end.astype(I32), padded.astype(I32), se, srow, snsub


def kernel(x, c, ada_w, ada_b, norm1_g, w_in, gla_gate_w2, gla_gate_b, sgu_ln_g, sgu_ln_b, sgu_w, sgu_b,
           gla_norm_g, w_branch_a, w_branch_b, w_out, norm2_g, router_w, router_b, exp_w_gate, exp_b_gate,
           exp_w_up, exp_b_up, exp_w_down, exp_b_down, final_g):
    x2d = x.reshape(TOKENS, D_MODEL)
    mod = _ada(c, ada_w[0], ada_b[0])

    w_in0 = w_in[0]
    w_in_t = w_in0.T
    w_glr = jnp.zeros((LANES, D_MODEL), BF16).at[:GLA_GATE_RANK].set(
        w_in_t[GLR_SRC:GLR_SRC + GLA_GATE_RANK].astype(BF16))
    h, glr = _norm1(x2d, mod, norm1_g, w_glr)
    proj = _inproj(h, w_in_t)

    y_a = _sgu(proj, sgu_ln_g, sgu_ln_b, sgu_w[0], sgu_b[0].T)

    w2p = jnp.zeros((LANES, GLA_DK), BF16).at[:GLA_GATE_RANK].set(gla_gate_w2[0].astype(BF16))
    y_b = _gla(proj, glr, w2p, gla_gate_b, gla_norm_g)

    merged = _merge(y_a, y_b, w_branch_a[0].astype(BF16), w_branch_b[0].astype(BF16), proj)

    rw_t = router_w[0].T
    rw_hi = rw_t.astype(BF16)
    rw_lo = (rw_t - rw_hi.astype(F32)).astype(BF16)
    x1, h_rows, idx, top_w, rank, cnt = _outproj(
        merged, x2d, mod, w_out[0].astype(BF16), norm2_g, rw_hi, rw_lo, router_b[0].reshape(N_EXPERTS, 1))

    dest, padded_end, padded, se, srow, snsub = _routing_tables(idx, rank, cnt[:, 0])
    xs = _dispatch(dest, padded_end, padded, h_rows)
    y = _experts(se, srow, snsub, xs, exp_w_gate[0], exp_b_gate[0], exp_w_up[0], exp_b_up[0],
                 exp_w_down[0], exp_b_down[0])
    out = _combine(dest, y, x1, top_w.T, mod, final_g.reshape(1, D_MODEL))
    return out.reshape(BATCH, SEQ, D_MODEL)
```

```python
import functools

import jax
import jax.numpy as jnp
from jax import lax
from jax.experimental import pallas as pl
from jax.experimental.pallas import tpu as pltpu

F32 = jnp.float32
BF16 = jnp.bfloat16
I32 = jnp.int32

D_MODEL = 2048
BATCH = 4
SEQ = 2048
TOKENS = BATCH * SEQ
CHUNK = 64
SPATIAL_BLOCK = 128
A_GROUPS = 8
A_GROUP_DIM = D_MODEL // A_GROUPS
GLA_HEADS = 4
GLA_DK = D_MODEL // 2
GLA_HEAD_K = GLA_DK // GLA_HEADS
GLA_HEAD_V = D_MODEL // GLA_HEADS
GLA_GATE_RANK = 16
GLA_GATE_TAU = 16.0
N_EXPERTS = 32
TOP_K = 4
D_FF = D_MODEL
SWIGLU_LIMIT = 7.0
SWIGLU_ALPHA = 1.702
NORM_EPS = 1e-6

LANES = 128
VMEM_LIMIT = 56 * 1024 * 1024

COL_U, COL_V, COL_Q, COL_K, COL_VV, COL_R = 0, 2048, 4096, 5120, 6144, 8192
COL_GA, COL_GB = 10240, 12288
PROJ_W = 14336
GLR_SRC = 10240

ROW_TILE = 256
SUPER_TILES = 6
SUPER_ROWS = ROW_TILE * SUPER_TILES
N_ROWS = TOKENS * TOP_K
N_ROW_TILES = (N_ROWS + N_EXPERTS * (ROW_TILE - 1)) // ROW_TILE
PADDED_ROWS = N_ROW_TILES * ROW_TILE
MAX_SUPER = -(-(N_ROW_TILES + N_EXPERTS * (SUPER_TILES - 1)) // SUPER_TILES)
FF_TILE = 256
N_FF = D_FF // FF_TILE
EXPERT_STEPS = MAX_SUPER * N_FF
CAST_PIECES = 2
ROW_DMA_GROUP = 8


def _cparams(*sem):
    return pltpu.CompilerParams(dimension_semantics=sem, vmem_limit_bytes=VMEM_LIMIT)


def _rms(x):
    return x * lax.rsqrt(jnp.mean(x * x, axis=-1, keepdims=True) + NORM_EPS)


def _ada_body(c_ref, w_ref, b_ref, o_ref):
    c = c_ref[...]
    cond = c * jax.nn.sigmoid(c)
    o_ref[...] = jnp.dot(cond.astype(BF16), w_ref[...].astype(BF16),
                         preferred_element_type=F32) + b_ref[...]


def _ada(c, ada_w, ada_b):
    tn = 1024
    cp = jnp.zeros((8, D_MODEL), F32).at[:BATCH].set(c)
    out = pl.pallas_call(
        _ada_body,
        grid=(6 * D_MODEL // tn,),
        in_specs=[pl.BlockSpec((8, D_MODEL), lambda j: (0, 0)),
                  pl.BlockSpec((D_MODEL, tn), lambda j: (0, j)),
                  pl.BlockSpec((1, tn), lambda j: (0, j))],
        out_specs=pl.BlockSpec((8, tn), lambda j: (0, j)),
        out_shape=jax.ShapeDtypeStruct((8, 6 * D_MODEL), F32),
        compiler_params=_cparams("arbitrary"),
        name="ada",
    )(cp, ada_w, ada_b.reshape(1, 6 * D_MODEL))
    return out[:BATCH].reshape(BATCH, 6, D_MODEL)


def _norm1_body(x_ref, mod_ref, g_ref, wglr_ref, h_ref, glr_ref):
    h = _rms(x_ref[...]) * g_ref[...] * (1.0 + mod_ref[1:2, :]) + mod_ref[0:1, :]
    hb = h.astype(BF16)
    h_ref[...] = hb
    glr_ref[...] = _dot_nt(hb, wglr_ref[...])


def _norm1(x2d, mod, norm1_g, w_glr):
    tm = 512
    per_batch = SEQ // tm
    return pl.pallas_call(
        _norm1_body,
        grid=(TOKENS // tm,),
        in_specs=[pl.BlockSpec((tm, D_MODEL), lambda i: (i, 0)),
                  pl.BlockSpec((None, 6, D_MODEL), lambda i: (i // per_batch, 0, 0)),
                  pl.BlockSpec((1, D_MODEL), lambda i: (0, 0)),
                  pl.BlockSpec((LANES, D_MODEL), lambda i: (0, 0))],
        out_specs=[pl.BlockSpec((tm, D_MODEL), lambda i: (i, 0)),
                   pl.BlockSpec((tm, LANES), lambda i: (i, 0))],
        out_shape=[jax.ShapeDtypeStruct((TOKENS, D_MODEL), BF16),
                   jax.ShapeDtypeStruct((TOKENS, LANES), F32)],
        compiler_params=_cparams("arbitrary"),
        name="norm1",
    )(x2d, mod, norm1_g, w_glr)


INPROJ_TN = 1024
INPROJ_MAIN_TILES = COL_GA // INPROJ_TN
MXU_N = 256


def _inproj_body(h_ref, wt_ref, proj_ref, wb_ref):
    j = pl.program_id(0)

    @pl.when(pl.program_id(1) == 0)
    def _():
        wb_ref[...] = wt_ref[...].astype(BF16)

    def run(act):
        for n in range(INPROJ_TN // MXU_N):
            cols = slice(n * MXU_N, (n + 1) * MXU_N)
            acc = _dot_nt(h_ref[...], wb_ref[cols, :])
            proj_ref[:, cols] = act(acc).astype(BF16)

    col = j * INPROJ_TN

    @pl.when(col < COL_Q)
    def _():
        run(jax.nn.gelu)

    @pl.when((col >= COL_Q) & (col < COL_R))
    def _():
        run(lambda a: a)

    @pl.when((col >= COL_R) & (col < COL_GA))
    def _():
        run(lambda a: a * jax.nn.sigmoid(a))

    @pl.when(col >= COL_GA)
    def _():
        run(jax.nn.sigmoid)


def _inproj(h, w_in_t):
    tm, tn = 1024, INPROJ_TN

    def w_row(j, i):
        return (pl.multiple_of(j * tn + jnp.where(j >= INPROJ_MAIN_TILES, GLA_GATE_RANK, 0), GLA_GATE_RANK), 0)

    return pl.pallas_call(
        _inproj_body,
        grid=(PROJ_W // tn, TOKENS // tm),
        in_specs=[pl.BlockSpec((tm, D_MODEL), lambda j, i: (i, 0)),
                  pl.BlockSpec((pl.Element(tn), pl.Element(D_MODEL)), w_row)],
        out_specs=pl.BlockSpec((tm, tn), lambda j, i: (i, j)),
        out_shape=jax.ShapeDtypeStruct((TOKENS, PROJ_W), BF16),
        scratch_shapes=[pltpu.VMEM((tn, D_MODEL), BF16)],
        compiler_params=_cparams("arbitrary", "arbitrary"),
        name="inproj",
    )(h, w_in_t)


def _sgu_body(u_ref, v_ref, lg_ref, lb_ref, ws_ref, bs_ref, o_ref):
    v = v_ref[...].astype(F32)
    mu = jnp.mean(v, axis=-1, keepdims=True)
    xc = v - mu
    var = jnp.mean(xc * xc, axis=-1, keepdims=True)
    vn = (xc * lax.rsqrt(var + NORM_EPS) * lg_ref[...] + lb_ref[...]).astype(BF16)
    t_chunk = lax.broadcasted_iota(I32, (SPATIAL_BLOCK, SPATIAL_BLOCK), 0) // CHUNK
    s_chunk = lax.broadcasted_iota(I32, (SPATIAL_BLOCK, SPATIAL_BLOCK), 1) // CHUNK
    mask = s_chunk <= t_chunk
    for g in range(A_GROUPS):
        cols = slice(g * A_GROUP_DIM, (g + 1) * A_GROUP_DIM)
        w = jnp.where(mask, ws_ref[g], 0.0).astype(BF16)
        mixed = jnp.dot(w, vn[:, cols], preferred_element_type=F32) + bs_ref[:, g:g + 1]
        o_ref[:, cols] = (u_ref[:, cols].astype(F32) * mixed).astype(BF16)


def _sgu(proj, ln_g, ln_b, w_s, b_s_t):
    nblk = TOKENS // SPATIAL_BLOCK
    wb = D_MODEL
    return pl.pallas_call(
        _sgu_body,
        grid=(nblk,),
        in_specs=[pl.BlockSpec((SPATIAL_BLOCK, wb), lambda i: (i, COL_U // wb)),
                  pl.BlockSpec((SPATIAL_BLOCK, wb), lambda i: (i, COL_V // wb)),
                  pl.BlockSpec((1, wb), lambda i: (0, 0)),
                  pl.BlockSpec((1, wb), lambda i: (0, 0)),
                  pl.BlockSpec((A_GROUPS, SPATIAL_BLOCK, SPATIAL_BLOCK), lambda i: (0, 0, 0)),
                  pl.BlockSpec((SPATIAL_BLOCK, A_GROUPS), lambda i: (0, 0))],
        out_specs=pl.BlockSpec((SPATIAL_BLOCK, wb), lambda i: (i, 0)),
        out_shape=jax.ShapeDtypeStruct((TOKENS, wb), BF16),
        compiler_params=_cparams("arbitrary"),
        name="sgu",
    )(proj, proj, ln_g, ln_b, w_s, b_s_t)


def _dot_nt(a, b):
    return lax.dot_general(a, b, (((1,), (1,)), ((), ())), preferred_element_type=F32)


def _dot_tn(a, b):
    return lax.dot_general(a, b, (((0,), (0,)), ((), ())), preferred_element_type=F32)


def _gla_body(q_ref, k_ref, v_ref, r_ref, glr_ref, w2_ref, gb_ref, ng_ref, o_ref, st_ref, la_ref, *, rows_blk):
    @pl.when(pl.program_id(1) == 0)
    def _():
        st_ref[...] = jnp.zeros_like(st_ref)

    z = jnp.dot(glr_ref[...].astype(BF16), w2_ref[...], preferred_element_type=F32) + gb_ref[...]
    la_ref[...] = jax.nn.log_sigmoid(z) / GLA_GATE_TAU

    r_i = lax.broadcasted_iota(I32, (CHUNK, CHUNK), 0)
    c_i = lax.broadcasted_iota(I32, (CHUNK, CHUNK), 1)
    causal = c_i <= r_i
    tril = causal.astype(BF16)
    scale = GLA_HEAD_K ** -0.5

    def chunk(c, carry):
        rows = pl.ds(pl.multiple_of(c * CHUNK, CHUNK), CHUNK)
        for h in range(GLA_HEADS):
            kc = slice(h * GLA_HEAD_K, (h + 1) * GLA_HEAD_K)
            vc = slice(h * GLA_HEAD_V, (h + 1) * GLA_HEAD_V)
            la = la_ref[rows, kc]
            hi = la.astype(BF16)
            r1 = la - hi.astype(F32)
            mid = r1.astype(BF16)
            lo = (r1 - mid.astype(F32)).astype(BF16)
            b = (jnp.dot(tril, hi, preferred_element_type=F32)
                 + jnp.dot(tril, mid, preferred_element_type=F32)
                 + jnp.dot(tril, lo, preferred_element_type=F32))
            b_last = b[CHUNK - 1:CHUNK, :]
            q = q_ref[rows, kc].astype(F32) * scale
            k = k_ref[rows, kc].astype(F32)
            v = v_ref[rows, vc]
            q_dec = (q * jnp.exp(b)).astype(BF16)
            k_intra = (k * jnp.exp(-b)).astype(BF16)
            k_state = (k * jnp.exp(b_last - b)).astype(BF16)
            att = jnp.where(causal, _dot_nt(q_dec, k_intra), 0.0).astype(BF16)
            st = st_ref[h]
            o = jnp.dot(att, v, preferred_element_type=F32) + _dot_nt(q_dec, st.astype(BF16))
            st_ref[h] = st * jnp.exp(b_last) + _dot_tn(v, k_state)
            on = _rms(o) * ng_ref[...]
            o_ref[rows, vc] = (on * r_ref[rows, vc].astype(F32)).astype(BF16)
        return carry

    lax.fori_loop(0, rows_blk // CHUNK, chunk, 0)


def _gla(proj, glr, w2p, gate_b, norm_g):
    rows_blk = 512
    nblk = SEQ // rows_blk
    dk, dv = GLA_DK, D_MODEL

    def row(b, n):
        return b * nblk + n

    return pl.pallas_call(
        functools.partial(_gla_body, rows_blk=rows_blk),
        grid=(BATCH, nblk),
        in_specs=[pl.BlockSpec((rows_blk, dk), lambda b, n: (row(b, n), COL_Q // dk)),
                  pl.BlockSpec((rows_blk, dk), lambda b, n: (row(b, n), COL_K // dk)),
                  pl.BlockSpec((rows_blk, dv), lambda b, n: (row(b, n), COL_VV // dv)),
                  pl.BlockSpec((rows_blk, dv), lambda b, n: (row(b, n), COL_R // dv)),
                  pl.BlockSpec((rows_blk, LANES), lambda b, n: (row(b, n), 0)),
                  pl.BlockSpec((LANES, dk), lambda b, n: (0, 0)),
                  pl.BlockSpec((1, dk), lambda b, n: (0, 0)),
                  pl.BlockSpec((1, GLA_HEAD_V), lambda b, n: (0, 0))],
        out_specs=pl.BlockSpec((rows_blk, dv), lambda b, n: (row(b, n), 0)),
        out_shape=jax.ShapeDtypeStruct((TOKENS, D_MODEL), BF16),
        scratch_shapes=[pltpu.VMEM((GLA_HEADS, GLA_HEAD_V, GLA_HEAD_K), F32),
                        pltpu.VMEM((rows_blk, dk), F32)],
        compiler_params=_cparams("arbitrary", "arbitrary"),
        name="gla",
    )(proj, proj, proj, proj, glr, w2p, gate_b, norm_g)


def _merge_body(ya_ref, yb_ref, wa_ref, wb_ref, ga_ref, gb_ref, o_ref):
    a = jnp.dot(ya_ref[...], wa_ref[...], preferred_element_type=F32)
    b = jnp.dot(yb_ref[...], wb_ref[...], preferred_element_type=F32)
    o_ref[...] = (ga_ref[...].astype(F32) * a + gb_ref[...].astype(F32) * b).astype(BF16)


def _merge(y_a, y_b, wa, wb, proj):
    tm, tn = 512, 1024
    return pl.pallas_call(
        _merge_body,
        grid=(TOKENS // tm, D_MODEL // tn),
        in_specs=[pl.BlockSpec((tm, D_MODEL), lambda i, j: (i, 0)),
                  pl.BlockSpec((tm, D_MODEL), lambda i, j: (i, 0)),
                  pl.BlockSpec((D_MODEL, tn), lambda i, j: (0, j)),
                  pl.BlockSpec((D_MODEL, tn), lambda i, j: (0, j)),
                  pl.BlockSpec((tm, tn), lambda i, j: (i, COL_GA // tn + j)),
                  pl.BlockSpec((tm, tn), lambda i, j: (i, COL_GB // tn + j))],
        out_specs=pl.BlockSpec((tm, tn), lambda i, j: (i, j)),
        out_shape=jax.ShapeDtypeStruct((TOKENS, D_MODEL), BF16),
        compiler_params=_cparams("arbitrary", "arbitrary"),
        name="merge",
    )(y_a, y_b, wa, wb, proj, proj)


def _outproj_body(m_ref, x_ref, mod_ref, wo_ref, g2_ref, rwh_ref, rwl_ref, rb_ref,
                  x1_ref, hp_ref, idx_ref, w_ref, rank_ref, cnt_ref, run_ref, *, tm):
    @pl.when(pl.program_id(0) == 0)
    def _():
        run_ref[...] = jnp.zeros_like(run_ref)

    y = jnp.dot(m_ref[...], wo_ref[...], preferred_element_type=F32)
    x1 = x_ref[...] + mod_ref[2:3, :] * y
    x1_ref[...] = x1
    h2 = _rms(x1) * g2_ref[...] * (1.0 + mod_ref[4:5, :]) + mod_ref[3:4, :]
    hp_ref[...] = h2
    hb = h2.astype(BF16)
    hf = hb.astype(F32)

    h_lo = (h2 - hf).astype(BF16)
    logits = (_dot_nt(rwh_ref[...], hb) + _dot_nt(rwh_ref[...], h_lo) + _dot_nt(rwl_ref[...], hb)
              + rb_ref[...])
    e_i = lax.broadcasted_iota(I32, (N_EXPERTS, tm), 0).astype(F32)
    vals, idxs = [], []
    l = logits
    for _ in range(TOP_K):
        m = jnp.max(l, axis=0, keepdims=True)
        i = jnp.min(jnp.where(l == m, e_i, float(N_EXPERTS)), axis=0, keepdims=True)
        vals.append(m)
        idxs.append(i)
        l = jnp.where(e_i == i, -jnp.inf, l)
    ex = [jnp.exp(v - vals[0]) for v in vals]
    den = ex[0] + ex[1] + ex[2] + ex[3]
    before = (lax.broadcasted_iota(I32, (tm, tm), 0) < lax.broadcasted_iota(I32, (tm, tm), 1)).astype(BF16)
    run = run_ref[...]
    for k in range(TOP_K):
        onehot = e_i == idxs[k]
        pref = jnp.dot(onehot.astype(BF16), before, preferred_element_type=F32) + run[:, 0:1]
        rank_ref[k:k + 1, :] = jnp.sum(jnp.where(onehot, pref, 0.0), axis=0, keepdims=True).astype(I32)
        run = run + jnp.sum(onehot.astype(F32), axis=1, keepdims=True)
        idx_ref[k:k + 1, :] = idxs[k].astype(I32)
        w_ref[k:k + 1, :] = ex[k] / den
    run_ref[...] = run
    cnt_ref[...] = run.astype(I32)


def _outproj(merged, x2d, mod, w_out, norm2_g, rw_hi, rw_lo, router_b):
    tm = 512
    per_batch = SEQ // tm
    body = functools.partial(_outproj_body, tm=tm)
    return pl.pallas_call(
        body,
        grid=(TOKENS // tm,),
        in_specs=[pl.BlockSpec((tm, D_MODEL), lambda i: (i, 0)),
                  pl.BlockSpec((tm, D_MODEL), lambda i: (i, 0)),
                  pl.BlockSpec((None, 6, D_MODEL), lambda i: (i // per_batch, 0, 0)),
                  pl.BlockSpec((D_MODEL, D_MODEL), lambda i: (0, 0)),
                  pl.BlockSpec((1, D_MODEL), lambda i: (0, 0)),
                  pl.BlockSpec((N_EXPERTS, D_MODEL), lambda i: (0, 0)),
                  pl.BlockSpec((N_EXPERTS, D_MODEL), lambda i: (0, 0)),
                  pl.BlockSpec((N_EXPERTS, 1), lambda i: (0, 0))],
        out_specs=[pl.BlockSpec((tm, D_MODEL), lambda i: (i, 0)),
                   pl.BlockSpec((tm, D_MODEL), lambda i: (i, 0)),
                   pl.BlockSpec((TOP_K, tm), lambda i: (0, i)),
                   pl.BlockSpec((TOP_K, tm), lambda i: (0, i)),
                   pl.BlockSpec((TOP_K, tm), lambda i: (0, i)),
                   pl.BlockSpec((N_EXPERTS, LANES), lambda i: (0, 0))],
        out_shape=[jax.ShapeDtypeStruct((TOKENS, D_MODEL), F32),
                   jax.ShapeDtypeStruct((TOKENS, D_MODEL), F32),
                   jax.ShapeDtypeStruct((TOP_K, TOKENS), I32),
                   jax.ShapeDtypeStruct((TOP_K, TOKENS), F32),
                   jax.ShapeDtypeStruct((TOP_K, TOKENS), I32),
                   jax.ShapeDtypeStruct((N_EXPERTS, LANES), I32)],
        scratch_shapes=[pltpu.VMEM((N_EXPERTS, LANES), F32)],
        compiler_params=_cparams("arbitrary"),
        name="outproj",
    )(merged, x2d, mod, w_out, norm2_g, rw_hi, rw_lo, router_b)


def _dispatch_body(dest_ref, pend_ref, padded_ref, h_ref, xs_hbm, zero_ref, zsem, sem, *, tm):
    i = pl.program_id(0)

    @pl.when(i == 0)
    def _():
        _dispatch_clear(pend_ref, padded_ref, xs_hbm, zero_ref, zsem)

    def group(g, carry):
        base = pl.multiple_of(g * ROW_DMA_GROUP, ROW_DMA_GROUP)
        for u in range(ROW_DMA_GROUP):
            src = h_ref.at[pl.ds(base + u, 1)]
            for k in range(TOP_K):
                d = dest_ref[(i * tm + base + u) * TOP_K + k]
                pltpu.make_async_copy(src, xs_hbm.at[pl.ds(d, 1)], sem).start()
        return carry
    lax.fori_loop(0, tm // ROW_DMA_GROUP, group, 0)
    for k in range(TOP_K):
        pltpu.make_async_copy(h_ref, xs_hbm.at[pl.ds(0, tm)], sem).wait()


def _dispatch_clear(pend_ref, padded_ref, xs_hbm, zero_ref, zsem):
    zero_ref[...] = jnp.zeros_like(zero_ref)

    def zero_copy(e):
        return pltpu.make_async_copy(
            zero_ref, xs_hbm.at[pl.ds(pl.multiple_of(pend_ref[e] - ROW_TILE, ROW_TILE), ROW_TILE)], zsem)

    def zstart(e, c):
        @pl.when(padded_ref[e] > 0)
        def _():
            zero_copy(e).start()
        return c

    def zwait(e, c):
        @pl.when(padded_ref[e] > 0)
        def _():
            zero_copy(e).wait()
        return c

    def tail_copy(t):
        return pltpu.make_async_copy(
            zero_ref, xs_hbm.at[pl.ds(pl.multiple_of(t * ROW_TILE, ROW_TILE), ROW_TILE)], zsem)

    def tstart(t, c):
        tail_copy(t).start()
        return c

    def twait(t, c):
        tail_copy(t).wait()
        return c

    first_tail = pend_ref[N_EXPERTS - 1] // ROW_TILE
    lax.fori_loop(0, N_EXPERTS, zstart, 0)
    lax.fori_loop(first_tail, N_ROW_TILES, tstart, 0)
    lax.fori_loop(0, N_EXPERTS, zwait, 0)
    lax.fori_loop(first_tail, N_ROW_TILES, twait, 0)


def _dispatch(dest_flat, padded_end, padded, h_rows):
    tm = 256
    return pl.pallas_call(
        functools.partial(_dispatch_body, tm=tm),
        grid_spec=pltpu.PrefetchScalarGridSpec(
            num_scalar_prefetch=3,
            grid=(TOKENS // tm,),
            in_specs=[pl.BlockSpec((tm, D_MODEL), lambda i, d, pe, pd: (i, 0))],
            out_specs=pl.BlockSpec(memory_space=pl.ANY),
            scratch_shapes=[pltpu.VMEM((ROW_TILE, D_MODEL), F32),
                            pltpu.SemaphoreType.DMA(()),
                            pltpu.SemaphoreType.DMA(())]),
        out_shape=jax.ShapeDtypeStruct((PADDED_ROWS, D_MODEL), F32),
        compiler_params=_cparams("arbitrary"),
        name="dispatch",
    )(dest_flat, padded_end, padded, h_rows)


def _expert_body(se_ref, srow_ref, snsub_ref, xs_hbm, wg_ref, wu_ref, wd_ref, bg_ref, bu_ref, bd_ref,
                 y_hbm, xstage, xb, acc, wg_a, wu_a, wd_a, wg_b, wu_b, wd_b, sem_in, sem_out):
    n = pl.program_id(0)
    m = jnp.maximum(n - 1, 0)
    s = m // N_FF
    f = m % N_FF
    nsub = jnp.where(n >= 1, snsub_ref[s], 0)
    row0 = srow_ref[s]
    cast_needed = (n < EXPERT_STEPS) & (snsub_ref[jnp.minimum(n, EXPERT_STEPS - 1) // N_FF] > 0)

    last_f = f == N_FF - 1
    n_pair = nsub // 2
    odd = nsub - 2 * n_pair
    pair_rows = 2 * ROW_TILE

    def in_copy(t, slot):
        src = xs_hbm.at[pl.ds(pl.multiple_of(row0 + t * ROW_TILE, ROW_TILE), ROW_TILE)]
        return pltpu.make_async_copy(src, xstage.at[slot], sem_in.at[slot])

    def out_copy(off, m):
        dst = y_hbm.at[pl.ds(pl.multiple_of(row0 + off, ROW_TILE), m)]
        return pltpu.make_async_copy(acc.at[pl.ds(off, m)], dst, sem_out)

    def cast_piece(dst, q):
        wg_d, wu_d, wd_d = dst
        r_in = pl.ds(pl.multiple_of(q * (D_MODEL // CAST_PIECES), D_MODEL // CAST_PIECES), D_MODEL // CAST_PIECES)
        r_ff = pl.ds(pl.multiple_of(q * (FF_TILE // CAST_PIECES), FF_TILE // CAST_PIECES), FF_TILE // CAST_PIECES)
        wg_d[r_in, :] = wg_ref[r_in, :].astype(BF16)
        wu_d[r_in, :] = wu_ref[r_in, :].astype(BF16)
        wd_d[r_ff, :] = wd_ref[r_ff, :].astype(BF16)

    def compute(cur, off, rows_n):
        wg_c, wu_c, wd_c = cur
        rows = pl.ds(off, rows_n)
        x = xb[rows, :]
        g = jnp.minimum(jnp.dot(x, wg_c[...], preferred_element_type=F32) + bg_ref[...], SWIGLU_LIMIT)
        u = jnp.clip(jnp.dot(x, wu_c[...], preferred_element_type=F32) + bu_ref[...],
                     -SWIGLU_LIMIT, SWIGLU_LIMIT)
        a = ((u + 1.0) * (g * jax.nn.sigmoid(SWIGLU_ALPHA * g))).astype(BF16)
        acc[rows, :] += jnp.dot(a, wd_c[...], preferred_element_type=F32)

    odd_off = pl.multiple_of((nsub - 1) * ROW_TILE, ROW_TILE)

    def load_rows():
        in_copy(0, 0).start()

        def load(t, c):
            slot = t % 2

            @pl.when(t + 1 < nsub)
            def _():
                in_copy(t + 1, 1 - slot).start()
            in_copy(t, slot).wait()
            rows = pl.ds(pl.multiple_of(t * ROW_TILE, ROW_TILE), ROW_TILE)
            xb[rows, :] = xstage[slot].astype(BF16)
            acc[rows, :] = jnp.broadcast_to(bd_ref[...], (ROW_TILE, D_MODEL))
            return c
        lax.fori_loop(0, nsub, load, 0)

    def step(cur, nxt):
        @pl.when((nsub > 0) & (f == 0))
        def _():
            load_rows()

        def pair(p, c):
            off = pl.multiple_of(p * pair_rows, pair_rows)
            compute(cur, off, pair_rows)
            cast_piece(nxt, jnp.minimum(p, CAST_PIECES - 1))

            @pl.when(last_f)
            def _():
                out_copy(off, pair_rows).start()
            return c
        lax.fori_loop(0, n_pair, pair, 0)

        @pl.when(odd == 1)
        def _():
            compute(cur, odd_off, ROW_TILE)

            @pl.when(last_f)
            def _():
                out_copy(odd_off, ROW_TILE).start()

        for q in range(CAST_PIECES):
            @pl.when(cast_needed & (n_pair <= q))
            def _(q=q):
                cast_piece(nxt, q)

        @pl.when((nsub > 0) & last_f)
        def _():
            def drain(p, c):
                out_copy(pl.multiple_of(p * pair_rows, pair_rows), pair_rows).wait()
                return c
            lax.fori_loop(0, n_pair, drain, 0)

            @pl.when(odd == 1)
            def _():
                out_copy(odd_off, ROW_TILE).wait()

    set_a = (wg_a, wu_a, wd_a)
    set_b = (wg_b, wu_b, wd_b)

    @pl.when(n % 2 == 1)
    def _():
        step(set_a, set_b)

    @pl.when(n % 2 == 0)
    def _():
        step(set_b, set_a)

    @pl.when(n == EXPERT_STEPS)
    def _():
        acc[0:ROW_TILE, :] = jnp.zeros((ROW_TILE, D_MODEL), F32)

        def tail_copy(t):
            dst = y_hbm.at[pl.ds(pl.multiple_of(t * ROW_TILE, ROW_TILE), ROW_TILE)]
            return pltpu.make_async_copy(acc.at[0:ROW_TILE], dst, sem_out)

        def tstart(t, c):
            tail_copy(t).start()
            return c

        def twait(t, c):
            tail_copy(t).wait()
            return c

        first_tail = snsub_ref[MAX_SUPER]
        lax.fori_loop(first_tail, N_ROW_TILES, tstart, 0)
        lax.fori_loop(first_tail, N_ROW_TILES, twait, 0)


def _experts(se, srow, snsub, xs, w_gate, b_gate, w_up, b_up, w_down, b_down):
    def expert_and_chunk(m, se, sn):
        s = m // N_FF
        return se[s], jnp.where(sn[s] > 0, m % N_FF, N_FF - 1)

    def w_in_map(n, se, sr, sn):
        e, f = expert_and_chunk(jnp.minimum(n, EXPERT_STEPS - 1), se, sn)
        return e, 0, f

    def w_down_map(n, se, sr, sn):
        e, f = expert_and_chunk(jnp.minimum(n, EXPERT_STEPS - 1), se, sn)
        return e, f, 0

    def b_in_map(n, se, sr, sn):
        e, f = expert_and_chunk(jnp.maximum(n - 1, 0), se, sn)
        return e, 0, f

    def b_down_map(n, se, sr, sn):
        e, _ = expert_and_chunk(jnp.maximum(n - 1, 0), se, sn)
        return e, 0, 0

    return pl.pallas_call(
        _expert_body,
        grid_spec=pltpu.PrefetchScalarGridSpec(
            num_scalar_prefetch=3,
            grid=(EXPERT_STEPS + 1,),
            in_specs=[pl.BlockSpec(memory_space=pl.ANY),
                      pl.BlockSpec((None, D_MODEL, FF_TILE), w_in_map),
                      pl.BlockSpec((None, D_MODEL, FF_TILE), w_in_map),
                      pl.BlockSpec((None, FF_TILE, D_MODEL), w_down_map),
                      pl.BlockSpec((None, 1, FF_TILE), b_in_map),
                      pl.BlockSpec((None, 1, FF_TILE), b_in_map),
                      pl.BlockSpec((None, 1, D_MODEL), b_down_map)],
            out_specs=pl.BlockSpec(memory_space=pl.ANY),
            scratch_shapes=[pltpu.VMEM((2, ROW_TILE, D_MODEL), F32),
                            pltpu.VMEM((SUPER_ROWS, D_MODEL), BF16),
                            pltpu.VMEM((SUPER_ROWS, D_MODEL), F32),
                            pltpu.VMEM((D_MODEL, FF_TILE), BF16),
                            pltpu.VMEM((D_MODEL, FF_TILE), BF16),
                            pltpu.VMEM((FF_TILE, D_MODEL), BF16),
                            pltpu.VMEM((D_MODEL, FF_TILE), BF16),
                            pltpu.VMEM((D_MODEL, FF_TILE), BF16),
                            pltpu.VMEM((FF_TILE, D_MODEL), BF16),
                            pltpu.SemaphoreType.DMA((2,)),
                            pltpu.SemaphoreType.DMA(())]),
        out_shape=jax.ShapeDtypeStruct((PADDED_ROWS, D_MODEL), F32),
        compiler_params=_cparams("arbitrary"),
        name="experts",
    )(se, srow, snsub, xs, w_gate, w_up, w_down,
      b_gate.reshape(N_EXPERTS, 1, D_FF), b_up.reshape(N_EXPERTS, 1, D_FF),
      b_down.reshape(N_EXPERTS, 1, D_MODEL))


def _combine_body(dest_ref, y_hbm, x1_ref, w_ref, mod_ref, fg_ref, o_ref, buf, sem, *, tm):
    i = pl.program_id(0)
    def group(g, carry):
        base = pl.multiple_of(g * ROW_DMA_GROUP, ROW_DMA_GROUP)
        for u in range(ROW_DMA_GROUP):
            for k in range(TOP_K):
                d = dest_ref[(i * tm + base + u) * TOP_K + k]
                pltpu.make_async_copy(y_hbm.at[pl.ds(d, 1)], buf.at[k, pl.ds(base + u, 1)], sem).start()
        return carry
    lax.fori_loop(0, tm // ROW_DMA_GROUP, group, 0)
    for k in range(TOP_K):
        pltpu.make_async_copy(y_hbm.at[pl.ds(0, tm)], buf.at[k], sem).wait()
    moe = buf[0] * w_ref[:, 0:1]
    for k in range(1, TOP_K):
        moe = moe + buf[k] * w_ref[:, k:k + 1]
    x2 = x1_ref[...] + mod_ref[5:6, :] * moe
    o_ref[...] = _rms(x2) * fg_ref[...]


def _combine(dest_flat, y, x1, w_t, mod, final_g):
    tm = 256
    per_batch = SEQ // tm
    return pl.pallas_call(
        functools.partial(_combine_body, tm=tm),
        grid_spec=pltpu.PrefetchScalarGridSpec(
            num_scalar_prefetch=1,
            grid=(TOKENS // tm,),
            in_specs=[pl.BlockSpec(memory_space=pl.ANY),
                      pl.BlockSpec((tm, D_MODEL), lambda i, d: (i, 0)),
                      pl.BlockSpec((tm, TOP_K), lambda i, d: (i, 0)),
                      pl.BlockSpec((None, 6, D_MODEL), lambda i, d: (i // per_batch, 0, 0)),
                      pl.BlockSpec((1, D_MODEL), lambda i, d: (0, 0))],
            out_specs=pl.BlockSpec((tm, D_MODEL), lambda i, d: (i, 0)),
            scratch_shapes=[pltpu.VMEM((TOP_K, tm, D_MODEL), F32),
                            pltpu.SemaphoreType.DMA(())]),
        out_shape=jax.ShapeDtypeStruct((TOKENS, D_MODEL), F32),
        compiler_params=_cparams("arbitrary"),
        name="combine",
    )(dest_flat, y, x1, w_t, mod, final_g)


def _routing_tables(idx, rank, counts):
    padded = (counts + ROW_TILE - 1) // ROW_TILE * ROW_TILE
    padded_end = jnp.cumsum(padded)
    padded_start = padded_end - padded
    e_i = jnp.arange(N_EXPERTS, dtype=I32)[:, None, None]
    start_of = jnp.sum(jnp.where(idx[None] == e_i, padded_start[:, None, None], 0), axis=0)
    dest = (start_of + rank).T.reshape(N_ROWS).astype(I32)

    tiles = padded // ROW_TILE
    n_super = (tiles + SUPER_TILES - 1) // SUPER_TILES
    super_end = jnp.cumsum(n_super)
    super_start = super_end - n_super
    s_i = jnp.arange(MAX_SUPER, dtype=I32)
    total = super_end[-1]
    valid = s_i < total
    e_of = jnp.minimum(jnp.sum((s_i[:, None] >= super_end[None, :]).astype(I32), axis=1), N_EXPERTS - 1)
    last_e = jnp.minimum(jnp.sum((total - 1 >= super_end).astype(I32)), N_EXPERTS - 1)
    local = s_i - super_start[e_of]
    srow = jnp.where(valid, padded_start[e_of] + local * SUPER_ROWS, 0).astype(I32)
    snsub = jnp.where(valid, jnp.minimum(tiles[e_of] - local * SUPER_TILES, SUPER_TILES), 0).astype(I32)
    snsub = jnp.concatenate([snsub, (padded_end[-1:] // ROW_TILE).astype(I32)])
    se = jnp.where(valid, e_of, last_e).astype(I32)
    return dest, padded_end.astype(I32), padded.astype(I32), se, srow, snsub


def kernel(x, c, ada_w, ada_b, norm1_g, w_in, gla_gate_w2, gla_gate_b, sgu_ln_g, sgu_ln_b, sgu_w, sgu_b,
           gla_norm_g, w_branch_a, w_branch_b, w_out, norm2_g, router_w, router_b, exp_w_gate, exp_b_gate,
           exp_w_up, exp_b_up, exp_w_down, exp_b_down, final_g):
    x2d = x.reshape(TOKENS, D_MODEL)
    mod = _ada(c, ada_w[0], ada_b[0])

    w_in0 = w_in[0]
    w_in_t = w_in0.T
    w_glr = jnp.zeros((LANES, D_MODEL), BF16).at[:GLA_GATE_RANK].set(
        w_in_t[GLR_SRC:GLR_SRC + GLA_GATE_RANK].astype(BF16))
    h, glr = _norm1(x2d, mod, norm1_g, w_glr)
    proj = _inproj(h, w_in_t)

    y_a = _sgu(proj, sgu_ln_g, sgu_ln_b, sgu_w[0], sgu_b[0].T)

    w2p = jnp.zeros((LANES, GLA_DK), BF16).at[:GLA_GATE_RANK].set(gla_gate_w2[0].astype(BF16))
    y_b = _gla(proj, glr, w2p, gla_gate_b, gla_norm_g)

    merged = _merge(y_a, y_b, w_branch_a[0].astype(BF16), w_branch_b[0].astype(BF16), proj)

    rw_t = router_w[0].T
    rw_hi = rw_t.astype(BF16)
    rw_lo = (rw_t - rw_hi.astype(F32)).astype(BF16)
    x1, h_rows, idx, top_w, rank, cnt = _outproj(
        merged, x2d, mod, w_out[0].astype(BF16), norm2_g, rw_hi, rw_lo, router_b[0].reshape(N_EXPERTS, 1))

    dest, padded_end, padded, se, srow, snsub = _routing_tables(idx, rank, cnt[:, 0])
    xs = _dispatch(dest, padded_end, padded, h_rows)
    y = _experts(se, srow, snsub, xs, exp_w_gate[0], exp_b_gate[0], exp_w_up[0], exp_b_up[0],
                 exp_w_down[0], exp_b_down[0])
    out = _combine(dest, y, x1, top_w.T, mod, final_g.reshape(1, D_MODEL))
    return out.reshape(BATCH, SEQ, D_MODEL)
```

```python
import functools

import jax
import jax.numpy as jnp
from jax import lax
from jax.experimental import pallas as pl
from jax.experimental.pallas import tpu as pltpu

F32 = jnp.float32
BF16 = jnp.bfloat16
I32 = jnp.int32

D_MODEL = 2048
BATCH = 4
SEQ = 2048
TOKENS = BATCH * SEQ
CHUNK = 64
SPATIAL_BLOCK = 128
A_GROUPS = 8
A_GROUP_DIM = D_MODEL // A_GROUPS
GLA_HEADS = 4
GLA_DK = D_MODEL // 2
GLA_HEAD_K = GLA_DK // GLA_HEADS
GLA_HEAD_V = D_MODEL // GLA_HEADS
GLA_GATE_RANK = 16
GLA_GATE_TAU = 16.0
N_EXPERTS = 32
TOP_K = 4
D_FF = D_MODEL
SWIGLU_LIMIT = 7.0
SWIGLU_ALPHA = 1.702
NORM_EPS = 1e-6

LANES = 128
VMEM_LIMIT = 56 * 1024 * 1024

COL_U, COL_V, COL_Q, COL_K, COL_VV, COL_R = 0, 2048, 4096, 5120, 6144, 8192
COL_GA, COL_GB = 10240, 12288
PROJ_W = 14336
GLR_SRC = 10240

ROW_TILE = 256
SUPER_TILES = 6
SUPER_ROWS = ROW_TILE * SUPER_TILES
N_ROWS = TOKENS * TOP_K
N_ROW_TILES = (N_ROWS + N_EXPERTS * (ROW_TILE - 1)) // ROW_TILE
PADDED_ROWS = N_ROW_TILES * ROW_TILE
MAX_SUPER = -(-(N_ROW_TILES + N_EXPERTS * (SUPER_TILES - 1)) // SUPER_TILES)
FF_TILE = 512
N_FF = D_FF // FF_TILE
ROW_DMA_GROUP = 8


def _cparams(*sem):
    return pltpu.CompilerParams(dimension_semantics=sem, vmem_limit_bytes=VMEM_LIMIT)


def _rms(x):
    return x * lax.rsqrt(jnp.mean(x * x, axis=-1, keepdims=True) + NORM_EPS)


def _ada_body(c_ref, w_ref, b_ref, o_ref):
    c = c_ref[...]
    cond = c * jax.nn.sigmoid(c)
    o_ref[...] = jnp.dot(cond.astype(BF16), w_ref[...].astype(BF16),
                         preferred_element_type=F32) + b_ref[...]


def _ada(c, ada_w, ada_b):
    tn = 1024
    cp = jnp.zeros((8, D_MODEL), F32).at[:BATCH].set(c)
    out = pl.pallas_call(
        _ada_body,
        grid=(6 * D_MODEL // tn,),
        in_specs=[pl.BlockSpec((8, D_MODEL), lambda j: (0, 0)),
                  pl.BlockSpec((D_MODEL, tn), lambda j: (0, j)),
                  pl.BlockSpec((1, tn), lambda j: (0, j))],
        out_specs=pl.BlockSpec((8, tn), lambda j: (0, j)),
        out_shape=jax.ShapeDtypeStruct((8, 6 * D_MODEL), F32),
        compiler_params=_cparams("arbitrary"),
        name="ada",
    )(cp, ada_w, ada_b.reshape(1, 6 * D_MODEL))
    return out[:BATCH].reshape(BATCH, 6, D_MODEL)


def _norm1_body(x_ref, mod_ref, g_ref, wglr_ref, h_ref, glr_ref):
    h = _rms(x_ref[...]) * g_ref[...] * (1.0 + mod_ref[1:2, :]) + mod_ref[0:1, :]
    hb = h.astype(BF16)
    h_ref[...] = hb
    glr_ref[...] = _dot_nt(hb, wglr_ref[...])


def _norm1(x2d, mod, norm1_g, w_glr):
    tm = 512
    per_batch = SEQ // tm
    return pl.pallas_call(
        _norm1_body,
        grid=(TOKENS // tm,),
        in_specs=[pl.BlockSpec((tm, D_MODEL), lambda i: (i, 0)),
                  pl.BlockSpec((None, 6, D_MODEL), lambda i: (i // per_batch, 0, 0)),
                  pl.BlockSpec((1, D_MODEL), lambda i: (0, 0)),
                  pl.BlockSpec((LANES, D_MODEL), lambda i: (0, 0))],
        out_specs=[pl.BlockSpec((tm, D_MODEL), lambda i: (i, 0)),
                   pl.BlockSpec((tm, LANES), lambda i: (i, 0))],
        out_shape=[jax.ShapeDtypeStruct((TOKENS, D_MODEL), BF16),
                   jax.ShapeDtypeStruct((TOKENS, LANES), F32)],
        compiler_params=_cparams("arbitrary"),
        name="norm1",
    )(x2d, mod, norm1_g, w_glr)


INPROJ_TN = 1024
INPROJ_MAIN_TILES = COL_GA // INPROJ_TN
MXU_N = 256


def _inproj_body(h_ref, wt_ref, proj_ref, wb_ref):
    j = pl.program_id(0)

    @pl.when(pl.program_id(1) == 0)
    def _():
        wb_ref[...] = wt_ref[...].astype(BF16)

    def run(act):
        for n in range(INPROJ_TN // MXU_N):
            cols = slice(n * MXU_N, (n + 1) * MXU_N)
            acc = _dot_nt(h_ref[...], wb_ref[cols, :])
            proj_ref[:, cols] = act(acc).astype(BF16)

    col = j * INPROJ_TN

    @pl.when(col < COL_Q)
    def _():
        run(jax.nn.gelu)

    @pl.when((col >= COL_Q) & (col < COL_R))
    def _():
        run(lambda a: a)

    @pl.when((col >= COL_R) & (col < COL_GA))
    def _():
        run(lambda a: a * jax.nn.sigmoid(a))

    @pl.when(col >= COL_GA)
    def _():
        run(jax.nn.sigmoid)


def _inproj(h, w_in_t):
    tm, tn = 1024, INPROJ_TN

    def w_row(j, i):
        return (pl.multiple_of(j * tn + jnp.where(j >= INPROJ_MAIN_TILES, GLA_GATE_RANK, 0), GLA_GATE_RANK), 0)

    return pl.pallas_call(
        _inproj_body,
        grid=(PROJ_W // tn, TOKENS // tm),
        in_specs=[pl.BlockSpec((tm, D_MODEL), lambda j, i: (i, 0)),
                  pl.BlockSpec((pl.Element(tn), pl.Element(D_MODEL)), w_row)],
        out_specs=pl.BlockSpec((tm, tn), lambda j, i: (i, j)),
        out_shape=jax.ShapeDtypeStruct((TOKENS, PROJ_W), BF16),
        scratch_shapes=[pltpu.VMEM((tn, D_MODEL), BF16)],
        compiler_params=_cparams("arbitrary", "arbitrary"),
        name="inproj",
    )(h, w_in_t)


def _sgu_body(u_ref, v_ref, lg_ref, lb_ref, ws_ref, bs_ref, o_ref):
    v = v_ref[...].astype(F32)
    mu = jnp.mean(v, axis=-1, keepdims=True)
    xc = v - mu
    var = jnp.mean(xc * xc, axis=-1, keepdims=True)
    vn = (xc * lax.rsqrt(var + NORM_EPS) * lg_ref[...] + lb_ref[...]).astype(BF16)
    t_chunk = lax.broadcasted_iota(I32, (SPATIAL_BLOCK, SPATIAL_BLOCK), 0) // CHUNK
    s_chunk = lax.broadcasted_iota(I32, (SPATIAL_BLOCK, SPATIAL_BLOCK), 1) // CHUNK
    mask = s_chunk <= t_chunk
    for g in range(A_GROUPS):
        cols = slice(g * A_GROUP_DIM, (g + 1) * A_GROUP_DIM)
        w = jnp.where(mask, ws_ref[g], 0.0).astype(BF16)
        mixed = jnp.dot(w, vn[:, cols], preferred_element_type=F32) + bs_ref[:, g:g + 1]
        o_ref[:, cols] = (u_ref[:, cols].astype(F32) * mixed).astype(BF16)


def _sgu(proj, ln_g, ln_b, w_s, b_s_t):
    nblk = TOKENS // SPATIAL_BLOCK
    wb = D_MODEL
    return pl.pallas_call(
        _sgu_body,
        grid=(nblk,),
        in_specs=[pl.BlockSpec((SPATIAL_BLOCK, wb), lambda i: (i, COL_U // wb)),
                  pl.BlockSpec((SPATIAL_BLOCK, wb), lambda i: (i, COL_V // wb)),
                  pl.BlockSpec((1, wb), lambda i: (0, 0)),
                  pl.BlockSpec((1, wb), lambda i: (0, 0)),
                  pl.BlockSpec((A_GROUPS, SPATIAL_BLOCK, SPATIAL_BLOCK), lambda i: (0, 0, 0)),
                  pl.BlockSpec((SPATIAL_BLOCK, A_GROUPS), lambda i: (0, 0))],
        out_specs=pl.BlockSpec((SPATIAL_BLOCK, wb), lambda i: (i, 0)),
        out_shape=jax.ShapeDtypeStruct((TOKENS, wb), BF16),
        compiler_params=_cparams("arbitrary"),
        name="sgu",
    )(proj, proj, ln_g, ln_b, w_s, b_s_t)


def _dot_nt(a, b):
    return lax.dot_general(a, b, (((1,), (1,)), ((), ())), preferred_element_type=F32)


def _dot_tn(a, b):
    return lax.dot_general(a, b, (((0,), (0,)), ((), ())), preferred_element_type=F32)


def _gla_body(q_ref, k_ref, v_ref, r_ref, glr_ref, w2_ref, gb_ref, ng_ref, o_ref, st_ref, la_ref, *, rows_blk):
    @pl.when(pl.program_id(1) == 0)
    def _():
        st_ref[...] = jnp.zeros_like(st_ref)

    z = jnp.dot(glr_ref[...].astype(BF16), w2_ref[...], preferred_element_type=F32) + gb_ref[...]
    la_ref[...] = jax.nn.log_sigmoid(z) / GLA_GATE_TAU

    r_i = lax.broadcasted_iota(I32, (CHUNK, CHUNK), 0)
    c_i = lax.broadcasted_iota(I32, (CHUNK, CHUNK), 1)
    causal = c_i <= r_i
    tril = causal.astype(BF16)
    scale = GLA_HEAD_K ** -0.5

    def chunk(c, carry):
        rows = pl.ds(pl.multiple_of(c * CHUNK, CHUNK), CHUNK)
        for h in range(GLA_HEADS):
            kc = slice(h * GLA_HEAD_K, (h + 1) * GLA_HEAD_K)
            vc = slice(h * GLA_HEAD_V, (h + 1) * GLA_HEAD_V)
            la = la_ref[rows, kc]
            hi = la.astype(BF16)
            r1 = la - hi.astype(F32)
            mid = r1.astype(BF16)
            lo = (r1 - mid.astype(F32)).astype(BF16)
            b = (jnp.dot(tril, hi, preferred_element_type=F32)
                 + jnp.dot(tril, mid, preferred_element_type=F32)
                 + jnp.dot(tril, lo, preferred_element_type=F32))
            b_last = b[CHUNK - 1:CHUNK, :]
            q = q_ref[rows, kc].astype(F32) * scale
            k = k_ref[rows, kc].astype(F32)
            v = v_ref[rows, vc]
            q_dec = (q * jnp.exp(b)).astype(BF16)
            k_intra = (k * jnp.exp(-b)).astype(BF16)
            k_state = (k * jnp.exp(b_last - b)).astype(BF16)
            att = jnp.where(causal, _dot_nt(q_dec, k_intra), 0.0).astype(BF16)
            st = st_ref[h]
            o = jnp.dot(att, v, preferred_element_type=F32) + _dot_nt(q_dec, st.astype(BF16))
            st_ref[h] = st * jnp.exp(b_last) + _dot_tn(v, k_state)
            on = _rms(o) * ng_ref[...]
            o_ref[rows, vc] = (on * r_ref[rows, vc].astype(F32)).astype(BF16)
        return carry

    lax.fori_loop(0, rows_blk // CHUNK, chunk, 0)


def _gla(proj, glr, w2p, gate_b, norm_g):
    rows_blk = 512
    nblk = SEQ // rows_blk
    dk, dv = GLA_DK, D_MODEL

    def row(b, n):
        return b * nblk + n

    return pl.pallas_call(
        functools.partial(_gla_body, rows_blk=rows_blk),
        grid=(BATCH, nblk),
        in_specs=[pl.BlockSpec((rows_blk, dk), lambda b, n: (row(b, n), COL_Q // dk)),
                  pl.BlockSpec((rows_blk, dk), lambda b, n: (row(b, n), COL_K // dk)),
                  pl.BlockSpec((rows_blk, dv), lambda b, n: (row(b, n), COL_VV // dv)),
                  pl.BlockSpec((rows_blk, dv), lambda b, n: (row(b, n), COL_R // dv)),
                  pl.BlockSpec((rows_blk, LANES), lambda b, n: (row(b, n), 0)),
                  pl.BlockSpec((LANES, dk), lambda b, n: (0, 0)),
                  pl.BlockSpec((1, dk), lambda b, n: (0, 0)),
                  pl.BlockSpec((1, GLA_HEAD_V), lambda b, n: (0, 0))],
        out_specs=pl.BlockSpec((rows_blk, dv), lambda b, n: (row(b, n), 0)),
        out_shape=jax.ShapeDtypeStruct((TOKENS, D_MODEL), BF16),
        scratch_shapes=[pltpu.VMEM((GLA_HEADS, GLA_HEAD_V, GLA_HEAD_K), F32),
                        pltpu.VMEM((rows_blk, dk), F32)],
        compiler_params=_cparams("arbitrary", "arbitrary"),
        name="gla",
    )(proj, proj, proj, proj, glr, w2p, gate_b, norm_g)


def _merge_body(ya_ref, yb_ref, wa_ref, wb_ref, ga_ref, gb_ref, o_ref):
    a = jnp.dot(ya_ref[...], wa_ref[...], preferred_element_type=F32)
    b = jnp.dot(yb_ref[...], wb_ref[...], preferred_element_type=F32)
    o_ref[...] = (ga_ref[...].astype(F32) * a + gb_ref[...].astype(F32) * b).astype(BF16)


def _merge(y_a, y_b, wa, wb, proj):
    tm, tn = 512, 1024
    return pl.pallas_call(
        _merge_body,
        grid=(TOKENS // tm, D_MODEL // tn),
        in_specs=[pl.BlockSpec((tm, D_MODEL), lambda i, j: (i, 0)),
                  pl.BlockSpec((tm, D_MODEL), lambda i, j: (i, 0)),
                  pl.BlockSpec((D_MODEL, tn), lambda i, j: (0, j)),
                  pl.BlockSpec((D_MODEL, tn), lambda i, j: (0, j)),
                  pl.BlockSpec((tm, tn), lambda i, j: (i, COL_GA // tn + j)),
                  pl.BlockSpec((tm, tn), lambda i, j: (i, COL_GB // tn + j))],
        out_specs=pl.BlockSpec((tm, tn), lambda i, j: (i, j)),
        out_shape=jax.ShapeDtypeStruct((TOKENS, D_MODEL), BF16),
        compiler_params=_cparams("arbitrary", "arbitrary"),
        name="merge",
    )(y_a, y_b, wa, wb, proj, proj)


def _outproj_body(m_ref, x_ref, mod_ref, wo_ref, g2_ref, rwh_ref, rwl_ref, rb_ref,
                  x1_ref, hp_ref, idx_ref, w_ref, rank_ref, cnt_ref, run_ref, *, tm):
    @pl.when(pl.program_id(0) == 0)
    def _():
        run_ref[...] = jnp.zeros_like(run_ref)

    y = jnp.dot(m_ref[...], wo_ref[...], preferred_element_type=F32)
    x1 = x_ref[...] + mod_ref[2:3, :] * y
    x1_ref[...] = x1
    h2 = _rms(x1) * g2_ref[...] * (1.0 + mod_ref[4:5, :]) + mod_ref[3:4, :]
    hp_ref[...] = h2
    hb = h2.astype(BF16)
    hf = hb.astype(F32)

    h_lo = (h2 - hf).astype(BF16)
    logits = (_dot_nt(rwh_ref[...], hb) + _dot_nt(rwh_ref[...], h_lo) + _dot_nt(rwl_ref[...], hb)
              + rb_ref[...])
    e_i = lax.broadcasted_iota(I32, (N_EXPERTS, tm), 0).astype(F32)
    vals, idxs = [], []
    l = logits
    for _ in range(TOP_K):
        m = jnp.max(l, axis=0, keepdims=True)
        i = jnp.min(jnp.where(l == m, e_i, float(N_EXPERTS)), axis=0, keepdims=True)
        vals.append(m)
        idxs.append(i)
        l = jnp.where(e_i == i, -jnp.inf, l)
    ex = [jnp.exp(v - vals[0]) for v in vals]
    den = ex[0] + ex[1] + ex[2] + ex[3]
    before = (lax.broadcasted_iota(I32, (tm, tm), 0) < lax.broadcasted_iota(I32, (tm, tm), 1)).astype(BF16)
    run = run_ref[...]
    for k in range(TOP_K):
        onehot = e_i == idxs[k]
        pref = jnp.dot(onehot.astype(BF16), before, preferred_element_type=F32) + run[:, 0:1]
        rank_ref[k:k + 1, :] = jnp.sum(jnp.where(onehot, pref, 0.0), axis=0, keepdims=True).astype(I32)
        run = run + jnp.sum(onehot.astype(F32), axis=1, keepdims=True)
        idx_ref[k:k + 1, :] = idxs[k].astype(I32)
        w_ref[k:k + 1, :] = ex[k] / den
    run_ref[...] = run
    cnt_ref[...] = run.astype(I32)


def _outproj(merged, x2d, mod, w_out, norm2_g, rw_hi, rw_lo, router_b):
    tm = 512
    per_batch = SEQ // tm
    body = functools.partial(_outproj_body, tm=tm)
    return pl.pallas_call(
        body,
        grid=(TOKENS // tm,),
        in_specs=[pl.BlockSpec((tm, D_MODEL), lambda i: (i, 0)),
                  pl.BlockSpec((tm, D_MODEL), lambda i: (i, 0)),
                  pl.BlockSpec((None, 6, D_MODEL), lambda i: (i // per_batch, 0, 0)),
                  pl.BlockSpec((D_MODEL, D_MODEL), lambda i: (0, 0)),
                  pl.BlockSpec((1, D_MODEL), lambda i: (0, 0)),
                  pl.BlockSpec((N_EXPERTS, D_MODEL), lambda i: (0, 0)),
                  pl.BlockSpec((N_EXPERTS, D_MODEL), lambda i: (0, 0)),
                  pl.BlockSpec((N_EXPERTS, 1), lambda i: (0, 0))],
        out_specs=[pl.BlockSpec((tm, D_MODEL), lambda i: (i, 0)),
                   pl.BlockSpec((tm, D_MODEL), lambda i: (i, 0)),
                   pl.BlockSpec((TOP_K, tm), lambda i: (0, i)),
                   pl.BlockSpec((TOP_K, tm), lambda i: (0, i)),
                   pl.BlockSpec((TOP_K, tm), lambda i: (0, i)),
                   pl.BlockSpec((N_EXPERTS, LANES), lambda i: (0, 0))],
        out_shape=[jax.ShapeDtypeStruct((TOKENS, D_MODEL), F32),
                   jax.ShapeDtypeStruct((TOKENS, D_MODEL), F32),
                   jax.ShapeDtypeStruct((TOP_K, TOKENS), I32),
                   jax.ShapeDtypeStruct((TOP_K, TOKENS), F32),
                   jax.ShapeDtypeStruct((TOP_K, TOKENS), I32),
                   jax.ShapeDtypeStruct((N_EXPERTS, LANES), I32)],
        scratch_shapes=[pltpu.VMEM((N_EXPERTS, LANES), F32)],
        compiler_params=_cparams("arbitrary"),
        name="outproj",
    )(merged, x2d, mod, w_out, norm2_g, rw_hi, rw_lo, router_b)


def _dispatch_body(dest_ref, pend_ref, padded_ref, h_ref, xs_hbm, zero_ref, zsem, sem, *, tm):
    i = pl.program_id(0)

    @pl.when(i == 0)
    def _():
        _dispatch_clear(pend_ref, padded_ref, xs_hbm, zero_ref, zsem)

    def group(g, carry):
        base = pl.multiple_of(g * ROW_DMA_GROUP, ROW_DMA_GROUP)
        for u in range(ROW_DMA_GROUP):
            src = h_ref.at[pl.ds(base + u, 1)]
            for k in range(TOP_K):
                d = dest_ref[(i * tm + base + u) * TOP_K + k]
                pltpu.make_async_copy(src, xs_hbm.at[pl.ds(d, 1)], sem).start()
        return carry
    lax.fori_loop(0, tm // ROW_DMA_GROUP, group, 0)
    for k in range(TOP_K):
        pltpu.make_async_copy(h_ref, xs_hbm.at[pl.ds(0, tm)], sem).wait()


def _dispatch_clear(pend_ref, padded_ref, xs_hbm, zero_ref, zsem):
    zero_ref[...] = jnp.zeros_like(zero_ref)

    def zero_copy(e):
        return pltpu.make_async_copy(
            zero_ref, xs_hbm.at[pl.ds(pl.multiple_of(pend_ref[e] - ROW_TILE, ROW_TILE), ROW_TILE)], zsem)

    def zstart(e, c):
        @pl.when(padded_ref[e] > 0)
        def _():
            zero_copy(e).start()
        return c

    def zwait(e, c):
        @pl.when(padded_ref[e] > 0)
        def _():
            zero_copy(e).wait()
        return c

    def tail_copy(t):
        return pltpu.make_async_copy(
            zero_ref, xs_hbm.at[pl.ds(pl.multiple_of(t * ROW_TILE, ROW_TILE), ROW_TILE)], zsem)

    def tstart(t, c):
        tail_copy(t).start()
        return c

    def twait(t, c):
        tail_copy(t).wait()
        return c

    first_tail = pend_ref[N_EXPERTS - 1] // ROW_TILE
    lax.fori_loop(0, N_EXPERTS, zstart, 0)
    lax.fori_loop(first_tail, N_ROW_TILES, tstart, 0)
    lax.fori_loop(0, N_EXPERTS, zwait, 0)
    lax.fori_loop(first_tail, N_ROW_TILES, twait, 0)


def _dispatch(dest_flat, padded_end, padded, h_rows):
    tm = 256
    return pl.pallas_call(
        functools.partial(_dispatch_body, tm=tm),
        grid_spec=pltpu.PrefetchScalarGridSpec(
            num_scalar_prefetch=3,
            grid=(TOKENS // tm,),
            in_specs=[pl.BlockSpec((tm, D_MODEL), lambda i, d, pe, pd: (i, 0))],
            out_specs=pl.BlockSpec(memory_space=pl.ANY),
            scratch_shapes=[pltpu.VMEM((ROW_TILE, D_MODEL), F32),
                            pltpu.SemaphoreType.DMA(()),
                            pltpu.SemaphoreType.DMA(())]),
        out_shape=jax.ShapeDtypeStruct((PADDED_ROWS, D_MODEL), F32),
        compiler_params=_cparams("arbitrary"),
        name="dispatch",
    )(dest_flat, padded_end, padded, h_rows)


def _expert_body(se_ref, srow_ref, snsub_ref, xs_hbm, wg_hbm, wu_hbm, wd_hbm, bg_ref, bu_ref, bd_ref,
                 y_hbm, xstage, xb, acc, wg_f, wu_f, wd_f, wg_c, wu_c, wd_c, sem_in, sem_out, sem_w):
    s = pl.program_id(0)
    f = pl.program_id(1)
    nsub = snsub_ref[s]
    row0 = srow_ref[s]

    def weight_copies(s_, f_):
        e = se_ref[s_]
        chunk = pl.ds(pl.multiple_of(f_ * FF_TILE, FF_TILE), FF_TILE)
        return (pltpu.make_async_copy(wg_hbm.at[e, :, chunk], wg_f, sem_w.at[0]),
                pltpu.make_async_copy(wu_hbm.at[e, :, chunk], wu_f, sem_w.at[1]),
                pltpu.make_async_copy(wd_hbm.at[e, chunk, :], wd_f, sem_w.at[2]))

    wrap = f == N_FF - 1
    s_next = jnp.minimum(jnp.where(wrap, s + 1, s), MAX_SUPER - 1)
    f_next = jnp.where(wrap, 0, f + 1)
    has_next = jnp.logical_not(wrap & (s == MAX_SUPER - 1)) & (snsub_ref[s_next] > 0)

    last_f = f == N_FF - 1
    n_pair = nsub // 2
    odd = nsub - 2 * n_pair
    pair_rows = 2 * ROW_TILE

    def in_copy(t, slot):
        src = xs_hbm.at[pl.ds(pl.multiple_of(row0 + t * ROW_TILE, ROW_TILE), ROW_TILE)]
        return pltpu.make_async_copy(src, xstage.at[slot], sem_in.at[slot])

    def out_copy(off, m):
        dst = y_hbm.at[pl.ds(pl.multiple_of(row0 + off, ROW_TILE), m)]
        return pltpu.make_async_copy(acc.at[pl.ds(off, m)], dst, sem_out)

    def compute(off, rows_n):
        rows = pl.ds(off, rows_n)
        x = xb[rows, :]
        g = jnp.minimum(jnp.dot(x, wg_c[...], preferred_element_type=F32) + bg_ref[...], SWIGLU_LIMIT)
        u = jnp.clip(jnp.dot(x, wu_c[...], preferred_element_type=F32) + bu_ref[...],
                     -SWIGLU_LIMIT, SWIGLU_LIMIT)
        a = ((u + 1.0) * (g * jax.nn.sigmoid(SWIGLU_ALPHA * g))).astype(BF16)
        acc[rows, :] += jnp.dot(a, wd_c[...], preferred_element_type=F32)

    odd_off = pl.multiple_of((nsub - 1) * ROW_TILE, ROW_TILE)

    def load_rows():
        in_copy(0, 0).start()

        def load(t, c):
            slot = t % 2

            @pl.when(t + 1 < nsub)
            def _():
                in_copy(t + 1, 1 - slot).start()
            in_copy(t, slot).wait()
            rows = pl.ds(pl.multiple_of(t * ROW_TILE, ROW_TILE), ROW_TILE)
            xb[rows, :] = xstage[slot].astype(BF16)
            acc[rows, :] = jnp.broadcast_to(bd_ref[...], (ROW_TILE, D_MODEL))
            return c
        lax.fori_loop(0, nsub, load, 0)

    @pl.when(nsub > 0)
    def _():
        @pl.when((s == 0) & (f == 0))
        def _():
            for cp in weight_copies(s, f):
                cp.start()

        copies = weight_copies(s, f)
        nxt = weight_copies(s_next, f_next)
        for cp, nx, stage, dst in zip(copies, nxt, (wg_f, wu_f, wd_f), (wg_c, wu_c, wd_c)):
            cp.wait()
            dst[...] = stage[...].astype(BF16)

            @pl.when(has_next)
            def _(nx=nx):
                nx.start()

        @pl.when(f == 0)
        def _():
            load_rows()

        def pair(p, c):
            off = pl.multiple_of(p * pair_rows, pair_rows)
            compute(off, pair_rows)

            @pl.when(last_f)
            def _():
                out_copy(off, pair_rows).start()
            return c
        lax.fori_loop(0, n_pair, pair, 0)

        @pl.when(odd == 1)
        def _():
            compute(odd_off, ROW_TILE)

            @pl.when(last_f)
            def _():
                out_copy(odd_off, ROW_TILE).start()

        @pl.when(last_f)
        def _():
            def drain(p, c):
                out_copy(pl.multiple_of(p * pair_rows, pair_rows), pair_rows).wait()
                return c
            lax.fori_loop(0, n_pair, drain, 0)

            @pl.when(odd == 1)
            def _():
                out_copy(odd_off, ROW_TILE).wait()

    @pl.when((s == MAX_SUPER - 1) & last_f)
    def _():
        acc[0:ROW_TILE, :] = jnp.zeros((ROW_TILE, D_MODEL), F32)

        def tail_copy(t):
            dst = y_hbm.at[pl.ds(pl.multiple_of(t * ROW_TILE, ROW_TILE), ROW_TILE)]
            return pltpu.make_async_copy(acc.at[0:ROW_TILE], dst, sem_out)

        def tstart(t, c):
            tail_copy(t).start()
            return c

        def twait(t, c):
            tail_copy(t).wait()
            return c

        first_tail = snsub_ref[MAX_SUPER]
        lax.fori_loop(first_tail, N_ROW_TILES, tstart, 0)
        lax.fori_loop(first_tail, N_ROW_TILES, twait, 0)


def _experts(se, srow, snsub, xs, w_gate, b_gate, w_up, b_up, w_down, b_down):
    def f_eff(s, f, sn):
        return jnp.where(sn[s] > 0, f, N_FF - 1)

    return pl.pallas_call(
        _expert_body,
        grid_spec=pltpu.PrefetchScalarGridSpec(
            num_scalar_prefetch=3,
            grid=(MAX_SUPER, N_FF),
            in_specs=[pl.BlockSpec(memory_space=pl.ANY),
                      pl.BlockSpec(memory_space=pl.ANY),
                      pl.BlockSpec(memory_space=pl.ANY),
                      pl.BlockSpec(memory_space=pl.ANY),
                      pl.BlockSpec((None, 1, FF_TILE), lambda s, f, se, sr, sn: (se[s], 0, f_eff(s, f, sn))),
                      pl.BlockSpec((None, 1, FF_TILE), lambda s, f, se, sr, sn: (se[s], 0, f_eff(s, f, sn))),
                      pl.BlockSpec((None, 1, D_MODEL), lambda s, f, se, sr, sn: (se[s], 0, 0))],
            out_specs=pl.BlockSpec(memory_space=pl.ANY),
            scratch_shapes=[pltpu.VMEM((2, ROW_TILE, D_MODEL), F32),
                            pltpu.VMEM((SUPER_ROWS, D_MODEL), BF16),
                            pltpu.VMEM((SUPER_ROWS, D_MODEL), F32),
                            pltpu.VMEM((D_MODEL, FF_TILE), F32),
                            pltpu.VMEM((D_MODEL, FF_TILE), F32),
                            pltpu.VMEM((FF_TILE, D_MODEL), F32),
                            pltpu.VMEM((D_MODEL, FF_TILE), BF16),
                            pltpu.VMEM((D_MODEL, FF_TILE), BF16),
                            pltpu.VMEM((FF_TILE, D_MODEL), BF16),
                            pltpu.SemaphoreType.DMA((2,)),
                            pltpu.SemaphoreType.DMA(()),
                            pltpu.SemaphoreType.DMA((3,))]),
        out_shape=jax.ShapeDtypeStruct((PADDED_ROWS, D_MODEL), F32),
        compiler_params=_cparams("arbitrary", "arbitrary"),
        name="experts",
    )(se, srow, snsub, xs, w_gate, w_up, w_down,
      b_gate.reshape(N_EXPERTS, 1, D_FF), b_up.reshape(N_EXPERTS, 1, D_FF),
      b_down.reshape(N_EXPERTS, 1, D_MODEL))


def _combine_body(dest_ref, y_hbm, x1_ref, w_ref, mod_ref, fg_ref, o_ref, buf, sem, *, tm):
    i = pl.program_id(0)
    def group(g, carry):
        base = pl.multiple_of(g * ROW_DMA_GROUP, ROW_DMA_GROUP)
        for u in range(ROW_DMA_GROUP):
            for k in range(TOP_K):
                d = dest_ref[(i * tm + base + u) * TOP_K + k]
                pltpu.make_async_copy(y_hbm.at[pl.ds(d, 1)], buf.at[k, pl.ds(base + u, 1)], sem).start()
        return carry
    lax.fori_loop(0, tm // ROW_DMA_GROUP, group, 0)
    for k in range(TOP_K):
        pltpu.make_async_copy(y_hbm.at[pl.ds(0, tm)], buf.at[k], sem).wait()
    moe = buf[0] * w_ref[:, 0:1]
    for k in range(1, TOP_K):
        moe = moe + buf[k] * w_ref[:, k:k + 1]
    x2 = x1_ref[...] + mod_ref[5:6, :] * moe
    o_ref[...] = _rms(x2) * fg_ref[...]


def _combine(dest_flat, y, x1, w_t, mod, final_g):
    tm = 256
    per_batch = SEQ // tm
    return pl.pallas_call(
        functools.partial(_combine_body, tm=tm),
        grid_spec=pltpu.PrefetchScalarGridSpec(
            num_scalar_prefetch=1,
            grid=(TOKENS // tm,),
            in_specs=[pl.BlockSpec(memory_space=pl.ANY),
                      pl.BlockSpec((tm, D_MODEL), lambda i, d: (i, 0)),
                      pl.BlockSpec((tm, TOP_K), lambda i, d: (i, 0)),
                      pl.BlockSpec((None, 6, D_MODEL), lambda i, d: (i // per_batch, 0, 0)),
                      pl.BlockSpec((1, D_MODEL), lambda i, d: (0, 0))],
            out_specs=pl.BlockSpec((tm, D_MODEL), lambda i, d: (i, 0)),
            scratch_shapes=[pltpu.VMEM((TOP_K, tm, D_MODEL), F32),
                            pltpu.SemaphoreType.DMA(())]),
        out_shape=jax.ShapeDtypeStruct((TOKENS, D_MODEL), F32),
        compiler_params=_cparams("arbitrary"),
        name="combine",
    )(dest_flat, y, x1, w_t, mod, final_g)


def _routing_tables(idx, rank, counts):
    padded = (counts + ROW_TILE - 1) // ROW_TILE * ROW_TILE
    padded_end = jnp.cumsum(padded)
    padded_start = padded_end - padded
    e_i = jnp.arange(N_EXPERTS, dtype=I32)[:, None, None]
    start_of = jnp.sum(jnp.where(idx[None] == e_i, padded_start[:, None, None], 0), axis=0)
    dest = (start_of + rank).T.reshape(N_ROWS).astype(I32)

    tiles = padded // ROW_TILE
    n_super = (tiles + SUPER_TILES - 1) // SUPER_TILES
    super_end = jnp.cumsum(n_super)
    super_start = super_end - n_super
    s_i = jnp.arange(MAX_SUPER, dtype=I32)
    total = super_end[-1]
    valid = s_i < total
    e_of = jnp.minimum(jnp.sum((s_i[:, None] >= super_end[None, :]).astype(I32), axis=1), N_EXPERTS - 1)
    last_e = jnp.minimum(jnp.sum((total - 1 >= super_end).astype(I32)), N_EXPERTS - 1)
    local = s_i - super_start[e_of]
    srow = jnp.where(valid, padded_start[e_of] + local * SUPER_ROWS, 0).astype(I32)
    snsub = jnp.where(valid, jnp.minimum(tiles[e_of] - local * SUPER_TILES, SUPER_TILES), 0).astype(I32)
    snsub = jnp.concatenate([snsub, (padded_end[-1:] // ROW_TILE).astype(I32)])
    se = jnp.where(valid, e_of, last_e).astype(I32)
    return dest, padded_end.astype(I32), padded.astype(I32), se, srow, snsub


def kernel(x, c, ada_w, ada_b, norm1_g, w_in, gla_gate_w2, gla_gate_b, sgu_ln_g, sgu_ln_b, sgu_w, sgu_b,
           gla_norm_g, w_branch_a, w_branch_b, w_out, norm2_g, router_w, router_b, exp_w_gate, exp_b_gate,
           exp_w_up, exp_b_up, exp_w_down, exp_b_down, final_g):
    x2d = x.reshape(TOKENS, D_MODEL)
    mod = _ada(c, ada_w[0], ada_b[0])

    w_in0 = w_in[0]
    w_in_t = w_in0.T
    w_glr = jnp.zeros((LANES, D_MODEL), BF16).at[:GLA_GATE_RANK].set(
        w_in_t[GLR_SRC:GLR_SRC + GLA_GATE_RANK].astype(BF16))
    h, glr = _norm1(x2d, mod, norm1_g, w_glr)
    proj = _inproj(h, w_in_t)

    y_a = _sgu(proj, sgu_ln_g, sgu_ln_b, sgu_w[0], sgu_b[0].T)

    w2p = jnp.zeros((LANES, GLA_DK), BF16).at[:GLA_GATE_RANK].set(gla_gate_w2[0].astype(BF16))
    y_b = _gla(proj, glr, w2p, gla_gate_b, gla_norm_g)

    merged = _merge(y_a, y_b, w_branch_a[0].astype(BF16), w_branch_b[0].astype(BF16), proj)

    rw_t = router_w[0].T
    rw_hi = rw_t.astype(BF16)
    rw_lo = (rw_t - rw_hi.astype(F32)).astype(BF16)
    x1, h_rows, idx, top_w, rank, cnt = _outproj(
        merged, x2d, mod, w_out[0].astype(BF16), norm2_g, rw_hi, rw_lo, router_b[0].reshape(N_EXPERTS, 1))

    dest, padded_end, padded, se, srow, snsub = _routing_tables(idx, rank, cnt[:, 0])
    xs = _dispatch(dest, padded_end, padded, h_rows)
    y = _experts(se, srow, snsub, xs, exp_w_gate[0], exp_b_gate[0], exp_w_up[0], exp_b_up[0],
                 exp_w_down[0], exp_b_down[0])
    out = _combine(dest, y, x1, top_w.T, mod, final_g.reshape(1, D_MODEL))
    return out.reshape(BATCH, SEQ, D_MODEL)
```

```python
import functools

import jax
import jax.numpy as jnp
from jax import lax
from jax.experimental import pallas as pl
from jax.experimental.pallas import tpu as pltpu

F32 = jnp.float32
BF16 = jnp.bfloat16
I32 = jnp.int32

D_MODEL = 2048
BATCH = 4
SEQ = 2048
TOKENS = BATCH * SEQ
CHUNK = 64
SPATIAL_BLOCK = 128
A_GROUPS = 8
A_GROUP_DIM = D_MODEL // A_GROUPS
GLA_HEADS = 4
GLA_DK = D_MODEL // 2
GLA_HEAD_K = GLA_DK // GLA_HEADS
GLA_HEAD_V = D_MODEL // GLA_HEADS
GLA_GATE_RANK = 16
GLA_GATE_TAU = 16.0
N_EXPERTS = 32
TOP_K = 4
D_FF = D_MODEL
SWIGLU_LIMIT = 7.0
SWIGLU_ALPHA = 1.702
NORM_EPS = 1e-6

LANES = 128
VMEM_LIMIT = 56 * 1024 * 1024

COL_U, COL_V, COL_Q, COL_K, COL_VV, COL_R = 0, 2048, 4096, 5120, 6144, 8192
COL_GA, COL_GB = 10240, 12288
PROJ_W = 14336
GLR_SRC = 10240

ROW_TILE = 256
SUPER_TILES = 6
SUPER_ROWS = ROW_TILE * SUPER_TILES
N_ROWS = TOKENS * TOP_K
N_ROW_TILES = (N_ROWS + N_EXPERTS * (ROW_TILE - 1)) // ROW_TILE
PADDED_ROWS = N_ROW_TILES * ROW_TILE
MAX_SUPER = -(-(N_ROW_TILES + N_EXPERTS * (SUPER_TILES - 1)) // SUPER_TILES)
FF_TILE = 512
N_FF = D_FF // FF_TILE
ROW_DMA_GROUP = 8


def _cparams(*sem):
    return pltpu.CompilerParams(dimension_semantics=sem, vmem_limit_bytes=VMEM_LIMIT)


def _rms(x):
    return x * lax.rsqrt(jnp.mean(x * x, axis=-1, keepdims=True) + NORM_EPS)


def _ada_body(c_ref, w_ref, b_ref, o_ref):
    c = c_ref[...]
    cond = c * jax.nn.sigmoid(c)
    o_ref[...] = jnp.dot(cond.astype(BF16), w_ref[...].astype(BF16),
                         preferred_element_type=F32) + b_ref[...]


def _ada(c, ada_w, ada_b):
    tn = 1024
    cp = jnp.zeros((8, D_MODEL), F32).at[:BATCH].set(c)
    out = pl.pallas_call(
        _ada_body,
        grid=(6 * D_MODEL // tn,),
        in_specs=[pl.BlockSpec((8, D_MODEL), lambda j: (0, 0)),
                  pl.BlockSpec((D_MODEL, tn), lambda j: (0, j)),
                  pl.BlockSpec((1, tn), lambda j: (0, j))],
        out_specs=pl.BlockSpec((8, tn), lambda j: (0, j)),
        out_shape=jax.ShapeDtypeStruct((8, 6 * D_MODEL), F32),
        compiler_params=_cparams("arbitrary"),
        name="ada",
    )(cp, ada_w, ada_b.reshape(1, 6 * D_MODEL))
    return out[:BATCH].reshape(BATCH, 6, D_MODEL)


def _norm1_body(x_ref, mod_ref, g_ref, wglr_ref, h_ref, glr_ref):
    h = _rms(x_ref[...]) * g_ref[...] * (1.0 + mod_ref[1:2, :]) + mod_ref[0:1, :]
    hb = h.astype(BF16)
    h_ref[...] = hb
    glr_ref[...] = _dot_nt(hb, wglr_ref[...])


def _norm1(x2d, mod, norm1_g, w_glr):
    tm = 512
    per_batch = SEQ // tm
    return pl.pallas_call(
        _norm1_body,
        grid=(TOKENS // tm,),
        in_specs=[pl.BlockSpec((tm, D_MODEL), lambda i: (i, 0)),
                  pl.BlockSpec((None, 6, D_MODEL), lambda i: (i // per_batch, 0, 0)),
                  pl.BlockSpec((1, D_MODEL), lambda i: (0, 0)),
                  pl.BlockSpec((LANES, D_MODEL), lambda i: (0, 0))],
        out_specs=[pl.BlockSpec((tm, D_MODEL), lambda i: (i, 0)),
                   pl.BlockSpec((tm, LANES), lambda i: (i, 0))],
        out_shape=[jax.ShapeDtypeStruct((TOKENS, D_MODEL), BF16),
                   jax.ShapeDtypeStruct((TOKENS, LANES), F32)],
        compiler_params=_cparams("arbitrary"),
        name="norm1",
    )(x2d, mod, norm1_g, w_glr)


INPROJ_TN = 1024
INPROJ_MAIN_TILES = COL_GA // INPROJ_TN
MXU_N = 256


def _inproj_body(h_ref, wt_ref, proj_ref, wb_ref):
    j = pl.program_id(0)

    @pl.when(pl.program_id(1) == 0)
    def _():
        wb_ref[...] = wt_ref[...].astype(BF16)

    def run(act):
        for n in range(INPROJ_TN // MXU_N):
            cols = slice(n * MXU_N, (n + 1) * MXU_N)
            acc = _dot_nt(h_ref[...], wb_ref[cols, :])
            proj_ref[:, cols] = act(acc).astype(BF16)

    col = j * INPROJ_TN

    @pl.when(col < COL_Q)
    def _():
        run(jax.nn.gelu)

    @pl.when((col >= COL_Q) & (col < COL_R))
    def _():
        run(lambda a: a)

    @pl.when((col >= COL_R) & (col < COL_GA))
    def _():
        run(lambda a: a * jax.nn.sigmoid(a))

    @pl.when(col >= COL_GA)
    def _():
        run(jax.nn.sigmoid)


def _inproj(h, w_in_t):
    tm, tn = 1024, INPROJ_TN

    def w_row(j, i):
        return (pl.multiple_of(j * tn + jnp.where(j >= INPROJ_MAIN_TILES, GLA_GATE_RANK, 0), GLA_GATE_RANK), 0)

    return pl.pallas_call(
        _inproj_body,
        grid=(PROJ_W // tn, TOKENS // tm),
        in_specs=[pl.BlockSpec((tm, D_MODEL), lambda j, i: (i, 0)),
                  pl.BlockSpec((pl.Element(tn), pl.Element(D_MODEL)), w_row)],
        out_specs=pl.BlockSpec((tm, tn), lambda j, i: (i, j)),
        out_shape=jax.ShapeDtypeStruct((TOKENS, PROJ_W), BF16),
        scratch_shapes=[pltpu.VMEM((tn, D_MODEL), BF16)],
        compiler_params=_cparams("arbitrary", "arbitrary"),
        name="inproj",
    )(h, w_in_t)


def _sgu_body(u_ref, v_ref, lg_ref, lb_ref, ws_ref, bs_ref, o_ref):
    v = v_ref[...].astype(F32)
    mu = jnp.mean(v, axis=-1, keepdims=True)
    xc = v - mu
    var = jnp.mean(xc * xc, axis=-1, keepdims=True)
    vn = (xc * lax.rsqrt(var + NORM_EPS) * lg_ref[...] + lb_ref[...]).astype(BF16)
    t_chunk = lax.broadcasted_iota(I32, (SPATIAL_BLOCK, SPATIAL_BLOCK), 0) // CHUNK
    s_chunk = lax.broadcasted_iota(I32, (SPATIAL_BLOCK, SPATIAL_BLOCK), 1) // CHUNK
    mask = s_chunk <= t_chunk
    for g in range(A_GROUPS):
        cols = slice(g * A_GROUP_DIM, (g + 1) * A_GROUP_DIM)
        w = jnp.where(mask, ws_ref[g], 0.0).astype(BF16)
        mixed = jnp.dot(w, vn[:, cols], preferred_element_type=F32) + bs_ref[:, g:g + 1]
        o_ref[:, cols] = (u_ref[:, cols].astype(F32) * mixed).astype(BF16)


def _sgu(proj, ln_g, ln_b, w_s, b_s_t):
    nblk = TOKENS // SPATIAL_BLOCK
    wb = D_MODEL
    return pl.pallas_call(
        _sgu_body,
        grid=(nblk,),
        in_specs=[pl.BlockSpec((SPATIAL_BLOCK, wb), lambda i: (i, COL_U // wb)),
                  pl.BlockSpec((SPATIAL_BLOCK, wb), lambda i: (i, COL_V // wb)),
                  pl.BlockSpec((1, wb), lambda i: (0, 0)),
                  pl.BlockSpec((1, wb), lambda i: (0, 0)),
                  pl.BlockSpec((A_GROUPS, SPATIAL_BLOCK, SPATIAL_BLOCK), lambda i: (0, 0, 0)),
                  pl.BlockSpec((SPATIAL_BLOCK, A_GROUPS), lambda i: (0, 0))],
        out_specs=pl.BlockSpec((SPATIAL_BLOCK, wb), lambda i: (i, 0)),
        out_shape=jax.ShapeDtypeStruct((TOKENS, wb), BF16),
        compiler_params=_cparams("arbitrary"),
        name="sgu",
    )(proj, proj, ln_g, ln_b, w_s, b_s_t)


def _dot_nt(a, b):
    return lax.dot_general(a, b, (((1,), (1,)), ((), ())), preferred_element_type=F32)


def _dot_tn(a, b):
    return lax.dot_general(a, b, (((0,), (0,)), ((), ())), preferred_element_type=F32)


def _gla_body(q_ref, k_ref, v_ref, r_ref, glr_ref, w2_ref, gb_ref, ng_ref, o_ref, st_ref, la_ref, *, rows_blk):
    @pl.when(pl.program_id(1) == 0)
    def _():
        st_ref[...] = jnp.zeros_like(st_ref)

    z = jnp.dot(glr_ref[...].astype(BF16), w2_ref[...], preferred_element_type=F32) + gb_ref[...]
    la_ref[...] = jax.nn.log_sigmoid(z) / GLA_GATE_TAU

    r_i = lax.broadcasted_iota(I32, (CHUNK, CHUNK), 0)
    c_i = lax.broadcasted_iota(I32, (CHUNK, CHUNK), 1)
    causal = c_i <= r_i
    tril = causal.astype(BF16)
    scale = GLA_HEAD_K ** -0.5

    def chunk(c, carry):
        rows = pl.ds(pl.multiple_of(c * CHUNK, CHUNK), CHUNK)
        for h in range(GLA_HEADS):
            kc = slice(h * GLA_HEAD_K, (h + 1) * GLA_HEAD_K)
            vc = slice(h * GLA_HEAD_V, (h + 1) * GLA_HEAD_V)
            la = la_ref[rows, kc]
            hi = la.astype(BF16)
            r1 = la - hi.astype(F32)
            mid = r1.astype(BF16)
            lo = (r1 - mid.astype(F32)).astype(BF16)
            b = (jnp.dot(tril, hi, preferred_element_type=F32)
                 + jnp.dot(tril, mid, preferred_element_type=F32)
                 + jnp.dot(tril, lo, preferred_element_type=F32))
            b_last = b[CHUNK - 1:CHUNK, :]
            q = q_ref[rows, kc].astype(F32) * scale
            k = k_ref[rows, kc].astype(F32)
            v = v_ref[rows, vc]
            q_dec = (q * jnp.exp(b)).astype(BF16)
            k_intra = (k * jnp.exp(-b)).astype(BF16)
            k_state = (k * jnp.exp(b_last - b)).astype(BF16)
            att = jnp.where(causal, _dot_nt(q_dec, k_intra), 0.0).astype(BF16)
            st = st_ref[h]
            o = jnp.dot(att, v, preferred_element_type=F32) + _dot_nt(q_dec, st.astype(BF16))
            st_ref[h] = st * jnp.exp(b_last) + _dot_tn(v, k_state)
            on = _rms(o) * ng_ref[...]
            o_ref[rows, vc] = (on * r_ref[rows, vc].astype(F32)).astype(BF16)
        return carry

    lax.fori_loop(0, rows_blk // CHUNK, chunk, 0)


def _gla(proj, glr, w2p, gate_b, norm_g):
    rows_blk = 512
    nblk = SEQ // rows_blk
    dk, dv = GLA_DK, D_MODEL

    def row(b, n):
        return b * nblk + n

    return pl.pallas_call(
        functools.partial(_gla_body, rows_blk=rows_blk),
        grid=(BATCH, nblk),
        in_specs=[pl.BlockSpec((rows_blk, dk), lambda b, n: (row(b, n), COL_Q // dk)),
                  pl.BlockSpec((rows_blk, dk), lambda b, n: (row(b, n), COL_K // dk)),
                  pl.BlockSpec((rows_blk, dv), lambda b, n: (row(b, n), COL_VV // dv)),
                  pl.BlockSpec((rows_blk, dv), lambda b, n: (row(b, n), COL_R // dv)),
                  pl.BlockSpec((rows_blk, LANES), lambda b, n: (row(b, n), 0)),
                  pl.BlockSpec((LANES, dk), lambda b, n: (0, 0)),
                  pl.BlockSpec((1, dk), lambda b, n: (0, 0)),
                  pl.BlockSpec((1, GLA_HEAD_V), lambda b, n: (0, 0))],
        out_specs=pl.BlockSpec((rows_blk, dv), lambda b, n: (row(b, n), 0)),
        out_shape=jax.ShapeDtypeStruct((TOKENS, D_MODEL), BF16),
        scratch_shapes=[pltpu.VMEM((GLA_HEADS, GLA_HEAD_V, GLA_HEAD_K), F32),
                        pltpu.VMEM((rows_blk, dk), F32)],
        compiler_params=_cparams("arbitrary", "arbitrary"),
        name="gla",
    )(proj, proj, proj, proj, glr, w2p, gate_b, norm_g)


def _merge_body(ya_ref, yb_ref, wa_ref, wb_ref, ga_ref, gb_ref, o_ref):
    a = jnp.dot(ya_ref[...], wa_ref[...], preferred_element_type=F32)
    b = jnp.dot(yb_ref[...], wb_ref[...], preferred_element_type=F32)
    o_ref[...] = (ga_ref[...].astype(F32) * a + gb_ref[...].astype(F32) * b).astype(BF16)


def _merge(y_a, y_b, wa, wb, proj):
    tm, tn = 512, 1024
    return pl.pallas_call(
        _merge_body,
        grid=(TOKENS // tm, D_MODEL // tn),
        in_specs=[pl.BlockSpec((tm, D_MODEL), lambda i, j: (i, 0)),
                  pl.BlockSpec((tm, D_MODEL), lambda i, j: (i, 0)),
                  pl.BlockSpec((D_MODEL, tn), lambda i, j: (0, j)),
                  pl.BlockSpec((D_MODEL, tn), lambda i, j: (0, j)),
                  pl.BlockSpec((tm, tn), lambda i, j: (i, COL_GA // tn + j)),
                  pl.BlockSpec((tm, tn), lambda i, j: (i, COL_GB // tn + j))],
        out_specs=pl.BlockSpec((tm, tn), lambda i, j: (i, j)),
        out_shape=jax.ShapeDtypeStruct((TOKENS, D_MODEL), BF16),
        compiler_params=_cparams("arbitrary", "arbitrary"),
        name="merge",
    )(y_a, y_b, wa, wb, proj, proj)


def _outproj_body(m_ref, x_ref, mod_ref, wo_ref, g2_ref, rwh_ref, rwl_ref, rb_ref,
                  x1_ref, hp_ref, idx_ref, w_ref, rank_ref, cnt_ref, run_ref, *, tm):
    @pl.when(pl.program_id(0) == 0)
    def _():
        run_ref[...] = jnp.zeros_like(run_ref)

    y = jnp.dot(m_ref[...], wo_ref[...], preferred_element_type=F32)
    x1 = x_ref[...] + mod_ref[2:3, :] * y
    x1_ref[...] = x1
    h2 = _rms(x1) * g2_ref[...] * (1.0 + mod_ref[4:5, :]) + mod_ref[3:4, :]
    hp_ref[...] = h2
    hb = h2.astype(BF16)
    hf = hb.astype(F32)

    h_lo = (h2 - hf).astype(BF16)
    logits = (_dot_nt(rwh_ref[...], hb) + _dot_nt(rwh_ref[...], h_lo) + _dot_nt(rwl_ref[...], hb)
              + rb_ref[...])
    e_i = lax.broadcasted_iota(I32, (N_EXPERTS, tm), 0).astype(F32)
    vals, idxs = [], []
    l = logits
    for _ in range(TOP_K):
        m = jnp.max(l, axis=0, keepdims=True)
        i = jnp.min(jnp.where(l == m, e_i, float(N_EXPERTS)), axis=0, keepdims=True)
        vals.append(m)
        idxs.append(i)
        l = jnp.where(e_i == i, -jnp.inf, l)
    ex = [jnp.exp(v - vals[0]) for v in vals]
    den = ex[0] + ex[1] + ex[2] + ex[3]
    before = (lax.broadcasted_iota(I32, (tm, tm), 0) < lax.broadcasted_iota(I32, (tm, tm), 1)).astype(BF16)
    run = run_ref[...]
    for k in range(TOP_K):
        onehot = e_i == idxs[k]
        pref = jnp.dot(onehot.astype(BF16), before, preferred_element_type=F32) + run[:, 0:1]
        rank_ref[k:k + 1, :] = jnp.sum(jnp.where(onehot, pref, 0.0), axis=0, keepdims=True).astype(I32)
        run = run + jnp.sum(onehot.astype(F32), axis=1, keepdims=True)
        idx_ref[k:k + 1, :] = idxs[k].astype(I32)
        w_ref[k:k + 1, :] = ex[k] / den
    run_ref[...] = run
    cnt_ref[...] = run.astype(I32)


def _outproj(merged, x2d, mod, w_out, norm2_g, rw_hi, rw_lo, router_b):
    tm = 512
    per_batch = SEQ // tm
    body = functools.partial(_outproj_body, tm=tm)
    return pl.pallas_call(
        body,
        grid=(TOKENS // tm,),
        in_specs=[pl.BlockSpec((tm, D_MODEL), lambda i: (i, 0)),
                  pl.BlockSpec((tm, D_MODEL), lambda i: (i, 0)),
                  pl.BlockSpec((None, 6, D_MODEL), lambda i: (i // per_batch, 0, 0)),
                  pl.BlockSpec((D_MODEL, D_MODEL), lambda i: (0, 0)),
                  pl.BlockSpec((1, D_MODEL), lambda i: (0, 0)),
                  pl.BlockSpec((N_EXPERTS, D_MODEL), lambda i: (0, 0)),
                  pl.BlockSpec((N_EXPERTS, D_MODEL), lambda i: (0, 0)),
                  pl.BlockSpec((N_EXPERTS, 1), lambda i: (0, 0))],
        out_specs=[pl.BlockSpec((tm, D_MODEL), lambda i: (i, 0)),
                   pl.BlockSpec((tm, D_MODEL), lambda i: (i, 0)),
                   pl.BlockSpec((TOP_K, tm), lambda i: (0, i)),
                   pl.BlockSpec((TOP_K, tm), lambda i: (0, i)),
                   pl.BlockSpec((TOP_K, tm), lambda i: (0, i)),
                   pl.BlockSpec((N_EXPERTS, LANES), lambda i: (0, 0))],
        out_shape=[jax.ShapeDtypeStruct((TOKENS, D_MODEL), F32),
                   jax.ShapeDtypeStruct((TOKENS, D_MODEL), F32),
                   jax.ShapeDtypeStruct((TOP_K, TOKENS), I32),
                   jax.ShapeDtypeStruct((TOP_K, TOKENS), F32),
                   jax.ShapeDtypeStruct((TOP_K, TOKENS), I32),
                   jax.ShapeDtypeStruct((N_EXPERTS, LANES), I32)],
        scratch_shapes=[pltpu.VMEM((N_EXPERTS, LANES), F32)],
        compiler_params=_cparams("arbitrary"),
        name="outproj",
    )(merged, x2d, mod, w_out, norm2_g, rw_hi, rw_lo, router_b)


def _dispatch_body(dest_ref, pend_ref, padded_ref, h_ref, xs_hbm, zero_ref, zsem, sem, *, tm):
    i = pl.program_id(0)

    @pl.when(i == 0)
    def _():
        _dispatch_clear(pend_ref, padded_ref, xs_hbm, zero_ref, zsem)

    def group(g, carry):
        base = pl.multiple_of(g * ROW_DMA_GROUP, ROW_DMA_GROUP)
        for u in range(ROW_DMA_GROUP):
            src = h_ref.at[pl.ds(base + u, 1)]
            for k in range(TOP_K):
                d = dest_ref[(i * tm + base + u) * TOP_K + k]
                pltpu.make_async_copy(src, xs_hbm.at[pl.ds(d, 1)], sem).start()
        return carry
    lax.fori_loop(0, tm // ROW_DMA_GROUP, group, 0)
    for k in range(TOP_K):
        pltpu.make_async_copy(h_ref, xs_hbm.at[pl.ds(0, tm)], sem).wait()


def _dispatch_clear(pend_ref, padded_ref, xs_hbm, zero_ref, zsem):
    zero_ref[...] = jnp.zeros_like(zero_ref)

    def zero_copy(e):
        return pltpu.make_async_copy(
            zero_ref, xs_hbm.at[pl.ds(pl.multiple_of(pend_ref[e] - ROW_TILE, ROW_TILE), ROW_TILE)], zsem)

    def zstart(e, c):
        @pl.when(padded_ref[e] > 0)
        def _():
            zero_copy(e).start()
        return c

    def zwait(e, c):
        @pl.when(padded_ref[e] > 0)
        def _():
            zero_copy(e).wait()
        return c

    def tail_copy(t):
        return pltpu.make_async_copy(
            zero_ref, xs_hbm.at[pl.ds(pl.multiple_of(t * ROW_TILE, ROW_TILE), ROW_TILE)], zsem)

    def tstart(t, c):
        tail_copy(t).start()
        return c

    def twait(t, c):
        tail_copy(t).wait()
        return c

    first_tail = pend_ref[N_EXPERTS - 1] // ROW_TILE
    lax.fori_loop(0, N_EXPERTS, zstart, 0)
    lax.fori_loop(first_tail, N_ROW_TILES, tstart, 0)
    lax.fori_loop(0, N_EXPERTS, zwait, 0)
    lax.fori_loop(first_tail, N_ROW_TILES, twait, 0)


def _dispatch(dest_flat, padded_end, padded, h_rows):
    tm = 256
    return pl.pallas_call(
        functools.partial(_dispatch_body, tm=tm),
        grid_spec=pltpu.PrefetchScalarGridSpec(
            num_scalar_prefetch=3,
            grid=(TOKENS // tm,),
            in_specs=[pl.BlockSpec((tm, D_MODEL), lambda i, d, pe, pd: (i, 0))],
            out_specs=pl.BlockSpec(memory_space=pl.ANY),
            scratch_shapes=[pltpu.VMEM((ROW_TILE, D_MODEL), F32),
                            pltpu.SemaphoreType.DMA(()),
                            pltpu.SemaphoreType.DMA(())]),
        out_shape=jax.ShapeDtypeStruct((PADDED_ROWS, D_MODEL), F32),
        compiler_params=_cparams("arbitrary"),
        name="dispatch",
    )(dest_flat, padded_end, padded, h_rows)


def _expert_body(se_ref, srow_ref, snsub_ref, xs_hbm, wg_hbm, wu_hbm, wd_hbm, bg_ref, bu_ref, bd_ref,
                 y_hbm, xstage, xb, acc, wg_f, wu_f, wd_f, wg_c, wu_c, wd_c, sem_in, sem_out, sem_w):
    s = pl.program_id(0)
    f = pl.program_id(1)
    nsub = snsub_ref[s]
    row0 = srow_ref[s]

    def weight_copies(s_, f_):
        e = se_ref[s_]
        chunk = pl.ds(pl.multiple_of(f_ * FF_TILE, FF_TILE), FF_TILE)
        return (pltpu.make_async_copy(wg_hbm.at[e, :, chunk], wg_f, sem_w.at[0]),
                pltpu.make_async_copy(wu_hbm.at[e, :, chunk], wu_f, sem_w.at[1]),
                pltpu.make_async_copy(wd_hbm.at[e, chunk, :], wd_f, sem_w.at[2]))

    wrap = f == N_FF - 1
    s_next = jnp.minimum(jnp.where(wrap, s + 1, s), MAX_SUPER - 1)
    f_next = jnp.where(wrap, 0, f + 1)
    has_next = jnp.logical_not(wrap & (s == MAX_SUPER - 1)) & (snsub_ref[s_next] > 0)

    last_f = f == N_FF - 1
    n_pair = nsub // 2
    odd = nsub - 2 * n_pair
    pair_rows = 2 * ROW_TILE

    next_super = (s + 1 < MAX_SUPER) & (snsub_ref[jnp.minimum(s + 1, MAX_SUPER - 1)] > 0)

    def in_copy(row_base, t):
        src = xs_hbm.at[pl.ds(pl.multiple_of(row_base + t * ROW_TILE, ROW_TILE), ROW_TILE)]
        return pltpu.make_async_copy(src, xstage.at[t], sem_in.at[t])

    def start_rows(row_base, count):
        def body(t, c):
            in_copy(row_base, t).start()
            return c
        lax.fori_loop(0, count, body, 0)

    def out_copy(off, m):
        dst = y_hbm.at[pl.ds(pl.multiple_of(row0 + off, ROW_TILE), m)]
        return pltpu.make_async_copy(acc.at[pl.ds(off, m)], dst, sem_out)

    def drain_outputs(tiles):
        def body(p, c):
            out_copy(0, pair_rows).wait()
            return c
        lax.fori_loop(0, tiles // 2, body, 0)

        @pl.when(tiles % 2 == 1)
        def _():
            out_copy(0, ROW_TILE).wait()

    def gate(x):
        return jnp.minimum(jnp.dot(x, wg_c[...], preferred_element_type=F32) + bg_ref[...], SWIGLU_LIMIT)

    def up(x):
        return jnp.clip(jnp.dot(x, wu_c[...], preferred_element_type=F32) + bu_ref[...],
                        -SWIGLU_LIMIT, SWIGLU_LIMIT)

    def down(rows, g, u):
        a = ((u + 1.0) * (g * jax.nn.sigmoid(SWIGLU_ALPHA * g))).astype(BF16)
        acc[rows, :] += jnp.dot(a, wd_c[...], preferred_element_type=F32)

    def finish(off, rows_n):
        @pl.when(last_f)
        def _():
            out_copy(off, rows_n).start()

    def compute(off, rows_n):
        rows = pl.ds(off, rows_n)
        x = xb[rows, :]
        down(rows, gate(x), up(x))
        finish(off, rows_n)

    def first_compute(rows_n, nxt):
        rows = pl.ds(0, rows_n)
        x = xb[rows, :]
        g = gate(x)
        wu_c[...] = wu_f[...].astype(BF16)

        @pl.when(has_next)
        def _():
            nxt[1].start()
        u = up(x)
        wd_c[...] = wd_f[...].astype(BF16)

        @pl.when(has_next)
        def _():
            nxt[2].start()
        down(rows, g, u)
        finish(0, rows_n)

    odd_off = pl.multiple_of((nsub - 1) * ROW_TILE, ROW_TILE)

    @pl.when(nsub > 0)
    def _():
        @pl.when((s == 0) & (f == 0))
        def _():
            for cp in weight_copies(s, f):
                cp.start()
            start_rows(row0, nsub)

        nxt = weight_copies(s_next, f_next)
        for cp in weight_copies(s, f):
            cp.wait()
        wg_c[...] = wg_f[...].astype(BF16)

        @pl.when(has_next)
        def _():
            nxt[0].start()

        @pl.when(f == 0)
        def _():
            @pl.when(s > 0)
            def _():
                drain_outputs(snsub_ref[jnp.maximum(s - 1, 0)])

            def load(t, c):
                in_copy(row0, t).wait()
                rows = pl.ds(pl.multiple_of(t * ROW_TILE, ROW_TILE), ROW_TILE)
                xb[rows, :] = xstage[t].astype(BF16)
                acc[rows, :] = jnp.broadcast_to(bd_ref[...], (ROW_TILE, D_MODEL))
                return c
            lax.fori_loop(0, nsub, load, 0)

        @pl.when(last_f & next_super)
        def _():
            s1 = jnp.minimum(s + 1, MAX_SUPER - 1)
            start_rows(srow_ref[s1], snsub_ref[s1])

        @pl.when(n_pair > 0)
        def _():
            first_compute(pair_rows, nxt)

        @pl.when(n_pair == 0)
        def _():
            first_compute(ROW_TILE, nxt)

        def pair(p, c):
            compute(pl.multiple_of(p * pair_rows, pair_rows), pair_rows)
            return c
        lax.fori_loop(1, n_pair, pair, 0)

        @pl.when((odd == 1) & (n_pair > 0))
        def _():
            compute(odd_off, ROW_TILE)

        @pl.when(last_f & jnp.logical_not(next_super))
        def _():
            drain_outputs(nsub)

    @pl.when((s == MAX_SUPER - 1) & last_f)
    def _():
        acc[0:ROW_TILE, :] = jnp.zeros((ROW_TILE, D_MODEL), F32)

        def tail_copy(t):
            dst = y_hbm.at[pl.ds(pl.multiple_of(t * ROW_TILE, ROW_TILE), ROW_TILE)]
            return pltpu.make_async_copy(acc.at[0:ROW_TILE], dst, sem_out)

        def tstart(t, c):
            tail_copy(t).start()
            return c

        def twait(t, c):
            tail_copy(t).wait()
            return c

        first_tail = snsub_ref[MAX_SUPER]
        lax.fori_loop(first_tail, N_ROW_TILES, tstart, 0)
        lax.fori_loop(first_tail, N_ROW_TILES, twait, 0)


def _experts(se, srow, snsub, xs, w_gate, b_gate, w_up, b_up, w_down, b_down):
    def f_eff(s, f, sn):
        return jnp.where(sn[s] > 0, f, N_FF - 1)

    return pl.pallas_call(
        _expert_body,
        grid_spec=pltpu.PrefetchScalarGridSpec(
            num_scalar_prefetch=3,
            grid=(MAX_SUPER, N_FF),
            in_specs=[pl.BlockSpec(memory_space=pl.ANY),
                      pl.BlockSpec(memory_space=pl.ANY),
                      pl.BlockSpec(memory_space=pl.ANY),
                      pl.BlockSpec(memory_space=pl.ANY),
                      pl.BlockSpec((None, 1, FF_TILE), lambda s, f, se, sr, sn: (se[s], 0, f_eff(s, f, sn))),
                      pl.BlockSpec((None, 1, FF_TILE), lambda s, f, se, sr, sn: (se[s], 0, f_eff(s, f, sn))),
                      pl.BlockSpec((None, 1, D_MODEL), lambda s, f, se, sr, sn: (se[s], 0, 0))],
            out_specs=pl.BlockSpec(memory_space=pl.ANY),
            scratch_shapes=[pltpu.VMEM((SUPER_TILES, ROW_TILE, D_MODEL), F32),
                            pltpu.VMEM((SUPER_ROWS, D_MODEL), BF16),
                            pltpu.VMEM((SUPER_ROWS, D_MODEL), F32),
                            pltpu.VMEM((D_MODEL, FF_TILE), F32),
                            pltpu.VMEM((D_MODEL, FF_TILE), F32),
                            pltpu.VMEM((FF_TILE, D_MODEL), F32),
                            pltpu.VMEM((D_MODEL, FF_TILE), BF16),
                            pltpu.VMEM((D_MODEL, FF_TILE), BF16),
                            pltpu.VMEM((FF_TILE, D_MODEL), BF16),
                            pltpu.SemaphoreType.DMA((SUPER_TILES,)),
                            pltpu.SemaphoreType.DMA(()),
                            pltpu.SemaphoreType.DMA((3,))]),
        out_shape=jax.ShapeDtypeStruct((PADDED_ROWS, D_MODEL), F32),
        compiler_params=_cparams("arbitrary", "arbitrary"),
        name="experts",
    )(se, srow, snsub, xs, w_gate, w_up, w_down,
      b_gate.reshape(N_EXPERTS, 1, D_FF), b_up.reshape(N_EXPERTS, 1, D_FF),
      b_down.reshape(N_EXPERTS, 1, D_MODEL))


def _combine_body(dest_ref, y_hbm, x1_ref, w_ref, mod_ref, fg_ref, o_ref, buf, sem, *, tm):
    i = pl.program_id(0)
    def group(g, carry):
        base = pl.multiple_of(g * ROW_DMA_GROUP, ROW_DMA_GROUP)
        for u in range(ROW_DMA_GROUP):
            for k in range(TOP_K):
                d = dest_ref[(i * tm + base + u) * TOP_K + k]
                pltpu.make_async_copy(y_hbm.at[pl.ds(d, 1)], buf.at[k, pl.ds(base + u, 1)], sem).start()
        return carry
    lax.fori_loop(0, tm // ROW_DMA_GROUP, group, 0)
    for k in range(TOP_K):
        pltpu.make_async_copy(y_hbm.at[pl.ds(0, tm)], buf.at[k], sem).wait()
    moe = buf[0] * w_ref[:, 0:1]
    for k in range(1, TOP_K):
        moe = moe + buf[k] * w_ref[:, k:k + 1]
    x2 = x1_ref[...] + mod_ref[5:6, :] * moe
    o_ref[...] = _rms(x2) * fg_ref[...]


def _combine(dest_flat, y, x1, w_t, mod, final_g):
    tm = 256
    per_batch = SEQ // tm
    return pl.pallas_call(
        functools.partial(_combine_body, tm=tm),
        grid_spec=pltpu.PrefetchScalarGridSpec(
            num_scalar_prefetch=1,
            grid=(TOKENS // tm,),
            in_specs=[pl.BlockSpec(memory_space=pl.ANY),
                      pl.BlockSpec((tm, D_MODEL), lambda i, d: (i, 0)),
                      pl.BlockSpec((tm, TOP_K), lambda i, d: (i, 0)),
                      pl.BlockSpec((None, 6, D_MODEL), lambda i, d: (i // per_batch, 0, 0)),
                      pl.BlockSpec((1, D_MODEL), lambda i, d: (0, 0))],
            out_specs=pl.BlockSpec((tm, D_MODEL), lambda i, d: (i, 0)),
            scratch_shapes=[pltpu.VMEM((TOP_K, tm, D_MODEL), F32),
                            pltpu.SemaphoreType.DMA(())]),
        out_shape=jax.ShapeDtypeStruct((TOKENS, D_MODEL), F32),
        compiler_params=_cparams("arbitrary"),
        name="combine",
    )(dest_flat, y, x1, w_t, mod, final_g)


def _routing_tables(idx, rank, counts):
    padded = (counts + ROW_TILE - 1) // ROW_TILE * ROW_TILE
    padded_end = jnp.cumsum(padded)
    padded_start = padded_end - padded
    e_i = jnp.arange(N_EXPERTS, dtype=I32)[:, None, None]
    start_of = jnp.sum(jnp.where(idx[None] == e_i, padded_start[:, None, None], 0), axis=0)
    dest = (start_of + rank).T.reshape(N_ROWS).astype(I32)

    tiles = padded // ROW_TILE
    n_super = (tiles + SUPER_TILES - 1) // SUPER_TILES
    super_end = jnp.cumsum(n_super)
    super_start = super_end - n_super
    s_i = jnp.arange(MAX_SUPER, dtype=I32)
    total = super_end[-1]
    valid = s_i < total
    e_of = jnp.minimum(jnp.sum((s_i[:, None] >= super_end[None, :]).astype(I32), axis=1), N_EXPERTS - 1)
    last_e = jnp.minimum(jnp.sum((total - 1 >= super_end).astype(I32)), N_EXPERTS - 1)
    local = s_i - super_start[e_of]
    srow = jnp.where(valid, padded_start[e_of] + local * SUPER_ROWS, 0).astype(I32)
    snsub = jnp.where(valid, jnp.minimum(tiles[e_of] - local * SUPER_TILES, SUPER_TILES), 0).astype(I32)
    snsub = jnp.concatenate([snsub, (padded_end[-1:] // ROW_TILE).astype(I32)])
    se = jnp.where(valid, e_of, last_e).astype(I32)
    return dest, padded_end.astype(I32), padded.astype(I32), se, srow, snsub


def kernel(x, c, ada_w, ada_b, norm1_g, w_in, gla_gate_w2, gla_gate_b, sgu_ln_g, sgu_ln_b, sgu_w, sgu_b,
           gla_norm_g, w_branch_a, w_branch_b, w_out, norm2_g, router_w, router_b, exp_w_gate, exp_b_gate,
           exp_w_up, exp_b_up, exp_w_down, exp_b_down, final_g):
    x2d = x.reshape(TOKENS, D_MODEL)
    mod = _ada(c, ada_w[0], ada_b[0])

    w_in0 = w_in[0]
    w_in_t = w_in0.T
    w_glr = jnp.zeros((LANES, D_MODEL), BF16).at[:GLA_GATE_RANK].set(
        w_in_t[GLR_SRC:GLR_SRC + GLA_GATE_RANK].astype(BF16))
    h, glr = _norm1(x2d, mod, norm1_g, w_glr)
    proj = _inproj(h, w_in_t)

    y_a = _sgu(proj, sgu_ln_g, sgu_ln_b, sgu_w[0], sgu_b[0].T)

    w2p = jnp.zeros((LANES, GLA_DK), BF16).at[:GLA_GATE_RANK].set(gla_gate_w2[0].astype(BF16))
    y_b = _gla(proj, glr, w2p, gla_gate_b, gla_norm_g)

    merged = _merge(y_a, y_b, w_branch_a[0].astype(BF16), w_branch_b[0].astype(BF16), proj)

    rw_t = router_w[0].T
    rw_hi = rw_t.astype(BF16)
    rw_lo = (rw_t - rw_hi.astype(F32)).astype(BF16)
    x1, h_rows, idx, top_w, rank, cnt = _outproj(
        merged, x2d, mod, w_out[0].astype(BF16), norm2_g, rw_hi, rw_lo, router_b[0].reshape(N_EXPERTS, 1))

    dest, padded_end, padded, se, srow, snsub = _routing_tables(idx, rank, cnt[:, 0])
    xs = _dispatch(dest, padded_end, padded, h_rows)
    y = _experts(se, srow, snsub, xs, exp_w_gate[0], exp_b_gate[0], exp_w_up[0], exp_b_up[0],
                 exp_w_down[0], exp_b_down[0])
    out = _combine(dest, y, x1, top_w.T, mod, final_g.reshape(1, D_MODEL))
    return out.reshape(BATCH, SEQ, D_MODEL)
```

```python
import functools

import jax
import jax.numpy as jnp
from jax import lax
from jax.experimental import pallas as pl
from jax.experimental.pallas import tpu as pltpu

F32 = jnp.float32
BF16 = jnp.bfloat16
I32 = jnp.int32

D_MODEL = 2048
BATCH = 4
SEQ = 2048
TOKENS = BATCH * SEQ
CHUNK = 64
SPATIAL_BLOCK = 128
A_GROUPS = 8
A_GROUP_DIM = D_MODEL // A_GROUPS
GLA_HEADS = 4
GLA_DK = D_MODEL // 2
GLA_HEAD_K = GLA_DK // GLA_HEADS
GLA_HEAD_V = D_MODEL // GLA_HEADS
GLA_GATE_RANK = 16
GLA_GATE_TAU = 16.0
N_EXPERTS = 32
TOP_K = 4
D_FF = D_MODEL
SWIGLU_LIMIT = 7.0
SWIGLU_ALPHA = 1.702
NORM_EPS = 1e-6

LANES = 128
VMEM_LIMIT = 56 * 1024 * 1024

COL_U, COL_V, COL_Q, COL_K, COL_VV, COL_R = 0, 2048, 4096, 5120, 6144, 8192
COL_GA, COL_GB = 10240, 12288
PROJ_W = 14336
GLR_SRC = 10240

ROW_TILE = 256
SUPER_TILES = 6
SUPER_ROWS = ROW_TILE * SUPER_TILES
N_ROWS = TOKENS * TOP_K
N_ROW_TILES = (N_ROWS + N_EXPERTS * (ROW_TILE - 1)) // ROW_TILE
PADDED_ROWS = N_ROW_TILES * ROW_TILE
MAX_SUPER = -(-(N_ROW_TILES + N_EXPERTS * (SUPER_TILES - 1)) // SUPER_TILES)
FF_TILE = 512
N_FF = D_FF // FF_TILE
ROW_DMA_GROUP = 8


def _cparams(*sem):
    return pltpu.CompilerParams(dimension_semantics=sem, vmem_limit_bytes=VMEM_LIMIT)


def _rms(x):
    return x * lax.rsqrt(jnp.mean(x * x, axis=-1, keepdims=True) + NORM_EPS)


def _ada_body(c_ref, w_ref, b_ref, o_ref):
    c = c_ref[...]
    cond = c * jax.nn.sigmoid(c)
    o_ref[...] = jnp.dot(cond.astype(BF16), w_ref[...].astype(BF16),
                         preferred_element_type=F32) + b_ref[...]


def _ada(c, ada_w, ada_b):
    tn = 1024
    cp = jnp.zeros((8, D_MODEL), F32).at[:BATCH].set(c)
    out = pl.pallas_call(
        _ada_body,
        grid=(6 * D_MODEL // tn,),
        in_specs=[pl.BlockSpec((8, D_MODEL), lambda j: (0, 0)),
                  pl.BlockSpec((D_MODEL, tn), lambda j: (0, j)),
                  pl.BlockSpec((1, tn), lambda j: (0, j))],
        out_specs=pl.BlockSpec((8, tn), lambda j: (0, j)),
        out_shape=jax.ShapeDtypeStruct((8, 6 * D_MODEL), F32),
        compiler_params=_cparams("arbitrary"),
        name="ada",
    )(cp, ada_w, ada_b.reshape(1, 6 * D_MODEL))
    return out[:BATCH].reshape(BATCH, 6, D_MODEL)


def _norm1_body(x_ref, mod_ref, g_ref, wglr_ref, h_ref, glr_ref):
    h = _rms(x_ref[...]) * g_ref[...] * (1.0 + mod_ref[1:2, :]) + mod_ref[0:1, :]
    hb = h.astype(BF16)
    h_ref[...] = hb
    glr_ref[...] = _dot_nt(hb, wglr_ref[...])


def _norm1(x2d, mod, norm1_g, w_glr):
    tm = 512
    per_batch = SEQ // tm
    return pl.pallas_call(
        _norm1_body,
        grid=(TOKENS // tm,),
        in_specs=[pl.BlockSpec((tm, D_MODEL), lambda i: (i, 0)),
                  pl.BlockSpec((None, 6, D_MODEL), lambda i: (i // per_batch, 0, 0)),
                  pl.BlockSpec((1, D_MODEL), lambda i: (0, 0)),
                  pl.BlockSpec((LANES, D_MODEL), lambda i: (0, 0))],
        out_specs=[pl.BlockSpec((tm, D_MODEL), lambda i: (i, 0)),
                   pl.BlockSpec((tm, LANES), lambda i: (i, 0))],
        out_shape=[jax.ShapeDtypeStruct((TOKENS, D_MODEL), BF16),
                   jax.ShapeDtypeStruct((TOKENS, LANES), F32)],
        compiler_params=_cparams("arbitrary"),
        name="norm1",
    )(x2d, mod, norm1_g, w_glr)


INPROJ_TN = 1024
INPROJ_MAIN_TILES = COL_GA // INPROJ_TN
MXU_N = 256


def _inproj_body(h_ref, wt_ref, proj_ref, wb_ref):
    j = pl.program_id(0)

    @pl.when(pl.program_id(1) == 0)
    def _():
        wb_ref[...] = wt_ref[...].astype(BF16)

    def run(act):
        for n in range(INPROJ_TN // MXU_N):
            cols = slice(n * MXU_N, (n + 1) * MXU_N)
            acc = _dot_nt(h_ref[...], wb_ref[cols, :])
            proj_ref[:, cols] = act(acc).astype(BF16)

    col = j * INPROJ_TN

    @pl.when(col < COL_Q)
    def _():
        run(jax.nn.gelu)

    @pl.when((col >= COL_Q) & (col < COL_R))
    def _():
        run(lambda a: a)

    @pl.when((col >= COL_R) & (col < COL_GA))
    def _():
        run(lambda a: a * jax.nn.sigmoid(a))

    @pl.when(col >= COL_GA)
    def _():
        run(jax.nn.sigmoid)


def _inproj(h, w_in_t):
    tm, tn = 1024, INPROJ_TN

    def w_row(j, i):
        return (pl.multiple_of(j * tn + jnp.where(j >= INPROJ_MAIN_TILES, GLA_GATE_RANK, 0), GLA_GATE_RANK), 0)

    return pl.pallas_call(
        _inproj_body,
        grid=(PROJ_W // tn, TOKENS // tm),
        in_specs=[pl.BlockSpec((tm, D_MODEL), lambda j, i: (i, 0)),
                  pl.BlockSpec((pl.Element(tn), pl.Element(D_MODEL)), w_row)],
        out_specs=pl.BlockSpec((tm, tn), lambda j, i: (i, j)),
        out_shape=jax.ShapeDtypeStruct((TOKENS, PROJ_W), BF16),
        scratch_shapes=[pltpu.VMEM((tn, D_MODEL), BF16)],
        compiler_params=_cparams("arbitrary", "arbitrary"),
        name="inproj",
    )(h, w_in_t)


def _sgu_body(u_ref, v_ref, lg_ref, lb_ref, ws_ref, bs_ref, o_ref):
    v = v_ref[...].astype(F32)
    mu = jnp.mean(v, axis=-1, keepdims=True)
    xc = v - mu
    var = jnp.mean(xc * xc, axis=-1, keepdims=True)
    vn = (xc * lax.rsqrt(var + NORM_EPS) * lg_ref[...] + lb_ref[...]).astype(BF16)
    t_chunk = lax.broadcasted_iota(I32, (SPATIAL_BLOCK, SPATIAL_BLOCK), 0) // CHUNK
    s_chunk = lax.broadcasted_iota(I32, (SPATIAL_BLOCK, SPATIAL_BLOCK), 1) // CHUNK
    mask = s_chunk <= t_chunk
    for g in range(A_GROUPS):
        cols = slice(g * A_GROUP_DIM, (g + 1) * A_GROUP_DIM)
        w = jnp.where(mask, ws_ref[g], 0.0).astype(BF16)
        mixed = jnp.dot(w, vn[:, cols], preferred_element_type=F32) + bs_ref[:, g:g + 1]
        o_ref[:, cols] = (u_ref[:, cols].astype(F32) * mixed).astype(BF16)


def _sgu(proj, ln_g, ln_b, w_s, b_s_t):
    nblk = TOKENS // SPATIAL_BLOCK
    wb = D_MODEL
    return pl.pallas_call(
        _sgu_body,
        grid=(nblk,),
        in_specs=[pl.BlockSpec((SPATIAL_BLOCK, wb), lambda i: (i, COL_U // wb)),
                  pl.BlockSpec((SPATIAL_BLOCK, wb), lambda i: (i, COL_V // wb)),
                  pl.BlockSpec((1, wb), lambda i: (0, 0)),
                  pl.BlockSpec((1, wb), lambda i: (0, 0)),
                  pl.BlockSpec((A_GROUPS, SPATIAL_BLOCK, SPATIAL_BLOCK), lambda i: (0, 0, 0)),
                  pl.BlockSpec((SPATIAL_BLOCK, A_GROUPS), lambda i: (0, 0))],
        out_specs=pl.BlockSpec((SPATIAL_BLOCK, wb), lambda i: (i, 0)),
        out_shape=jax.ShapeDtypeStruct((TOKENS, wb), BF16),
        compiler_params=_cparams("arbitrary"),
        name="sgu",
    )(proj, proj, ln_g, ln_b, w_s, b_s_t)


def _dot_nt(a, b):
    return lax.dot_general(a, b, (((1,), (1,)), ((), ())), preferred_element_type=F32)


def _dot_tn(a, b):
    return lax.dot_general(a, b, (((0,), (0,)), ((), ())), preferred_element_type=F32)


def _gla_body(q_ref, k_ref, v_ref, r_ref, glr_ref, w2_ref, gb_ref, ng_ref, o_ref,
              st_ref, la_ref, qd_ref, ks_ref, dec_ref, oi_ref, *, rows_blk):
    @pl.when(pl.program_id(1) == 0)
    def _():
        st_ref[...] = jnp.zeros_like(st_ref)

    z = jnp.dot(glr_ref[...].astype(BF16), w2_ref[...], preferred_element_type=F32) + gb_ref[...]
    la_ref[...] = jax.nn.log_sigmoid(z) / GLA_GATE_TAU

    r_i = lax.broadcasted_iota(I32, (CHUNK, CHUNK), 0)
    c_i = lax.broadcasted_iota(I32, (CHUNK, CHUNK), 1)
    causal = c_i <= r_i
    tril = causal.astype(BF16)
    scale = GLA_HEAD_K ** -0.5

    def head_cols(h):
        return (slice(h * GLA_HEAD_K, (h + 1) * GLA_HEAD_K), slice(h * GLA_HEAD_V, (h + 1) * GLA_HEAD_V))

    def intra(c, carry):
        rows = pl.ds(pl.multiple_of(c * CHUNK, CHUNK), CHUNK)
        for h in range(GLA_HEADS):
            kc, vc = head_cols(h)
            la = la_ref[rows, kc]
            hi = la.astype(BF16)
            r1 = la - hi.astype(F32)
            mid = r1.astype(BF16)
            lo = (r1 - mid.astype(F32)).astype(BF16)
            b = (jnp.dot(tril, hi, preferred_element_type=F32)
                 + jnp.dot(tril, mid, preferred_element_type=F32)
                 + jnp.dot(tril, lo, preferred_element_type=F32))
            b_last = b[CHUNK - 1:CHUNK, :]
            q = q_ref[rows, kc].astype(F32) * scale
            k = k_ref[rows, kc].astype(F32)
            q_dec = (q * jnp.exp(b)).astype(BF16)
            k_intra = (k * jnp.exp(-b)).astype(BF16)
            qd_ref[rows, kc] = q_dec
            ks_ref[rows, kc] = (k * jnp.exp(b_last - b)).astype(BF16)
            dec_ref[c, :, kc] = jnp.exp(b_last)
            att = jnp.where(causal, _dot_nt(q_dec, k_intra), 0.0).astype(BF16)
            oi_ref[rows, vc] = jnp.dot(att, v_ref[rows, vc], preferred_element_type=F32)
        return carry

    def inter(c, carry):
        rows = pl.ds(pl.multiple_of(c * CHUNK, CHUNK), CHUNK)
        for h in range(GLA_HEADS):
            kc, vc = head_cols(h)
            st = st_ref[h]
            o = oi_ref[rows, vc] + _dot_nt(qd_ref[rows, kc], st.astype(BF16))
            st_ref[h] = st * dec_ref[c, :, kc] + _dot_tn(v_ref[rows, vc], ks_ref[rows, kc])
            on = _rms(o) * ng_ref[...]
            o_ref[rows, vc] = (on * r_ref[rows, vc].astype(F32)).astype(BF16)
        return carry

    n_chunks = rows_blk // CHUNK
    lax.fori_loop(0, n_chunks, intra, 0, unroll=2)
    lax.fori_loop(0, n_chunks, inter, 0)


def _gla(proj, glr, w2p, gate_b, norm_g):
    rows_blk = 512
    nblk = SEQ // rows_blk
    dk, dv = GLA_DK, D_MODEL

    def row(b, n):
        return b * nblk + n

    return pl.pallas_call(
        functools.partial(_gla_body, rows_blk=rows_blk),
        grid=(BATCH, nblk),
        in_specs=[pl.BlockSpec((rows_blk, dk), lambda b, n: (row(b, n), COL_Q // dk)),
                  pl.BlockSpec((rows_blk, dk), lambda b, n: (row(b, n), COL_K // dk)),
                  pl.BlockSpec((rows_blk, dv), lambda b, n: (row(b, n), COL_VV // dv)),
                  pl.BlockSpec((rows_blk, dv), lambda b, n: (row(b, n), COL_R // dv)),
                  pl.BlockSpec((rows_blk, LANES), lambda b, n: (row(b, n), 0)),
                  pl.BlockSpec((LANES, dk), lambda b, n: (0, 0)),
                  pl.BlockSpec((1, dk), lambda b, n: (0, 0)),
                  pl.BlockSpec((1, GLA_HEAD_V), lambda b, n: (0, 0))],
        out_specs=pl.BlockSpec((rows_blk, dv), lambda b, n: (row(b, n), 0)),
        out_shape=jax.ShapeDtypeStruct((TOKENS, D_MODEL), BF16),
        scratch_shapes=[pltpu.VMEM((GLA_HEADS, GLA_HEAD_V, GLA_HEAD_K), F32),
                        pltpu.VMEM((rows_blk, dk), F32),
                        pltpu.VMEM((rows_blk, dk), BF16),
                        pltpu.VMEM((rows_blk, dk), BF16),
                        pltpu.VMEM((rows_blk // CHUNK, 1, dk), F32),
                        pltpu.VMEM((rows_blk, dv), F32)],
        compiler_params=_cparams("arbitrary", "arbitrary"),
        name="gla",
    )(proj, proj, proj, proj, glr, w2p, gate_b, norm_g)


def _merge_body(ya_ref, yb_ref, wa_ref, wb_ref, ga_ref, gb_ref, o_ref):
    a = jnp.dot(ya_ref[...], wa_ref[...], preferred_element_type=F32)
    b = jnp.dot(yb_ref[...], wb_ref[...], preferred_element_type=F32)
    o_ref[...] = (ga_ref[...].astype(F32) * a + gb_ref[...].astype(F32) * b).astype(BF16)


def _merge(y_a, y_b, wa, wb, proj):
    tm, tn = 512, 1024
    return pl.pallas_call(
        _merge_body,
        grid=(TOKENS // tm, D_MODEL // tn),
        in_specs=[pl.BlockSpec((tm, D_MODEL), lambda i, j: (i, 0)),
                  pl.BlockSpec((tm, D_MODEL), lambda i, j: (i, 0)),
                  pl.BlockSpec((D_MODEL, tn), lambda i, j: (0, j)),
                  pl.BlockSpec((D_MODEL, tn), lambda i, j: (0, j)),
                  pl.BlockSpec((tm, tn), lambda i, j: (i, COL_GA // tn + j)),
                  pl.BlockSpec((tm, tn), lambda i, j: (i, COL_GB // tn + j))],
        out_specs=pl.BlockSpec((tm, tn), lambda i, j: (i, j)),
        out_shape=jax.ShapeDtypeStruct((TOKENS, D_MODEL), BF16),
        compiler_params=_cparams("arbitrary", "arbitrary"),
        name="merge",
    )(y_a, y_b, wa, wb, proj, proj)


def _outproj_body(m_ref, x_ref, mod_ref, wo_ref, g2_ref, rwh_ref, rwl_ref, rb_ref,
                  x1_ref, hp_ref, idx_ref, w_ref, rank_ref, cnt_ref, run_ref, *, tm):
    @pl.when(pl.program_id(0) == 0)
    def _():
        run_ref[...] = jnp.zeros_like(run_ref)

    y = jnp.dot(m_ref[...], wo_ref[...], preferred_element_type=F32)
    x1 = x_ref[...] + mod_ref[2:3, :] * y
    x1_ref[...] = x1
    h2 = _rms(x1) * g2_ref[...] * (1.0 + mod_ref[4:5, :]) + mod_ref[3:4, :]
    hp_ref[...] = h2
    hb = h2.astype(BF16)
    hf = hb.astype(F32)

    h_lo = (h2 - hf).astype(BF16)
    logits = (_dot_nt(rwh_ref[...], hb) + _dot_nt(rwh_ref[...], h_lo) + _dot_nt(rwl_ref[...], hb)
              + rb_ref[...])
    e_i = lax.broadcasted_iota(I32, (N_EXPERTS, tm), 0).astype(F32)
    vals, idxs = [], []
    l = logits
    for _ in range(TOP_K):
        m = jnp.max(l, axis=0, keepdims=True)
        i = jnp.min(jnp.where(l == m, e_i, float(N_EXPERTS)), axis=0, keepdims=True)
        vals.append(m)
        idxs.append(i)
        l = jnp.where(e_i == i, -jnp.inf, l)
    ex = [jnp.exp(v - vals[0]) for v in vals]
    den = ex[0] + ex[1] + ex[2] + ex[3]
    before = (lax.broadcasted_iota(I32, (tm, tm), 0) < lax.broadcasted_iota(I32, (tm, tm), 1)).astype(BF16)
    run = run_ref[...]
    for k in range(TOP_K):
        onehot = e_i == idxs[k]
        pref = jnp.dot(onehot.astype(BF16), before, preferred_element_type=F32) + run[:, 0:1]
        rank_ref[k:k + 1, :] = jnp.sum(jnp.where(onehot, pref, 0.0), axis=0, keepdims=True).astype(I32)
        run = run + jnp.sum(onehot.astype(F32), axis=1, keepdims=True)
        idx_ref[k:k + 1, :] = idxs[k].astype(I32)
        w_ref[k:k + 1, :] = ex[k] / den
    run_ref[...] = run
    cnt_ref[...] = run.astype(I32)


def _outproj(merged, x2d, mod, w_out, norm2_g, rw_hi, rw_lo, router_b):
    tm = 512
    per_batch = SEQ // tm
    body = functools.partial(_outproj_body, tm=tm)
    return pl.pallas_call(
        body,
        grid=(TOKENS // tm,),
        in_specs=[pl.BlockSpec((tm, D_MODEL), lambda i: (i, 0)),
                  pl.BlockSpec((tm, D_MODEL), lambda i: (i, 0)),
                  pl.BlockSpec((None, 6, D_MODEL), lambda i: (i // per_batch, 0, 0)),
                  pl.BlockSpec((D_MODEL, D_MODEL), lambda i: (0, 0)),
                  pl.BlockSpec((1, D_MODEL), lambda i: (0, 0)),
                  pl.BlockSpec((N_EXPERTS, D_MODEL), lambda i: (0, 0)),
                  pl.BlockSpec((N_EXPERTS, D_MODEL), lambda i: (0, 0)),
                  pl.BlockSpec((N_EXPERTS, 1), lambda i: (0, 0))],
        out_specs=[pl.BlockSpec((tm, D_MODEL), lambda i: (i, 0)),
                   pl.BlockSpec((tm, D_MODEL), lambda i: (i, 0)),
                   pl.BlockSpec((TOP_K, tm), lambda i: (0, i)),
                   pl.BlockSpec((TOP_K, tm), lambda i: (0, i)),
                   pl.BlockSpec((TOP_K, tm), lambda i: (0, i)),
                   pl.BlockSpec((N_EXPERTS, LANES), lambda i: (0, 0))],
        out_shape=[jax.ShapeDtypeStruct((TOKENS, D_MODEL), F32),
                   jax.ShapeDtypeStruct((TOKENS, D_MODEL), F32),
                   jax.ShapeDtypeStruct((TOP_K, TOKENS), I32),
                   jax.ShapeDtypeStruct((TOP_K, TOKENS), F32),
                   jax.ShapeDtypeStruct((TOP_K, TOKENS), I32),
                   jax.ShapeDtypeStruct((N_EXPERTS, LANES), I32)],
        scratch_shapes=[pltpu.VMEM((N_EXPERTS, LANES), F32)],
        compiler_params=_cparams("arbitrary"),
        name="outproj",
    )(merged, x2d, mod, w_out, norm2_g, rw_hi, rw_lo, router_b)


def _dispatch_body(dest_ref, pend_ref, padded_ref, h_ref, xs_hbm, zero_ref, zsem, sem, *, tm):
    i = pl.program_id(0)

    @pl.when(i == 0)
    def _():
        _dispatch_clear(pend_ref, padded_ref, xs_hbm, zero_ref, zsem)

    def group(g, carry):
        base = pl.multiple_of(g * ROW_DMA_GROUP, ROW_DMA_GROUP)
        for u in range(ROW_DMA_GROUP):
            src = h_ref.at[pl.ds(base + u, 1)]
            for k in range(TOP_K):
                d = dest_ref[(i * tm + base + u) * TOP_K + k]
                pltpu.make_async_copy(src, xs_hbm.at[pl.ds(d, 1)], sem).start()
        return carry
    lax.fori_loop(0, tm // ROW_DMA_GROUP, group, 0)
    for k in range(TOP_K):
        pltpu.make_async_copy(h_ref, xs_hbm.at[pl.ds(0, tm)], sem).wait()


def _dispatch_clear(pend_ref, padded_ref, xs_hbm, zero_ref, zsem):
    zero_ref[...] = jnp.zeros_like(zero_ref)

    def zero_copy(e):
        return pltpu.make_async_copy(
            zero_ref, xs_hbm.at[pl.ds(pl.multiple_of(pend_ref[e] - ROW_TILE, ROW_TILE), ROW_TILE)], zsem)

    def zstart(e, c):
        @pl.when(padded_ref[e] > 0)
        def _():
            zero_copy(e).start()
        return c

    def zwait(e, c):
        @pl.when(padded_ref[e] > 0)
        def _():
            zero_copy(e).wait()
        return c

    def tail_copy(t):
        return pltpu.make_async_copy(
            zero_ref, xs_hbm.at[pl.ds(pl.multiple_of(t * ROW_TILE, ROW_TILE), ROW_TILE)], zsem)

    def tstart(t, c):
        tail_copy(t).start()
        return c

    def twait(t, c):
        tail_copy(t).wait()
        return c

    first_tail = pend_ref[N_EXPERTS - 1] // ROW_TILE
    lax.fori_loop(0, N_EXPERTS, zstart, 0)
    lax.fori_loop(first_tail, N_ROW_TILES, tstart, 0)
    lax.fori_loop(0, N_EXPERTS, zwait, 0)
    lax.fori_loop(first_tail, N_ROW_TILES, twait, 0)


def _dispatch(dest_flat, padded_end, padded, h_rows):
    tm = 1024
    return pl.pallas_call(
        functools.partial(_dispatch_body, tm=tm),
        grid_spec=pltpu.PrefetchScalarGridSpec(
            num_scalar_prefetch=3,
            grid=(TOKENS // tm,),
            in_specs=[pl.BlockSpec((tm, D_MODEL), lambda i, d, pe, pd: (i, 0))],
            out_specs=pl.BlockSpec(memory_space=pl.ANY),
            scratch_shapes=[pltpu.VMEM((ROW_TILE, D_MODEL), F32),
                            pltpu.SemaphoreType.DMA(()),
                            pltpu.SemaphoreType.DMA(())]),
        out_shape=jax.ShapeDtypeStruct((PADDED_ROWS, D_MODEL), F32),
        compiler_params=_cparams("arbitrary"),
        name="dispatch",
    )(dest_flat, padded_end, padded, h_rows)


def _expert_body(se_ref, srow_ref, snsub_ref, xs_hbm, wg_hbm, wu_hbm, wd_hbm, bg_ref, bu_ref, bd_ref,
                 y_hbm, xstage, xb, acc, wg_f, wu_f, wd_f, wg_c, wu_c, wd_c, sem_in, sem_out, sem_w):
    s = pl.program_id(0)
    f = pl.program_id(1)
    nsub = snsub_ref[s]
    row0 = srow_ref[s]

    def weight_copies(s_, f_):
        e = se_ref[s_]
        chunk = pl.ds(pl.multiple_of(f_ * FF_TILE, FF_TILE), FF_TILE)
        return (pltpu.make_async_copy(wg_hbm.at[e, :, chunk], wg_f, sem_w.at[0]),
                pltpu.make_async_copy(wu_hbm.at[e, :, chunk], wu_f, sem_w.at[1]),
                pltpu.make_async_copy(wd_hbm.at[e, chunk, :], wd_f, sem_w.at[2]))

    wrap = f == N_FF - 1
    s_next = jnp.minimum(jnp.where(wrap, s + 1, s), MAX_SUPER - 1)
    f_next = jnp.where(wrap, 0, f + 1)
    has_next = jnp.logical_not(wrap & (s == MAX_SUPER - 1)) & (snsub_ref[s_next] > 0)

    last_f = f == N_FF - 1
    n_pair = nsub // 2
    odd = nsub - 2 * n_pair
    pair_rows = 2 * ROW_TILE

    next_super = (s + 1 < MAX_SUPER) & (snsub_ref[jnp.minimum(s + 1, MAX_SUPER - 1)] > 0)

    def in_copy(row_base, t):
        src = xs_hbm.at[pl.ds(pl.multiple_of(row_base + t * ROW_TILE, ROW_TILE), ROW_TILE)]
        return pltpu.make_async_copy(src, xstage.at[t], sem_in.at[t])

    def start_rows(row_base, count):
        def body(t, c):
            in_copy(row_base, t).start()
            return c
        lax.fori_loop(0, count, body, 0)

    def out_copy(off, m):
        dst = y_hbm.at[pl.ds(pl.multiple_of(row0 + off, ROW_TILE), m)]
        return pltpu.make_async_copy(acc.at[pl.ds(off, m)], dst, sem_out)

    def drain_outputs(tiles):
        def body(p, c):
            out_copy(0, pair_rows).wait()
            return c
        lax.fori_loop(0, tiles // 2, body, 0)

        @pl.when(tiles % 2 == 1)
        def _():
            out_copy(0, ROW_TILE).wait()

    def gate(x):
        return jnp.minimum(jnp.dot(x, wg_c[...], preferred_element_type=F32) + bg_ref[...], SWIGLU_LIMIT)

    def up(x):
        return jnp.clip(jnp.dot(x, wu_c[...], preferred_element_type=F32) + bu_ref[...],
                        -SWIGLU_LIMIT, SWIGLU_LIMIT)

    def down(rows, g, u):
        a = ((u + 1.0) * (g * jax.nn.sigmoid(SWIGLU_ALPHA * g))).astype(BF16)
        acc[rows, :] += jnp.dot(a, wd_c[...], preferred_element_type=F32)

    def finish(off, rows_n):
        @pl.when(last_f)
        def _():
            out_copy(off, rows_n).start()

    def compute(off, rows_n):
        rows = pl.ds(off, rows_n)
        x = xb[rows, :]
        down(rows, gate(x), up(x))
        finish(off, rows_n)

    def first_compute(rows_n, nxt):
        rows = pl.ds(0, rows_n)
        x = xb[rows, :]
        g = gate(x)
        wu_c[...] = wu_f[...].astype(BF16)

        @pl.when(has_next)
        def _():
            nxt[1].start()
        u = up(x)
        wd_c[...] = wd_f[...].astype(BF16)

        @pl.when(has_next)
        def _():
            nxt[2].start()
        down(rows, g, u)
        finish(0, rows_n)

    odd_off = pl.multiple_of((nsub - 1) * ROW_TILE, ROW_TILE)

    @pl.when(nsub > 0)
    def _():
        @pl.when((s == 0) & (f == 0))
        def _():
            for cp in weight_copies(s, f):
                cp.start()
            start_rows(row0, nsub)

        nxt = weight_copies(s_next, f_next)
        for cp in weight_copies(s, f):
            cp.wait()
        wg_c[...] = wg_f[...].astype(BF16)

        @pl.when(has_next)
        def _():
            nxt[0].start()

        @pl.when(f == 0)
        def _():
            @pl.when(s > 0)
            def _():
                drain_outputs(snsub_ref[jnp.maximum(s - 1, 0)])

            def load(t, c):
                in_copy(row0, t).wait()
                rows = pl.ds(pl.multiple_of(t * ROW_TILE, ROW_TILE), ROW_TILE)
                xb[rows, :] = xstage[t].astype(BF16)
                acc[rows, :] = jnp.broadcast_to(bd_ref[...], (ROW_TILE, D_MODEL))
                return c
            lax.fori_loop(0, nsub, load, 0)

        @pl.when(last_f & next_super)
        def _():
            s1 = jnp.minimum(s + 1, MAX_SUPER - 1)
            start_rows(srow_ref[s1], snsub_ref[s1])

        @pl.when(n_pair > 0)
        def _():
            first_compute(pair_rows, nxt)

        @pl.when(n_pair == 0)
        def _():
            first_compute(ROW_TILE, nxt)

        def pair(p, c):
            compute(pl.multiple_of(p * pair_rows, pair_rows), pair_rows)
            return c
        lax.fori_loop(1, n_pair, pair, 0)

        @pl.when((odd == 1) & (n_pair > 0))
        def _():
            compute(odd_off, ROW_TILE)

        @pl.when(last_f & jnp.logical_not(next_super))
        def _():
            drain_outputs(nsub)

    @pl.when((s == MAX_SUPER - 1) & last_f)
    def _():
        acc[0:ROW_TILE, :] = jnp.zeros((ROW_TILE, D_MODEL), F32)

        def tail_copy(t):
            dst = y_hbm.at[pl.ds(pl.multiple_of(t * ROW_TILE, ROW_TILE), ROW_TILE)]
            return pltpu.make_async_copy(acc.at[0:ROW_TILE], dst, sem_out)

        def tstart(t, c):
            tail_copy(t).start()
            return c

        def twait(t, c):
            tail_copy(t).wait()
            return c

        first_tail = snsub_ref[MAX_SUPER]
        lax.fori_loop(first_tail, N_ROW_TILES, tstart, 0)
        lax.fori_loop(first_tail, N_ROW_TILES, twait, 0)


def _experts(se, srow, snsub, xs, w_gate, b_gate, w_up, b_up, w_down, b_down):
    def f_eff(s, f, sn):
        return jnp.where(sn[s] > 0, f, N_FF - 1)

    return pl.pallas_call(
        _expert_body,
        grid_spec=pltpu.PrefetchScalarGridSpec(
            num_scalar_prefetch=3,
            grid=(MAX_SUPER, N_FF),
            in_specs=[pl.BlockSpec(memory_space=pl.ANY),
                      pl.BlockSpec(memory_space=pl.ANY),
                      pl.BlockSpec(memory_space=pl.ANY),
                      pl.BlockSpec(memory_space=pl.ANY),
                      pl.BlockSpec((None, 1, FF_TILE), lambda s, f, se, sr, sn: (se[s], 0, f_eff(s, f, sn))),
                      pl.BlockSpec((None, 1, FF_TILE), lambda s, f, se, sr, sn: (se[s], 0, f_eff(s, f, sn))),
                      pl.BlockSpec((None, 1, D_MODEL), lambda s, f, se, sr, sn: (se[s], 0, 0))],
            out_specs=pl.BlockSpec(memory_space=pl.ANY),
            scratch_shapes=[pltpu.VMEM((SUPER_TILES, ROW_TILE, D_MODEL), F32),
                            pltpu.VMEM((SUPER_ROWS, D_MODEL), BF16),
                            pltpu.VMEM((SUPER_ROWS, D_MODEL), F32),
                            pltpu.VMEM((D_MODEL, FF_TILE), F32),
                            pltpu.VMEM((D_MODEL, FF_TILE), F32),
                            pltpu.VMEM((FF_TILE, D_MODEL), F32),
                            pltpu.VMEM((D_MODEL, FF_TILE), BF16),
                            pltpu.VMEM((D_MODEL, FF_TILE), BF16),
                            pltpu.VMEM((FF_TILE, D_MODEL), BF16),
                            pltpu.SemaphoreType.DMA((SUPER_TILES,)),
                            pltpu.SemaphoreType.DMA(()),
                            pltpu.SemaphoreType.DMA((3,))]),
        out_shape=jax.ShapeDtypeStruct((PADDED_ROWS, D_MODEL), F32),
        compiler_params=_cparams("arbitrary", "arbitrary"),
        name="experts",
    )(se, srow, snsub, xs, w_gate, w_up, w_down,
      b_gate.reshape(N_EXPERTS, 1, D_FF), b_up.reshape(N_EXPERTS, 1, D_FF),
      b_down.reshape(N_EXPERTS, 1, D_MODEL))


def _combine_body(dest_ref, y_hbm, x1_ref, w_ref, mod_ref, fg_ref, o_ref, buf, sem, *, tm):
    i = pl.program_id(0)
    slot = i % 2

    def gather(tile, dst_slot):
        def group(g, carry):
            base = pl.multiple_of(g * ROW_DMA_GROUP, ROW_DMA_GROUP)
            for u in range(ROW_DMA_GROUP):
                for k in range(TOP_K):
                    d = dest_ref[(tile * tm + base + u) * TOP_K + k]
                    pltpu.make_async_copy(y_hbm.at[pl.ds(d, 1)], buf.at[dst_slot, k, pl.ds(base + u, 1)],
                                          sem.at[dst_slot]).start()
            return carry
        lax.fori_loop(0, tm // ROW_DMA_GROUP, group, 0)

    @pl.when(i == 0)
    def _():
        gather(i, slot)

    @pl.when(i + 1 < pl.num_programs(0))
    def _():
        gather(i + 1, 1 - slot)

    for k in range(TOP_K):
        pltpu.make_async_copy(y_hbm.at[pl.ds(0, tm)], buf.at[slot, k], sem.at[slot]).wait()
    moe = buf[slot, 0] * w_ref[:, 0:1]
    for k in range(1, TOP_K):
        moe = moe + buf[slot, k] * w_ref[:, k:k + 1]
    x2 = x1_ref[...] + mod_ref[5:6, :] * moe
    o_ref[...] = _rms(x2) * fg_ref[...]


def _combine(dest_flat, y, x1, w_t, mod, final_g):
    tm = 256
    per_batch = SEQ // tm
    return pl.pallas_call(
        functools.partial(_combine_body, tm=tm),
        grid_spec=pltpu.PrefetchScalarGridSpec(
            num_scalar_prefetch=1,
            grid=(TOKENS // tm,),
            in_specs=[pl.BlockSpec(memory_space=pl.ANY),
                      pl.BlockSpec((tm, D_MODEL), lambda i, d: (i, 0)),
                      pl.BlockSpec((tm, TOP_K), lambda i, d: (i, 0)),
                      pl.BlockSpec((None, 6, D_MODEL), lambda i, d: (i // per_batch, 0, 0)),
                      pl.BlockSpec((1, D_MODEL), lambda i, d: (0, 0))],
            out_specs=pl.BlockSpec((tm, D_MODEL), lambda i, d: (i, 0)),
            scratch_shapes=[pltpu.VMEM((2, TOP_K, tm, D_MODEL), F32),
                            pltpu.SemaphoreType.DMA((2,))]),
        out_shape=jax.ShapeDtypeStruct((TOKENS, D_MODEL), F32),
        compiler_params=_cparams("arbitrary"),
        name="combine",
    )(dest_flat, y, x1, w_t, mod, final_g)


def _routing_tables(idx, rank, counts):
    padded = (counts + ROW_TILE - 1) // ROW_TILE * ROW_TILE
    padded_end = jnp.cumsum(padded)
    padded_start = padded_end - padded
    e_i = jnp.arange(N_EXPERTS, dtype=I32)[:, None, None]
    start_of = jnp.sum(jnp.where(idx[None] == e_i, padded_start[:, None, None], 0), axis=0)
    dest = (start_of + rank).T.reshape(N_ROWS).astype(I32)

    tiles = padded // ROW_TILE
    n_super = (tiles + SUPER_TILES - 1) // SUPER_TILES
    super_end = jnp.cumsum(n_super)
    super_start = super_end - n_super
    s_i = jnp.arange(MAX_SUPER, dtype=I32)
    total = super_end[-1]
    valid = s_i < total
    e_of = jnp.minimum(jnp.sum((s_i[:, None] >= super_end[None, :]).astype(I32), axis=1), N_EXPERTS - 1)
    last_e = jnp.minimum(jnp.sum((total - 1 >= super_end).astype(I32)), N_EXPERTS - 1)
    local = s_i - super_start[e_of]
    srow = jnp.where(valid, padded_start[e_of] + local * SUPER_ROWS, 0).astype(I32)
    snsub = jnp.where(valid, jnp.minimum(tiles[e_of] - local * SUPER_TILES, SUPER_TILES), 0).astype(I32)
    snsub = jnp.concatenate([snsub, (padded_end[-1:] // ROW_TILE).astype(I32)])
    se = jnp.where(valid, e_of, last_e).astype(I32)
    return dest, padded_end.astype(I32), padded.astype(I32), se, srow, snsub


def kernel(x, c, ada_w, ada_b, norm1_g, w_in, gla_gate_w2, gla_gate_b, sgu_ln_g, sgu_ln_b, sgu_w, sgu_b,
           gla_norm_g, w_branch_a, w_branch_b, w_out, norm2_g, router_w, router_b, exp_w_gate, exp_b_gate,
           exp_w_up, exp_b_up, exp_w_down, exp_b_down, final_g):
    x2d = x.reshape(TOKENS, D_MODEL)
    mod = _ada(c, ada_w[0], ada_b[0])

    w_in0 = w_in[0]
    w_in_t = w_in0.T
    w_glr = jnp.zeros((LANES, D_MODEL), BF16).at[:GLA_GATE_RANK].set(
        w_in_t[GLR_SRC:GLR_SRC + GLA_GATE_RANK].astype(BF16))
    h, glr = _norm1(x2d, mod, norm1_g, w_glr)
    proj = _inproj(h, w_in_t)

    y_a = _sgu(proj, sgu_ln_g, sgu_ln_b, sgu_w[0], sgu_b[0].T)

    w2p = jnp.zeros((LANES, GLA_DK), BF16).at[:GLA_GATE_RANK].set(gla_gate_w2[0].astype(BF16))
    y_b = _gla(proj, glr, w2p, gla_gate_b, gla_norm_g)

    merged = _merge(y_a, y_b, w_branch_a[0].astype(BF16), w_branch_b[0].astype(BF16), proj)

    rw_t = router_w[0].T
    rw_hi = rw_t.astype(BF16)
    rw_lo = (rw_t - rw_hi.astype(F32)).astype(BF16)
    x1, h_rows, idx, top_w, rank, cnt = _outproj(
        merged, x2d, mod, w_out[0].astype(BF16), norm2_g, rw_hi, rw_lo, router_b[0].reshape(N_EXPERTS, 1))

    dest, padded_end, padded, se, srow, snsub = _routing_tables(idx, rank, cnt[:, 0])
    xs = _dispatch(dest, padded_end, padded, h_rows)
    y = _experts(se, srow, snsub, xs, exp_w_gate[0], exp_b_gate[0], exp_w_up[0], exp_b_up[0],
                 exp_w_down[0], exp_b_down[0])
    out = _combine(dest, y, x1, top_w.T, mod, final_g.reshape(1, D_MODEL))
    return out.reshape(BATCH, SEQ, D_MODEL)
```

```python
import functools

import jax
import jax.numpy as jnp
from jax import lax
from jax.experimental import pallas as pl
from jax.experimental.pallas import tpu as pltpu

F32 = jnp.float32
BF16 = jnp.bfloat16
I32 = jnp.int32

D_MODEL = 2048
BATCH = 4
SEQ = 2048
TOKENS = BATCH * SEQ
CHUNK = 64
SPATIAL_BLOCK = 128
A_GROUPS = 8
A_GROUP_DIM = D_MODEL // A_GROUPS
GLA_HEADS = 4
GLA_DK = D_MODEL // 2
GLA_HEAD_K = GLA_DK // GLA_HEADS
GLA_HEAD_V = D_MODEL // GLA_HEADS
GLA_GATE_RANK = 16
GLA_GATE_TAU = 16.0
N_EXPERTS = 32
TOP_K = 4
D_FF = D_MODEL
SWIGLU_LIMIT = 7.0
SWIGLU_ALPHA = 1.702
NORM_EPS = 1e-6

LANES = 128
VMEM_LIMIT = 56 * 1024 * 1024

COL_U, COL_V, COL_Q, COL_K, COL_VV, COL_R = 0, 2048, 4096, 5120, 6144, 8192
COL_GA, COL_GB = 10240, 12288
PROJ_W = 14336
GLR_SRC = 10240

ROW_TILE = 256
SUPER_TILES = 6
SUPER_ROWS = ROW_TILE * SUPER_TILES
N_ROWS = TOKENS * TOP_K
N_ROW_TILES = (N_ROWS + N_EXPERTS * (ROW_TILE - 1)) // ROW_TILE
PADDED_ROWS = N_ROW_TILES * ROW_TILE
MAX_SUPER = -(-(N_ROW_TILES + N_EXPERTS * (SUPER_TILES - 1)) // SUPER_TILES)
FF_TILE = 512
N_FF = D_FF // FF_TILE
ROW_DMA_GROUP = 8


def _cparams(*sem):
    return pltpu.CompilerParams(dimension_semantics=sem, vmem_limit_bytes=VMEM_LIMIT)


def _rms(x):
    return x * lax.rsqrt(jnp.mean(x * x, axis=-1, keepdims=True) + NORM_EPS)


def _ada_body(c_ref, w_ref, b_ref, o_ref):
    c = c_ref[...]
    cond = c * jax.nn.sigmoid(c)
    o_ref[...] = jnp.dot(cond.astype(BF16), w_ref[...].astype(BF16),
                         preferred_element_type=F32) + b_ref[...]


def _ada(c, ada_w, ada_b):
    tn = 1024
    cp = jnp.zeros((8, D_MODEL), F32).at[:BATCH].set(c)
    out = pl.pallas_call(
        _ada_body,
        grid=(6 * D_MODEL // tn,),
        in_specs=[pl.BlockSpec((8, D_MODEL), lambda j: (0, 0)),
                  pl.BlockSpec((D_MODEL, tn), lambda j: (0, j)),
                  pl.BlockSpec((1, tn), lambda j: (0, j))],
        out_specs=pl.BlockSpec((8, tn), lambda j: (0, j)),
        out_shape=jax.ShapeDtypeStruct((8, 6 * D_MODEL), F32),
        compiler_params=_cparams("arbitrary"),
        name="ada",
    )(cp, ada_w, ada_b.reshape(1, 6 * D_MODEL))
    return out[:BATCH].reshape(BATCH, 6, D_MODEL)


def _norm1_body(x_ref, mod_ref, g_ref, wglr_ref, h_ref, glr_ref):
    h = _rms(x_ref[...]) * g_ref[...] * (1.0 + mod_ref[1:2, :]) + mod_ref[0:1, :]
    hb = h.astype(BF16)
    h_ref[...] = hb
    glr_ref[...] = _dot_nt(hb, wglr_ref[...])


def _norm1(x2d, mod, norm1_g, w_glr):
    tm = 512
    per_batch = SEQ // tm
    return pl.pallas_call(
        _norm1_body,
        grid=(TOKENS // tm,),
        in_specs=[pl.BlockSpec((tm, D_MODEL), lambda i: (i, 0)),
                  pl.BlockSpec((None, 6, D_MODEL), lambda i: (i // per_batch, 0, 0)),
                  pl.BlockSpec((1, D_MODEL), lambda i: (0, 0)),
                  pl.BlockSpec((LANES, D_MODEL), lambda i: (0, 0))],
        out_specs=[pl.BlockSpec((tm, D_MODEL), lambda i: (i, 0)),
                   pl.BlockSpec((tm, LANES), lambda i: (i, 0))],
        out_shape=[jax.ShapeDtypeStruct((TOKENS, D_MODEL), BF16),
                   jax.ShapeDtypeStruct((TOKENS, LANES), F32)],
        compiler_params=_cparams("arbitrary"),
        name="norm1",
    )(x2d, mod, norm1_g, w_glr)


INPROJ_TN = 1024
INPROJ_MAIN_TILES = COL_GA // INPROJ_TN
MXU_N = 256


def _inproj_body(h_ref, wt_ref, proj_ref, wb_ref):
    j = pl.program_id(0)

    @pl.when(pl.program_id(1) == 0)
    def _():
        wb_ref[...] = wt_ref[...].astype(BF16)

    def run(act):
        for n in range(INPROJ_TN // MXU_N):
            cols = slice(n * MXU_N, (n + 1) * MXU_N)
            acc = _dot_nt(h_ref[...], wb_ref[cols, :])
            proj_ref[:, cols] = act(acc).astype(BF16)

    col = j * INPROJ_TN

    @pl.when(col < COL_Q)
    def _():
        run(jax.nn.gelu)

    @pl.when((col >= COL_Q) & (col < COL_R))
    def _():
        run(lambda a: a)

    @pl.when((col >= COL_R) & (col < COL_GA))
    def _():
        run(lambda a: a * jax.nn.sigmoid(a))

    @pl.when(col >= COL_GA)
    def _():
        run(jax.nn.sigmoid)


def _inproj(h, w_in_t):
    tm, tn = 1024, INPROJ_TN

    def w_row(j, i):
        return (pl.multiple_of(j * tn + jnp.where(j >= INPROJ_MAIN_TILES, GLA_GATE_RANK, 0), GLA_GATE_RANK), 0)

    return pl.pallas_call(
        _inproj_body,
        grid=(PROJ_W // tn, TOKENS // tm),
        in_specs=[pl.BlockSpec((tm, D_MODEL), lambda j, i: (i, 0)),
                  pl.BlockSpec((pl.Element(tn), pl.Element(D_MODEL)), w_row)],
        out_specs=pl.BlockSpec((tm, tn), lambda j, i: (i, j)),
        out_shape=jax.ShapeDtypeStruct((TOKENS, PROJ_W), BF16),
        scratch_shapes=[pltpu.VMEM((tn, D_MODEL), BF16)],
        compiler_params=_cparams("arbitrary", "arbitrary"),
        name="inproj",
    )(h, w_in_t)


SGU_BLOCKS = 4


def _sgu_body(u_ref, v_ref, lg_ref, lb_ref, ws_ref, bs_ref, o_ref):
    t_chunk = lax.broadcasted_iota(I32, (SPATIAL_BLOCK, SPATIAL_BLOCK), 0) // CHUNK
    s_chunk = lax.broadcasted_iota(I32, (SPATIAL_BLOCK, SPATIAL_BLOCK), 1) // CHUNK
    mask = s_chunk <= t_chunk
    w = [jnp.where(mask, ws_ref[g], 0.0).astype(BF16) for g in range(A_GROUPS)]
    for blk in range(SGU_BLOCKS):
        rows = slice(blk * SPATIAL_BLOCK, (blk + 1) * SPATIAL_BLOCK)
        v = v_ref[rows, :].astype(F32)
        mu = jnp.mean(v, axis=-1, keepdims=True)
        xc = v - mu
        var = jnp.mean(xc * xc, axis=-1, keepdims=True)
        vn = (xc * lax.rsqrt(var + NORM_EPS) * lg_ref[...] + lb_ref[...]).astype(BF16)
        for g in range(A_GROUPS):
            cols = slice(g * A_GROUP_DIM, (g + 1) * A_GROUP_DIM)
            mixed = jnp.dot(w[g], vn[:, cols], preferred_element_type=F32) + bs_ref[:, g:g + 1]
            o_ref[rows, cols] = (u_ref[rows, cols].astype(F32) * mixed).astype(BF16)


def _sgu(proj, ln_g, ln_b, w_s, b_s_t):
    rows_blk = SGU_BLOCKS * SPATIAL_BLOCK
    nblk = TOKENS // rows_blk
    wb = D_MODEL
    return pl.pallas_call(
        _sgu_body,
        grid=(nblk,),
        in_specs=[pl.BlockSpec((rows_blk, wb), lambda i: (i, COL_U // wb)),
                  pl.BlockSpec((rows_blk, wb), lambda i: (i, COL_V // wb)),
                  pl.BlockSpec((1, wb), lambda i: (0, 0)),
                  pl.BlockSpec((1, wb), lambda i: (0, 0)),
                  pl.BlockSpec((A_GROUPS, SPATIAL_BLOCK, SPATIAL_BLOCK), lambda i: (0, 0, 0)),
                  pl.BlockSpec((SPATIAL_BLOCK, A_GROUPS), lambda i: (0, 0))],
        out_specs=pl.BlockSpec((rows_blk, wb), lambda i: (i, 0)),
        out_shape=jax.ShapeDtypeStruct((TOKENS, wb), BF16),
        compiler_params=_cparams("arbitrary"),
        name="sgu",
    )(proj, proj, ln_g, ln_b, w_s, b_s_t)


def _dot_nt(a, b):
    return lax.dot_general(a, b, (((1,), (1,)), ((), ())), preferred_element_type=F32)


def _dot_tn(a, b):
    return lax.dot_general(a, b, (((0,), (0,)), ((), ())), preferred_element_type=F32)


def _gla_body(q_ref, k_ref, v_ref, r_ref, glr_ref, w2_ref, gb_ref, ng_ref, o_ref,
              st_ref, la_ref, qd_ref, ks_ref, dec_ref, oi_ref, *, rows_blk):
    @pl.when(pl.program_id(1) == 0)
    def _():
        st_ref[...] = jnp.zeros_like(st_ref)

    z = jnp.dot(glr_ref[...].astype(BF16), w2_ref[...], preferred_element_type=F32) + gb_ref[...]
    la_ref[...] = jax.nn.log_sigmoid(z) / GLA_GATE_TAU

    r_i = lax.broadcasted_iota(I32, (CHUNK, CHUNK), 0)
    c_i = lax.broadcasted_iota(I32, (CHUNK, CHUNK), 1)
    causal = c_i <= r_i
    tril = causal.astype(BF16)
    scale = GLA_HEAD_K ** -0.5

    def head_cols(h):
        return (slice(h * GLA_HEAD_K, (h + 1) * GLA_HEAD_K), slice(h * GLA_HEAD_V, (h + 1) * GLA_HEAD_V))

    def intra(c, carry):
        rows = pl.ds(pl.multiple_of(c * CHUNK, CHUNK), CHUNK)
        for h in range(GLA_HEADS):
            kc, vc = head_cols(h)
            la = la_ref[rows, kc]
            hi = la.astype(BF16)
            r1 = la - hi.astype(F32)
            mid = r1.astype(BF16)
            lo = (r1 - mid.astype(F32)).astype(BF16)
            b = (jnp.dot(tril, hi, preferred_element_type=F32)
                 + jnp.dot(tril, mid, preferred_element_type=F32)
                 + jnp.dot(tril, lo, preferred_element_type=F32))
            b_last = b[CHUNK - 1:CHUNK, :]
            q = q_ref[rows, kc].astype(F32) * scale
            k = k_ref[rows, kc].astype(F32)
            q_dec = (q * jnp.exp(b)).astype(BF16)
            k_intra = (k * jnp.exp(-b)).astype(BF16)
            qd_ref[rows, kc] = q_dec
            ks_ref[rows, kc] = (k * jnp.exp(b_last - b)).astype(BF16)
            dec_ref[c, :, kc] = jnp.exp(b_last)
            att = jnp.where(causal, _dot_nt(q_dec, k_intra), 0.0).astype(BF16)
            oi_ref[rows, vc] = jnp.dot(att, v_ref[rows, vc], preferred_element_type=F32)
        return carry

    def inter(c, carry):
        rows = pl.ds(pl.multiple_of(c * CHUNK, CHUNK), CHUNK)
        for h in range(GLA_HEADS):
            kc, vc = head_cols(h)
            st = st_ref[h]
            o = oi_ref[rows, vc] + _dot_nt(qd_ref[rows, kc], st.astype(BF16))
            st_ref[h] = st * dec_ref[c, :, kc] + _dot_tn(v_ref[rows, vc], ks_ref[rows, kc])
            on = _rms(o) * ng_ref[...]
            o_ref[rows, vc] = (on * r_ref[rows, vc].astype(F32)).astype(BF16)
        return carry

    n_chunks = rows_blk // CHUNK
    lax.fori_loop(0, n_chunks, intra, 0, unroll=4)
    lax.fori_loop(0, n_chunks, inter, 0)


def _gla(proj, glr, w2p, gate_b, norm_g):
    rows_blk = 512
    nblk = SEQ // rows_blk
    dk, dv = GLA_DK, D_MODEL

    def row(b, n):
        return b * nblk + n

    return pl.pallas_call(
        functools.partial(_gla_body, rows_blk=rows_blk),
        grid=(BATCH, nblk),
        in_specs=[pl.BlockSpec((rows_blk, dk), lambda b, n: (row(b, n), COL_Q // dk)),
                  pl.BlockSpec((rows_blk, dk), lambda b, n: (row(b, n), COL_K // dk)),
                  pl.BlockSpec((rows_blk, dv), lambda b, n: (row(b, n), COL_VV // dv)),
                  pl.BlockSpec((rows_blk, dv), lambda b, n: (row(b, n), COL_R // dv)),
                  pl.BlockSpec((rows_blk, LANES), lambda b, n: (row(b, n), 0)),
                  pl.BlockSpec((LANES, dk), lambda b, n: (0, 0)),
                  pl.BlockSpec((1, dk), lambda b, n: (0, 0)),
                  pl.BlockSpec((1, GLA_HEAD_V), lambda b, n: (0, 0))],
        out_specs=pl.BlockSpec((rows_blk, dv), lambda b, n: (row(b, n), 0)),
        out_shape=jax.ShapeDtypeStruct((TOKENS, D_MODEL), BF16),
        scratch_shapes=[pltpu.VMEM((GLA_HEADS, GLA_HEAD_V, GLA_HEAD_K), F32),
                        pltpu.VMEM((rows_blk, dk), F32),
                        pltpu.VMEM((rows_blk, dk), BF16),
                        pltpu.VMEM((rows_blk, dk), BF16),
                        pltpu.VMEM((rows_blk // CHUNK, 1, dk), F32),
                        pltpu.VMEM((rows_blk, dv), F32)],
        compiler_params=_cparams("arbitrary", "arbitrary"),
        name="gla",
    )(proj, proj, proj, proj, glr, w2p, gate_b, norm_g)


def _merge_body(ya_ref, yb_ref, wa_ref, wb_ref, ga_ref, gb_ref, o_ref, wa_c, wb_c):
    @pl.when(pl.program_id(1) == 0)
    def _():
        wa_c[...] = wa_ref[...].astype(BF16)
        wb_c[...] = wb_ref[...].astype(BF16)

    a = jnp.dot(ya_ref[...], wa_c[...], preferred_element_type=F32)
    b = jnp.dot(yb_ref[...], wb_c[...], preferred_element_type=F32)
    o_ref[...] = (ga_ref[...].astype(F32) * a + gb_ref[...].astype(F32) * b).astype(BF16)


def _merge(y_a, y_b, wa, wb, proj):
    tm, tn = 1024, 512
    return pl.pallas_call(
        _merge_body,
        grid=(D_MODEL // tn, TOKENS // tm),
        in_specs=[pl.BlockSpec((tm, D_MODEL), lambda j, i: (i, 0)),
                  pl.BlockSpec((tm, D_MODEL), lambda j, i: (i, 0)),
                  pl.BlockSpec((D_MODEL, tn), lambda j, i: (0, j)),
                  pl.BlockSpec((D_MODEL, tn), lambda j, i: (0, j)),
                  pl.BlockSpec((tm, tn), lambda j, i: (i, COL_GA // tn + j)),
                  pl.BlockSpec((tm, tn), lambda j, i: (i, COL_GB // tn + j))],
        out_specs=pl.BlockSpec((tm, tn), lambda j, i: (i, j)),
        out_shape=jax.ShapeDtypeStruct((TOKENS, D_MODEL), BF16),
        scratch_shapes=[pltpu.VMEM((D_MODEL, tn), BF16), pltpu.VMEM((D_MODEL, tn), BF16)],
        compiler_params=_cparams("arbitrary", "arbitrary"),
        name="merge",
    )(y_a, y_b, wa, wb, proj, proj)


OUTPROJ_PARTS = 1


def _outproj_body(m_ref, x_ref, mod_ref, wo_ref, g2_ref, rwh_ref, rwl_ref, rb_ref,
                  x1_ref, hp_ref, idx_ref, w_ref, rank_ref, cnt_ref, run_ref, *, tm):
    @pl.when(pl.program_id(0) == 0)
    def _():
        run_ref[...] = jnp.zeros_like(run_ref)

    th = tm // OUTPROJ_PARTS
    e_i = lax.broadcasted_iota(I32, (N_EXPERTS, th), 0).astype(F32)
    before = (lax.broadcasted_iota(I32, (th, th), 0) < lax.broadcasted_iota(I32, (th, th), 1)).astype(BF16)
    run = run_ref[...]
    for part in range(OUTPROJ_PARTS):
        rows = slice(part * th, (part + 1) * th)
        y = jnp.dot(m_ref[rows, :], wo_ref[...], preferred_element_type=F32)
        x1 = x_ref[rows, :] + mod_ref[2:3, :] * y
        x1_ref[rows, :] = x1
        h2 = _rms(x1) * g2_ref[...] * (1.0 + mod_ref[4:5, :]) + mod_ref[3:4, :]
        hp_ref[rows, :] = h2
        hb = h2.astype(BF16)
        hf = hb.astype(F32)

        h_lo = (h2 - hf).astype(BF16)
        logits = (_dot_nt(rwh_ref[...], hb) + _dot_nt(rwh_ref[...], h_lo) + _dot_nt(rwl_ref[...], hb)
                  + rb_ref[...])
        vals, idxs = [], []
        l = logits
        for _ in range(TOP_K):
            m = jnp.max(l, axis=0, keepdims=True)
            i = jnp.min(jnp.where(l == m, e_i, float(N_EXPERTS)), axis=0, keepdims=True)
            vals.append(m)
            idxs.append(i)
            l = jnp.where(e_i == i, -jnp.inf, l)
        ex = [jnp.exp(v - vals[0]) for v in vals]
        den = ex[0] + ex[1] + ex[2] + ex[3]
        for k in range(TOP_K):
            onehot = e_i == idxs[k]
            pref = jnp.dot(onehot.astype(BF16), before, preferred_element_type=F32) + run[:, 0:1]
            rank_ref[k:k + 1, rows] = jnp.sum(jnp.where(onehot, pref, 0.0), axis=0, keepdims=True).astype(I32)
            run = run + jnp.sum(onehot.astype(F32), axis=1, keepdims=True)
            idx_ref[k:k + 1, rows] = idxs[k].astype(I32)
            w_ref[k:k + 1, rows] = ex[k] / den
    run_ref[...] = run
    cnt_ref[...] = run.astype(I32)


def _outproj(merged, x2d, mod, w_out, norm2_g, rw_hi, rw_lo, router_b):
    tm = 512
    per_batch = SEQ // tm
    body = functools.partial(_outproj_body, tm=tm)
    return pl.pallas_call(
        body,
        grid=(TOKENS // tm,),
        in_specs=[pl.BlockSpec((tm, D_MODEL), lambda i: (i, 0)),
                  pl.BlockSpec((tm, D_MODEL), lambda i: (i, 0)),
                  pl.BlockSpec((None, 6, D_MODEL), lambda i: (i // per_batch, 0, 0)),
                  pl.BlockSpec((D_MODEL, D_MODEL), lambda i: (0, 0)),
                  pl.BlockSpec((1, D_MODEL), lambda i: (0, 0)),
                  pl.BlockSpec((N_EXPERTS, D_MODEL), lambda i: (0, 0)),
                  pl.BlockSpec((N_EXPERTS, D_MODEL), lambda i: (0, 0)),
                  pl.BlockSpec((N_EXPERTS, 1), lambda i: (0, 0))],
        out_specs=[pl.BlockSpec((tm, D_MODEL), lambda i: (i, 0)),
                   pl.BlockSpec((tm, D_MODEL), lambda i: (i, 0)),
                   pl.BlockSpec((TOP_K, tm), lambda i: (0, i)),
                   pl.BlockSpec((TOP_K, tm), lambda i: (0, i)),
                   pl.BlockSpec((TOP_K, tm), lambda i: (0, i)),
                   pl.BlockSpec((N_EXPERTS, LANES), lambda i: (0, 0))],
        out_shape=[jax.ShapeDtypeStruct((TOKENS, D_MODEL), F32),
                   jax.ShapeDtypeStruct((TOKENS, D_MODEL), F32),
                   jax.ShapeDtypeStruct((TOP_K, TOKENS), I32),
                   jax.ShapeDtypeStruct((TOP_K, TOKENS), F32),
                   jax.ShapeDtypeStruct((TOP_K, TOKENS), I32),
                   jax.ShapeDtypeStruct((N_EXPERTS, LANES), I32)],
        scratch_shapes=[pltpu.VMEM((N_EXPERTS, LANES), F32)],
        compiler_params=_cparams("arbitrary"),
        name="outproj",
    )(merged, x2d, mod, w_out, norm2_g, rw_hi, rw_lo, router_b)


def _dispatch_body(dest_ref, pend_ref, padded_ref, h_ref, xs_hbm, zero_ref, zsem, sem, *, tm):
    i = pl.program_id(0)

    @pl.when(i == 0)
    def _():
        _dispatch_clear(pend_ref, padded_ref, xs_hbm, zero_ref, zsem)

    def group(g, carry):
        base = pl.multiple_of(g * ROW_DMA_GROUP, ROW_DMA_GROUP)
        for u in range(ROW_DMA_GROUP):
            src = h_ref.at[pl.ds(base + u, 1)]
            for k in range(TOP_K):
                d = dest_ref[(i * tm + base + u) * TOP_K + k]
                pltpu.make_async_copy(src, xs_hbm.at[pl.ds(d, 1)], sem).start()
        return carry
    lax.fori_loop(0, tm // ROW_DMA_GROUP, group, 0)
    for k in range(TOP_K):
        pltpu.make_async_copy(h_ref, xs_hbm.at[pl.ds(0, tm)], sem).wait()


def _dispatch_clear(pend_ref, padded_ref, xs_hbm, zero_ref, zsem):
    zero_ref[...] = jnp.zeros_like(zero_ref)

    def zero_copy(e):
        return pltpu.make_async_copy(
            zero_ref, xs_hbm.at[pl.ds(pl.multiple_of(pend_ref[e] - ROW_TILE, ROW_TILE), ROW_TILE)], zsem)

    def zstart(e, c):
        @pl.when(padded_ref[e] > 0)
        def _():
            zero_copy(e).start()
        return c

    def zwait(e, c):
        @pl.when(padded_ref[e] > 0)
        def _():
            zero_copy(e).wait()
        return c

    def tail_copy(t):
        return pltpu.make_async_copy(
            zero_ref, xs_hbm.at[pl.ds(pl.multiple_of(t * ROW_TILE, ROW_TILE), ROW_TILE)], zsem)

    def tstart(t, c):
        tail_copy(t).start()
        return c

    def twait(t, c):
        tail_copy(t).wait()
        return c

    first_tail = pend_ref[N_EXPERTS - 1] // ROW_TILE
    lax.fori_loop(0, N_EXPERTS, zstart, 0)
    lax.fori_loop(first_tail, N_ROW_TILES, tstart, 0)
    lax.fori_loop(0, N_EXPERTS, zwait, 0)
    lax.fori_loop(first_tail, N_ROW_TILES, twait, 0)


def _dispatch(dest_flat, padded_end, padded, h_rows):
    tm = 1024
    return pl.pallas_call(
        functools.partial(_dispatch_body, tm=tm),
        grid_spec=pltpu.PrefetchScalarGridSpec(
            num_scalar_prefetch=3,
            grid=(TOKENS // tm,),
            in_specs=[pl.BlockSpec((tm, D_MODEL), lambda i, d, pe, pd: (i, 0))],
            out_specs=pl.BlockSpec(memory_space=pl.ANY),
            scratch_shapes=[pltpu.VMEM((ROW_TILE, D_MODEL), F32),
                            pltpu.SemaphoreType.DMA(()),
                            pltpu.SemaphoreType.DMA(())]),
        out_shape=jax.ShapeDtypeStruct((PADDED_ROWS, D_MODEL), F32),
        compiler_params=_cparams("arbitrary"),
        name="dispatch",
    )(dest_flat, padded_end, padded, h_rows)


def _expert_body(se_ref, srow_ref, snsub_ref, xs_hbm, wg_hbm, wu_hbm, wd_hbm, bg_ref, bu_ref, bd_ref,
                 y_hbm, xstage, xb, acc, wg_f, wu_f, wd_f, wg_c, wu_c, wd_c, sem_in, sem_out, sem_w):
    s = pl.program_id(0)
    f = pl.program_id(1)
    nsub = snsub_ref[s]
    row0 = srow_ref[s]

    def weight_copies(s_, f_):
        e = se_ref[s_]
        chunk = pl.ds(pl.multiple_of(f_ * FF_TILE, FF_TILE), FF_TILE)
        return (pltpu.make_async_copy(wg_hbm.at[e, :, chunk], wg_f, sem_w.at[0]),
                pltpu.make_async_copy(wu_hbm.at[e, :, chunk], wu_f, sem_w.at[1]),
                pltpu.make_async_copy(wd_hbm.at[e, chunk, :], wd_f, sem_w.at[2]))

    wrap = f == N_FF - 1
    s_next = jnp.minimum(jnp.where(wrap, s + 1, s), MAX_SUPER - 1)
    f_next = jnp.where(wrap, 0, f + 1)
    has_next = jnp.logical_not(wrap & (s == MAX_SUPER - 1)) & (snsub_ref[s_next] > 0)

    last_f = f == N_FF - 1
    n_pair = nsub // 2
    odd = nsub - 2 * n_pair
    pair_rows = 2 * ROW_TILE

    next_super = (s + 1 < MAX_SUPER) & (snsub_ref[jnp.minimum(s + 1, MAX_SUPER - 1)] > 0)

    def in_copy(row_base, t):
        src = xs_hbm.at[pl.ds(pl.multiple_of(row_base + t * ROW_TILE, ROW_TILE), ROW_TILE)]
        return pltpu.make_async_copy(src, xstage.at[t], sem_in.at[t])

    def start_rows(row_base, count):
        def body(t, c):
            in_copy(row_base, t).start()
            return c
        lax.fori_loop(0, count, body, 0)

    def out_copy(off, m):
        dst = y_hbm.at[pl.ds(pl.multiple_of(row0 + off, ROW_TILE), m)]
        return pltpu.make_async_copy(acc.at[pl.ds(off, m)], dst, sem_out)

    def drain_outputs(tiles):
        def body(p, c):
            out_copy(0, pair_rows).wait()
            return c
        lax.fori_loop(0, tiles // 2, body, 0)

        @pl.when(tiles % 2 == 1)
        def _():
            out_copy(0, ROW_TILE).wait()

    def gate(x):
        return jnp.minimum(jnp.dot(x, wg_c[...], preferred_element_type=F32) + bg_ref[...], SWIGLU_LIMIT)

    def up(x):
        return jnp.clip(jnp.dot(x, wu_c[...], preferred_element_type=F32) + bu_ref[...],
                        -SWIGLU_LIMIT, SWIGLU_LIMIT)

    def down(rows, g, u):
        a = ((u + 1.0) * (g * jax.nn.sigmoid(SWIGLU_ALPHA * g))).astype(BF16)
        acc[rows, :] += jnp.dot(a, wd_c[...], preferred_element_type=F32)

    def finish(off, rows_n):
        @pl.when(last_f)
        def _():
            out_copy(off, rows_n).start()

    def compute(off, rows_n):
        rows = pl.ds(off, rows_n)
        x = xb[rows, :]
        down(rows, gate(x), up(x))
        finish(off, rows_n)

    def first_compute(rows_n, nxt):
        rows = pl.ds(0, rows_n)
        x = xb[rows, :]
        g = gate(x)
        wu_c[...] = wu_f[...].astype(BF16)

        @pl.when(has_next)
        def _():
            nxt[1].start()
        u = up(x)
        wd_c[...] = wd_f[...].astype(BF16)

        @pl.when(has_next)
        def _():
            nxt[2].start()
        down(rows, g, u)
        finish(0, rows_n)

    odd_off = pl.multiple_of((nsub - 1) * ROW_TILE, ROW_TILE)

    @pl.when(nsub > 0)
    def _():
        @pl.when((s == 0) & (f == 0))
        def _():
            for cp in weight_copies(s, f):
                cp.start()
            start_rows(row0, nsub)

        nxt = weight_copies(s_next, f_next)
        for cp in weight_copies(s, f):
            cp.wait()
        wg_c[...] = wg_f[...].astype(BF16)

        @pl.when(has_next)
        def _():
            nxt[0].start()

        @pl.when(f == 0)
        def _():
            @pl.when(s > 0)
            def _():
                drain_outputs(snsub_ref[jnp.maximum(s - 1, 0)])

            def load(t, c):
                in_copy(row0, t).wait()
                rows = pl.ds(pl.multiple_of(t * ROW_TILE, ROW_TILE), ROW_TILE)
                xb[rows, :] = xstage[t].astype(BF16)
                acc[rows, :] = jnp.broadcast_to(bd_ref[...], (ROW_TILE, D_MODEL))
                return c
            lax.fori_loop(0, nsub, load, 0)

        @pl.when(last_f & next_super)
        def _():
            s1 = jnp.minimum(s + 1, MAX_SUPER - 1)
            start_rows(srow_ref[s1], snsub_ref[s1])

        @pl.when(n_pair > 0)
        def _():
            first_compute(pair_rows, nxt)

        @pl.when(n_pair == 0)
        def _():
            first_compute(ROW_TILE, nxt)

        def pair(p, c):
            compute(pl.multiple_of(p * pair_rows, pair_rows), pair_rows)
            return c
        lax.fori_loop(1, n_pair, pair, 0)

        @pl.when((odd == 1) & (n_pair > 0))
        def _():
            compute(odd_off, ROW_TILE)

        @pl.when(last_f & jnp.logical_not(next_super))
        def _():
            drain_outputs(nsub)

    @pl.when((s == MAX_SUPER - 1) & last_f)
    def _():
        acc[0:ROW_TILE, :] = jnp.zeros((ROW_TILE, D_MODEL), F32)

        def tail_copy(t):
            dst = y_hbm.at[pl.ds(pl.multiple_of(t * ROW_TILE, ROW_TILE), ROW_TILE)]
            return pltpu.make_async_copy(acc.at[0:ROW_TILE], dst, sem_out)

        def tstart(t, c):
            tail_copy(t).start()
            return c

        def twait(t, c):
            tail_copy(t).wait()
            return c

        first_tail = snsub_ref[MAX_SUPER]
        lax.fori_loop(first_tail, N_ROW_TILES, tstart, 0)
        lax.fori_loop(first_tail, N_ROW_TILES, twait, 0)


def _experts(se, srow, snsub, xs, w_gate, b_gate, w_up, b_up, w_down, b_down):
    def f_eff(s, f, sn):
        return jnp.where(sn[s] > 0, f, N_FF - 1)

    return pl.pallas_call(
        _expert_body,
        grid_spec=pltpu.PrefetchScalarGridSpec(
            num_scalar_prefetch=3,
            grid=(MAX_SUPER, N_FF),
            in_specs=[pl.BlockSpec(memory_space=pl.ANY),
                      pl.BlockSpec(memory_space=pl.ANY),
                      pl.BlockSpec(memory_space=pl.ANY),
                      pl.BlockSpec(memory_space=pl.ANY),
                      pl.BlockSpec((None, 1, FF_TILE), lambda s, f, se, sr, sn: (se[s], 0, f_eff(s, f, sn))),
                      pl.BlockSpec((None, 1, FF_TILE), lambda s, f, se, sr, sn: (se[s], 0, f_eff(s, f, sn))),
                      pl.BlockSpec((None, 1, D_MODEL), lambda s, f, se, sr, sn: (se[s], 0, 0))],
            out_specs=pl.BlockSpec(memory_space=pl.ANY),
            scratch_shapes=[pltpu.VMEM((SUPER_TILES, ROW_TILE, D_MODEL), F32),
                            pltpu.VMEM((SUPER_ROWS, D_MODEL), BF16),
                            pltpu.VMEM((SUPER_ROWS, D_MODEL), F32),
                            pltpu.VMEM((D_MODEL, FF_TILE), F32),
                            pltpu.VMEM((D_MODEL, FF_TILE), F32),
                            pltpu.VMEM((FF_TILE, D_MODEL), F32),
                            pltpu.VMEM((D_MODEL, FF_TILE), BF16),
                            pltpu.VMEM((D_MODEL, FF_TILE), BF16),
                            pltpu.VMEM((FF_TILE, D_MODEL), BF16),
                            pltpu.SemaphoreType.DMA((SUPER_TILES,)),
                            pltpu.SemaphoreType.DMA(()),
                            pltpu.SemaphoreType.DMA((3,))]),
        out_shape=jax.ShapeDtypeStruct((PADDED_ROWS, D_MODEL), F32),
        compiler_params=_cparams("arbitrary", "arbitrary"),
        name="experts",
    )(se, srow, snsub, xs, w_gate, w_up, w_down,
      b_gate.reshape(N_EXPERTS, 1, D_FF), b_up.reshape(N_EXPERTS, 1, D_FF),
      b_down.reshape(N_EXPERTS, 1, D_MODEL))


def _combine_body(dest_ref, y_hbm, x1_ref, w_ref, mod_ref, fg_ref, o_ref, buf, sem, *, tm):
    i = pl.program_id(0)
    slot = i % 2

    def gather(tile, dst_slot):
        def group(g, carry):
            base = pl.multiple_of(g * ROW_DMA_GROUP, ROW_DMA_GROUP)
            for u in range(ROW_DMA_GROUP):
                for k in range(TOP_K):
                    d = dest_ref[(tile * tm + base + u) * TOP_K + k]
                    pltpu.make_async_copy(y_hbm.at[pl.ds(d, 1)], buf.at[dst_slot, k, pl.ds(base + u, 1)],
                                          sem.at[dst_slot]).start()
            return carry
        lax.fori_loop(0, tm // ROW_DMA_GROUP, group, 0)

    @pl.when(i == 0)
    def _():
        gather(i, slot)

    @pl.when(i + 1 < pl.num_programs(0))
    def _():
        gather(i + 1, 1 - slot)

    for k in range(TOP_K):
        pltpu.make_async_copy(y_hbm.at[pl.ds(0, tm)], buf.at[slot, k], sem.at[slot]).wait()
    moe = buf[slot, 0] * w_ref[:, 0:1]
    for k in range(1, TOP_K):
        moe = moe + buf[slot, k] * w_ref[:, k:k + 1]
    x2 = x1_ref[...] + mod_ref[5:6, :] * moe
    o_ref[...] = _rms(x2) * fg_ref[...]


def _combine(dest_flat, y, x1, w_t, mod, final_g):
    tm = 256
    per_batch = SEQ // tm
    return pl.pallas_call(
        functools.partial(_combine_body, tm=tm),
        grid_spec=pltpu.PrefetchScalarGridSpec(
            num_scalar_prefetch=1,
            grid=(TOKENS // tm,),
            in_specs=[pl.BlockSpec(memory_space=pl.ANY),
                      pl.BlockSpec((tm, D_MODEL), lambda i, d: (i, 0)),
                      pl.BlockSpec((tm, TOP_K), lambda i, d: (i, 0)),
                      pl.BlockSpec((None, 6, D_MODEL), lambda i, d: (i // per_batch, 0, 0)),
                      pl.BlockSpec((1, D_MODEL), lambda i, d: (0, 0))],
            out_specs=pl.BlockSpec((tm, D_MODEL), lambda i, d: (i, 0)),
            scratch_shapes=[pltpu.VMEM((2, TOP_K, tm, D_MODEL), F32),
                            pltpu.SemaphoreType.DMA((2,))]),
        out_shape=jax.ShapeDtypeStruct((TOKENS, D_MODEL), F32),
        compiler_params=_cparams("arbitrary"),
        name="combine",
    )(dest_flat, y, x1, w_t, mod, final_g)


def _routing_tables(idx, rank, counts):
    padded = (counts + ROW_TILE - 1) // ROW_TILE * ROW_TILE
    padded_end = jnp.cumsum(padded)
    padded_start = padded_end - padded
    e_i = jnp.arange(N_EXPERTS, dtype=I32)[:, None, None]
    start_of = jnp.sum(jnp.where(idx[None] == e_i, padded_start[:, None, None], 0), axis=0)
    dest = (start_of + rank).T.reshape(N_ROWS).astype(I32)

    tiles = padded // ROW_TILE
    n_super = (tiles + SUPER_TILES - 1) // SUPER_TILES
    super_end = jnp.cumsum(n_super)
    super_start = super_end - n_super
    s_i = jnp.arange(MAX_SUPER, dtype=I32)
    total = super_end[-1]
    valid = s_i < total
    e_of = jnp.minimum(jnp.sum((s_i[:, None] >= super_end[None, :]).astype(I32), axis=1), N_EXPERTS - 1)
    last_e = jnp.minimum(jnp.sum((total - 1 >= super_end).astype(I32)), N_EXPERTS - 1)
    local = s_i - super_start[e_of]
    srow = jnp.where(valid, padded_start[e_of] + local * SUPER_ROWS, 0).astype(I32)
    snsub = jnp.where(valid, jnp.minimum(tiles[e_of] - local * SUPER_TILES, SUPER_TILES), 0).astype(I32)
    snsub = jnp.concatenate([snsub, (padded_end[-1:] // ROW_TILE).astype(I32)])
    se = jnp.where(valid, e_of, last_e).astype(I32)
    return dest, padded_end.astype(I32), padded.astype(I32), se, srow, snsub


def kernel(x, c, ada_w, ada_b, norm1_g, w_in, gla_gate_w2, gla_gate_b, sgu_ln_g, sgu_ln_b, sgu_w, sgu_b,
           gla_norm_g, w_branch_a, w_branch_b, w_out, norm2_g, router_w, router_b, exp_w_gate, exp_b_gate,
           exp_w_up, exp_b_up, exp_w_down, exp_b_down, final_g):
    x2d = x.reshape(TOKENS, D_MODEL)
    mod = _ada(c, ada_w[0], ada_b[0])

    w_in0 = w_in[0]
    w_in_t = w_in0.T
    w_glr = jnp.zeros((LANES, D_MODEL), BF16).at[:GLA_GATE_RANK].set(
        w_in_t[GLR_SRC:GLR_SRC + GLA_GATE_RANK].astype(BF16))
    h, glr = _norm1(x2d, mod, norm1_g, w_glr)
    proj = _inproj(h, w_in_t)

    y_a = _sgu(proj, sgu_ln_g, sgu_ln_b, sgu_w[0], sgu_b[0].T)

    w2p = jnp.zeros((LANES, GLA_DK), BF16).at[:GLA_GATE_RANK].set(gla_gate_w2[0].astype(BF16))
    y_b = _gla(proj, glr, w2p, gla_gate_b, gla_norm_g)

    merged = _merge(y_a, y_b, w_branch_a[0], w_branch_b[0], proj)

    rw_t = router_w[0].T
    rw_hi = rw_t.astype(BF16)
    rw_lo = (rw_t - rw_hi.astype(F32)).astype(BF16)
    x1, h_rows, idx, top_w, rank, cnt = _outproj(
        merged, x2d, mod, w_out[0].astype(BF16), norm2_g, rw_hi, rw_lo, router_b[0].reshape(N_EXPERTS, 1))

    dest, padded_end, padded, se, srow, snsub = _routing_tables(idx, rank, cnt[:, 0])
    xs = _dispatch(dest, padded_end, padded, h_rows)
    y = _experts(se, srow, snsub, xs, exp_w_gate[0], exp_b_gate[0], exp_w_up[0], exp_b_up[0],
                 exp_w_down[0], exp_b_down[0])
    out = _combine(dest, y, x1, top_w.T, mod, final_g.reshape(1, D_MODEL))
    return out.reshape(BATCH, SEQ, D_MODEL)
```

```python
import functools

import jax
import jax.numpy as jnp
from jax import lax
from jax.experimental import pallas as pl
from jax.experimental.pallas import tpu as pltpu

F32 = jnp.float32
BF16 = jnp.bfloat16
I32 = jnp.int32

D_MODEL = 2048
BATCH = 4
SEQ = 2048
TOKENS = BATCH * SEQ
CHUNK = 64
SPATIAL_BLOCK = 128
A_GROUPS = 8
A_GROUP_DIM = D_MODEL // A_GROUPS
GLA_HEADS = 4
GLA_DK = D_MODEL // 2
GLA_HEAD_K = GLA_DK // GLA_HEADS
GLA_HEAD_V = D_MODEL // GLA_HEADS
GLA_GATE_RANK = 16
GLA_GATE_TAU = 16.0
N_EXPERTS = 32
TOP_K = 4
D_FF = D_MODEL
SWIGLU_LIMIT = 7.0
SWIGLU_ALPHA = 1.702
NORM_EPS = 1e-6

LANES = 128
VMEM_LIMIT = 56 * 1024 * 1024

COL_U, COL_V, COL_Q, COL_K, COL_VV, COL_R = 0, 2048, 4096, 5120, 6144, 8192
COL_GA, COL_GB = 10240, 12288
PROJ_W = 14336
GLR_SRC = 10240

ROW_TILE = 256
SUPER_TILES = 6
SUPER_ROWS = ROW_TILE * SUPER_TILES
N_ROWS = TOKENS * TOP_K
N_ROW_TILES = (N_ROWS + N_EXPERTS * (ROW_TILE - 1)) // ROW_TILE
PADDED_ROWS = N_ROW_TILES * ROW_TILE
MAX_SUPER = -(-(N_ROW_TILES + N_EXPERTS * (SUPER_TILES - 1)) // SUPER_TILES)
FF_TILE = 512
N_FF = D_FF // FF_TILE
ROW_DMA_GROUP = 8


def _cparams(*sem):
    return pltpu.CompilerParams(dimension_semantics=sem, vmem_limit_bytes=VMEM_LIMIT)


def _rms(x):
    return x * lax.rsqrt(jnp.mean(x * x, axis=-1, keepdims=True) + NORM_EPS)


def _ada_body(c_ref, w_ref, b_ref, o_ref):
    c = c_ref[...]
    cond = c * jax.nn.sigmoid(c)
    o_ref[...] = jnp.dot(cond.astype(BF16), w_ref[...].astype(BF16),
                         preferred_element_type=F32) + b_ref[...]


def _ada(c, ada_w, ada_b):
    tn = 1024
    cp = jnp.zeros((8, D_MODEL), F32).at[:BATCH].set(c)
    out = pl.pallas_call(
        _ada_body,
        grid=(6 * D_MODEL // tn,),
        in_specs=[pl.BlockSpec((8, D_MODEL), lambda j: (0, 0)),
                  pl.BlockSpec((D_MODEL, tn), lambda j: (0, j)),
                  pl.BlockSpec((1, tn), lambda j: (0, j))],
        out_specs=pl.BlockSpec((8, tn), lambda j: (0, j)),
        out_shape=jax.ShapeDtypeStruct((8, 6 * D_MODEL), F32),
        compiler_params=_cparams("arbitrary"),
        name="ada",
    )(cp, ada_w, ada_b.reshape(1, 6 * D_MODEL))
    return out[:BATCH].reshape(BATCH, 6, D_MODEL)


def _norm1_body(x_ref, mod_ref, g_ref, wglr_ref, h_ref, glr_ref):
    h = _rms(x_ref[...]) * g_ref[...] * (1.0 + mod_ref[1:2, :]) + mod_ref[0:1, :]
    hb = h.astype(BF16)
    h_ref[...] = hb
    glr_ref[...] = _dot_nt(hb, wglr_ref[...])


def _norm1(x2d, mod, norm1_g, w_glr):
    tm = 512
    per_batch = SEQ // tm
    return pl.pallas_call(
        _norm1_body,
        grid=(TOKENS // tm,),
        in_specs=[pl.BlockSpec((tm, D_MODEL), lambda i: (i, 0)),
                  pl.BlockSpec((None, 6, D_MODEL), lambda i: (i // per_batch, 0, 0)),
                  pl.BlockSpec((1, D_MODEL), lambda i: (0, 0)),
                  pl.BlockSpec((LANES, D_MODEL), lambda i: (0, 0))],
        out_specs=[pl.BlockSpec((tm, D_MODEL), lambda i: (i, 0)),
                   pl.BlockSpec((tm, LANES), lambda i: (i, 0))],
        out_shape=[jax.ShapeDtypeStruct((TOKENS, D_MODEL), BF16),
                   jax.ShapeDtypeStruct((TOKENS, LANES), F32)],
        compiler_params=_cparams("arbitrary"),
        name="norm1",
    )(x2d, mod, norm1_g, w_glr)


INPROJ_TN = 1024
INPROJ_MAIN_TILES = COL_GA // INPROJ_TN
MXU_N = 256


def _inproj_body(h_ref, wt_ref, proj_ref, wb_ref):
    j = pl.program_id(0)

    @pl.when(pl.program_id(1) == 0)
    def _():
        wb_ref[...] = wt_ref[...].astype(BF16)

    def run(act):
        for n in range(INPROJ_TN // MXU_N):
            cols = slice(n * MXU_N, (n + 1) * MXU_N)
            acc = _dot_nt(h_ref[...], wb_ref[cols, :])
            proj_ref[:, cols] = act(acc).astype(BF16)

    col = j * INPROJ_TN

    @pl.when(col < COL_Q)
    def _():
        run(jax.nn.gelu)

    @pl.when((col >= COL_Q) & (col < COL_R))
    def _():
        run(lambda a: a)

    @pl.when((col >= COL_R) & (col < COL_GA))
    def _():
        run(lambda a: a * jax.nn.sigmoid(a))

    @pl.when(col >= COL_GA)
    def _():
        run(jax.nn.sigmoid)


def _inproj(h, w_in_t):
    tm, tn = 1024, INPROJ_TN

    def w_row(j, i):
        return (pl.multiple_of(j * tn + jnp.where(j >= INPROJ_MAIN_TILES, GLA_GATE_RANK, 0), GLA_GATE_RANK), 0)

    return pl.pallas_call(
        _inproj_body,
        grid=(PROJ_W // tn, TOKENS // tm),
        in_specs=[pl.BlockSpec((tm, D_MODEL), lambda j, i: (i, 0)),
                  pl.BlockSpec((pl.Element(tn), pl.Element(D_MODEL)), w_row)],
        out_specs=pl.BlockSpec((tm, tn), lambda j, i: (i, j)),
        out_shape=jax.ShapeDtypeStruct((TOKENS, PROJ_W), BF16),
        scratch_shapes=[pltpu.VMEM((tn, D_MODEL), BF16)],
        compiler_params=_cparams("arbitrary", "arbitrary"),
        name="inproj",
    )(h, w_in_t)


SGU_BLOCKS = 4


def _sgu_body(u_ref, v_ref, lg_ref, lb_ref, ws_ref, bs_ref, o_ref):
    t_chunk = lax.broadcasted_iota(I32, (SPATIAL_BLOCK, SPATIAL_BLOCK), 0) // CHUNK
    s_chunk = lax.broadcasted_iota(I32, (SPATIAL_BLOCK, SPATIAL_BLOCK), 1) // CHUNK
    mask = s_chunk <= t_chunk
    w = [jnp.where(mask, ws_ref[g], 0.0).astype(BF16) for g in range(A_GROUPS)]
    for blk in range(SGU_BLOCKS):
        rows = slice(blk * SPATIAL_BLOCK, (blk + 1) * SPATIAL_BLOCK)
        v = v_ref[rows, :].astype(F32)
        mu = jnp.mean(v, axis=-1, keepdims=True)
        xc = v - mu
        var = jnp.mean(xc * xc, axis=-1, keepdims=True)
        vn = (xc * lax.rsqrt(var + NORM_EPS) * lg_ref[...] + lb_ref[...]).astype(BF16)
        for g in range(A_GROUPS):
            cols = slice(g * A_GROUP_DIM, (g + 1) * A_GROUP_DIM)
            mixed = jnp.dot(w[g], vn[:, cols], preferred_element_type=F32) + bs_ref[:, g:g + 1]
            o_ref[rows, cols] = (u_ref[rows, cols].astype(F32) * mixed).astype(BF16)


def _sgu(proj, ln_g, ln_b, w_s, b_s_t):
    rows_blk = SGU_BLOCKS * SPATIAL_BLOCK
    nblk = TOKENS // rows_blk
    wb = D_MODEL
    return pl.pallas_call(
        _sgu_body,
        grid=(nblk,),
        in_specs=[pl.BlockSpec((rows_blk, wb), lambda i: (i, COL_U // wb)),
                  pl.BlockSpec((rows_blk, wb), lambda i: (i, COL_V // wb)),
                  pl.BlockSpec((1, wb), lambda i: (0, 0)),
                  pl.BlockSpec((1, wb), lambda i: (0, 0)),
                  pl.BlockSpec((A_GROUPS, SPATIAL_BLOCK, SPATIAL_BLOCK), lambda i: (0, 0, 0)),
                  pl.BlockSpec((SPATIAL_BLOCK, A_GROUPS), lambda i: (0, 0))],
        out_specs=pl.BlockSpec((rows_blk, wb), lambda i: (i, 0)),
        out_shape=jax.ShapeDtypeStruct((TOKENS, wb), BF16),
        compiler_params=_cparams("arbitrary"),
        name="sgu",
    )(proj, proj, ln_g, ln_b, w_s, b_s_t)


def _dot_nt(a, b):
    return lax.dot_general(a, b, (((1,), (1,)), ((), ())), preferred_element_type=F32)


def _dot_tn(a, b):
    return lax.dot_general(a, b, (((0,), (0,)), ((), ())), preferred_element_type=F32)


def _gla_body(q_ref, k_ref, v_ref, r_ref, glr_ref, w2_ref, gb_ref, ng_ref, o_ref,
              st_ref, la_ref, qd_ref, ks_ref, dec_ref, oi_ref, *, rows_blk):
    @pl.when(pl.program_id(1) == 0)
    def _():
        st_ref[...] = jnp.zeros_like(st_ref)

    z = jnp.dot(glr_ref[...].astype(BF16), w2_ref[...], preferred_element_type=F32) + gb_ref[...]
    la_ref[...] = jax.nn.log_sigmoid(z) / GLA_GATE_TAU

    r_i = lax.broadcasted_iota(I32, (CHUNK, CHUNK), 0)
    c_i = lax.broadcasted_iota(I32, (CHUNK, CHUNK), 1)
    causal = c_i <= r_i
    tril = causal.astype(BF16)
    scale = GLA_HEAD_K ** -0.5

    def head_cols(h):
        return (slice(h * GLA_HEAD_K, (h + 1) * GLA_HEAD_K), slice(h * GLA_HEAD_V, (h + 1) * GLA_HEAD_V))

    def intra(c, carry):
        rows = pl.ds(pl.multiple_of(c * CHUNK, CHUNK), CHUNK)
        for h in range(GLA_HEADS):
            kc, vc = head_cols(h)
            la = la_ref[rows, kc]
            hi = la.astype(BF16)
            r1 = la - hi.astype(F32)
            mid = r1.astype(BF16)
            lo = (r1 - mid.astype(F32)).astype(BF16)
            b = (jnp.dot(tril, hi, preferred_element_type=F32)
                 + jnp.dot(tril, mid, preferred_element_type=F32)
                 + jnp.dot(tril, lo, preferred_element_type=F32))
            b_last = b[CHUNK - 1:CHUNK, :]
            q = q_ref[rows, kc].astype(F32) * scale
            k = k_ref[rows, kc].astype(F32)
            q_dec = (q * jnp.exp(b)).astype(BF16)
            k_intra = (k * jnp.exp(-b)).astype(BF16)
            qd_ref[rows, kc] = q_dec
            ks_ref[rows, kc] = (k * jnp.exp(b_last - b)).astype(BF16)
            dec_ref[c, :, kc] = jnp.exp(b_last)
            att = jnp.where(causal, _dot_nt(q_dec, k_intra), 0.0).astype(BF16)
            oi_ref[rows, vc] = jnp.dot(att, v_ref[rows, vc], preferred_element_type=F32)
        return carry

    def inter(c, carry):
        rows = pl.ds(pl.multiple_of(c * CHUNK, CHUNK), CHUNK)
        for h in range(GLA_HEADS):
            kc, vc = head_cols(h)
            st = st_ref[h]
            o = oi_ref[rows, vc] + _dot_nt(qd_ref[rows, kc], st.astype(BF16))
            st_ref[h] = st * dec_ref[c, :, kc] + _dot_tn(v_ref[rows, vc], ks_ref[rows, kc])
            on = _rms(o) * ng_ref[...]
            o_ref[rows, vc] = (on * r_ref[rows, vc].astype(F32)).astype(BF16)
        return carry

    n_chunks = rows_blk // CHUNK
    lax.fori_loop(0, n_chunks, intra, 0, unroll=4)
    lax.fori_loop(0, n_chunks, inter, 0)


def _gla(proj, glr, w2p, gate_b, norm_g):
    rows_blk = 512
    nblk = SEQ // rows_blk
    dk, dv = GLA_DK, D_MODEL

    def row(b, n):
        return b * nblk + n

    return pl.pallas_call(
        functools.partial(_gla_body, rows_blk=rows_blk),
        grid=(BATCH, nblk),
        in_specs=[pl.BlockSpec((rows_blk, dk), lambda b, n: (row(b, n), COL_Q // dk)),
                  pl.BlockSpec((rows_blk, dk), lambda b, n: (row(b, n), COL_K // dk)),
                  pl.BlockSpec((rows_blk, dv), lambda b, n: (row(b, n), COL_VV // dv)),
                  pl.BlockSpec((rows_blk, dv), lambda b, n: (row(b, n), COL_R // dv)),
                  pl.BlockSpec((rows_blk, LANES), lambda b, n: (row(b, n), 0)),
                  pl.BlockSpec((LANES, dk), lambda b, n: (0, 0)),
                  pl.BlockSpec((1, dk), lambda b, n: (0, 0)),
                  pl.BlockSpec((1, GLA_HEAD_V), lambda b, n: (0, 0))],
        out_specs=pl.BlockSpec((rows_blk, dv), lambda b, n: (row(b, n), 0)),
        out_shape=jax.ShapeDtypeStruct((TOKENS, D_MODEL), BF16),
        scratch_shapes=[pltpu.VMEM((GLA_HEADS, GLA_HEAD_V, GLA_HEAD_K), F32),
                        pltpu.VMEM((rows_blk, dk), F32),
                        pltpu.VMEM((rows_blk, dk), BF16),
                        pltpu.VMEM((rows_blk, dk), BF16),
                        pltpu.VMEM((rows_blk // CHUNK, 1, dk), F32),
                        pltpu.VMEM((rows_blk, dv), F32)],
        compiler_params=_cparams("arbitrary", "arbitrary"),
        name="gla",
    )(proj, proj, proj, proj, glr, w2p, gate_b, norm_g)


def _merge_body(ya_ref, yb_ref, wa_ref, wb_ref, ga_ref, gb_ref, o_ref, wa_c, wb_c):
    @pl.when(pl.program_id(1) == 0)
    def _():
        wa_c[...] = wa_ref[...].astype(BF16)
        wb_c[...] = wb_ref[...].astype(BF16)

    a = jnp.dot(ya_ref[...], wa_c[...], preferred_element_type=F32)
    b = jnp.dot(yb_ref[...], wb_c[...], preferred_element_type=F32)
    o_ref[...] = (ga_ref[...].astype(F32) * a + gb_ref[...].astype(F32) * b).astype(BF16)


def _merge(y_a, y_b, wa, wb, proj):
    tm, tn = 1024, 512
    return pl.pallas_call(
        _merge_body,
        grid=(D_MODEL // tn, TOKENS // tm),
        in_specs=[pl.BlockSpec((tm, D_MODEL), lambda j, i: (i, 0)),
                  pl.BlockSpec((tm, D_MODEL), lambda j, i: (i, 0)),
                  pl.BlockSpec((D_MODEL, tn), lambda j, i: (0, j)),
                  pl.BlockSpec((D_MODEL, tn), lambda j, i: (0, j)),
                  pl.BlockSpec((tm, tn), lambda j, i: (i, COL_GA // tn + j)),
                  pl.BlockSpec((tm, tn), lambda j, i: (i, COL_GB // tn + j))],
        out_specs=pl.BlockSpec((tm, tn), lambda j, i: (i, j)),
        out_shape=jax.ShapeDtypeStruct((TOKENS, D_MODEL), BF16),
        scratch_shapes=[pltpu.VMEM((D_MODEL, tn), BF16), pltpu.VMEM((D_MODEL, tn), BF16)],
        compiler_params=_cparams("arbitrary", "arbitrary"),
        name="merge",
    )(y_a, y_b, wa, wb, proj, proj)


def _outproj_body(m_ref, x_ref, mod_ref, wo_ref, g2_ref, rwh_ref, rwl_ref, rb_ref,
                  x1_ref, hp_ref, idx_ref, w_ref, rank_ref, cnt_ref, run_ref, before_ref, *, tm):
    @pl.when(pl.program_id(0) == 0)
    def _():
        run_ref[...] = jnp.zeros_like(run_ref)
        before_ref[...] = (lax.broadcasted_iota(I32, (tm, tm), 0)
                           < lax.broadcasted_iota(I32, (tm, tm), 1)).astype(BF16)

    y = jnp.dot(m_ref[...], wo_ref[...], preferred_element_type=F32)
    x1 = x_ref[...] + mod_ref[2:3, :] * y
    x1_ref[...] = x1
    h2 = _rms(x1) * g2_ref[...] * (1.0 + mod_ref[4:5, :]) + mod_ref[3:4, :]
    hp_ref[...] = h2
    hb = h2.astype(BF16)
    hf = hb.astype(F32)

    h_lo = (h2 - hf).astype(BF16)
    logits = (_dot_nt(rwh_ref[...], hb) + _dot_nt(rwh_ref[...], h_lo) + _dot_nt(rwl_ref[...], hb)
              + rb_ref[...])
    e_i = lax.broadcasted_iota(I32, (N_EXPERTS, tm), 0).astype(F32)
    vals, idxs = [], []
    l = logits
    for _ in range(TOP_K):
        m = jnp.max(l, axis=0, keepdims=True)
        i = jnp.min(jnp.where(l == m, e_i, float(N_EXPERTS)), axis=0, keepdims=True)
        vals.append(m)
        idxs.append(i)
        l = jnp.where(e_i == i, -jnp.inf, l)
    ex = [jnp.exp(v - vals[0]) for v in vals]
    den = ex[0] + ex[1] + ex[2] + ex[3]
    onehots = [e_i == idxs[k] for k in range(TOP_K)]
    stacked = jnp.concatenate([o.astype(BF16) for o in onehots], axis=0)
    earlier = jnp.dot(stacked, before_ref[...], preferred_element_type=F32)
    run = run_ref[...]
    for k in range(TOP_K):
        pref = earlier[k * N_EXPERTS:(k + 1) * N_EXPERTS, :] + run[:, 0:1]
        rank_ref[k:k + 1, :] = jnp.sum(jnp.where(onehots[k], pref, 0.0), axis=0, keepdims=True).astype(I32)
        run = run + jnp.sum(onehots[k].astype(F32), axis=1, keepdims=True)
        idx_ref[k:k + 1, :] = idxs[k].astype(I32)
        w_ref[k:k + 1, :] = ex[k] / den
    run_ref[...] = run
    cnt_ref[...] = run.astype(I32)


def _outproj(merged, x2d, mod, w_out, norm2_g, rw_hi, rw_lo, router_b):
    tm = 512
    per_batch = SEQ // tm
    body = functools.partial(_outproj_body, tm=tm)
    return pl.pallas_call(
        body,
        grid=(TOKENS // tm,),
        in_specs=[pl.BlockSpec((tm, D_MODEL), lambda i: (i, 0)),
                  pl.BlockSpec((tm, D_MODEL), lambda i: (i, 0)),
                  pl.BlockSpec((None, 6, D_MODEL), lambda i: (i // per_batch, 0, 0)),
                  pl.BlockSpec((D_MODEL, D_MODEL), lambda i: (0, 0)),
                  pl.BlockSpec((1, D_MODEL), lambda i: (0, 0)),
                  pl.BlockSpec((N_EXPERTS, D_MODEL), lambda i: (0, 0)),
                  pl.BlockSpec((N_EXPERTS, D_MODEL), lambda i: (0, 0)),
                  pl.BlockSpec((N_EXPERTS, 1), lambda i: (0, 0))],
        out_specs=[pl.BlockSpec((tm, D_MODEL), lambda i: (i, 0)),
                   pl.BlockSpec((tm, D_MODEL), lambda i: (i, 0)),
                   pl.BlockSpec((TOP_K, tm), lambda i: (0, i)),
                   pl.BlockSpec((TOP_K, tm), lambda i: (0, i)),
                   pl.BlockSpec((TOP_K, tm), lambda i: (0, i)),
                   pl.BlockSpec((N_EXPERTS, LANES), lambda i: (0, 0))],
        out_shape=[jax.ShapeDtypeStruct((TOKENS, D_MODEL), F32),
                   jax.ShapeDtypeStruct((TOKENS, D_MODEL), F32),
                   jax.ShapeDtypeStruct((TOP_K, TOKENS), I32),
                   jax.ShapeDtypeStruct((TOP_K, TOKENS), F32),
                   jax.ShapeDtypeStruct((TOP_K, TOKENS), I32),
                   jax.ShapeDtypeStruct((N_EXPERTS, LANES), I32)],
        scratch_shapes=[pltpu.VMEM((N_EXPERTS, LANES), F32), pltpu.VMEM((tm, tm), BF16)],
        compiler_params=_cparams("arbitrary"),
        name="outproj",
    )(merged, x2d, mod, w_out, norm2_g, rw_hi, rw_lo, router_b)


def _dispatch_body(dest_ref, pend_ref, padded_ref, h_ref, xs_hbm, zero_ref, zsem, sem, *, tm):
    i = pl.program_id(0)

    @pl.when(i == 0)
    def _():
        _dispatch_clear(pend_ref, padded_ref, xs_hbm, zero_ref, zsem)

    def group(g, carry):
        base = pl.multiple_of(g * ROW_DMA_GROUP, ROW_DMA_GROUP)
        for u in range(ROW_DMA_GROUP):
            src = h_ref.at[pl.ds(base + u, 1)]
            for k in range(TOP_K):
                d = dest_ref[(i * tm + base + u) * TOP_K + k]
                pltpu.make_async_copy(src, xs_hbm.at[pl.ds(d, 1)], sem).start()
        return carry
    lax.fori_loop(0, tm // ROW_DMA_GROUP, group, 0)
    for k in range(TOP_K):
        pltpu.make_async_copy(h_ref, xs_hbm.at[pl.ds(0, tm)], sem).wait()


def _dispatch_clear(pend_ref, padded_ref, xs_hbm, zero_ref, zsem):
    zero_ref[...] = jnp.zeros_like(zero_ref)

    def zero_copy(e):
        return pltpu.make_async_copy(
            zero_ref, xs_hbm.at[pl.ds(pl.multiple_of(pend_ref[e] - ROW_TILE, ROW_TILE), ROW_TILE)], zsem)

    def zstart(e, c):
        @pl.when(padded_ref[e] > 0)
        def _():
            zero_copy(e).start()
        return c

    def zwait(e, c):
        @pl.when(padded_ref[e] > 0)
        def _():
            zero_copy(e).wait()
        return c

    def tail_copy(t):
        return pltpu.make_async_copy(
            zero_ref, xs_hbm.at[pl.ds(pl.multiple_of(t * ROW_TILE, ROW_TILE), ROW_TILE)], zsem)

    def tstart(t, c):
        tail_copy(t).start()
        return c

    def twait(t, c):
        tail_copy(t).wait()
        return c

    first_tail = pend_ref[N_EXPERTS - 1] // ROW_TILE
    lax.fori_loop(0, N_EXPERTS, zstart, 0)
    lax.fori_loop(first_tail, N_ROW_TILES, tstart, 0)
    lax.fori_loop(0, N_EXPERTS, zwait, 0)
    lax.fori_loop(first_tail, N_ROW_TILES, twait, 0)


def _dispatch(dest_flat, padded_end, padded, h_rows):
    tm = 1024
    return pl.pallas_call(
        functools.partial(_dispatch_body, tm=tm),
        grid_spec=pltpu.PrefetchScalarGridSpec(
            num_scalar_prefetch=3,
            grid=(TOKENS // tm,),
            in_specs=[pl.BlockSpec((tm, D_MODEL), lambda i, d, pe, pd: (i, 0))],
            out_specs=pl.BlockSpec(memory_space=pl.ANY),
            scratch_shapes=[pltpu.VMEM((ROW_TILE, D_MODEL), F32),
                            pltpu.SemaphoreType.DMA(()),
                            pltpu.SemaphoreType.DMA(())]),
        out_shape=jax.ShapeDtypeStruct((PADDED_ROWS, D_MODEL), F32),
        compiler_params=_cparams("arbitrary"),
        name="dispatch",
    )(dest_flat, padded_end, padded, h_rows)


def _expert_body(se_ref, srow_ref, snsub_ref, xs_hbm, wg_hbm, wu_hbm, wd_hbm, bg_ref, bu_ref, bd_ref,
                 y_hbm, xstage, xb, acc, wg_f, wu_f, wd_f, wg_c, wu_c, wd_c, sem_in, sem_out, sem_w):
    s = pl.program_id(0)
    f = pl.program_id(1)
    nsub = snsub_ref[s]
    row0 = srow_ref[s]
    expert = se_ref[s]
    ff_chunk = pl.ds(pl.multiple_of(f * FF_TILE, FF_TILE), FF_TILE)

    def weight_copies(s_, f_):
        e = se_ref[s_]
        chunk = pl.ds(pl.multiple_of(f_ * FF_TILE, FF_TILE), FF_TILE)
        return (pltpu.make_async_copy(wg_hbm.at[e, :, chunk], wg_f, sem_w.at[0]),
                pltpu.make_async_copy(wu_hbm.at[e, :, chunk], wu_f, sem_w.at[1]),
                pltpu.make_async_copy(wd_hbm.at[e, chunk, :], wd_f, sem_w.at[2]))

    wrap = f == N_FF - 1
    s_next = jnp.minimum(jnp.where(wrap, s + 1, s), MAX_SUPER - 1)
    f_next = jnp.where(wrap, 0, f + 1)
    has_next = jnp.logical_not(wrap & (s == MAX_SUPER - 1)) & (snsub_ref[s_next] > 0)

    last_f = f == N_FF - 1
    n_pair = nsub // 2
    odd = nsub - 2 * n_pair
    pair_rows = 2 * ROW_TILE

    next_super = (s + 1 < MAX_SUPER) & (snsub_ref[jnp.minimum(s + 1, MAX_SUPER - 1)] > 0)

    def in_copy(row_base, t):
        src = xs_hbm.at[pl.ds(pl.multiple_of(row_base + t * ROW_TILE, ROW_TILE), ROW_TILE)]
        return pltpu.make_async_copy(src, xstage.at[t], sem_in.at[t])

    def start_rows(row_base, count):
        def body(t, c):
            in_copy(row_base, t).start()
            return c
        lax.fori_loop(0, count, body, 0)

    def out_copy(off, m):
        dst = y_hbm.at[pl.ds(pl.multiple_of(row0 + off, ROW_TILE), m)]
        return pltpu.make_async_copy(acc.at[pl.ds(off, m)], dst, sem_out)

    def drain_outputs(tiles):
        def body(p, c):
            out_copy(0, pair_rows).wait()
            return c
        lax.fori_loop(0, tiles // 2, body, 0)

        @pl.when(tiles % 2 == 1)
        def _():
            out_copy(0, ROW_TILE).wait()

    def gate(x):
        return jnp.minimum(jnp.dot(x, wg_c[...], preferred_element_type=F32) + bg_ref[pl.ds(expert, 1), ff_chunk],
                           SWIGLU_LIMIT)

    def up(x):
        return jnp.clip(jnp.dot(x, wu_c[...], preferred_element_type=F32) + bu_ref[pl.ds(expert, 1), ff_chunk],
                        -SWIGLU_LIMIT, SWIGLU_LIMIT)

    def down(rows, g, u):
        a = ((u + 1.0) * (g * jax.nn.sigmoid(SWIGLU_ALPHA * g))).astype(BF16)
        acc[rows, :] += jnp.dot(a, wd_c[...], preferred_element_type=F32)

    def finish(off, rows_n):
        @pl.when(last_f)
        def _():
            out_copy(off, rows_n).start()

    def compute(off, rows_n):
        rows = pl.ds(off, rows_n)
        x = xb[rows, :]
        down(rows, gate(x), up(x))
        finish(off, rows_n)

    def first_compute(rows_n, nxt):
        rows = pl.ds(0, rows_n)
        x = xb[rows, :]
        g = gate(x)
        wu_c[...] = wu_f[...].astype(BF16)

        @pl.when(has_next)
        def _():
            nxt[1].start()
        u = up(x)
        wd_c[...] = wd_f[...].astype(BF16)

        @pl.when(has_next)
        def _():
            nxt[2].start()
        down(rows, g, u)
        finish(0, rows_n)

    odd_off = pl.multiple_of((nsub - 1) * ROW_TILE, ROW_TILE)

    @pl.when(nsub > 0)
    def _():
        @pl.when((s == 0) & (f == 0))
        def _():
            for cp in weight_copies(s, f):
                cp.start()
            start_rows(row0, nsub)

        nxt = weight_copies(s_next, f_next)
        for cp in weight_copies(s, f):
            cp.wait()
        wg_c[...] = wg_f[...].astype(BF16)

        @pl.when(has_next)
        def _():
            nxt[0].start()

        @pl.when(f == 0)
        def _():
            @pl.when(s > 0)
            def _():
                drain_outputs(snsub_ref[jnp.maximum(s - 1, 0)])

            def load(t, c):
                in_copy(row0, t).wait()
                rows = pl.ds(pl.multiple_of(t * ROW_TILE, ROW_TILE), ROW_TILE)
                xb[rows, :] = xstage[t].astype(BF16)
                acc[rows, :] = jnp.broadcast_to(bd_ref[pl.ds(expert, 1), :], (ROW_TILE, D_MODEL))
                return c
            lax.fori_loop(0, nsub, load, 0)

        @pl.when(last_f & next_super)
        def _():
            s1 = jnp.minimum(s + 1, MAX_SUPER - 1)
            start_rows(srow_ref[s1], snsub_ref[s1])

        @pl.when(n_pair > 0)
        def _():
            first_compute(pair_rows, nxt)

        @pl.when(n_pair == 0)
        def _():
            first_compute(ROW_TILE, nxt)

        def pair(p, c):
            compute(pl.multiple_of(p * pair_rows, pair_rows), pair_rows)
            return c
        lax.fori_loop(1, n_pair, pair, 0)

        @pl.when((odd == 1) & (n_pair > 0))
        def _():
            compute(odd_off, ROW_TILE)

        @pl.when(last_f & jnp.logical_not(next_super))
        def _():
            drain_outputs(nsub)

    @pl.when((s == MAX_SUPER - 1) & last_f)
    def _():
        acc[0:ROW_TILE, :] = jnp.zeros((ROW_TILE, D_MODEL), F32)

        def tail_copy(t):
            dst = y_hbm.at[pl.ds(pl.multiple_of(t * ROW_TILE, ROW_TILE), ROW_TILE)]
            return pltpu.make_async_copy(acc.at[0:ROW_TILE], dst, sem_out)

        def tstart(t, c):
            tail_copy(t).start()
            return c

        def twait(t, c):
            tail_copy(t).wait()
            return c

        first_tail = snsub_ref[MAX_SUPER]
        lax.fori_loop(first_tail, N_ROW_TILES, tstart, 0)
        lax.fori_loop(first_tail, N_ROW_TILES, twait, 0)


def _experts(se, srow, snsub, xs, w_gate, b_gate, w_up, b_up, w_down, b_down):
    def whole(shape):
        return pl.BlockSpec(shape, lambda s, f, se, sr, sn: (0, 0))

    return pl.pallas_call(
        _expert_body,
        grid_spec=pltpu.PrefetchScalarGridSpec(
            num_scalar_prefetch=3,
            grid=(MAX_SUPER, N_FF),
            in_specs=[pl.BlockSpec(memory_space=pl.ANY),
                      pl.BlockSpec(memory_space=pl.ANY),
                      pl.BlockSpec(memory_space=pl.ANY),
                      pl.BlockSpec(memory_space=pl.ANY),
                      whole((N_EXPERTS, D_FF)),
                      whole((N_EXPERTS, D_FF)),
                      whole((N_EXPERTS, D_MODEL))],
            out_specs=pl.BlockSpec(memory_space=pl.ANY),
            scratch_shapes=[pltpu.VMEM((SUPER_TILES, ROW_TILE, D_MODEL), F32),
                            pltpu.VMEM((SUPER_ROWS, D_MODEL), BF16),
                            pltpu.VMEM((SUPER_ROWS, D_MODEL), F32),
                            pltpu.VMEM((D_MODEL, FF_TILE), F32),
                            pltpu.VMEM((D_MODEL, FF_TILE), F32),
                            pltpu.VMEM((FF_TILE, D_MODEL), F32),
                            pltpu.VMEM((D_MODEL, FF_TILE), BF16),
                            pltpu.VMEM((D_MODEL, FF_TILE), BF16),
                            pltpu.VMEM((FF_TILE, D_MODEL), BF16),
                            pltpu.SemaphoreType.DMA((SUPER_TILES,)),
                            pltpu.SemaphoreType.DMA(()),
                            pltpu.SemaphoreType.DMA((3,))]),
        out_shape=jax.ShapeDtypeStruct((PADDED_ROWS, D_MODEL), F32),
        compiler_params=_cparams("arbitrary", "arbitrary"),
        name="experts",
    )(se, srow, snsub, xs, w_gate, w_up, w_down,
      b_gate, b_up, b_down)


def _combine_body(dest_ref, y_hbm, x1_ref, w_ref, mod_ref, fg_ref, o_ref, buf, sem, *, tm):
    i = pl.program_id(0)
    slot = i % 2

    def gather(tile, dst_slot):
        def group(g, carry):
            base = pl.multiple_of(g * ROW_DMA_GROUP, ROW_DMA_GROUP)
            for u in range(ROW_DMA_GROUP):
                for k in range(TOP_K):
                    d = dest_ref[(tile * tm + base + u) * TOP_K + k]
                    pltpu.make_async_copy(y_hbm.at[pl.ds(d, 1)], buf.at[dst_slot, k, pl.ds(base + u, 1)],
                                          sem.at[dst_slot]).start()
            return carry
        lax.fori_loop(0, tm // ROW_DMA_GROUP, group, 0)

    @pl.when(i == 0)
    def _():
        gather(i, slot)

    @pl.when(i + 1 < pl.num_programs(0))
    def _():
        gather(i + 1, 1 - slot)

    for k in range(TOP_K):
        pltpu.make_async_copy(y_hbm.at[pl.ds(0, tm)], buf.at[slot, k], sem.at[slot]).wait()
    moe = buf[slot, 0] * w_ref[:, 0:1]
    for k in range(1, TOP_K):
        moe = moe + buf[slot, k] * w_ref[:, k:k + 1]
    x2 = x1_ref[...] + mod_ref[5:6, :] * moe
    o_ref[...] = _rms(x2) * fg_ref[...]


def _combine(dest_flat, y, x1, w_t, mod, final_g):
    tm = 256
    per_batch = SEQ // tm
    return pl.pallas_call(
        functools.partial(_combine_body, tm=tm),
        grid_spec=pltpu.PrefetchScalarGridSpec(
            num_scalar_prefetch=1,
            grid=(TOKENS // tm,),
            in_specs=[pl.BlockSpec(memory_space=pl.ANY),
                      pl.BlockSpec((tm, D_MODEL), lambda i, d: (i, 0)),
                      pl.BlockSpec((tm, TOP_K), lambda i, d: (i, 0)),
                      pl.BlockSpec((None, 6, D_MODEL), lambda i, d: (i // per_batch, 0, 0)),
                      pl.BlockSpec((1, D_MODEL), lambda i, d: (0, 0))],
            out_specs=pl.BlockSpec((tm, D_MODEL), lambda i, d: (i, 0)),
            scratch_shapes=[pltpu.VMEM((2, TOP_K, tm, D_MODEL), F32),
                            pltpu.SemaphoreType.DMA((2,))]),
        out_shape=jax.ShapeDtypeStruct((TOKENS, D_MODEL), F32),
        compiler_params=_cparams("arbitrary"),
        name="combine",
    )(dest_flat, y, x1, w_t, mod, final_g)


def _routing_tables(idx, rank, counts):
    padded = (counts + ROW_TILE - 1) // ROW_TILE * ROW_TILE
    padded_end = jnp.cumsum(padded)
    padded_start = padded_end - padded
    e_i = jnp.arange(N_EXPERTS, dtype=I32)[:, None, None]
    start_of = jnp.sum(jnp.where(idx[None] == e_i, padded_start[:, None, None], 0), axis=0)
    dest = (start_of + rank).T.reshape(N_ROWS).astype(I32)

    tiles = padded // ROW_TILE
    n_super = (tiles + SUPER_TILES - 1) // SUPER_TILES
    super_end = jnp.cumsum(n_super)
    super_start = super_end - n_super
    s_i = jnp.arange(MAX_SUPER, dtype=I32)
    total = super_end[-1]
    valid = s_i < total
    e_of = jnp.minimum(jnp.sum((s_i[:, None] >= super_end[None, :]).astype(I32), axis=1), N_EXPERTS - 1)
    last_e = jnp.minimum(jnp.sum((total - 1 >= super_end).astype(I32)), N_EXPERTS - 1)
    local = s_i - super_start[e_of]
    srow = jnp.where(valid, padded_start[e_of] + local * SUPER_ROWS, 0).astype(I32)
    snsub = jnp.where(valid, jnp.minimum(tiles[e_of] - local * SUPER_TILES, SUPER_TILES), 0).astype(I32)
    snsub = jnp.concatenate([snsub, (padded_end[-1:] // ROW_TILE).astype(I32)])
    se = jnp.where(valid, e_of, last_e).astype(I32)
    return dest, padded_end.astype(I32), padded.astype(I32), se, srow, snsub


def kernel(x, c, ada_w, ada_b, norm1_g, w_in, gla_gate_w2, gla_gate_b, sgu_ln_g, sgu_ln_b, sgu_w, sgu_b,
           gla_norm_g, w_branch_a, w_branch_b, w_out, norm2_g, router_w, router_b, exp_w_gate, exp_b_gate,
           exp_w_up, exp_b_up, exp_w_down, exp_b_down, final_g):
    x2d = x.reshape(TOKENS, D_MODEL)
    mod = _ada(c, ada_w[0], ada_b[0])

    w_in0 = w_in[0]
    w_in_t = w_in0.T
    w_glr = jnp.zeros((LANES, D_MODEL), BF16).at[:GLA_GATE_RANK].set(
        w_in_t[GLR_SRC:GLR_SRC + GLA_GATE_RANK].astype(BF16))
    h, glr = _norm1(x2d, mod, norm1_g, w_glr)
    proj = _inproj(h, w_in_t)

    y_a = _sgu(proj, sgu_ln_g, sgu_ln_b, sgu_w[0], sgu_b[0].T)

    w2p = jnp.zeros((LANES, GLA_DK), BF16).at[:GLA_GATE_RANK].set(gla_gate_w2[0].astype(BF16))
    y_b = _gla(proj, glr, w2p, gla_gate_b, gla_norm_g)

    merged = _merge(y_a, y_b, w_branch_a[0], w_branch_b[0], proj)

    rw_t = router_w[0].T
    rw_hi = rw_t.astype(BF16)
    rw_lo = (rw_t - rw_hi.astype(F32)).astype(BF16)
    x1, h_rows, idx, top_w, rank, cnt = _outproj(
        merged, x2d, mod, w_out[0].astype(BF16), norm2_g, rw_hi, rw_lo, router_b[0].reshape(N_EXPERTS, 1))

    dest, padded_end, padded, se, srow, snsub = _routing_tables(idx, rank, cnt[:, 0])
    xs = _dispatch(dest, padded_end, padded, h_rows)
    y = _experts(se, srow, snsub, xs, exp_w_gate[0], exp_b_gate[0], exp_w_up[0], exp_b_up[0],
                 exp_w_down[0], exp_b_down[0])
    out = _combine(dest, y, x1, top_w.T, mod, final_g.reshape(1, D_MODEL))
    return out.reshape(BATCH, SEQ, D_MODEL)
```

```python
import functools

import jax
import jax.numpy as jnp
from jax import lax
from jax.experimental import pallas as pl
from jax.experimental.pallas import tpu as pltpu

F32 = jnp.float32
BF16 = jnp.bfloat16
I32 = jnp.int32

D_MODEL = 2048
BATCH = 4
SEQ = 2048
TOKENS = BATCH * SEQ
CHUNK = 64
SPATIAL_BLOCK = 128
A_GROUPS = 8
A_GROUP_DIM = D_MODEL // A_GROUPS
GLA_HEADS = 4
GLA_DK = D_MODEL // 2
GLA_HEAD_K = GLA_DK // GLA_HEADS
GLA_HEAD_V = D_MODEL // GLA_HEADS
GLA_GATE_RANK = 16
GLA_GATE_TAU = 16.0
N_EXPERTS = 32
TOP_K = 4
D_FF = D_MODEL
SWIGLU_LIMIT = 7.0
SWIGLU_ALPHA = 1.702
NORM_EPS = 1e-6

LANES = 128
VMEM_LIMIT = 56 * 1024 * 1024

COL_U, COL_V, COL_Q, COL_K, COL_VV, COL_R = 0, 2048, 4096, 5120, 6144, 8192
COL_GA, COL_GB = 10240, 12288
PROJ_W = 14336
GLR_SRC = 10240

ROW_TILE = 256
SUPER_TILES = 6
SUPER_ROWS = ROW_TILE * SUPER_TILES
N_ROWS = TOKENS * TOP_K
N_ROW_TILES = (N_ROWS + N_EXPERTS * (ROW_TILE - 1)) // ROW_TILE
PADDED_ROWS = N_ROW_TILES * ROW_TILE
MAX_SUPER = -(-(N_ROW_TILES + N_EXPERTS * (SUPER_TILES - 1)) // SUPER_TILES)
FF_TILE = 512
N_FF = D_FF // FF_TILE
ROW_DMA_GROUP = 8


def _cparams(*sem):
    return pltpu.CompilerParams(dimension_semantics=sem, vmem_limit_bytes=VMEM_LIMIT)


def _rms(x):
    return x * lax.rsqrt(jnp.mean(x * x, axis=-1, keepdims=True) + NORM_EPS)


def _ada_body(c_ref, w_ref, b_ref, o_ref):
    c = c_ref[...]
    cond = c * jax.nn.sigmoid(c)
    o_ref[...] = jnp.dot(cond.astype(BF16), w_ref[...].astype(BF16),
                         preferred_element_type=F32) + b_ref[...]


def _ada(c, ada_w, ada_b):
    tn = 1024
    cp = jnp.zeros((8, D_MODEL), F32).at[:BATCH].set(c)
    out = pl.pallas_call(
        _ada_body,
        grid=(6 * D_MODEL // tn,),
        in_specs=[pl.BlockSpec((8, D_MODEL), lambda j: (0, 0)),
                  pl.BlockSpec((D_MODEL, tn), lambda j: (0, j)),
                  pl.BlockSpec((1, tn), lambda j: (0, j))],
        out_specs=pl.BlockSpec((8, tn), lambda j: (0, j)),
        out_shape=jax.ShapeDtypeStruct((8, 6 * D_MODEL), F32),
        compiler_params=_cparams("arbitrary"),
        name="ada",
    )(cp, ada_w, ada_b.reshape(1, 6 * D_MODEL))
    return out[:BATCH].reshape(BATCH, 6, D_MODEL)


def _norm1_body(x_ref, mod_ref, g_ref, wglr_ref, h_ref, glr_ref):
    h = _rms(x_ref[...]) * g_ref[...] * (1.0 + mod_ref[1:2, :]) + mod_ref[0:1, :]
    hb = h.astype(BF16)
    h_ref[...] = hb
    glr_ref[...] = _dot_nt(hb, wglr_ref[...])


def _norm1(x2d, mod, norm1_g, w_glr):
    tm = 512
    per_batch = SEQ // tm
    return pl.pallas_call(
        _norm1_body,
        grid=(TOKENS // tm,),
        in_specs=[pl.BlockSpec((tm, D_MODEL), lambda i: (i, 0)),
                  pl.BlockSpec((None, 6, D_MODEL), lambda i: (i // per_batch, 0, 0)),
                  pl.BlockSpec((1, D_MODEL), lambda i: (0, 0)),
                  pl.BlockSpec((LANES, D_MODEL), lambda i: (0, 0))],
        out_specs=[pl.BlockSpec((tm, D_MODEL), lambda i: (i, 0)),
                   pl.BlockSpec((tm, LANES), lambda i: (i, 0))],
        out_shape=[jax.ShapeDtypeStruct((TOKENS, D_MODEL), BF16),
                   jax.ShapeDtypeStruct((TOKENS, LANES), F32)],
        compiler_params=_cparams("arbitrary"),
        name="norm1",
    )(x2d, mod, norm1_g, w_glr)


INPROJ_TN = 1024
INPROJ_MAIN_TILES = COL_GA // INPROJ_TN
MXU_N = 256


def _inproj_body(h_ref, wt_ref, proj_ref, wb_ref):
    j = pl.program_id(0)

    @pl.when(pl.program_id(1) == 0)
    def _():
        wb_ref[...] = wt_ref[...].astype(BF16)

    def run(act):
        for n in range(INPROJ_TN // MXU_N):
            cols = slice(n * MXU_N, (n + 1) * MXU_N)
            acc = _dot_nt(h_ref[...], wb_ref[cols, :])
            proj_ref[:, cols] = act(acc).astype(BF16)

    col = j * INPROJ_TN

    @pl.when(col < COL_Q)
    def _():
        run(jax.nn.gelu)

    @pl.when((col >= COL_Q) & (col < COL_R))
    def _():
        run(lambda a: a)

    @pl.when((col >= COL_R) & (col < COL_GA))
    def _():
        run(lambda a: a * jax.nn.sigmoid(a))

    @pl.when(col >= COL_GA)
    def _():
        run(jax.nn.sigmoid)


def _inproj(h, w_in_t):
    tm, tn = 1024, INPROJ_TN

    def w_row(j, i):
        return (pl.multiple_of(j * tn + jnp.where(j >= INPROJ_MAIN_TILES, GLA_GATE_RANK, 0), GLA_GATE_RANK), 0)

    return pl.pallas_call(
        _inproj_body,
        grid=(PROJ_W // tn, TOKENS // tm),
        in_specs=[pl.BlockSpec((tm, D_MODEL), lambda j, i: (i, 0)),
                  pl.BlockSpec((pl.Element(tn), pl.Element(D_MODEL)), w_row)],
        out_specs=pl.BlockSpec((tm, tn), lambda j, i: (i, j)),
        out_shape=jax.ShapeDtypeStruct((TOKENS, PROJ_W), BF16),
        scratch_shapes=[pltpu.VMEM((tn, D_MODEL), BF16)],
        compiler_params=_cparams("arbitrary", "arbitrary"),
        name="inproj",
    )(h, w_in_t)


SGU_BLOCKS = 4


def _sgu_body(u_ref, v_ref, lg_ref, lb_ref, ws_ref, bs_ref, o_ref):
    t_chunk = lax.broadcasted_iota(I32, (SPATIAL_BLOCK, SPATIAL_BLOCK), 0) // CHUNK
    s_chunk = lax.broadcasted_iota(I32, (SPATIAL_BLOCK, SPATIAL_BLOCK), 1) // CHUNK
    mask = s_chunk <= t_chunk
    w = [jnp.where(mask, ws_ref[g], 0.0).astype(BF16) for g in range(A_GROUPS)]
    for blk in range(SGU_BLOCKS):
        rows = slice(blk * SPATIAL_BLOCK, (blk + 1) * SPATIAL_BLOCK)
        v = v_ref[rows, :].astype(F32)
        mu = jnp.mean(v, axis=-1, keepdims=True)
        xc = v - mu
        var = jnp.mean(xc * xc, axis=-1, keepdims=True)
        vn = (xc * lax.rsqrt(var + NORM_EPS) * lg_ref[...] + lb_ref[...]).astype(BF16)
        for g in range(A_GROUPS):
            cols = slice(g * A_GROUP_DIM, (g + 1) * A_GROUP_DIM)
            mixed = jnp.dot(w[g], vn[:, cols], preferred_element_type=F32) + bs_ref[:, g:g + 1]
            o_ref[rows, cols] = (u_ref[rows, cols].astype(F32) * mixed).astype(BF16)


def _sgu(proj, ln_g, ln_b, w_s, b_s_t):
    rows_blk = SGU_BLOCKS * SPATIAL_BLOCK
    nblk = TOKENS // rows_blk
    wb = D_MODEL
    return pl.pallas_call(
        _sgu_body,
        grid=(nblk,),
        in_specs=[pl.BlockSpec((rows_blk, wb), lambda i: (i, COL_U // wb)),
                  pl.BlockSpec((rows_blk, wb), lambda i: (i, COL_V // wb)),
                  pl.BlockSpec((1, wb), lambda i: (0, 0)),
                  pl.BlockSpec((1, wb), lambda i: (0, 0)),
                  pl.BlockSpec((A_GROUPS, SPATIAL_BLOCK, SPATIAL_BLOCK), lambda i: (0, 0, 0)),
                  pl.BlockSpec((SPATIAL_BLOCK, A_GROUPS), lambda i: (0, 0))],
        out_specs=pl.BlockSpec((rows_blk, wb), lambda i: (i, 0)),
        out_shape=jax.ShapeDtypeStruct((TOKENS, wb), BF16),
        compiler_params=_cparams("arbitrary"),
        name="sgu",
    )(proj, proj, ln_g, ln_b, w_s, b_s_t)


def _dot_nt(a, b):
    return lax.dot_general(a, b, (((1,), (1,)), ((), ())), preferred_element_type=F32)


def _dot_tn(a, b):
    return lax.dot_general(a, b, (((0,), (0,)), ((), ())), preferred_element_type=F32)


def _gla_body(q_ref, k_ref, v_ref, r_ref, glr_ref, w2_ref, gb_ref, ng_ref, o_ref,
              st_ref, la_ref, qd_ref, ks_ref, dec_ref, oi_ref, *, rows_blk):
    @pl.when(pl.program_id(1) == 0)
    def _():
        st_ref[...] = jnp.zeros_like(st_ref)

    z = jnp.dot(glr_ref[...].astype(BF16), w2_ref[...], preferred_element_type=F32) + gb_ref[...]
    la_ref[...] = jax.nn.log_sigmoid(z) / GLA_GATE_TAU

    r_i = lax.broadcasted_iota(I32, (CHUNK, CHUNK), 0)
    c_i = lax.broadcasted_iota(I32, (CHUNK, CHUNK), 1)
    causal = c_i <= r_i
    tril = causal.astype(BF16)
    scale = GLA_HEAD_K ** -0.5

    def head_cols(h):
        return (slice(h * GLA_HEAD_K, (h + 1) * GLA_HEAD_K), slice(h * GLA_HEAD_V, (h + 1) * GLA_HEAD_V))

    def intra(c, carry):
        rows = pl.ds(pl.multiple_of(c * CHUNK, CHUNK), CHUNK)
        for h in range(GLA_HEADS):
            kc, vc = head_cols(h)
            la = la_ref[rows, kc]
            hi = la.astype(BF16)
            r1 = la - hi.astype(F32)
            mid = r1.astype(BF16)
            lo = (r1 - mid.astype(F32)).astype(BF16)
            b = (jnp.dot(tril, hi, preferred_element_type=F32)
                 + jnp.dot(tril, mid, preferred_element_type=F32)
                 + jnp.dot(tril, lo, preferred_element_type=F32))
            b_last = b[CHUNK - 1:CHUNK, :]
            q = q_ref[rows, kc].astype(F32) * scale
            k = k_ref[rows, kc].astype(F32)
            q_dec = (q * jnp.exp(b)).astype(BF16)
            k_intra = (k * jnp.exp(-b)).astype(BF16)
            qd_ref[rows, kc] = q_dec
            ks_ref[rows, kc] = (k * jnp.exp(b_last - b)).astype(BF16)
            dec_ref[c, :, kc] = jnp.exp(b_last)
            att = jnp.where(causal, _dot_nt(q_dec, k_intra), 0.0).astype(BF16)
            oi_ref[rows, vc] = jnp.dot(att, v_ref[rows, vc], preferred_element_type=F32)
        return carry

    def inter(c, carry):
        rows = pl.ds(pl.multiple_of(c * CHUNK, CHUNK), CHUNK)
        for h in range(GLA_HEADS):
            kc, vc = head_cols(h)
            st = st_ref[h]
            o = oi_ref[rows, vc] + _dot_nt(qd_ref[rows, kc], st.astype(BF16))
            st_ref[h] = st * dec_ref[c, :, kc] + _dot_tn(v_ref[rows, vc], ks_ref[rows, kc])
            on = _rms(o) * ng_ref[...]
            o_ref[rows, vc] = (on * r_ref[rows, vc].astype(F32)).astype(BF16)
        return carry

    n_chunks = rows_blk // CHUNK
    lax.fori_loop(0, n_chunks, intra, 0, unroll=4)
    lax.fori_loop(0, n_chunks, inter, 0)


def _gla(proj, glr, w2p, gate_b, norm_g):
    rows_blk = 512
    nblk = SEQ // rows_blk
    dk, dv = GLA_DK, D_MODEL

    def row(b, n):
        return b * nblk + n

    return pl.pallas_call(
        functools.partial(_gla_body, rows_blk=rows_blk),
        grid=(BATCH, nblk),
        in_specs=[pl.BlockSpec((rows_blk, dk), lambda b, n: (row(b, n), COL_Q // dk)),
                  pl.BlockSpec((rows_blk, dk), lambda b, n: (row(b, n), COL_K // dk)),
                  pl.BlockSpec((rows_blk, dv), lambda b, n: (row(b, n), COL_VV // dv)),
                  pl.BlockSpec((rows_blk, dv), lambda b, n: (row(b, n), COL_R // dv)),
                  pl.BlockSpec((rows_blk, LANES), lambda b, n: (row(b, n), 0)),
                  pl.BlockSpec((LANES, dk), lambda b, n: (0, 0)),
                  pl.BlockSpec((1, dk), lambda b, n: (0, 0)),
                  pl.BlockSpec((1, GLA_HEAD_V), lambda b, n: (0, 0))],
        out_specs=pl.BlockSpec((rows_blk, dv), lambda b, n: (row(b, n), 0)),
        out_shape=jax.ShapeDtypeStruct((TOKENS, D_MODEL), BF16),
        scratch_shapes=[pltpu.VMEM((GLA_HEADS, GLA_HEAD_V, GLA_HEAD_K), F32),
                        pltpu.VMEM((rows_blk, dk), F32),
                        pltpu.VMEM((rows_blk, dk), BF16),
                        pltpu.VMEM((rows_blk, dk), BF16),
                        pltpu.VMEM((rows_blk // CHUNK, 1, dk), F32),
                        pltpu.VMEM((rows_blk, dv), F32)],
        compiler_params=_cparams("arbitrary", "arbitrary"),
        name="gla",
    )(proj, proj, proj, proj, glr, w2p, gate_b, norm_g)


def _merge_body(ya_ref, yb_ref, wa_ref, wb_ref, ga_ref, gb_ref, o_ref, wa_c, wb_c):
    @pl.when(pl.program_id(1) == 0)
    def _():
        wa_c[...] = wa_ref[...].astype(BF16)
        wb_c[...] = wb_ref[...].astype(BF16)

    a = jnp.dot(ya_ref[...], wa_c[...], preferred_element_type=F32)
    b = jnp.dot(yb_ref[...], wb_c[...], preferred_element_type=F32)
    o_ref[...] = (ga_ref[...].astype(F32) * a + gb_ref[...].astype(F32) * b).astype(BF16)


def _merge(y_a, y_b, wa, wb, proj):
    tm, tn = 1024, 512
    return pl.pallas_call(
        _merge_body,
        grid=(D_MODEL // tn, TOKENS // tm),
        in_specs=[pl.BlockSpec((tm, D_MODEL), lambda j, i: (i, 0)),
                  pl.BlockSpec((tm, D_MODEL), lambda j, i: (i, 0)),
                  pl.BlockSpec((D_MODEL, tn), lambda j, i: (0, j)),
                  pl.BlockSpec((D_MODEL, tn), lambda j, i: (0, j)),
                  pl.BlockSpec((tm, tn), lambda j, i: (i, COL_GA // tn + j)),
                  pl.BlockSpec((tm, tn), lambda j, i: (i, COL_GB // tn + j))],
        out_specs=pl.BlockSpec((tm, tn), lambda j, i: (i, j)),
        out_shape=jax.ShapeDtypeStruct((TOKENS, D_MODEL), BF16),
        scratch_shapes=[pltpu.VMEM((D_MODEL, tn), BF16), pltpu.VMEM((D_MODEL, tn), BF16)],
        compiler_params=_cparams("arbitrary", "arbitrary"),
        name="merge",
    )(y_a, y_b, wa, wb, proj, proj)


def _outproj_body(m_ref, x_ref, mod_ref, wo_ref, g2_ref, rwh_ref, rwl_ref, rb_ref,
                  x1_ref, hp_ref, idx_ref, w_ref, rank_ref, cnt_ref, run_ref, before_ref, *, tm):
    @pl.when(pl.program_id(0) == 0)
    def _():
        run_ref[...] = jnp.zeros_like(run_ref)
        before_ref[...] = (lax.broadcasted_iota(I32, (tm, tm), 0)
                           < lax.broadcasted_iota(I32, (tm, tm), 1)).astype(BF16)

    y = jnp.dot(m_ref[...], wo_ref[...], preferred_element_type=F32)
    x1 = x_ref[...] + mod_ref[2:3, :] * y
    x1_ref[...] = x1
    h2 = _rms(x1) * g2_ref[...] * (1.0 + mod_ref[4:5, :]) + mod_ref[3:4, :]
    hp_ref[...] = h2
    hb = h2.astype(BF16)
    hf = hb.astype(F32)

    h_lo = (h2 - hf).astype(BF16)
    logits = (_dot_nt(rwh_ref[...], hb) + _dot_nt(rwh_ref[...], h_lo) + _dot_nt(rwl_ref[...], hb)
              + rb_ref[...])
    e_i = lax.broadcasted_iota(I32, (N_EXPERTS, tm), 0).astype(F32)
    vals, idxs = [], []
    l = logits
    for _ in range(TOP_K):
        m = jnp.max(l, axis=0, keepdims=True)
        i = jnp.min(jnp.where(l == m, e_i, float(N_EXPERTS)), axis=0, keepdims=True)
        vals.append(m)
        idxs.append(i)
        l = jnp.where(e_i == i, -jnp.inf, l)
    ex = [jnp.exp(v - vals[0]) for v in vals]
    den = ex[0] + ex[1] + ex[2] + ex[3]
    onehots = [e_i == idxs[k] for k in range(TOP_K)]
    stacked = jnp.concatenate([o.astype(BF16) for o in onehots], axis=0)
    earlier = jnp.dot(stacked, before_ref[...], preferred_element_type=F32)
    run = run_ref[...]
    for k in range(TOP_K):
        pref = earlier[k * N_EXPERTS:(k + 1) * N_EXPERTS, :] + run[:, 0:1]
        rank_ref[k:k + 1, :] = jnp.sum(jnp.where(onehots[k], pref, 0.0), axis=0, keepdims=True).astype(I32)
        run = run + jnp.sum(onehots[k].astype(F32), axis=1, keepdims=True)
        idx_ref[k:k + 1, :] = idxs[k].astype(I32)
        w_ref[k:k + 1, :] = ex[k] / den
    run_ref[...] = run
    cnt_ref[...] = run.astype(I32)


def _outproj(merged, x2d, mod, w_out, norm2_g, rw_hi, rw_lo, router_b):
    tm = 512
    per_batch = SEQ // tm
    body = functools.partial(_outproj_body, tm=tm)
    return pl.pallas_call(
        body,
        grid=(TOKENS // tm,),
        in_specs=[pl.BlockSpec((tm, D_MODEL), lambda i: (i, 0)),
                  pl.BlockSpec((tm, D_MODEL), lambda i: (i, 0)),
                  pl.BlockSpec((None, 6, D_MODEL), lambda i: (i // per_batch, 0, 0)),
                  pl.BlockSpec((D_MODEL, D_MODEL), lambda i: (0, 0)),
                  pl.BlockSpec((1, D_MODEL), lambda i: (0, 0)),
                  pl.BlockSpec((N_EXPERTS, D_MODEL), lambda i: (0, 0)),
                  pl.BlockSpec((N_EXPERTS, D_MODEL), lambda i: (0, 0)),
                  pl.BlockSpec((N_EXPERTS, 1), lambda i: (0, 0))],
        out_specs=[pl.BlockSpec((tm, D_MODEL), lambda i: (i, 0)),
                   pl.BlockSpec((tm, D_MODEL), lambda i: (i, 0)),
                   pl.BlockSpec((TOP_K, tm), lambda i: (0, i)),
                   pl.BlockSpec((TOP_K, tm), lambda i: (0, i)),
                   pl.BlockSpec((TOP_K, tm), lambda i: (0, i)),
                   pl.BlockSpec((N_EXPERTS, LANES), lambda i: (0, 0))],
        out_shape=[jax.ShapeDtypeStruct((TOKENS, D_MODEL), F32),
                   jax.ShapeDtypeStruct((TOKENS, D_MODEL), F32),
                   jax.ShapeDtypeStruct((TOP_K, TOKENS), I32),
                   jax.ShapeDtypeStruct((TOP_K, TOKENS), F32),
                   jax.ShapeDtypeStruct((TOP_K, TOKENS), I32),
                   jax.ShapeDtypeStruct((N_EXPERTS, LANES), I32)],
        scratch_shapes=[pltpu.VMEM((N_EXPERTS, LANES), F32), pltpu.VMEM((tm, tm), BF16)],
        compiler_params=_cparams("arbitrary"),
        name="outproj",
    )(merged, x2d, mod, w_out, norm2_g, rw_hi, rw_lo, router_b)


def _dispatch_body(dest_ref, pend_ref, padded_ref, h_ref, xs_hbm, zero_ref, zsem, sem, *, tm):
    i = pl.program_id(0)

    @pl.when(i == 0)
    def _():
        _dispatch_clear(pend_ref, padded_ref, xs_hbm, zero_ref, zsem)

    def group(g, carry):
        base = pl.multiple_of(g * ROW_DMA_GROUP, ROW_DMA_GROUP)
        for u in range(ROW_DMA_GROUP):
            src = h_ref.at[pl.ds(base + u, 1)]
            for k in range(TOP_K):
                d = dest_ref[(i * tm + base + u) * TOP_K + k]
                pltpu.make_async_copy(src, xs_hbm.at[pl.ds(d, 1)], sem).start(priority=(u * TOP_K + k) % 2)
        return carry
    lax.fori_loop(0, tm // ROW_DMA_GROUP, group, 0)
    for k in range(TOP_K):
        pltpu.make_async_copy(h_ref, xs_hbm.at[pl.ds(0, tm)], sem).wait()


def _dispatch_clear(pend_ref, padded_ref, xs_hbm, zero_ref, zsem):
    zero_ref[...] = jnp.zeros_like(zero_ref)

    def zero_copy(e):
        return pltpu.make_async_copy(
            zero_ref, xs_hbm.at[pl.ds(pl.multiple_of(pend_ref[e] - ROW_TILE, ROW_TILE), ROW_TILE)], zsem)

    def zstart(e, c):
        @pl.when(padded_ref[e] > 0)
        def _():
            zero_copy(e).start()
        return c

    def zwait(e, c):
        @pl.when(padded_ref[e] > 0)
        def _():
            zero_copy(e).wait()
        return c

    def tail_copy(t):
        return pltpu.make_async_copy(
            zero_ref, xs_hbm.at[pl.ds(pl.multiple_of(t * ROW_TILE, ROW_TILE), ROW_TILE)], zsem)

    def tstart(t, c):
        tail_copy(t).start()
        return c

    def twait(t, c):
        tail_copy(t).wait()
        return c

    first_tail = pend_ref[N_EXPERTS - 1] // ROW_TILE
    lax.fori_loop(0, N_EXPERTS, zstart, 0)
    lax.fori_loop(first_tail, N_ROW_TILES, tstart, 0)
    lax.fori_loop(0, N_EXPERTS, zwait, 0)
    lax.fori_loop(first_tail, N_ROW_TILES, twait, 0)


def _dispatch(dest_flat, padded_end, padded, h_rows):
    tm = 1024
    return pl.pallas_call(
        functools.partial(_dispatch_body, tm=tm),
        grid_spec=pltpu.PrefetchScalarGridSpec(
            num_scalar_prefetch=3,
            grid=(TOKENS // tm,),
            in_specs=[pl.BlockSpec((tm, D_MODEL), lambda i, d, pe, pd: (i, 0))],
            out_specs=pl.BlockSpec(memory_space=pl.ANY),
            scratch_shapes=[pltpu.VMEM((ROW_TILE, D_MODEL), F32),
                            pltpu.SemaphoreType.DMA(()),
                            pltpu.SemaphoreType.DMA(())]),
        out_shape=jax.ShapeDtypeStruct((PADDED_ROWS, D_MODEL), F32),
        compiler_params=_cparams("arbitrary"),
        name="dispatch",
    )(dest_flat, padded_end, padded, h_rows)


def _expert_body(se_ref, srow_ref, snsub_ref, xs_hbm, wg_hbm, wu_hbm, wd_hbm, bg_ref, bu_ref, bd_ref,
                 y_hbm, xstage, xb, acc, wg_f, wu_f, wd_f, wg_c, wu_c, wd_c, sem_in, sem_out, sem_w):
    s = pl.program_id(0)
    f = pl.program_id(1)
    nsub = snsub_ref[s]
    row0 = srow_ref[s]
    expert = se_ref[s]
    ff_chunk = pl.ds(pl.multiple_of(f * FF_TILE, FF_TILE), FF_TILE)

    def weight_copies(s_, f_):
        e = se_ref[s_]
        chunk = pl.ds(pl.multiple_of(f_ * FF_TILE, FF_TILE), FF_TILE)
        return (pltpu.make_async_copy(wg_hbm.at[e, :, chunk], wg_f, sem_w.at[0]),
                pltpu.make_async_copy(wu_hbm.at[e, :, chunk], wu_f, sem_w.at[1]),
                pltpu.make_async_copy(wd_hbm.at[e, chunk, :], wd_f, sem_w.at[2]))

    wrap = f == N_FF - 1
    s_next = jnp.minimum(jnp.where(wrap, s + 1, s), MAX_SUPER - 1)
    f_next = jnp.where(wrap, 0, f + 1)
    has_next = jnp.logical_not(wrap & (s == MAX_SUPER - 1)) & (snsub_ref[s_next] > 0)

    last_f = f == N_FF - 1
    n_pair = nsub // 2
    odd = nsub - 2 * n_pair
    pair_rows = 2 * ROW_TILE

    next_super = (s + 1 < MAX_SUPER) & (snsub_ref[jnp.minimum(s + 1, MAX_SUPER - 1)] > 0)

    def in_copy(row_base, t):
        src = xs_hbm.at[pl.ds(pl.multiple_of(row_base + t * ROW_TILE, ROW_TILE), ROW_TILE)]
        return pltpu.make_async_copy(src, xstage.at[t], sem_in.at[t])

    def start_rows(row_base, count):
        def body(t, c):
            in_copy(row_base, t).start()
            return c
        lax.fori_loop(0, count, body, 0)

    def out_copy(off, m):
        dst = y_hbm.at[pl.ds(pl.multiple_of(row0 + off, ROW_TILE), m)]
        return pltpu.make_async_copy(acc.at[pl.ds(off, m)], dst, sem_out)

    def drain_outputs(tiles):
        def body(p, c):
            out_copy(0, pair_rows).wait()
            return c
        lax.fori_loop(0, tiles // 2, body, 0)

        @pl.when(tiles % 2 == 1)
        def _():
            out_copy(0, ROW_TILE).wait()

    def gate(x):
        return jnp.minimum(jnp.dot(x, wg_c[...], preferred_element_type=F32) + bg_ref[pl.ds(expert, 1), ff_chunk],
                           SWIGLU_LIMIT)

    def up(x):
        return jnp.clip(jnp.dot(x, wu_c[...], preferred_element_type=F32) + bu_ref[pl.ds(expert, 1), ff_chunk],
                        -SWIGLU_LIMIT, SWIGLU_LIMIT)

    def down(rows, g, u):
        a = ((u + 1.0) * (g * jax.nn.sigmoid(SWIGLU_ALPHA * g))).astype(BF16)
        acc[rows, :] += jnp.dot(a, wd_c[...], preferred_element_type=F32)

    def finish(off, rows_n):
        @pl.when(last_f)
        def _():
            out_copy(off, rows_n).start()

    def compute(off, rows_n):
        rows = pl.ds(off, rows_n)
        x = xb[rows, :]
        down(rows, gate(x), up(x))
        finish(off, rows_n)

    def first_compute(rows_n, nxt):
        rows = pl.ds(0, rows_n)
        x = xb[rows, :]
        g = gate(x)
        wu_c[...] = wu_f[...].astype(BF16)

        @pl.when(has_next)
        def _():
            nxt[1].start()
        u = up(x)
        wd_c[...] = wd_f[...].astype(BF16)

        @pl.when(has_next)
        def _():
            nxt[2].start()
        down(rows, g, u)
        finish(0, rows_n)

    odd_off = pl.multiple_of((nsub - 1) * ROW_TILE, ROW_TILE)

    @pl.when(nsub > 0)
    def _():
        @pl.when((s == 0) & (f == 0))
        def _():
            for cp in weight_copies(s, f):
                cp.start()
            start_rows(row0, nsub)

        nxt = weight_copies(s_next, f_next)
        for cp in weight_copies(s, f):
            cp.wait()
        wg_c[...] = wg_f[...].astype(BF16)

        @pl.when(has_next)
        def _():
            nxt[0].start()

        @pl.when(f == 0)
        def _():
            @pl.when(s > 0)
            def _():
                drain_outputs(snsub_ref[jnp.maximum(s - 1, 0)])

            def load(t, c):
                in_copy(row0, t).wait()
                rows = pl.ds(pl.multiple_of(t * ROW_TILE, ROW_TILE), ROW_TILE)
                xb[rows, :] = xstage[t].astype(BF16)
                acc[rows, :] = jnp.broadcast_to(bd_ref[pl.ds(expert, 1), :], (ROW_TILE, D_MODEL))
                return c
            lax.fori_loop(0, nsub, load, 0)

        @pl.when(last_f & next_super)
        def _():
            s1 = jnp.minimum(s + 1, MAX_SUPER - 1)
            start_rows(srow_ref[s1], snsub_ref[s1])

        @pl.when(n_pair > 0)
        def _():
            first_compute(pair_rows, nxt)

        @pl.when(n_pair == 0)
        def _():
            first_compute(ROW_TILE, nxt)

        def pair(p, c):
            compute(pl.multiple_of(p * pair_rows, pair_rows), pair_rows)
            return c
        lax.fori_loop(1, n_pair, pair, 0)

        @pl.when((odd == 1) & (n_pair > 0))
        def _():
            compute(odd_off, ROW_TILE)

        @pl.when(last_f & jnp.logical_not(next_super))
        def _():
            drain_outputs(nsub)

    @pl.when((s == MAX_SUPER - 1) & last_f)
    def _():
        acc[0:ROW_TILE, :] = jnp.zeros((ROW_TILE, D_MODEL), F32)

        def tail_copy(t):
            dst = y_hbm.at[pl.ds(pl.multiple_of(t * ROW_TILE, ROW_TILE), ROW_TILE)]
            return pltpu.make_async_copy(acc.at[0:ROW_TILE], dst, sem_out)

        def tstart(t, c):
            tail_copy(t).start()
            return c

        def twait(t, c):
            tail_copy(t).wait()
            return c

        first_tail = snsub_ref[MAX_SUPER]
        lax.fori_loop(first_tail, N_ROW_TILES, tstart, 0)
        lax.fori_loop(first_tail, N_ROW_TILES, twait, 0)


def _experts(se, srow, snsub, xs, w_gate, b_gate, w_up, b_up, w_down, b_down):
    def whole(shape):
        return pl.BlockSpec(shape, lambda s, f, se, sr, sn: (0, 0))

    return pl.pallas_call(
        _expert_body,
        grid_spec=pltpu.PrefetchScalarGridSpec(
            num_scalar_prefetch=3,
            grid=(MAX_SUPER, N_FF),
            in_specs=[pl.BlockSpec(memory_space=pl.ANY),
                      pl.BlockSpec(memory_space=pl.ANY),
                      pl.BlockSpec(memory_space=pl.ANY),
                      pl.BlockSpec(memory_space=pl.ANY),
                      whole((N_EXPERTS, D_FF)),
                      whole((N_EXPERTS, D_FF)),
                      whole((N_EXPERTS, D_MODEL))],
            out_specs=pl.BlockSpec(memory_space=pl.ANY),
            scratch_shapes=[pltpu.VMEM((SUPER_TILES, ROW_TILE, D_MODEL), F32),
                            pltpu.VMEM((SUPER_ROWS, D_MODEL), BF16),
                            pltpu.VMEM((SUPER_ROWS, D_MODEL), F32),
                            pltpu.VMEM((D_MODEL, FF_TILE), F32),
                            pltpu.VMEM((D_MODEL, FF_TILE), F32),
                            pltpu.VMEM((FF_TILE, D_MODEL), F32),
                            pltpu.VMEM((D_MODEL, FF_TILE), BF16),
                            pltpu.VMEM((D_MODEL, FF_TILE), BF16),
                            pltpu.VMEM((FF_TILE, D_MODEL), BF16),
                            pltpu.SemaphoreType.DMA((SUPER_TILES,)),
                            pltpu.SemaphoreType.DMA(()),
                            pltpu.SemaphoreType.DMA((3,))]),
        out_shape=jax.ShapeDtypeStruct((PADDED_ROWS, D_MODEL), F32),
        compiler_params=_cparams("arbitrary", "arbitrary"),
        name="experts",
    )(se, srow, snsub, xs, w_gate, w_up, w_down,
      b_gate, b_up, b_down)


def _combine_body(dest_ref, y_hbm, x1_ref, w_ref, mod_ref, fg_ref, o_ref, buf, sem, *, tm):
    i = pl.program_id(0)
    slot = i % 2

    def gather(tile, dst_slot):
        def group(g, carry):
            base = pl.multiple_of(g * ROW_DMA_GROUP, ROW_DMA_GROUP)
            for u in range(ROW_DMA_GROUP):
                for k in range(TOP_K):
                    d = dest_ref[(tile * tm + base + u) * TOP_K + k]
                    pltpu.make_async_copy(y_hbm.at[pl.ds(d, 1)], buf.at[dst_slot, k, pl.ds(base + u, 1)],
                                          sem.at[dst_slot]).start(priority=(u * TOP_K + k) % 2)
            return carry
        lax.fori_loop(0, tm // ROW_DMA_GROUP, group, 0)

    @pl.when(i == 0)
    def _():
        gather(i, slot)

    @pl.when(i + 1 < pl.num_programs(0))
    def _():
        gather(i + 1, 1 - slot)

    for k in range(TOP_K):
        pltpu.make_async_copy(y_hbm.at[pl.ds(0, tm)], buf.at[slot, k], sem.at[slot]).wait()
    moe = buf[slot, 0] * w_ref[:, 0:1]
    for k in range(1, TOP_K):
        moe = moe + buf[slot, k] * w_ref[:, k:k + 1]
    x2 = x1_ref[...] + mod_ref[5:6, :] * moe
    o_ref[...] = _rms(x2) * fg_ref[...]


def _combine(dest_flat, y, x1, w_t, mod, final_g):
    tm = 256
    per_batch = SEQ // tm
    return pl.pallas_call(
        functools.partial(_combine_body, tm=tm),
        grid_spec=pltpu.PrefetchScalarGridSpec(
            num_scalar_prefetch=1,
            grid=(TOKENS // tm,),
            in_specs=[pl.BlockSpec(memory_space=pl.ANY),
                      pl.BlockSpec((tm, D_MODEL), lambda i, d: (i, 0)),
                      pl.BlockSpec((tm, TOP_K), lambda i, d: (i, 0)),
                      pl.BlockSpec((None, 6, D_MODEL), lambda i, d: (i // per_batch, 0, 0)),
                      pl.BlockSpec((1, D_MODEL), lambda i, d: (0, 0))],
            out_specs=pl.BlockSpec((tm, D_MODEL), lambda i, d: (i, 0)),
            scratch_shapes=[pltpu.VMEM((2, TOP_K, tm, D_MODEL), F32),
                            pltpu.SemaphoreType.DMA((2,))]),
        out_shape=jax.ShapeDtypeStruct((TOKENS, D_MODEL), F32),
        compiler_params=_cparams("arbitrary"),
        name="combine",
    )(dest_flat, y, x1, w_t, mod, final_g)


def _routing_tables(idx, rank, counts):
    padded = (counts + ROW_TILE - 1) // ROW_TILE * ROW_TILE
    padded_end = jnp.cumsum(padded)
    padded_start = padded_end - padded
    e_i = jnp.arange(N_EXPERTS, dtype=I32)[:, None, None]
    start_of = jnp.sum(jnp.where(idx[None] == e_i, padded_start[:, None, None], 0), axis=0)
    dest = (start_of + rank).T.reshape(N_ROWS).astype(I32)

    tiles = padded // ROW_TILE
    n_super = (tiles + SUPER_TILES - 1) // SUPER_TILES
    super_end = jnp.cumsum(n_super)
    super_start = super_end - n_super
    s_i = jnp.arange(MAX_SUPER, dtype=I32)
    total = super_end[-1]
    valid = s_i < total
    e_of = jnp.minimum(jnp.sum((s_i[:, None] >= super_end[None, :]).astype(I32), axis=1), N_EXPERTS - 1)
    last_e = jnp.minimum(jnp.sum((total - 1 >= super_end).astype(I32)), N_EXPERTS - 1)
    local = s_i - super_start[e_of]
    srow = jnp.where(valid, padded_start[e_of] + local * SUPER_ROWS, 0).astype(I32)
    snsub = jnp.where(valid, jnp.minimum(tiles[e_of] - local * SUPER_TILES, SUPER_TILES), 0).astype(I32)
    snsub = jnp.concatenate([snsub, (padded_end[-1:] // ROW_TILE).astype(I32)])
    se = jnp.where(valid, e_of, last_e).astype(I32)
    return dest, padded_end.astype(I32), padded.astype(I32), se, srow, snsub


def kernel(x, c, ada_w, ada_b, norm1_g, w_in, gla_gate_w2, gla_gate_b, sgu_ln_g, sgu_ln_b, sgu_w, sgu_b,
           gla_norm_g, w_branch_a, w_branch_b, w_out, norm2_g, router_w, router_b, exp_w_gate, exp_b_gate,
           exp_w_up, exp_b_up, exp_w_down, exp_b_down, final_g):
    x2d = x.reshape(TOKENS, D_MODEL)
    mod = _ada(c, ada_w[0], ada_b[0])

    w_in0 = w_in[0]
    w_in_t = w_in0.T
    w_glr = jnp.zeros((LANES, D_MODEL), BF16).at[:GLA_GATE_RANK].set(
        w_in_t[GLR_SRC:GLR_SRC + GLA_GATE_RANK].astype(BF16))
    h, glr = _norm1(x2d, mod, norm1_g, w_glr)
    proj = _inproj(h, w_in_t)

    y_a = _sgu(proj, sgu_ln_g, sgu_ln_b, sgu_w[0], sgu_b[0].T)

    w2p = jnp.zeros((LANES, GLA_DK), BF16).at[:GLA_GATE_RANK].set(gla_gate_w2[0].astype(BF16))
    y_b = _gla(proj, glr, w2p, gla_gate_b, gla_norm_g)

    merged = _merge(y_a, y_b, w_branch_a[0], w_branch_b[0], proj)

    rw_t = router_w[0].T
    rw_hi = rw_t.astype(BF16)
    rw_lo = (rw_t - rw_hi.astype(F32)).astype(BF16)
    x1, h_rows, idx, top_w, rank, cnt = _outproj(
        merged, x2d, mod, w_out[0].astype(BF16), norm2_g, rw_hi, rw_lo, router_b[0].reshape(N_EXPERTS, 1))

    dest, padded_end, padded, se, srow, snsub = _routing_tables(idx, rank, cnt[:, 0])
    xs = _dispatch(dest, padded_end, padded, h_rows)
    y = _experts(se, srow, snsub, xs, exp_w_gate[0], exp_b_gate[0], exp_w_up[0], exp_b_up[0],
                 exp_w_down[0], exp_b_down[0])
    out = _combine(dest, y, x1, top_w.T, mod, final_g.reshape(1, D_MODEL))
    return out.reshape(BATCH, SEQ, D_MODEL)
```

```python
import functools

import jax
import jax.numpy as jnp
from jax import lax
from jax.experimental import pallas as pl
from jax.experimental.pallas import tpu as pltpu

F32 = jnp.float32
BF16 = jnp.bfloat16
I32 = jnp.int32

D_MODEL = 2048
BATCH = 4
SEQ = 2048
TOKENS = BATCH * SEQ
CHUNK = 64
SPATIAL_BLOCK = 128
A_GROUPS = 8
A_GROUP_DIM = D_MODEL // A_GROUPS
GLA_HEADS = 4
GLA_DK = D_MODEL // 2
GLA_HEAD_K = GLA_DK // GLA_HEADS
GLA_HEAD_V = D_MODEL // GLA_HEADS
GLA_GATE_RANK = 16
GLA_GATE_TAU = 16.0
N_EXPERTS = 32
TOP_K = 4
D_FF = D_MODEL
SWIGLU_LIMIT = 7.0
SWIGLU_ALPHA = 1.702
NORM_EPS = 1e-6

LANES = 128
VMEM_LIMIT = 56 * 1024 * 1024

COL_U, COL_V, COL_Q, COL_K, COL_VV, COL_R = 0, 2048, 4096, 5120, 6144, 8192
COL_GA, COL_GB = 10240, 12288
PROJ_W = 14336
GLR_SRC = 10240

ROW_TILE = 256
SUPER_TILES = 6
SUPER_ROWS = ROW_TILE * SUPER_TILES
N_ROWS = TOKENS * TOP_K
N_ROW_TILES = (N_ROWS + N_EXPERTS * (ROW_TILE - 1)) // ROW_TILE
PADDED_ROWS = N_ROW_TILES * ROW_TILE
MAX_SUPER = -(-(N_ROW_TILES + N_EXPERTS * (SUPER_TILES - 1)) // SUPER_TILES)
FF_TILE = 512
N_FF = D_FF // FF_TILE
ROW_DMA_GROUP = 8


def _cparams(*sem):
    return pltpu.CompilerParams(dimension_semantics=sem, vmem_limit_bytes=VMEM_LIMIT)


def _rms(x):
    return x * lax.rsqrt(jnp.mean(x * x, axis=-1, keepdims=True) + NORM_EPS)


def _ada_body(c_ref, w_ref, b_ref, o_ref):
    c = c_ref[...]
    cond = c * jax.nn.sigmoid(c)
    o_ref[...] = jnp.dot(cond.astype(BF16), w_ref[...].astype(BF16),
                         preferred_element_type=F32) + b_ref[...]


def _ada(c, ada_w, ada_b):
    tn = 1024
    cp = jnp.zeros((8, D_MODEL), F32).at[:BATCH].set(c)
    out = pl.pallas_call(
        _ada_body,
        grid=(6 * D_MODEL // tn,),
        in_specs=[pl.BlockSpec((8, D_MODEL), lambda j: (0, 0)),
                  pl.BlockSpec((D_MODEL, tn), lambda j: (0, j)),
                  pl.BlockSpec((1, tn), lambda j: (0, j))],
        out_specs=pl.BlockSpec((8, tn), lambda j: (0, j)),
        out_shape=jax.ShapeDtypeStruct((8, 6 * D_MODEL), F32),
        compiler_params=_cparams("arbitrary"),
        name="ada",
    )(cp, ada_w, ada_b.reshape(1, 6 * D_MODEL))
    return out[:BATCH].reshape(BATCH, 6, D_MODEL)


def _norm1_body(x_ref, mod_ref, g_ref, wglr_ref, h_ref, glr_ref):
    h = _rms(x_ref[...]) * g_ref[...] * (1.0 + mod_ref[1:2, :]) + mod_ref[0:1, :]
    hb = h.astype(BF16)
    h_ref[...] = hb
    glr_ref[...] = _dot_nt(hb, wglr_ref[...])


def _norm1(x2d, mod, norm1_g, w_glr):
    tm = 512
    per_batch = SEQ // tm
    return pl.pallas_call(
        _norm1_body,
        grid=(TOKENS // tm,),
        in_specs=[pl.BlockSpec((tm, D_MODEL), lambda i: (i, 0)),
                  pl.BlockSpec((None, 6, D_MODEL), lambda i: (i // per_batch, 0, 0)),
                  pl.BlockSpec((1, D_MODEL), lambda i: (0, 0)),
                  pl.BlockSpec((LANES, D_MODEL), lambda i: (0, 0))],
        out_specs=[pl.BlockSpec((tm, D_MODEL), lambda i: (i, 0)),
                   pl.BlockSpec((tm, LANES), lambda i: (i, 0))],
        out_shape=[jax.ShapeDtypeStruct((TOKENS, D_MODEL), BF16),
                   jax.ShapeDtypeStruct((TOKENS, LANES), F32)],
        compiler_params=_cparams("arbitrary"),
        name="norm1",
    )(x2d, mod, norm1_g, w_glr)


INPROJ_TN = 1024
INPROJ_MAIN_TILES = COL_GA // INPROJ_TN
MXU_N = 256


def _inproj_body(h_ref, wt_ref, proj_ref, wb_ref):
    j = pl.program_id(0)

    @pl.when(pl.program_id(1) == 0)
    def _():
        wb_ref[...] = wt_ref[...].astype(BF16)

    def run(act):
        for n in range(INPROJ_TN // MXU_N):
            cols = slice(n * MXU_N, (n + 1) * MXU_N)
            acc = _dot_nt(h_ref[...], wb_ref[cols, :])
            proj_ref[:, cols] = act(acc).astype(BF16)

    col = j * INPROJ_TN

    @pl.when(col < COL_Q)
    def _():
        run(jax.nn.gelu)

    @pl.when((col >= COL_Q) & (col < COL_R))
    def _():
        run(lambda a: a)

    @pl.when((col >= COL_R) & (col < COL_GA))
    def _():
        run(lambda a: a * jax.nn.sigmoid(a))

    @pl.when(col >= COL_GA)
    def _():
        run(jax.nn.sigmoid)


def _inproj(h, w_in_t):
    tm, tn = 1024, INPROJ_TN

    def w_row(j, i):
        return (pl.multiple_of(j * tn + jnp.where(j >= INPROJ_MAIN_TILES, GLA_GATE_RANK, 0), GLA_GATE_RANK), 0)

    return pl.pallas_call(
        _inproj_body,
        grid=(PROJ_W // tn, TOKENS // tm),
        in_specs=[pl.BlockSpec((tm, D_MODEL), lambda j, i: (i, 0)),
                  pl.BlockSpec((pl.Element(tn), pl.Element(D_MODEL)), w_row)],
        out_specs=pl.BlockSpec((tm, tn), lambda j, i: (i, j)),
        out_shape=jax.ShapeDtypeStruct((TOKENS, PROJ_W), BF16),
        scratch_shapes=[pltpu.VMEM((tn, D_MODEL), BF16)],
        compiler_params=_cparams("arbitrary", "arbitrary"),
        name="inproj",
    )(h, w_in_t)


SGU_BLOCKS = 4


def _sgu_body(u_ref, v_ref, lg_ref, lb_ref, ws_ref, bs_ref, o_ref):
    t_chunk = lax.broadcasted_iota(I32, (SPATIAL_BLOCK, SPATIAL_BLOCK), 0) // CHUNK
    s_chunk = lax.broadcasted_iota(I32, (SPATIAL_BLOCK, SPATIAL_BLOCK), 1) // CHUNK
    mask = s_chunk <= t_chunk
    w = [jnp.where(mask, ws_ref[g], 0.0).astype(BF16) for g in range(A_GROUPS)]
    for blk in range(SGU_BLOCKS):
        rows = slice(blk * SPATIAL_BLOCK, (blk + 1) * SPATIAL_BLOCK)
        v = v_ref[rows, :].astype(F32)
        mu = jnp.mean(v, axis=-1, keepdims=True)
        xc = v - mu
        var = jnp.mean(xc * xc, axis=-1, keepdims=True)
        vn = (xc * lax.rsqrt(var + NORM_EPS) * lg_ref[...] + lb_ref[...]).astype(BF16)
        for g in range(A_GROUPS):
            cols = slice(g * A_GROUP_DIM, (g + 1) * A_GROUP_DIM)
            mixed = jnp.dot(w[g], vn[:, cols], preferred_element_type=F32) + bs_ref[:, g:g + 1]
            o_ref[rows, cols] = (u_ref[rows, cols].astype(F32) * mixed).astype(BF16)


def _sgu(proj, ln_g, ln_b, w_s, b_s_t):
    rows_blk = SGU_BLOCKS * SPATIAL_BLOCK
    nblk = TOKENS // rows_blk
    wb = D_MODEL
    return pl.pallas_call(
        _sgu_body,
        grid=(nblk,),
        in_specs=[pl.BlockSpec((rows_blk, wb), lambda i: (i, COL_U // wb)),
                  pl.BlockSpec((rows_blk, wb), lambda i: (i, COL_V // wb)),
                  pl.BlockSpec((1, wb), lambda i: (0, 0)),
                  pl.BlockSpec((1, wb), lambda i: (0, 0)),
                  pl.BlockSpec((A_GROUPS, SPATIAL_BLOCK, SPATIAL_BLOCK), lambda i: (0, 0, 0)),
                  pl.BlockSpec((SPATIAL_BLOCK, A_GROUPS), lambda i: (0, 0))],
        out_specs=pl.BlockSpec((rows_blk, wb), lambda i: (i, 0)),
        out_shape=jax.ShapeDtypeStruct((TOKENS, wb), BF16),
        compiler_params=_cparams("arbitrary"),
        name="sgu",
    )(proj, proj, ln_g, ln_b, w_s, b_s_t)


def _dot_nt(a, b):
    return lax.dot_general(a, b, (((1,), (1,)), ((), ())), preferred_element_type=F32)


def _dot_tn(a, b):
    return lax.dot_general(a, b, (((0,), (0,)), ((), ())), preferred_element_type=F32)


def _gla_body(q_ref, k_ref, v_ref, r_ref, glr_ref, w2_ref, gb_ref, ng_ref, o_ref,
              st_ref, la_ref, qd_ref, ks_ref, dec_ref, oi_ref, *, rows_blk):
    @pl.when(pl.program_id(1) == 0)
    def _():
        st_ref[...] = jnp.zeros_like(st_ref)

    z = jnp.dot(glr_ref[...].astype(BF16), w2_ref[...], preferred_element_type=F32) + gb_ref[...]
    la_ref[...] = jax.nn.log_sigmoid(z) / GLA_GATE_TAU

    r_i = lax.broadcasted_iota(I32, (CHUNK, CHUNK), 0)
    c_i = lax.broadcasted_iota(I32, (CHUNK, CHUNK), 1)
    causal = c_i <= r_i
    tril = causal.astype(BF16)
    scale = GLA_HEAD_K ** -0.5

    def head_cols(h):
        return (slice(h * GLA_HEAD_K, (h + 1) * GLA_HEAD_K), slice(h * GLA_HEAD_V, (h + 1) * GLA_HEAD_V))

    def intra(c, carry):
        rows = pl.ds(pl.multiple_of(c * CHUNK, CHUNK), CHUNK)
        for h in range(GLA_HEADS):
            kc, vc = head_cols(h)
            la = la_ref[rows, kc]
            hi = la.astype(BF16)
            r1 = la - hi.astype(F32)
            mid = r1.astype(BF16)
            lo = (r1 - mid.astype(F32)).astype(BF16)
            b = (jnp.dot(tril, hi, preferred_element_type=F32)
                 + jnp.dot(tril, mid, preferred_element_type=F32)
                 + jnp.dot(tril, lo, preferred_element_type=F32))
            b_last = b[CHUNK - 1:CHUNK, :]
            q = q_ref[rows, kc].astype(F32) * scale
            k = k_ref[rows, kc].astype(F32)
            q_dec = (q * jnp.exp(b)).astype(BF16)
            k_intra = (k * jnp.exp(-b)).astype(BF16)
            qd_ref[rows, kc] = q_dec
            ks_ref[rows, kc] = (k * jnp.exp(b_last - b)).astype(BF16)
            dec_ref[c, :, kc] = jnp.exp(b_last)
            att = jnp.where(causal, _dot_nt(q_dec, k_intra), 0.0).astype(BF16)
            oi_ref[rows, vc] = jnp.dot(att, v_ref[rows, vc], preferred_element_type=F32)
        return carry

    def inter(c, carry):
        rows = pl.ds(pl.multiple_of(c * CHUNK, CHUNK), CHUNK)
        for h in range(GLA_HEADS):
            kc, vc = head_cols(h)
            st = st_ref[h]
            o = oi_ref[rows, vc] + _dot_nt(qd_ref[rows, kc], st.astype(BF16))
            st_ref[h] = st * dec_ref[c, :, kc] + _dot_tn(v_ref[rows, vc], ks_ref[rows, kc])
            on = _rms(o) * ng_ref[...]
            o_ref[rows, vc] = (on * r_ref[rows, vc].astype(F32)).astype(BF16)
        return carry

    n_chunks = rows_blk // CHUNK
    lax.fori_loop(0, n_chunks, intra, 0, unroll=4)
    lax.fori_loop(0, n_chunks, inter, 0)


def _gla(proj, glr, w2p, gate_b, norm_g):
    rows_blk = 512
    nblk = SEQ // rows_blk
    dk, dv = GLA_DK, D_MODEL

    def row(b, n):
        return b * nblk + n

    return pl.pallas_call(
        functools.partial(_gla_body, rows_blk=rows_blk),
        grid=(BATCH, nblk),
        in_specs=[pl.BlockSpec((rows_blk, dk), lambda b, n: (row(b, n), COL_Q // dk)),
                  pl.BlockSpec((rows_blk, dk), lambda b, n: (row(b, n), COL_K // dk)),
                  pl.BlockSpec((rows_blk, dv), lambda b, n: (row(b, n), COL_VV // dv)),
                  pl.BlockSpec((rows_blk, dv), lambda b, n: (row(b, n), COL_R // dv)),
                  pl.BlockSpec((rows_blk, LANES), lambda b, n: (row(b, n), 0)),
                  pl.BlockSpec((LANES, dk), lambda b, n: (0, 0)),
                  pl.BlockSpec((1, dk), lambda b, n: (0, 0)),
                  pl.BlockSpec((1, GLA_HEAD_V), lambda b, n: (0, 0))],
        out_specs=pl.BlockSpec((rows_blk, dv), lambda b, n: (row(b, n), 0)),
        out_shape=jax.ShapeDtypeStruct((TOKENS, D_MODEL), BF16),
        scratch_shapes=[pltpu.VMEM((GLA_HEADS, GLA_HEAD_V, GLA_HEAD_K), F32),
                        pltpu.VMEM((rows_blk, dk), F32),
                        pltpu.VMEM((rows_blk, dk), BF16),
                        pltpu.VMEM((rows_blk, dk), BF16),
                        pltpu.VMEM((rows_blk // CHUNK, 1, dk), F32),
                        pltpu.VMEM((rows_blk, dv), F32)],
        compiler_params=_cparams("arbitrary", "arbitrary"),
        name="gla",
    )(proj, proj, proj, proj, glr, w2p, gate_b, norm_g)


def _merge_body(ya_ref, yb_ref, wa_ref, wb_ref, ga_ref, gb_ref, o_ref, wa_c, wb_c):
    @pl.when(pl.program_id(1) == 0)
    def _():
        wa_c[...] = wa_ref[...].astype(BF16)
        wb_c[...] = wb_ref[...].astype(BF16)

    a = jnp.dot(ya_ref[...], wa_c[...], preferred_element_type=F32)
    b = jnp.dot(yb_ref[...], wb_c[...], preferred_element_type=F32)
    o_ref[...] = (ga_ref[...].astype(F32) * a + gb_ref[...].astype(F32) * b).astype(BF16)


def _merge(y_a, y_b, wa, wb, proj):
    tm, tn = 1024, 512
    return pl.pallas_call(
        _merge_body,
        grid=(D_MODEL // tn, TOKENS // tm),
        in_specs=[pl.BlockSpec((tm, D_MODEL), lambda j, i: (i, 0)),
                  pl.BlockSpec((tm, D_MODEL), lambda j, i: (i, 0)),
                  pl.BlockSpec((D_MODEL, tn), lambda j, i: (0, j)),
                  pl.BlockSpec((D_MODEL, tn), lambda j, i: (0, j)),
                  pl.BlockSpec((tm, tn), lambda j, i: (i, COL_GA // tn + j)),
                  pl.BlockSpec((tm, tn), lambda j, i: (i, COL_GB // tn + j))],
        out_specs=pl.BlockSpec((tm, tn), lambda j, i: (i, j)),
        out_shape=jax.ShapeDtypeStruct((TOKENS, D_MODEL), BF16),
        scratch_shapes=[pltpu.VMEM((D_MODEL, tn), BF16), pltpu.VMEM((D_MODEL, tn), BF16)],
        compiler_params=_cparams("arbitrary", "arbitrary"),
        name="merge",
    )(y_a, y_b, wa, wb, proj, proj)


def _outproj_body(m_ref, x_ref, mod_ref, wo_ref, g2_ref, rwh_ref, rwl_ref, rb_ref,
                  x1_ref, hp_ref, idx_ref, w_ref, rank_ref, cnt_ref, run_ref, before_ref, *, tm):
    @pl.when(pl.program_id(0) == 0)
    def _():
        run_ref[...] = jnp.zeros_like(run_ref)
        before_ref[...] = (lax.broadcasted_iota(I32, (tm, tm), 0)
                           < lax.broadcasted_iota(I32, (tm, tm), 1)).astype(BF16)

    y = jnp.dot(m_ref[...], wo_ref[...], preferred_element_type=F32)
    x1 = x_ref[...] + mod_ref[2:3, :] * y
    x1_ref[...] = x1
    h2 = _rms(x1) * g2_ref[...] * (1.0 + mod_ref[4:5, :]) + mod_ref[3:4, :]
    hp_ref[...] = h2
    hb = h2.astype(BF16)
    hf = hb.astype(F32)

    h_lo = (h2 - hf).astype(BF16)
    logits = (_dot_nt(rwh_ref[...], hb) + _dot_nt(rwh_ref[...], h_lo) + _dot_nt(rwl_ref[...], hb)
              + rb_ref[...])
    e_i = lax.broadcasted_iota(I32, (N_EXPERTS, tm), 0).astype(F32)
    vals, idxs = [], []
    l = logits
    for _ in range(TOP_K):
        m = jnp.max(l, axis=0, keepdims=True)
        i = jnp.min(jnp.where(l == m, e_i, float(N_EXPERTS)), axis=0, keepdims=True)
        vals.append(m)
        idxs.append(i)
        l = jnp.where(e_i == i, -jnp.inf, l)
    ex = [jnp.exp(v - vals[0]) for v in vals]
    den = ex[0] + ex[1] + ex[2] + ex[3]
    onehots = [e_i == idxs[k] for k in range(TOP_K)]
    stacked = jnp.concatenate([o.astype(BF16) for o in onehots], axis=0)
    earlier = jnp.dot(stacked, before_ref[...], preferred_element_type=F32)
    run = run_ref[...]
    for k in range(TOP_K):
        pref = earlier[k * N_EXPERTS:(k + 1) * N_EXPERTS, :] + run[:, 0:1]
        rank_ref[k:k + 1, :] = jnp.sum(jnp.where(onehots[k], pref, 0.0), axis=0, keepdims=True).astype(I32)
        run = run + jnp.sum(onehots[k].astype(F32), axis=1, keepdims=True)
        idx_ref[k:k + 1, :] = idxs[k].astype(I32)
        w_ref[k:k + 1, :] = ex[k] / den
    run_ref[...] = run
    cnt_ref[...] = run.astype(I32)


def _outproj(merged, x2d, mod, w_out, norm2_g, rw_hi, rw_lo, router_b):
    tm = 512
    per_batch = SEQ // tm
    body = functools.partial(_outproj_body, tm=tm)
    return pl.pallas_call(
        body,
        grid=(TOKENS // tm,),
        in_specs=[pl.BlockSpec((tm, D_MODEL), lambda i: (i, 0)),
                  pl.BlockSpec((tm, D_MODEL), lambda i: (i, 0)),
                  pl.BlockSpec((None, 6, D_MODEL), lambda i: (i // per_batch, 0, 0)),
                  pl.BlockSpec((D_MODEL, D_MODEL), lambda i: (0, 0)),
                  pl.BlockSpec((1, D_MODEL), lambda i: (0, 0)),
                  pl.BlockSpec((N_EXPERTS, D_MODEL), lambda i: (0, 0)),
                  pl.BlockSpec((N_EXPERTS, D_MODEL), lambda i: (0, 0)),
                  pl.BlockSpec((N_EXPERTS, 1), lambda i: (0, 0))],
        out_specs=[pl.BlockSpec((tm, D_MODEL), lambda i: (i, 0)),
                   pl.BlockSpec((tm, D_MODEL), lambda i: (i, 0)),
                   pl.BlockSpec((TOP_K, tm), lambda i: (0, i)),
                   pl.BlockSpec((TOP_K, tm), lambda i: (0, i)),
                   pl.BlockSpec((TOP_K, tm), lambda i: (0, i)),
                   pl.BlockSpec((N_EXPERTS, LANES), lambda i: (0, 0))],
        out_shape=[jax.ShapeDtypeStruct((TOKENS, D_MODEL), F32),
                   jax.ShapeDtypeStruct((TOKENS, D_MODEL), F32),
                   jax.ShapeDtypeStruct((TOP_K, TOKENS), I32),
                   jax.ShapeDtypeStruct((TOP_K, TOKENS), F32),
                   jax.ShapeDtypeStruct((TOP_K, TOKENS), I32),
                   jax.ShapeDtypeStruct((N_EXPERTS, LANES), I32)],
        scratch_shapes=[pltpu.VMEM((N_EXPERTS, LANES), F32), pltpu.VMEM((tm, tm), BF16)],
        compiler_params=_cparams("arbitrary"),
        name="outproj",
    )(merged, x2d, mod, w_out, norm2_g, rw_hi, rw_lo, router_b)


def _dispatch_body(dest_ref, pend_ref, count_ref, h_ref, xs_hbm, zero_ref, zsem, sem, *, tm):
    i = pl.program_id(0)

    @pl.when(i == 0)
    def _():
        zero_ref[...] = jnp.zeros_like(zero_ref)
        _dispatch_pad(pend_ref, count_ref, xs_hbm, zero_ref, zsem, start=True)

    def group(g, carry):
        base = pl.multiple_of(g * ROW_DMA_GROUP, ROW_DMA_GROUP)
        for u in range(ROW_DMA_GROUP):
            src = h_ref.at[pl.ds(base + u, 1)]
            for k in range(TOP_K):
                d = dest_ref[(i * tm + base + u) * TOP_K + k]
                pltpu.make_async_copy(src, xs_hbm.at[pl.ds(d, 1)], sem).start(priority=(u * TOP_K + k) % 2)
        return carry
    lax.fori_loop(0, tm // ROW_DMA_GROUP, group, 0)
    for k in range(TOP_K):
        pltpu.make_async_copy(h_ref, xs_hbm.at[pl.ds(0, tm)], sem).wait()

    @pl.when(i == pl.num_programs(0) - 1)
    def _():
        _dispatch_pad(pend_ref, count_ref, xs_hbm, zero_ref, zsem, start=False)


def _dispatch_pad(pend_ref, count_ref, xs_hbm, zero_ref, zsem, start):
    sub = 8

    def go(cp):
        if start:
            cp.start()
        else:
            cp.wait()

    def rows_copy(r, n):
        return pltpu.make_async_copy(zero_ref.at[0:n], xs_hbm.at[pl.ds(r, n)], zsem)

    def expert(e, c):
        hi = pend_ref[e]
        cnt = count_ref[e]
        lo = hi - (cnt + ROW_TILE - 1) // ROW_TILE * ROW_TILE + cnt
        lo_sub = jnp.minimum((lo + sub - 1) // sub * sub, hi)

        def single(r, c2):
            go(rows_copy(r, 1))
            return c2
        lax.fori_loop(lo, lo_sub, single, 0)

        def block(b, c2):
            go(rows_copy(pl.multiple_of(b * sub, sub), sub))
            return c2
        lax.fori_loop(lo_sub // sub, hi // sub, block, 0)
        return c
    lax.fori_loop(0, N_EXPERTS, expert, 0)

    def tail(t, c):
        go(rows_copy(pl.multiple_of(t * ROW_TILE, ROW_TILE), ROW_TILE))
        return c
    lax.fori_loop(pend_ref[N_EXPERTS - 1] // ROW_TILE, N_ROW_TILES, tail, 0)


def _dispatch(dest_flat, padded_end, counts, h_rows):
    tm = 1024
    return pl.pallas_call(
        functools.partial(_dispatch_body, tm=tm),
        grid_spec=pltpu.PrefetchScalarGridSpec(
            num_scalar_prefetch=3,
            grid=(TOKENS // tm,),
            in_specs=[pl.BlockSpec((tm, D_MODEL), lambda i, d, pe, pd: (i, 0))],
            out_specs=pl.BlockSpec(memory_space=pl.ANY),
            scratch_shapes=[pltpu.VMEM((ROW_TILE, D_MODEL), F32),
                            pltpu.SemaphoreType.DMA(()),
                            pltpu.SemaphoreType.DMA(())]),
        out_shape=jax.ShapeDtypeStruct((PADDED_ROWS, D_MODEL), F32),
        compiler_params=_cparams("arbitrary"),
        name="dispatch",
    )(dest_flat, padded_end, counts, h_rows)


def _expert_body(se_ref, srow_ref, snsub_ref, xs_hbm, wg_hbm, wu_hbm, wd_hbm, bg_ref, bu_ref, bd_ref,
                 y_hbm, xstage, xb, acc, wg_f, wu_f, wd_f, wg_c, wu_c, wd_c, sem_in, sem_out, sem_w):
    s = pl.program_id(0)
    f = pl.program_id(1)
    nsub = snsub_ref[s]
    row0 = srow_ref[s]
    expert = se_ref[s]
    ff_chunk = pl.ds(pl.multiple_of(f * FF_TILE, FF_TILE), FF_TILE)

    def weight_copies(s_, f_):
        e = se_ref[s_]
        chunk = pl.ds(pl.multiple_of(f_ * FF_TILE, FF_TILE), FF_TILE)
        return (pltpu.make_async_copy(wg_hbm.at[e, :, chunk], wg_f, sem_w.at[0]),
                pltpu.make_async_copy(wu_hbm.at[e, :, chunk], wu_f, sem_w.at[1]),
                pltpu.make_async_copy(wd_hbm.at[e, chunk, :], wd_f, sem_w.at[2]))

    wrap = f == N_FF - 1
    s_next = jnp.minimum(jnp.where(wrap, s + 1, s), MAX_SUPER - 1)
    f_next = jnp.where(wrap, 0, f + 1)
    has_next = jnp.logical_not(wrap & (s == MAX_SUPER - 1)) & (snsub_ref[s_next] > 0)

    last_f = f == N_FF - 1
    n_pair = nsub // 2
    odd = nsub - 2 * n_pair
    pair_rows = 2 * ROW_TILE

    next_super = (s + 1 < MAX_SUPER) & (snsub_ref[jnp.minimum(s + 1, MAX_SUPER - 1)] > 0)

    def in_copy(row_base, t):
        src = xs_hbm.at[pl.ds(pl.multiple_of(row_base + t * ROW_TILE, ROW_TILE), ROW_TILE)]
        return pltpu.make_async_copy(src, xstage.at[t], sem_in.at[t])

    def start_rows(row_base, count):
        def body(t, c):
            in_copy(row_base, t).start()
            return c
        lax.fori_loop(0, count, body, 0)

    def out_copy(off, m):
        dst = y_hbm.at[pl.ds(pl.multiple_of(row0 + off, ROW_TILE), m)]
        return pltpu.make_async_copy(acc.at[pl.ds(off, m)], dst, sem_out)

    def drain_outputs(tiles):
        def body(p, c):
            out_copy(0, pair_rows).wait()
            return c
        lax.fori_loop(0, tiles // 2, body, 0)

        @pl.when(tiles % 2 == 1)
        def _():
            out_copy(0, ROW_TILE).wait()

    def gate(x):
        return jnp.minimum(jnp.dot(x, wg_c[...], preferred_element_type=F32) + bg_ref[pl.ds(expert, 1), ff_chunk],
                           SWIGLU_LIMIT)

    def up(x):
        return jnp.clip(jnp.dot(x, wu_c[...], preferred_element_type=F32) + bu_ref[pl.ds(expert, 1), ff_chunk],
                        -SWIGLU_LIMIT, SWIGLU_LIMIT)

    def down(rows, g, u):
        a = ((u + 1.0) * (g * jax.nn.sigmoid(SWIGLU_ALPHA * g))).astype(BF16)
        acc[rows, :] += jnp.dot(a, wd_c[...], preferred_element_type=F32)

    def finish(off, rows_n):
        @pl.when(last_f)
        def _():
            out_copy(off, rows_n).start()

    def compute(off, rows_n):
        rows = pl.ds(off, rows_n)
        x = xb[rows, :]
        down(rows, gate(x), up(x))
        finish(off, rows_n)

    def first_compute(rows_n, nxt):
        rows = pl.ds(0, rows_n)
        x = xb[rows, :]
        g = gate(x)
        wu_c[...] = wu_f[...].astype(BF16)

        @pl.when(has_next)
        def _():
            nxt[1].start()
        u = up(x)
        wd_c[...] = wd_f[...].astype(BF16)

        @pl.when(has_next)
        def _():
            nxt[2].start()
        down(rows, g, u)
        finish(0, rows_n)

    odd_off = pl.multiple_of((nsub - 1) * ROW_TILE, ROW_TILE)

    @pl.when(nsub > 0)
    def _():
        @pl.when((s == 0) & (f == 0))
        def _():
            for cp in weight_copies(s, f):
                cp.start()
            start_rows(row0, nsub)

        nxt = weight_copies(s_next, f_next)
        for cp in weight_copies(s, f):
            cp.wait()
        wg_c[...] = wg_f[...].astype(BF16)

        @pl.when(has_next)
        def _():
            nxt[0].start()

        @pl.when(f == 0)
        def _():
            @pl.when(s > 0)
            def _():
                drain_outputs(snsub_ref[jnp.maximum(s - 1, 0)])

            def load(t, c):
                in_copy(row0, t).wait()
                rows = pl.ds(pl.multiple_of(t * ROW_TILE, ROW_TILE), ROW_TILE)
                xb[rows, :] = xstage[t].astype(BF16)
                acc[rows, :] = jnp.broadcast_to(bd_ref[pl.ds(expert, 1), :], (ROW_TILE, D_MODEL))
                return c
            lax.fori_loop(0, nsub, load, 0)

        @pl.when(last_f & next_super)
        def _():
            s1 = jnp.minimum(s + 1, MAX_SUPER - 1)
            start_rows(srow_ref[s1], snsub_ref[s1])

        @pl.when(n_pair > 0)
        def _():
            first_compute(pair_rows, nxt)

        @pl.when(n_pair == 0)
        def _():
            first_compute(ROW_TILE, nxt)

        def pair(p, c):
            compute(pl.multiple_of(p * pair_rows, pair_rows), pair_rows)
            return c
        lax.fori_loop(1, n_pair, pair, 0)

        @pl.when((odd == 1) & (n_pair > 0))
        def _():
            compute(odd_off, ROW_TILE)

        @pl.when(last_f & jnp.logical_not(next_super))
        def _():
            drain_outputs(nsub)

    @pl.when((s == MAX_SUPER - 1) & last_f)
    def _():
        acc[0:ROW_TILE, :] = jnp.zeros((ROW_TILE, D_MODEL), F32)

        def tail_copy(t):
            dst = y_hbm.at[pl.ds(pl.multiple_of(t * ROW_TILE, ROW_TILE), ROW_TILE)]
            return pltpu.make_async_copy(acc.at[0:ROW_TILE], dst, sem_out)

        def tstart(t, c):
            tail_copy(t).start()
            return c

        def twait(t, c):
            tail_copy(t).wait()
            return c

        first_tail = snsub_ref[MAX_SUPER]
        lax.fori_loop(first_tail, N_ROW_TILES, tstart, 0)
        lax.fori_loop(first_tail, N_ROW_TILES, twait, 0)


def _experts(se, srow, snsub, xs, w_gate, b_gate, w_up, b_up, w_down, b_down):
    def whole(shape):
        return pl.BlockSpec(shape, lambda s, f, se, sr, sn: (0, 0))

    return pl.pallas_call(
        _expert_body,
        grid_spec=pltpu.PrefetchScalarGridSpec(
            num_scalar_prefetch=3,
            grid=(MAX_SUPER, N_FF),
            in_specs=[pl.BlockSpec(memory_space=pl.ANY),
                      pl.BlockSpec(memory_space=pl.ANY),
                      pl.BlockSpec(memory_space=pl.ANY),
                      pl.BlockSpec(memory_space=pl.ANY),
                      whole((N_EXPERTS, D_FF)),
                      whole((N_EXPERTS, D_FF)),
                      whole((N_EXPERTS, D_MODEL))],
            out_specs=pl.BlockSpec(memory_space=pl.ANY),
            scratch_shapes=[pltpu.VMEM((SUPER_TILES, ROW_TILE, D_MODEL), F32),
                            pltpu.VMEM((SUPER_ROWS, D_MODEL), BF16),
                            pltpu.VMEM((SUPER_ROWS, D_MODEL), F32),
                            pltpu.VMEM((D_MODEL, FF_TILE), F32),
                            pltpu.VMEM((D_MODEL, FF_TILE), F32),
                            pltpu.VMEM((FF_TILE, D_MODEL), F32),
                            pltpu.VMEM((D_MODEL, FF_TILE), BF16),
                            pltpu.VMEM((D_MODEL, FF_TILE), BF16),
                            pltpu.VMEM((FF_TILE, D_MODEL), BF16),
                            pltpu.SemaphoreType.DMA((SUPER_TILES,)),
                            pltpu.SemaphoreType.DMA(()),
                            pltpu.SemaphoreType.DMA((3,))]),
        out_shape=jax.ShapeDtypeStruct((PADDED_ROWS, D_MODEL), F32),
        compiler_params=_cparams("arbitrary", "arbitrary"),
        name="experts",
    )(se, srow, snsub, xs, w_gate, w_up, w_down,
      b_gate, b_up, b_down)


def _combine_body(dest_ref, y_hbm, x1_ref, w_ref, mod_ref, fg_ref, o_ref, buf, sem, *, tm):
    i = pl.program_id(0)
    slot = i % 2

    def slot_cols(k):
        return pl.ds(k * D_MODEL, D_MODEL)

    def gather(tile, dst_slot):
        def group(g, carry):
            base = pl.multiple_of(g * ROW_DMA_GROUP, ROW_DMA_GROUP)
            for u in range(ROW_DMA_GROUP):
                for k in range(TOP_K):
                    d = dest_ref[(tile * tm + base + u) * TOP_K + k]
                    pltpu.make_async_copy(y_hbm.at[pl.ds(d, 1)],
                                          buf.at[dst_slot, pl.ds(base + u, 1), slot_cols(k)],
                                          sem.at[dst_slot]).start(priority=(u * TOP_K + k) % 2)
            return carry
        lax.fori_loop(0, tm // ROW_DMA_GROUP, group, 0)

    @pl.when(i == 0)
    def _():
        gather(i, slot)

    @pl.when(i + 1 < pl.num_programs(0))
    def _():
        gather(i + 1, 1 - slot)

    for k in range(TOP_K):
        pltpu.make_async_copy(y_hbm.at[pl.ds(0, tm)], buf.at[slot, :, slot_cols(k)], sem.at[slot]).wait()
    moe = buf[slot, :, 0:D_MODEL] * w_ref[:, 0:1]
    for k in range(1, TOP_K):
        moe = moe + buf[slot, :, k * D_MODEL:(k + 1) * D_MODEL] * w_ref[:, k:k + 1]
    x2 = x1_ref[...] + mod_ref[5:6, :] * moe
    o_ref[...] = _rms(x2) * fg_ref[...]


def _combine(dest_flat, y, x1, w_t, mod, final_g):
    tm = 256
    per_batch = SEQ // tm
    return pl.pallas_call(
        functools.partial(_combine_body, tm=tm),
        grid_spec=pltpu.PrefetchScalarGridSpec(
            num_scalar_prefetch=1,
            grid=(TOKENS // tm,),
            in_specs=[pl.BlockSpec(memory_space=pl.ANY),
                      pl.BlockSpec((tm, D_MODEL), lambda i, d: (i, 0)),
                      pl.BlockSpec((tm, TOP_K), lambda i, d: (i, 0)),
                      pl.BlockSpec((None, 6, D_MODEL), lambda i, d: (i // per_batch, 0, 0)),
                      pl.BlockSpec((1, D_MODEL), lambda i, d: (0, 0))],
            out_specs=pl.BlockSpec((tm, D_MODEL), lambda i, d: (i, 0)),
            scratch_shapes=[pltpu.VMEM((2, tm, TOP_K * D_MODEL), F32),
                            pltpu.SemaphoreType.DMA((2,))]),
        out_shape=jax.ShapeDtypeStruct((TOKENS, D_MODEL), F32),
        compiler_params=_cparams("arbitrary"),
        name="combine",
    )(dest_flat, y, x1, w_t, mod, final_g)


def _routing_tables(idx, rank, counts):
    padded = (counts + ROW_TILE - 1) // ROW_TILE * ROW_TILE
    padded_end = jnp.cumsum(padded)
    padded_start = padded_end - padded
    e_i = jnp.arange(N_EXPERTS, dtype=I32)[:, None, None]
    start_of = jnp.sum(jnp.where(idx[None] == e_i, padded_start[:, None, None], 0), axis=0)
    dest = (start_of + rank).T.reshape(N_ROWS).astype(I32)

    tiles = padded // ROW_TILE
    n_super = (tiles + SUPER_TILES - 1) // SUPER_TILES
    super_end = jnp.cumsum(n_super)
    super_start = super_end - n_super
    s_i = jnp.arange(MAX_SUPER, dtype=I32)
    total = super_end[-1]
    valid = s_i < total
    e_of = jnp.minimum(jnp.sum((s_i[:, None] >= super_end[None, :]).astype(I32), axis=1), N_EXPERTS - 1)
    last_e = jnp.minimum(jnp.sum((total - 1 >= super_end).astype(I32)), N_EXPERTS - 1)
    local = s_i - super_start[e_of]
    srow = jnp.where(valid, padded_start[e_of] + local * SUPER_ROWS, 0).astype(I32)
    snsub = jnp.where(valid, jnp.minimum(tiles[e_of] - local * SUPER_TILES, SUPER_TILES), 0).astype(I32)
    snsub = jnp.concatenate([snsub, (padded_end[-1:] // ROW_TILE).astype(I32)])
    se = jnp.where(valid, e_of, last_e).astype(I32)
    return dest, padded_end.astype(I32), se, srow, snsub


def kernel(x, c, ada_w, ada_b, norm1_g, w_in, gla_gate_w2, gla_gate_b, sgu_ln_g, sgu_ln_b, sgu_w, sgu_b,
           gla_norm_g, w_branch_a, w_branch_b, w_out, norm2_g, router_w, router_b, exp_w_gate, exp_b_gate,
           exp_w_up, exp_b_up, exp_w_down, exp_b_down, final_g):
    x2d = x.reshape(TOKENS, D_MODEL)
    mod = _ada(c, ada_w[0], ada_b[0])

    w_in0 = w_in[0]
    w_in_t = w_in0.T
    w_glr = jnp.zeros((LANES, D_MODEL), BF16).at[:GLA_GATE_RANK].set(
        w_in_t[GLR_SRC:GLR_SRC + GLA_GATE_RANK].astype(BF16))
    h, glr = _norm1(x2d, mod, norm1_g, w_glr)
    proj = _inproj(h, w_in_t)

    y_a = _sgu(proj, sgu_ln_g, sgu_ln_b, sgu_w[0], sgu_b[0].T)

    w2p = jnp.zeros((LANES, GLA_DK), BF16).at[:GLA_GATE_RANK].set(gla_gate_w2[0].astype(BF16))
    y_b = _gla(proj, glr, w2p, gla_gate_b, gla_norm_g)

    merged = _merge(y_a, y_b, w_branch_a[0], w_branch_b[0], proj)

    rw_t = router_w[0].T
    rw_hi = rw_t.astype(BF16)
    rw_lo = (rw_t - rw_hi.astype(F32)).astype(BF16)
    x1, h_rows, idx, top_w, rank, cnt = _outproj(
        merged, x2d, mod, w_out[0].astype(BF16), norm2_g, rw_hi, rw_lo, router_b[0].reshape(N_EXPERTS, 1))

    counts = cnt[:, 0]
    dest, padded_end, se, srow, snsub = _routing_tables(idx, rank, counts)
    xs = _dispatch(dest, padded_end, counts, h_rows)
    y = _experts(se, srow, snsub, xs, exp_w_gate[0], exp_b_gate[0], exp_w_up[0], exp_b_up[0],
                 exp_w_down[0], exp_b_down[0])
    out = _combine(dest, y, x1, top_w.T, mod, final_g.reshape(1, D_MODEL))
    return out.reshape(BATCH, SEQ, D_MODEL)
```

```python
import functools

import jax
import jax.numpy as jnp
from jax import lax
from jax.experimental import pallas as pl
from jax.experimental.pallas import tpu as pltpu

F32 = jnp.float32
BF16 = jnp.bfloat16
I32 = jnp.int32

D_MODEL = 2048
BATCH = 4
SEQ = 2048
TOKENS = BATCH * SEQ
CHUNK = 64
SPATIAL_BLOCK = 128
A_GROUPS = 8
A_GROUP_DIM = D_MODEL // A_GROUPS
GLA_HEADS = 4
GLA_DK = D_MODEL // 2
GLA_HEAD_K = GLA_DK // GLA_HEADS
GLA_HEAD_V = D_MODEL // GLA_HEADS
GLA_GATE_RANK = 16
GLA_GATE_TAU = 16.0
N_EXPERTS = 32
TOP_K = 4
D_FF = D_MODEL
SWIGLU_LIMIT = 7.0
SWIGLU_ALPHA = 1.702
NORM_EPS = 1e-6

LANES = 128
VMEM_LIMIT = 56 * 1024 * 1024

COL_U, COL_V, COL_Q, COL_K, COL_VV, COL_R = 0, 2048, 4096, 5120, 6144, 8192
COL_GA, COL_GB = 10240, 12288
PROJ_W = 14336
GLR_SRC = 10240

ROW_TILE = 256
SUPER_TILES = 6
SUPER_ROWS = ROW_TILE * SUPER_TILES
N_ROWS = TOKENS * TOP_K
N_ROW_TILES = (N_ROWS + N_EXPERTS * (ROW_TILE - 1)) // ROW_TILE
PADDED_ROWS = N_ROW_TILES * ROW_TILE
MAX_SUPER = -(-(N_ROW_TILES + N_EXPERTS * (SUPER_TILES - 1)) // SUPER_TILES)
FF_TILE = 512
N_FF = D_FF // FF_TILE
ROW_DMA_GROUP = 8


def _cparams(*sem):
    return pltpu.CompilerParams(dimension_semantics=sem, vmem_limit_bytes=VMEM_LIMIT)


def _rms(x):
    return x * lax.rsqrt(jnp.mean(x * x, axis=-1, keepdims=True) + NORM_EPS)


def _ada_body(c_ref, w_ref, b_ref, o_ref):
    c = c_ref[...]
    cond = c * jax.nn.sigmoid(c)
    o_ref[...] = jnp.dot(cond.astype(BF16), w_ref[...].astype(BF16),
                         preferred_element_type=F32) + b_ref[...]


def _ada(c, ada_w, ada_b):
    tn = 1024
    cp = jnp.zeros((8, D_MODEL), F32).at[:BATCH].set(c)
    out = pl.pallas_call(
        _ada_body,
        grid=(6 * D_MODEL // tn,),
        in_specs=[pl.BlockSpec((8, D_MODEL), lambda j: (0, 0)),
                  pl.BlockSpec((D_MODEL, tn), lambda j: (0, j)),
                  pl.BlockSpec((1, tn), lambda j: (0, j))],
        out_specs=pl.BlockSpec((8, tn), lambda j: (0, j)),
        out_shape=jax.ShapeDtypeStruct((8, 6 * D_MODEL), F32),
        compiler_params=_cparams("arbitrary"),
        name="ada",
    )(cp, ada_w, ada_b.reshape(1, 6 * D_MODEL))
    return out[:BATCH].reshape(BATCH, 6, D_MODEL)


def _norm1_body(x_ref, mod_ref, g_ref, wglr_ref, h_ref, glr_ref):
    h = _rms(x_ref[...]) * g_ref[...] * (1.0 + mod_ref[1:2, :]) + mod_ref[0:1, :]
    hb = h.astype(BF16)
    h_ref[...] = hb
    glr_ref[...] = _dot_nt(hb, wglr_ref[...])


def _norm1(x2d, mod, norm1_g, w_glr):
    tm = 512
    per_batch = SEQ // tm
    return pl.pallas_call(
        _norm1_body,
        grid=(TOKENS // tm,),
        in_specs=[pl.BlockSpec((tm, D_MODEL), lambda i: (i, 0)),
                  pl.BlockSpec((None, 6, D_MODEL), lambda i: (i // per_batch, 0, 0)),
                  pl.BlockSpec((1, D_MODEL), lambda i: (0, 0)),
                  pl.BlockSpec((LANES, D_MODEL), lambda i: (0, 0))],
        out_specs=[pl.BlockSpec((tm, D_MODEL), lambda i: (i, 0)),
                   pl.BlockSpec((tm, LANES), lambda i: (i, 0))],
        out_shape=[jax.ShapeDtypeStruct((TOKENS, D_MODEL), BF16),
                   jax.ShapeDtypeStruct((TOKENS, LANES), F32)],
        compiler_params=_cparams("arbitrary"),
        name="norm1",
    )(x2d, mod, norm1_g, w_glr)


INPROJ_TN = 1024
INPROJ_MAIN_TILES = COL_GA // INPROJ_TN
MXU_N = 256


def _inproj_body(h_ref, wt_ref, proj_ref, wb_ref):
    j = pl.program_id(0)

    @pl.when(pl.program_id(1) == 0)
    def _():
        wb_ref[...] = wt_ref[...].astype(BF16)

    def run(act):
        for n in range(INPROJ_TN // MXU_N):
            cols = slice(n * MXU_N, (n + 1) * MXU_N)
            acc = _dot_nt(h_ref[...], wb_ref[cols, :])
            proj_ref[:, cols] = act(acc).astype(BF16)

    col = j * INPROJ_TN

    @pl.when(col < COL_Q)
    def _():
        run(jax.nn.gelu)

    @pl.when((col >= COL_Q) & (col < COL_R))
    def _():
        run(lambda a: a)

    @pl.when((col >= COL_R) & (col < COL_GA))
    def _():
        run(lambda a: a * jax.nn.sigmoid(a))

    @pl.when(col >= COL_GA)
    def _():
        run(jax.nn.sigmoid)


def _inproj(h, w_in_t):
    tm, tn = 1024, INPROJ_TN

    def w_row(j, i):
        return (pl.multiple_of(j * tn + jnp.where(j >= INPROJ_MAIN_TILES, GLA_GATE_RANK, 0), GLA_GATE_RANK), 0)

    return pl.pallas_call(
        _inproj_body,
        grid=(PROJ_W // tn, TOKENS // tm),
        in_specs=[pl.BlockSpec((tm, D_MODEL), lambda j, i: (i, 0)),
                  pl.BlockSpec((pl.Element(tn), pl.Element(D_MODEL)), w_row)],
        out_specs=pl.BlockSpec((tm, tn), lambda j, i: (i, j)),
        out_shape=jax.ShapeDtypeStruct((TOKENS, PROJ_W), BF16),
        scratch_shapes=[pltpu.VMEM((tn, D_MODEL), BF16)],
        compiler_params=_cparams("arbitrary", "arbitrary"),
        name="inproj",
    )(h, w_in_t)


SGU_BLOCKS = 4


def _sgu_body(u_ref, v_ref, lg_ref, lb_ref, ws_ref, bs_ref, o_ref):
    t_chunk = lax.broadcasted_iota(I32, (SPATIAL_BLOCK, SPATIAL_BLOCK), 0) // CHUNK
    s_chunk = lax.broadcasted_iota(I32, (SPATIAL_BLOCK, SPATIAL_BLOCK), 1) // CHUNK
    mask = s_chunk <= t_chunk
    w = [jnp.where(mask, ws_ref[g], 0.0).astype(BF16) for g in range(A_GROUPS)]
    for blk in range(SGU_BLOCKS):
        rows = slice(blk * SPATIAL_BLOCK, (blk + 1) * SPATIAL_BLOCK)
        v = v_ref[rows, :].astype(F32)
        mu = jnp.mean(v, axis=-1, keepdims=True)
        xc = v - mu
        var = jnp.mean(xc * xc, axis=-1, keepdims=True)
        vn = (xc * lax.rsqrt(var + NORM_EPS) * lg_ref[...] + lb_ref[...]).astype(BF16)
        for g in range(A_GROUPS):
            cols = slice(g * A_GROUP_DIM, (g + 1) * A_GROUP_DIM)
            mixed = jnp.dot(w[g], vn[:, cols], preferred_element_type=F32) + bs_ref[:, g:g + 1]
            o_ref[rows, cols] = (u_ref[rows, cols].astype(F32) * mixed).astype(BF16)


def _sgu(proj, ln_g, ln_b, w_s, b_s_t):
    rows_blk = SGU_BLOCKS * SPATIAL_BLOCK
    nblk = TOKENS // rows_blk
    wb = D_MODEL
    return pl.pallas_call(
        _sgu_body,
        grid=(nblk,),
        in_specs=[pl.BlockSpec((rows_blk, wb), lambda i: (i, COL_U // wb)),
                  pl.BlockSpec((rows_blk, wb), lambda i: (i, COL_V // wb)),
                  pl.BlockSpec((1, wb), lambda i: (0, 0)),
                  pl.BlockSpec((1, wb), lambda i: (0, 0)),
                  pl.BlockSpec((A_GROUPS, SPATIAL_BLOCK, SPATIAL_BLOCK), lambda i: (0, 0, 0)),
                  pl.BlockSpec((SPATIAL_BLOCK, A_GROUPS), lambda i: (0, 0))],
        out_specs=pl.BlockSpec((rows_blk, wb), lambda i: (i, 0)),
        out_shape=jax.ShapeDtypeStruct((TOKENS, wb), BF16),
        compiler_params=_cparams("arbitrary"),
        name="sgu",
    )(proj, proj, ln_g, ln_b, w_s, b_s_t)


def _dot_nt(a, b):
    return lax.dot_general(a, b, (((1,), (1,)), ((), ())), preferred_element_type=F32)


def _dot_tn(a, b):
    return lax.dot_general(a, b, (((0,), (0,)), ((), ())), preferred_element_type=F32)


def _gla_body(q_ref, k_ref, v_ref, r_ref, glr_ref, w2_ref, gb_ref, ng_ref, o_ref,
              st_ref, la_ref, qd_ref, ks_ref, dec_ref, oi_ref, *, rows_blk):
    @pl.when(pl.program_id(1) == 0)
    def _():
        st_ref[...] = jnp.zeros_like(st_ref)

    z = jnp.dot(glr_ref[...].astype(BF16), w2_ref[...], preferred_element_type=F32) + gb_ref[...]
    la_ref[...] = jax.nn.log_sigmoid(z) / GLA_GATE_TAU

    r_i = lax.broadcasted_iota(I32, (CHUNK, CHUNK), 0)
    c_i = lax.broadcasted_iota(I32, (CHUNK, CHUNK), 1)
    causal = c_i <= r_i
    tril = causal.astype(BF16)
    scale = GLA_HEAD_K ** -0.5

    def head_cols(h):
        return (slice(h * GLA_HEAD_K, (h + 1) * GLA_HEAD_K), slice(h * GLA_HEAD_V, (h + 1) * GLA_HEAD_V))

    def intra(c, carry):
        rows = pl.ds(pl.multiple_of(c * CHUNK, CHUNK), CHUNK)
        for h in range(GLA_HEADS):
            kc, vc = head_cols(h)
            la = la_ref[rows, kc]
            hi = la.astype(BF16)
            r1 = la - hi.astype(F32)
            mid = r1.astype(BF16)
            lo = (r1 - mid.astype(F32)).astype(BF16)
            b = (jnp.dot(tril, hi, preferred_element_type=F32)
                 + jnp.dot(tril, mid, preferred_element_type=F32)
                 + jnp.dot(tril, lo, preferred_element_type=F32))
            b_last = b[CHUNK - 1:CHUNK, :]
            q = q_ref[rows, kc].astype(F32) * scale
            k = k_ref[rows, kc].astype(F32)
            q_dec = (q * jnp.exp(b)).astype(BF16)
            k_intra = (k * jnp.exp(-b)).astype(BF16)
            qd_ref[rows, kc] = q_dec
            ks_ref[rows, kc] = (k * jnp.exp(b_last - b)).astype(BF16)
            dec_ref[c, :, kc] = jnp.exp(b_last)
            att = jnp.where(causal, _dot_nt(q_dec, k_intra), 0.0).astype(BF16)
            oi_ref[rows, vc] = jnp.dot(att, v_ref[rows, vc], preferred_element_type=F32)
        return carry

    def inter(c, carry):
        rows = pl.ds(pl.multiple_of(c * CHUNK, CHUNK), CHUNK)
        for h in range(GLA_HEADS):
            kc, vc = head_cols(h)
            st = st_ref[h]
            o = oi_ref[rows, vc] + _dot_nt(qd_ref[rows, kc], st.astype(BF16))
            st_ref[h] = st * dec_ref[c, :, kc] + _dot_tn(v_ref[rows, vc], ks_ref[rows, kc])
            on = _rms(o) * ng_ref[...]
            o_ref[rows, vc] = (on * r_ref[rows, vc].astype(F32)).astype(BF16)
        return carry

    n_chunks = rows_blk // CHUNK
    lax.fori_loop(0, n_chunks, intra, 0, unroll=True)
    lax.fori_loop(0, n_chunks, inter, 0, unroll=True)


def _gla(proj, glr, w2p, gate_b, norm_g):
    rows_blk = 512
    nblk = SEQ // rows_blk
    dk, dv = GLA_DK, D_MODEL

    def row(b, n):
        return b * nblk + n

    return pl.pallas_call(
        functools.partial(_gla_body, rows_blk=rows_blk),
        grid=(BATCH, nblk),
        in_specs=[pl.BlockSpec((rows_blk, dk), lambda b, n: (row(b, n), COL_Q // dk)),
                  pl.BlockSpec((rows_blk, dk), lambda b, n: (row(b, n), COL_K // dk)),
                  pl.BlockSpec((rows_blk, dv), lambda b, n: (row(b, n), COL_VV // dv)),
                  pl.BlockSpec((rows_blk, dv), lambda b, n: (row(b, n), COL_R // dv)),
                  pl.BlockSpec((rows_blk, LANES), lambda b, n: (row(b, n), 0)),
                  pl.BlockSpec((LANES, dk), lambda b, n: (0, 0)),
                  pl.BlockSpec((1, dk), lambda b, n: (0, 0)),
                  pl.BlockSpec((1, GLA_HEAD_V), lambda b, n: (0, 0))],
        out_specs=pl.BlockSpec((rows_blk, dv), lambda b, n: (row(b, n), 0)),
        out_shape=jax.ShapeDtypeStruct((TOKENS, D_MODEL), BF16),
        scratch_shapes=[pltpu.VMEM((GLA_HEADS, GLA_HEAD_V, GLA_HEAD_K), F32),
                        pltpu.VMEM((rows_blk, dk), F32),
                        pltpu.VMEM((rows_blk, dk), BF16),
                        pltpu.VMEM((rows_blk, dk), BF16),
                        pltpu.VMEM((rows_blk // CHUNK, 1, dk), F32),
                        pltpu.VMEM((rows_blk, dv), F32)],
        compiler_params=_cparams("arbitrary", "arbitrary"),
        name="gla",
    )(proj, proj, proj, proj, glr, w2p, gate_b, norm_g)


def _merge_body(ya_ref, yb_ref, wa_ref, wb_ref, ga_ref, gb_ref, o_ref, wa_c, wb_c):
    @pl.when(pl.program_id(1) == 0)
    def _():
        wa_c[...] = wa_ref[...].astype(BF16)
        wb_c[...] = wb_ref[...].astype(BF16)

    a = jnp.dot(ya_ref[...], wa_c[...], preferred_element_type=F32)
    b = jnp.dot(yb_ref[...], wb_c[...], preferred_element_type=F32)
    o_ref[...] = (ga_ref[...].astype(F32) * a + gb_ref[...].astype(F32) * b).astype(BF16)


def _merge(y_a, y_b, wa, wb, proj):
    tm, tn = 1024, 512
    return pl.pallas_call(
        _merge_body,
        grid=(D_MODEL // tn, TOKENS // tm),
        in_specs=[pl.BlockSpec((tm, D_MODEL), lambda j, i: (i, 0)),
                  pl.BlockSpec((tm, D_MODEL), lambda j, i: (i, 0)),
                  pl.BlockSpec((D_MODEL, tn), lambda j, i: (0, j)),
                  pl.BlockSpec((D_MODEL, tn), lambda j, i: (0, j)),
                  pl.BlockSpec((tm, tn), lambda j, i: (i, COL_GA // tn + j)),
                  pl.BlockSpec((tm, tn), lambda j, i: (i, COL_GB // tn + j))],
        out_specs=pl.BlockSpec((tm, tn), lambda j, i: (i, j)),
        out_shape=jax.ShapeDtypeStruct((TOKENS, D_MODEL), BF16),
        scratch_shapes=[pltpu.VMEM((D_MODEL, tn), BF16), pltpu.VMEM((D_MODEL, tn), BF16)],
        compiler_params=_cparams("arbitrary", "arbitrary"),
        name="merge",
    )(y_a, y_b, wa, wb, proj, proj)


def _outproj_body(m_ref, x_ref, mod_ref, wo_ref, g2_ref, rwh_ref, rwl_ref, rb_ref,
                  x1_ref, hp_ref, idx_ref, w_ref, rank_ref, cnt_ref, run_ref, before_ref, *, tm):
    @pl.when(pl.program_id(0) == 0)
    def _():
        run_ref[...] = jnp.zeros_like(run_ref)
        before_ref[...] = (lax.broadcasted_iota(I32, (tm, tm), 0)
                           < lax.broadcasted_iota(I32, (tm, tm), 1)).astype(BF16)

    y = jnp.dot(m_ref[...], wo_ref[...], preferred_element_type=F32)
    x1 = x_ref[...] + mod_ref[2:3, :] * y
    x1_ref[...] = x1
    h2 = _rms(x1) * g2_ref[...] * (1.0 + mod_ref[4:5, :]) + mod_ref[3:4, :]
    hp_ref[...] = h2
    hb = h2.astype(BF16)
    hf = hb.astype(F32)

    h_lo = (h2 - hf).astype(BF16)
    logits = (_dot_nt(rwh_ref[...], hb) + _dot_nt(rwh_ref[...], h_lo) + _dot_nt(rwl_ref[...], hb)
              + rb_ref[...])
    e_i = lax.broadcasted_iota(I32, (N_EXPERTS, tm), 0).astype(F32)
    vals, idxs = [], []
    l = logits
    for _ in range(TOP_K):
        m = jnp.max(l, axis=0, keepdims=True)
        i = jnp.min(jnp.where(l == m, e_i, float(N_EXPERTS)), axis=0, keepdims=True)
        vals.append(m)
        idxs.append(i)
        l = jnp.where(e_i == i, -jnp.inf, l)
    ex = [jnp.exp(v - vals[0]) for v in vals]
    den = ex[0] + ex[1] + ex[2] + ex[3]
    onehots = [e_i == idxs[k] for k in range(TOP_K)]
    stacked = jnp.concatenate([o.astype(BF16) for o in onehots], axis=0)
    earlier = jnp.dot(stacked, before_ref[...], preferred_element_type=F32)
    run = run_ref[...]
    for k in range(TOP_K):
        pref = earlier[k * N_EXPERTS:(k + 1) * N_EXPERTS, :] + run[:, 0:1]
        rank_ref[k:k + 1, :] = jnp.sum(jnp.where(onehots[k], pref, 0.0), axis=0, keepdims=True).astype(I32)
        run = run + jnp.sum(onehots[k].astype(F32), axis=1, keepdims=True)
        idx_ref[k:k + 1, :] = idxs[k].astype(I32)
        w_ref[k:k + 1, :] = ex[k] / den
    run_ref[...] = run
    cnt_ref[...] = run.astype(I32)


def _outproj(merged, x2d, mod, w_out, norm2_g, rw_hi, rw_lo, router_b):
    tm = 512
    per_batch = SEQ // tm
    body = functools.partial(_outproj_body, tm=tm)
    return pl.pallas_call(
        body,
        grid=(TOKENS // tm,),
        in_specs=[pl.BlockSpec((tm, D_MODEL), lambda i: (i, 0)),
                  pl.BlockSpec((tm, D_MODEL), lambda i: (i, 0)),
                  pl.BlockSpec((None, 6, D_MODEL), lambda i: (i // per_batch, 0, 0)),
                  pl.BlockSpec((D_MODEL, D_MODEL), lambda i: (0, 0)),
                  pl.BlockSpec((1, D_MODEL), lambda i: (0, 0)),
                  pl.BlockSpec((N_EXPERTS, D_MODEL), lambda i: (0, 0)),
                  pl.BlockSpec((N_EXPERTS, D_MODEL), lambda i: (0, 0)),
                  pl.BlockSpec((N_EXPERTS, 1), lambda i: (0, 0))],
        out_specs=[pl.BlockSpec((tm, D_MODEL), lambda i: (i, 0)),
                   pl.BlockSpec((tm, D_MODEL), lambda i: (i, 0)),
                   pl.BlockSpec((TOP_K, tm), lambda i: (0, i)),
                   pl.BlockSpec((TOP_K, tm), lambda i: (0, i)),
                   pl.BlockSpec((TOP_K, tm), lambda i: (0, i)),
                   pl.BlockSpec((N_EXPERTS, LANES), lambda i: (0, 0))],
        out_shape=[jax.ShapeDtypeStruct((TOKENS, D_MODEL), F32),
                   jax.ShapeDtypeStruct((TOKENS, D_MODEL), F32),
                   jax.ShapeDtypeStruct((TOP_K, TOKENS), I32),
                   jax.ShapeDtypeStruct((TOP_K, TOKENS), F32),
                   jax.ShapeDtypeStruct((TOP_K, TOKENS), I32),
                   jax.ShapeDtypeStruct((N_EXPERTS, LANES), I32)],
        scratch_shapes=[pltpu.VMEM((N_EXPERTS, LANES), F32), pltpu.VMEM((tm, tm), BF16)],
        compiler_params=_cparams("arbitrary"),
        name="outproj",
    )(merged, x2d, mod, w_out, norm2_g, rw_hi, rw_lo, router_b)


def _dispatch_body(dest_ref, pend_ref, count_ref, h_ref, xs_hbm, zero_ref, zsem, sem, *, tm):
    i = pl.program_id(0)

    @pl.when(i == 0)
    def _():
        zero_ref[...] = jnp.zeros_like(zero_ref)
        _dispatch_pad(pend_ref, count_ref, xs_hbm, zero_ref, zsem, start=True)

    def group(g, carry):
        base = pl.multiple_of(g * ROW_DMA_GROUP, ROW_DMA_GROUP)
        for u in range(ROW_DMA_GROUP):
            src = h_ref.at[pl.ds(base + u, 1)]
            for k in range(TOP_K):
                d = dest_ref[(i * tm + base + u) * TOP_K + k]
                pltpu.make_async_copy(src, xs_hbm.at[pl.ds(d, 1)], sem).start(priority=(u * TOP_K + k) % 2)
        return carry
    lax.fori_loop(0, tm // ROW_DMA_GROUP, group, 0)
    for k in range(TOP_K):
        pltpu.make_async_copy(h_ref, xs_hbm.at[pl.ds(0, tm)], sem).wait()

    @pl.when(i == pl.num_programs(0) - 1)
    def _():
        _dispatch_pad(pend_ref, count_ref, xs_hbm, zero_ref, zsem, start=False)


def _dispatch_pad(pend_ref, count_ref, xs_hbm, zero_ref, zsem, start):
    sub = 8

    def go(cp):
        if start:
            cp.start()
        else:
            cp.wait()

    def rows_copy(r, n):
        return pltpu.make_async_copy(zero_ref.at[0:n], xs_hbm.at[pl.ds(r, n)], zsem)

    def expert(e, c):
        hi = pend_ref[e]
        cnt = count_ref[e]
        lo = hi - (cnt + ROW_TILE - 1) // ROW_TILE * ROW_TILE + cnt
        lo_sub = jnp.minimum((lo + sub - 1) // sub * sub, hi)

        def single(r, c2):
            go(rows_copy(r, 1))
            return c2
        lax.fori_loop(lo, lo_sub, single, 0)

        def block(b, c2):
            go(rows_copy(pl.multiple_of(b * sub, sub), sub))
            return c2
        lax.fori_loop(lo_sub // sub, hi // sub, block, 0)
        return c
    lax.fori_loop(0, N_EXPERTS, expert, 0)

    def tail(t, c):
        go(rows_copy(pl.multiple_of(t * ROW_TILE, ROW_TILE), ROW_TILE))
        return c
    lax.fori_loop(pend_ref[N_EXPERTS - 1] // ROW_TILE, N_ROW_TILES, tail, 0)


def _dispatch(dest_flat, padded_end, counts, h_rows):
    tm = 1024
    return pl.pallas_call(
        functools.partial(_dispatch_body, tm=tm),
        grid_spec=pltpu.PrefetchScalarGridSpec(
            num_scalar_prefetch=3,
            grid=(TOKENS // tm,),
            in_specs=[pl.BlockSpec((tm, D_MODEL), lambda i, d, pe, pd: (i, 0))],
            out_specs=pl.BlockSpec(memory_space=pl.ANY),
            scratch_shapes=[pltpu.VMEM((ROW_TILE, D_MODEL), F32),
                            pltpu.SemaphoreType.DMA(()),
                            pltpu.SemaphoreType.DMA(())]),
        out_shape=jax.ShapeDtypeStruct((PADDED_ROWS, D_MODEL), F32),
        compiler_params=_cparams("arbitrary"),
        name="dispatch",
    )(dest_flat, padded_end, counts, h_rows)


def _expert_body(se_ref, srow_ref, snsub_ref, xs_hbm, wg_hbm, wu_hbm, wd_hbm, bg_ref, bu_ref, bd_ref,
                 y_hbm, xstage, xb, acc, wg_f, wu_f, wd_f, wg_c, wu_c, wd_c, sem_in, sem_out, sem_w):
    s = pl.program_id(0)
    f = pl.program_id(1)
    nsub = snsub_ref[s]
    row0 = srow_ref[s]
    expert = se_ref[s]
    ff_chunk = pl.ds(pl.multiple_of(f * FF_TILE, FF_TILE), FF_TILE)

    def weight_copies(s_, f_):
        e = se_ref[s_]
        chunk = pl.ds(pl.multiple_of(f_ * FF_TILE, FF_TILE), FF_TILE)
        return (pltpu.make_async_copy(wg_hbm.at[e, :, chunk], wg_f, sem_w.at[0]),
                pltpu.make_async_copy(wu_hbm.at[e, :, chunk], wu_f, sem_w.at[1]),
                pltpu.make_async_copy(wd_hbm.at[e, chunk, :], wd_f, sem_w.at[2]))

    wrap = f == N_FF - 1
    s_next = jnp.minimum(jnp.where(wrap, s + 1, s), MAX_SUPER - 1)
    f_next = jnp.where(wrap, 0, f + 1)
    has_next = jnp.logical_not(wrap & (s == MAX_SUPER - 1)) & (snsub_ref[s_next] > 0)

    last_f = f == N_FF - 1
    n_pair = nsub // 2
    odd = nsub - 2 * n_pair
    pair_rows = 2 * ROW_TILE

    next_super = (s + 1 < MAX_SUPER) & (snsub_ref[jnp.minimum(s + 1, MAX_SUPER - 1)] > 0)

    def in_copy(row_base, t):
        src = xs_hbm.at[pl.ds(pl.multiple_of(row_base + t * ROW_TILE, ROW_TILE), ROW_TILE)]
        return pltpu.make_async_copy(src, xstage.at[t], sem_in.at[t])

    def start_rows(row_base, count):
        def body(t, c):
            in_copy(row_base, t).start()
            return c
        lax.fori_loop(0, count, body, 0)

    def out_copy(off, m):
        dst = y_hbm.at[pl.ds(pl.multiple_of(row0 + off, ROW_TILE), m)]
        return pltpu.make_async_copy(acc.at[pl.ds(off, m)], dst, sem_out)

    def drain_outputs(tiles):
        def body(p, c):
            out_copy(0, pair_rows).wait()
            return c
        lax.fori_loop(0, tiles // 2, body, 0)

        @pl.when(tiles % 2 == 1)
        def _():
            out_copy(0, ROW_TILE).wait()

    def gate(x):
        return jnp.minimum(jnp.dot(x, wg_c[...], preferred_element_type=F32) + bg_ref[pl.ds(expert, 1), ff_chunk],
                           SWIGLU_LIMIT)

    def up(x):
        return jnp.clip(jnp.dot(x, wu_c[...], preferred_element_type=F32) + bu_ref[pl.ds(expert, 1), ff_chunk],
                        -SWIGLU_LIMIT, SWIGLU_LIMIT)

    def down(rows, g, u):
        a = ((u + 1.0) * (g * jax.nn.sigmoid(SWIGLU_ALPHA * g))).astype(BF16)
        acc[rows, :] += jnp.dot(a, wd_c[...], preferred_element_type=F32)

    def finish(off, rows_n):
        @pl.when(last_f)
        def _():
            out_copy(off, rows_n).start()

    def compute(off, rows_n):
        rows = pl.ds(off, rows_n)
        x = xb[rows, :]
        down(rows, gate(x), up(x))
        finish(off, rows_n)

    def first_compute(rows_n, nxt):
        rows = pl.ds(0, rows_n)
        x = xb[rows, :]
        g = gate(x)
        wu_c[...] = wu_f[...].astype(BF16)

        @pl.when(has_next)
        def _():
            nxt[1].start()
        u = up(x)
        wd_c[...] = wd_f[...].astype(BF16)

        @pl.when(has_next)
        def _():
            nxt[2].start()
        down(rows, g, u)
        finish(0, rows_n)

    odd_off = pl.multiple_of((nsub - 1) * ROW_TILE, ROW_TILE)

    @pl.when(nsub > 0)
    def _():
        @pl.when((s == 0) & (f == 0))
        def _():
            for cp in weight_copies(s, f):
                cp.start()
            start_rows(row0, nsub)

        nxt = weight_copies(s_next, f_next)
        for cp in weight_copies(s, f):
            cp.wait()
        wg_c[...] = wg_f[...].astype(BF16)

        @pl.when(has_next)
        def _():
            nxt[0].start()

        @pl.when(f == 0)
        def _():
            @pl.when(s > 0)
            def _():
                drain_outputs(snsub_ref[jnp.maximum(s - 1, 0)])

            def load(t, c):
                in_copy(row0, t).wait()
                rows = pl.ds(pl.multiple_of(t * ROW_TILE, ROW_TILE), ROW_TILE)
                xb[rows, :] = xstage[t].astype(BF16)
                acc[rows, :] = jnp.broadcast_to(bd_ref[pl.ds(expert, 1), :], (ROW_TILE, D_MODEL))
                return c
            lax.fori_loop(0, nsub, load, 0)

        @pl.when(last_f & next_super)
        def _():
            s1 = jnp.minimum(s + 1, MAX_SUPER - 1)
            start_rows(srow_ref[s1], snsub_ref[s1])

        @pl.when(n_pair > 0)
        def _():
            first_compute(pair_rows, nxt)

        @pl.when(n_pair == 0)
        def _():
            first_compute(ROW_TILE, nxt)

        def pair(p, c):
            compute(pl.multiple_of(p * pair_rows, pair_rows), pair_rows)
            return c
        lax.fori_loop(1, n_pair, pair, 0)

        @pl.when((odd == 1) & (n_pair > 0))
        def _():
            compute(odd_off, ROW_TILE)

        @pl.when(last_f & jnp.logical_not(next_super))
        def _():
            drain_outputs(nsub)

    @pl.when((s == MAX_SUPER - 1) & last_f)
    def _():
        acc[0:ROW_TILE, :] = jnp.zeros((ROW_TILE, D_MODEL), F32)

        def tail_copy(t):
            dst = y_hbm.at[pl.ds(pl.multiple_of(t * ROW_TILE, ROW_TILE), ROW_TILE)]
            return pltpu.make_async_copy(acc.at[0:ROW_TILE], dst, sem_out)

        def tstart(t, c):
            tail_copy(t).start()
            return c

        def twait(t, c):
            tail_copy(t).wait()
            return c

        first_tail = snsub_ref[MAX_SUPER]
        lax.fori_loop(first_tail, N_ROW_TILES, tstart, 0)
        lax.fori_loop(first_tail, N_ROW_TILES, twait, 0)


def _experts(se, srow, snsub, xs, w_gate, b_gate, w_up, b_up, w_down, b_down):
    def whole(shape):
        return pl.BlockSpec(shape, lambda s, f, se, sr, sn: (0, 0))

    return pl.pallas_call(
        _expert_body,
        grid_spec=pltpu.PrefetchScalarGridSpec(
            num_scalar_prefetch=3,
            grid=(MAX_SUPER, N_FF),
            in_specs=[pl.BlockSpec(memory_space=pl.ANY),
                      pl.BlockSpec(memory_space=pl.ANY),
                      pl.BlockSpec(memory_space=pl.ANY),
                      pl.BlockSpec(memory_space=pl.ANY),
                      whole((N_EXPERTS, D_FF)),
                      whole((N_EXPERTS, D_FF)),
                      whole((N_EXPERTS, D_MODEL))],
            out_specs=pl.BlockSpec(memory_space=pl.ANY),
            scratch_shapes=[pltpu.VMEM((SUPER_TILES, ROW_TILE, D_MODEL), F32),
                            pltpu.VMEM((SUPER_ROWS, D_MODEL), BF16),
                            pltpu.VMEM((SUPER_ROWS, D_MODEL), F32),
                            pltpu.VMEM((D_MODEL, FF_TILE), F32),
                            pltpu.VMEM((D_MODEL, FF_TILE), F32),
                            pltpu.VMEM((FF_TILE, D_MODEL), F32),
                            pltpu.VMEM((D_MODEL, FF_TILE), BF16),
                            pltpu.VMEM((D_MODEL, FF_TILE), BF16),
                            pltpu.VMEM((FF_TILE, D_MODEL), BF16),
                            pltpu.SemaphoreType.DMA((SUPER_TILES,)),
                            pltpu.SemaphoreType.DMA(()),
                            pltpu.SemaphoreType.DMA((3,))]),
        out_shape=jax.ShapeDtypeStruct((PADDED_ROWS, D_MODEL), F32),
        compiler_params=_cparams("arbitrary", "arbitrary"),
        name="experts",
    )(se, srow, snsub, xs, w_gate, w_up, w_down,
      b_gate, b_up, b_down)


def _combine_body(dest_ref, y_hbm, x1_ref, w_ref, mod_ref, fg_ref, o_ref, buf, sem, *, tm):
    i = pl.program_id(0)
    slot = i % 2

    def slot_cols(k):
        return pl.ds(k * D_MODEL, D_MODEL)

    def gather(tile, dst_slot):
        def group(g, carry):
            base = pl.multiple_of(g * ROW_DMA_GROUP, ROW_DMA_GROUP)
            for u in range(ROW_DMA_GROUP):
                for k in range(TOP_K):
                    d = dest_ref[(tile * tm + base + u) * TOP_K + k]
                    pltpu.make_async_copy(y_hbm.at[pl.ds(d, 1)],
                                          buf.at[dst_slot, pl.ds(base + u, 1), slot_cols(k)],
                                          sem.at[dst_slot]).start(priority=(u * TOP_K + k) % 2)
            return carry
        lax.fori_loop(0, tm // ROW_DMA_GROUP, group, 0)

    @pl.when(i == 0)
    def _():
        gather(i, slot)

    @pl.when(i + 1 < pl.num_programs(0))
    def _():
        gather(i + 1, 1 - slot)

    for k in range(TOP_K):
        pltpu.make_async_copy(y_hbm.at[pl.ds(0, tm)], buf.at[slot, :, slot_cols(k)], sem.at[slot]).wait()
    moe = buf[slot, :, 0:D_MODEL] * w_ref[:, 0:1]
    for k in range(1, TOP_K):
        moe = moe + buf[slot, :, k * D_MODEL:(k + 1) * D_MODEL] * w_ref[:, k:k + 1]
    x2 = x1_ref[...] + mod_ref[5:6, :] * moe
    o_ref[...] = _rms(x2) * fg_ref[...]


def _combine(dest_flat, y, x1, w_t, mod, final_g):
    tm = 256
    per_batch = SEQ // tm
    return pl.pallas_call(
        functools.partial(_combine_body, tm=tm),
        grid_spec=pltpu.PrefetchScalarGridSpec(
            num_scalar_prefetch=1,
            grid=(TOKENS // tm,),
            in_specs=[pl.BlockSpec(memory_space=pl.ANY),
                      pl.BlockSpec((tm, D_MODEL), lambda i, d: (i, 0)),
                      pl.BlockSpec((tm, TOP_K), lambda i, d: (i, 0)),
                      pl.BlockSpec((None, 6, D_MODEL), lambda i, d: (i // per_batch, 0, 0)),
                      pl.BlockSpec((1, D_MODEL), lambda i, d: (0, 0))],
            out_specs=pl.BlockSpec((tm, D_MODEL), lambda i, d: (i, 0)),
            scratch_shapes=[pltpu.VMEM((2, tm, TOP_K * D_MODEL), F32),
                            pltpu.SemaphoreType.DMA((2,))]),
        out_shape=jax.ShapeDtypeStruct((TOKENS, D_MODEL), F32),
        compiler_params=_cparams("arbitrary"),
        name="combine",
    )(dest_flat, y, x1, w_t, mod, final_g)


def _routing_tables(idx, rank, counts):
    padded = (counts + ROW_TILE - 1) // ROW_TILE * ROW_TILE
    padded_end = jnp.cumsum(padded)
    padded_start = padded_end - padded
    e_i = jnp.arange(N_EXPERTS, dtype=I32)[:, None, None]
    start_of = jnp.sum(jnp.where(idx[None] == e_i, padded_start[:, None, None], 0), axis=0)
    dest = (start_of + rank).T.reshape(N_ROWS).astype(I32)

    tiles = padded // ROW_TILE
    n_super = (tiles + SUPER_TILES - 1) // SUPER_TILES
    super_end = jnp.cumsum(n_super)
    super_start = super_end - n_super
    s_i = jnp.arange(MAX_SUPER, dtype=I32)
    total = super_end[-1]
    valid = s_i < total
    e_of = jnp.minimum(jnp.sum((s_i[:, None] >= super_end[None, :]).astype(I32), axis=1), N_EXPERTS - 1)
    last_e = jnp.minimum(jnp.sum((total - 1 >= super_end).astype(I32)), N_EXPERTS - 1)
    local = s_i - super_start[e_of]
    srow = jnp.where(valid, padded_start[e_of] + local * SUPER_ROWS, 0).astype(I32)
    snsub = jnp.where(valid, jnp.minimum(tiles[e_of] - local * SUPER_TILES, SUPER_TILES), 0).astype(I32)
    snsub = jnp.concatenate([snsub, (padded_end[-1:] // ROW_TILE).astype(I32)])
    se = jnp.where(valid, e_of, last_e).astype(I32)
    return dest, padded_end.astype(I32), se, srow, snsub


def kernel(x, c, ada_w, ada_b, norm1_g, w_in, gla_gate_w2, gla_gate_b, sgu_ln_g, sgu_ln_b, sgu_w, sgu_b,
           gla_norm_g, w_branch_a, w_branch_b, w_out, norm2_g, router_w, router_b, exp_w_gate, exp_b_gate,
           exp_w_up, exp_b_up, exp_w_down, exp_b_down, final_g):
    x2d = x.reshape(TOKENS, D_MODEL)
    mod = _ada(c, ada_w[0], ada_b[0])

    w_in0 = w_in[0]
    w_in_t = w_in0.T
    w_glr = jnp.zeros((LANES, D_MODEL), BF16).at[:GLA_GATE_RANK].set(
        w_in_t[GLR_SRC:GLR_SRC + GLA_GATE_RANK].astype(BF16))
    h, glr = _norm1(x2d, mod, norm1_g, w_glr)
    proj = _inproj(h, w_in_t)

    y_a = _sgu(proj, sgu_ln_g, sgu_ln_b, sgu_w[0], sgu_b[0].T)

    w2p = jnp.zeros((LANES, GLA_DK), BF16).at[:GLA_GATE_RANK].set(gla_gate_w2[0].astype(BF16))
    y_b = _gla(proj, glr, w2p, gla_gate_b, gla_norm_g)

    merged = _merge(y_a, y_b, w_branch_a[0], w_branch_b[0], proj)

    rw_t = router_w[0].T
    rw_hi = rw_t.astype(BF16)
    rw_lo = (rw_t - rw_hi.astype(F32)).astype(BF16)
    x1, h_rows, idx, top_w, rank, cnt = _outproj(
        merged, x2d, mod, w_out[0].astype(BF16), norm2_g, rw_hi, rw_lo, router_b[0].reshape(N_EXPERTS, 1))

    counts = cnt[:, 0]
    dest, padded_end, se, srow, snsub = _routing_tables(idx, rank, counts)
    xs = _dispatch(dest, padded_end, counts, h_rows)
    y = _experts(se, srow, snsub, xs, exp_w_gate[0], exp_b_gate[0], exp_w_up[0], exp_b_up[0],
                 exp_w_down[0], exp_b_down[0])
    out = _combine(dest, y, x1, top_w.T, mod, final_g.reshape(1, D_MODEL))
    return out.reshape(BATCH, SEQ, D_MODEL)
```

```python
import functools

import jax
import jax.numpy as jnp
from jax import lax
from jax.experimental import pallas as pl
from jax.experimental.pallas import tpu as pltpu

F32 = jnp.float32
BF16 = jnp.bfloat16
I32 = jnp.int32

D_MODEL = 2048
BATCH = 4
SEQ = 2048
TOKENS = BATCH * SEQ
CHUNK = 64
SPATIAL_BLOCK = 128
A_GROUPS = 8
A_GROUP_DIM = D_MODEL // A_GROUPS
GLA_HEADS = 4
GLA_DK = D_MODEL // 2
GLA_HEAD_K = GLA_DK // GLA_HEADS
GLA_HEAD_V = D_MODEL // GLA_HEADS
GLA_GATE_RANK = 16
GLA_GATE_TAU = 16.0
N_EXPERTS = 32
TOP_K = 4
D_FF = D_MODEL
SWIGLU_LIMIT = 7.0
SWIGLU_ALPHA = 1.702
NORM_EPS = 1e-6

LANES = 128
VMEM_LIMIT = 56 * 1024 * 1024

COL_U, COL_V, COL_Q, COL_K, COL_VV, COL_R = 0, 2048, 4096, 5120, 6144, 8192
COL_GA, COL_GB = 10240, 12288
PROJ_W = 14336
GLR_SRC = 10240

ROW_TILE = 256
SUPER_TILES = 6
SUPER_ROWS = ROW_TILE * SUPER_TILES
N_ROWS = TOKENS * TOP_K
N_ROW_TILES = (N_ROWS + N_EXPERTS * (ROW_TILE - 1)) // ROW_TILE
PADDED_ROWS = N_ROW_TILES * ROW_TILE
MAX_SUPER = -(-(N_ROW_TILES + N_EXPERTS * (SUPER_TILES - 1)) // SUPER_TILES)
FF_TILE = 512
N_FF = D_FF // FF_TILE
ROW_DMA_GROUP = 8


def _cparams(*sem):
    return pltpu.CompilerParams(dimension_semantics=sem, vmem_limit_bytes=VMEM_LIMIT)


def _rms(x):
    return x * lax.rsqrt(jnp.mean(x * x, axis=-1, keepdims=True) + NORM_EPS)


def _ada_body(c_ref, w_ref, b_ref, o_ref):
    c = c_ref[...]
    cond = c * jax.nn.sigmoid(c)
    o_ref[...] = jnp.dot(cond.astype(BF16), w_ref[...].astype(BF16),
                         preferred_element_type=F32) + b_ref[...]


def _ada(c, ada_w, ada_b):
    tn = 1024
    cp = jnp.zeros((8, D_MODEL), F32).at[:BATCH].set(c)
    out = pl.pallas_call(
        _ada_body,
        grid=(6 * D_MODEL // tn,),
        in_specs=[pl.BlockSpec((8, D_MODEL), lambda j: (0, 0)),
                  pl.BlockSpec((D_MODEL, tn), lambda j: (0, j)),
                  pl.BlockSpec((1, tn), lambda j: (0, j))],
        out_specs=pl.BlockSpec((8, tn), lambda j: (0, j)),
        out_shape=jax.ShapeDtypeStruct((8, 6 * D_MODEL), F32),
        compiler_params=_cparams("arbitrary"),
        name="ada",
    )(cp, ada_w, ada_b.reshape(1, 6 * D_MODEL))
    return out[:BATCH].reshape(BATCH, 6, D_MODEL)


def _norm1_body(x_ref, mod_ref, g_ref, wglr_ref, h_ref, glr_ref):
    h = _rms(x_ref[...]) * g_ref[...] * (1.0 + mod_ref[1:2, :]) + mod_ref[0:1, :]
    hb = h.astype(BF16)
    h_ref[...] = hb
    glr_ref[...] = _dot_nt(hb, wglr_ref[...])


def _norm1(x2d, mod, norm1_g, w_glr):
    tm = 512
    per_batch = SEQ // tm
    return pl.pallas_call(
        _norm1_body,
        grid=(TOKENS // tm,),
        in_specs=[pl.BlockSpec((tm, D_MODEL), lambda i: (i, 0)),
                  pl.BlockSpec((None, 6, D_MODEL), lambda i: (i // per_batch, 0, 0)),
                  pl.BlockSpec((1, D_MODEL), lambda i: (0, 0)),
                  pl.BlockSpec((LANES, D_MODEL), lambda i: (0, 0))],
        out_specs=[pl.BlockSpec((tm, D_MODEL), lambda i: (i, 0)),
                   pl.BlockSpec((tm, LANES), lambda i: (i, 0))],
        out_shape=[jax.ShapeDtypeStruct((TOKENS, D_MODEL), BF16),
                   jax.ShapeDtypeStruct((TOKENS, LANES), F32)],
        compiler_params=_cparams("arbitrary"),
        name="norm1",
    )(x2d, mod, norm1_g, w_glr)


INPROJ_TN = 1024
INPROJ_MAIN_TILES = COL_GA // INPROJ_TN
MXU_N = 256


def _inproj_body(h_ref, wt_ref, proj_ref, wb_ref):
    j = pl.program_id(0)

    @pl.when(pl.program_id(1) == 0)
    def _():
        wb_ref[...] = wt_ref[...].astype(BF16)

    def run(act):
        for n in range(INPROJ_TN // MXU_N):
            cols = slice(n * MXU_N, (n + 1) * MXU_N)
            acc = _dot_nt(h_ref[...], wb_ref[cols, :])
            proj_ref[:, cols] = act(acc).astype(BF16)

    col = j * INPROJ_TN

    @pl.when(col < COL_Q)
    def _():
        run(jax.nn.gelu)

    @pl.when((col >= COL_Q) & (col < COL_R))
    def _():
        run(lambda a: a)

    @pl.when((col >= COL_R) & (col < COL_GA))
    def _():
        run(lambda a: a * jax.nn.sigmoid(a))

    @pl.when(col >= COL_GA)
    def _():
        run(jax.nn.sigmoid)


def _inproj(h, w_in_t):
    tm, tn = 1024, INPROJ_TN

    def w_row(j, i):
        return (pl.multiple_of(j * tn + jnp.where(j >= INPROJ_MAIN_TILES, GLA_GATE_RANK, 0), GLA_GATE_RANK), 0)

    return pl.pallas_call(
        _inproj_body,
        grid=(PROJ_W // tn, TOKENS // tm),
        in_specs=[pl.BlockSpec((tm, D_MODEL), lambda j, i: (i, 0)),
                  pl.BlockSpec((pl.Element(tn), pl.Element(D_MODEL)), w_row)],
        out_specs=pl.BlockSpec((tm, tn), lambda j, i: (i, j)),
        out_shape=jax.ShapeDtypeStruct((TOKENS, PROJ_W), BF16),
        scratch_shapes=[pltpu.VMEM((tn, D_MODEL), BF16)],
        compiler_params=_cparams("arbitrary", "arbitrary"),
        name="inproj",
    )(h, w_in_t)


SGU_BLOCKS = 8


def _sgu_body(u_ref, v_ref, lg_ref, lb_ref, ws_ref, bs_ref, o_ref):
    t_chunk = lax.broadcasted_iota(I32, (SPATIAL_BLOCK, SPATIAL_BLOCK), 0) // CHUNK
    s_chunk = lax.broadcasted_iota(I32, (SPATIAL_BLOCK, SPATIAL_BLOCK), 1) // CHUNK
    mask = s_chunk <= t_chunk
    w = [jnp.where(mask, ws_ref[g], 0.0).astype(BF16) for g in range(A_GROUPS)]
    for blk in range(SGU_BLOCKS):
        rows = slice(blk * SPATIAL_BLOCK, (blk + 1) * SPATIAL_BLOCK)
        v = v_ref[rows, :].astype(F32)
        mu = jnp.mean(v, axis=-1, keepdims=True)
        xc = v - mu
        var = jnp.mean(xc * xc, axis=-1, keepdims=True)
        vn = (xc * lax.rsqrt(var + NORM_EPS) * lg_ref[...] + lb_ref[...]).astype(BF16)
        for g in range(A_GROUPS):
            cols = slice(g * A_GROUP_DIM, (g + 1) * A_GROUP_DIM)
            mixed = jnp.dot(w[g], vn[:, cols], preferred_element_type=F32) + bs_ref[:, g:g + 1]
            o_ref[rows, cols] = (u_ref[rows, cols].astype(F32) * mixed).astype(BF16)


def _sgu(proj, ln_g, ln_b, w_s, b_s_t):
    rows_blk = SGU_BLOCKS * SPATIAL_BLOCK
    nblk = TOKENS // rows_blk
    wb = D_MODEL
    return pl.pallas_call(
        _sgu_body,
        grid=(nblk,),
        in_specs=[pl.BlockSpec((rows_blk, wb), lambda i: (i, COL_U // wb)),
                  pl.BlockSpec((rows_blk, wb), lambda i: (i, COL_V // wb)),
                  pl.BlockSpec((1, wb), lambda i: (0, 0)),
                  pl.BlockSpec((1, wb), lambda i: (0, 0)),
                  pl.BlockSpec((A_GROUPS, SPATIAL_BLOCK, SPATIAL_BLOCK), lambda i: (0, 0, 0)),
                  pl.BlockSpec((SPATIAL_BLOCK, A_GROUPS), lambda i: (0, 0))],
        out_specs=pl.BlockSpec((rows_blk, wb), lambda i: (i, 0)),
        out_shape=jax.ShapeDtypeStruct((TOKENS, wb), BF16),
        compiler_params=_cparams("arbitrary"),
        name="sgu",
    )(proj, proj, ln_g, ln_b, w_s, b_s_t)


def _dot_nt(a, b):
    return lax.dot_general(a, b, (((1,), (1,)), ((), ())), preferred_element_type=F32)


def _dot_tn(a, b):
    return lax.dot_general(a, b, (((0,), (0,)), ((), ())), preferred_element_type=F32)


def _gla_body(q_ref, k_ref, v_ref, r_ref, glr_ref, w2_ref, gb_ref, ng_ref, o_ref,
              st_ref, la_ref, qd_ref, ks_ref, dec_ref, oi_ref, *, rows_blk):
    @pl.when(pl.program_id(1) == 0)
    def _():
        st_ref[...] = jnp.zeros_like(st_ref)

    z = jnp.dot(glr_ref[...].astype(BF16), w2_ref[...], preferred_element_type=F32) + gb_ref[...]
    la_ref[...] = jax.nn.log_sigmoid(z) / GLA_GATE_TAU

    r_i = lax.broadcasted_iota(I32, (CHUNK, CHUNK), 0)
    c_i = lax.broadcasted_iota(I32, (CHUNK, CHUNK), 1)
    causal = c_i <= r_i
    tril = causal.astype(BF16)
    scale = GLA_HEAD_K ** -0.5

    def head_cols(h):
        return (slice(h * GLA_HEAD_K, (h + 1) * GLA_HEAD_K), slice(h * GLA_HEAD_V, (h + 1) * GLA_HEAD_V))

    def intra(c, carry):
        rows = pl.ds(pl.multiple_of(c * CHUNK, CHUNK), CHUNK)
        for h in range(GLA_HEADS):
            kc, vc = head_cols(h)
            la = la_ref[rows, kc]
            hi = la.astype(BF16)
            r1 = la - hi.astype(F32)
            mid = r1.astype(BF16)
            lo = (r1 - mid.astype(F32)).astype(BF16)
            b = (jnp.dot(tril, hi, preferred_element_type=F32)
                 + jnp.dot(tril, mid, preferred_element_type=F32)
                 + jnp.dot(tril, lo, preferred_element_type=F32))
            b_last = b[CHUNK - 1:CHUNK, :]
            q = q_ref[rows, kc].astype(F32) * scale
            k = k_ref[rows, kc].astype(F32)
            q_dec = (q * jnp.exp(b)).astype(BF16)
            k_intra = (k * jnp.exp(-b)).astype(BF16)
            qd_ref[rows, kc] = q_dec
            ks_ref[rows, kc] = (k * jnp.exp(b_last - b)).astype(BF16)
            dec_ref[c, :, kc] = jnp.exp(b_last)
            att = jnp.where(causal, _dot_nt(q_dec, k_intra), 0.0).astype(BF16)
            oi_ref[rows, vc] = jnp.dot(att, v_ref[rows, vc], preferred_element_type=F32)
        return carry

    def inter(c, carry):
        rows = pl.ds(pl.multiple_of(c * CHUNK, CHUNK), CHUNK)
        for h in range(GLA_HEADS):
            kc, vc = head_cols(h)
            st = st_ref[h]
            o = oi_ref[rows, vc] + _dot_nt(qd_ref[rows, kc], st.astype(BF16))
            st_ref[h] = st * dec_ref[c, :, kc] + _dot_tn(v_ref[rows, vc], ks_ref[rows, kc])
            on = _rms(o) * ng_ref[...]
            o_ref[rows, vc] = (on * r_ref[rows, vc].astype(F32)).astype(BF16)
        return carry

    n_chunks = rows_blk // CHUNK
    lax.fori_loop(0, n_chunks, intra, 0, unroll=True)
    lax.fori_loop(0, n_chunks, inter, 0, unroll=True)


def _gla(proj, glr, w2p, gate_b, norm_g):
    rows_blk = 512
    nblk = SEQ // rows_blk
    dk, dv = GLA_DK, D_MODEL

    def row(b, n):
        return b * nblk + n

    return pl.pallas_call(
        functools.partial(_gla_body, rows_blk=rows_blk),
        grid=(BATCH, nblk),
        in_specs=[pl.BlockSpec((rows_blk, dk), lambda b, n: (row(b, n), COL_Q // dk)),
                  pl.BlockSpec((rows_blk, dk), lambda b, n: (row(b, n), COL_K // dk)),
                  pl.BlockSpec((rows_blk, dv), lambda b, n: (row(b, n), COL_VV // dv)),
                  pl.BlockSpec((rows_blk, dv), lambda b, n: (row(b, n), COL_R // dv)),
                  pl.BlockSpec((rows_blk, LANES), lambda b, n: (row(b, n), 0)),
                  pl.BlockSpec((LANES, dk), lambda b, n: (0, 0)),
                  pl.BlockSpec((1, dk), lambda b, n: (0, 0)),
                  pl.BlockSpec((1, GLA_HEAD_V), lambda b, n: (0, 0))],
        out_specs=pl.BlockSpec((rows_blk, dv), lambda b, n: (row(b, n), 0)),
        out_shape=jax.ShapeDtypeStruct((TOKENS, D_MODEL), BF16),
        scratch_shapes=[pltpu.VMEM((GLA_HEADS, GLA_HEAD_V, GLA_HEAD_K), F32),
                        pltpu.VMEM((rows_blk, dk), F32),
                        pltpu.VMEM((rows_blk, dk), BF16),
                        pltpu.VMEM((rows_blk, dk), BF16),
                        pltpu.VMEM((rows_blk // CHUNK, 1, dk), F32),
                        pltpu.VMEM((rows_blk, dv), F32)],
        compiler_params=_cparams("arbitrary", "arbitrary"),
        name="gla",
    )(proj, proj, proj, proj, glr, w2p, gate_b, norm_g)


def _merge_body(ya_ref, yb_ref, wa_ref, wb_ref, ga_ref, gb_ref, o_ref, wa_c, wb_c):
    @pl.when(pl.program_id(1) == 0)
    def _():
        wa_c[...] = wa_ref[...].astype(BF16)
        wb_c[...] = wb_ref[...].astype(BF16)

    a = jnp.dot(ya_ref[...], wa_c[...], preferred_element_type=F32)
    b = jnp.dot(yb_ref[...], wb_c[...], preferred_element_type=F32)
    o_ref[...] = (ga_ref[...].astype(F32) * a + gb_ref[...].astype(F32) * b).astype(BF16)


def _merge(y_a, y_b, wa, wb, proj):
    tm, tn = 1024, 512
    return pl.pallas_call(
        _merge_body,
        grid=(D_MODEL // tn, TOKENS // tm),
        in_specs=[pl.BlockSpec((tm, D_MODEL), lambda j, i: (i, 0)),
                  pl.BlockSpec((tm, D_MODEL), lambda j, i: (i, 0)),
                  pl.BlockSpec((D_MODEL, tn), lambda j, i: (0, j)),
                  pl.BlockSpec((D_MODEL, tn), lambda j, i: (0, j)),
                  pl.BlockSpec((tm, tn), lambda j, i: (i, COL_GA // tn + j)),
                  pl.BlockSpec((tm, tn), lambda j, i: (i, COL_GB // tn + j))],
        out_specs=pl.BlockSpec((tm, tn), lambda j, i: (i, j)),
        out_shape=jax.ShapeDtypeStruct((TOKENS, D_MODEL), BF16),
        scratch_shapes=[pltpu.VMEM((D_MODEL, tn), BF16), pltpu.VMEM((D_MODEL, tn), BF16)],
        compiler_params=_cparams("arbitrary", "arbitrary"),
        name="merge",
    )(y_a, y_b, wa, wb, proj, proj)


def _outproj_body(m_ref, x_ref, mod_ref, wo_ref, g2_ref, rwh_ref, rwl_ref, rb_ref,
                  x1_ref, hp_ref, idx_ref, w_ref, rank_ref, cnt_ref, run_ref, before_ref, *, tm):
    @pl.when(pl.program_id(0) == 0)
    def _():
        run_ref[...] = jnp.zeros_like(run_ref)
        before_ref[...] = (lax.broadcasted_iota(I32, (tm, tm), 0)
                           < lax.broadcasted_iota(I32, (tm, tm), 1)).astype(BF16)

    y = jnp.dot(m_ref[...], wo_ref[...], preferred_element_type=F32)
    x1 = x_ref[...] + mod_ref[2:3, :] * y
    x1_ref[...] = x1
    h2 = _rms(x1) * g2_ref[...] * (1.0 + mod_ref[4:5, :]) + mod_ref[3:4, :]
    hp_ref[...] = h2
    hb = h2.astype(BF16)
    hf = hb.astype(F32)

    h_lo = (h2 - hf).astype(BF16)
    logits = (_dot_nt(rwh_ref[...], hb) + _dot_nt(rwh_ref[...], h_lo) + _dot_nt(rwl_ref[...], hb)
              + rb_ref[...])
    e_i = lax.broadcasted_iota(I32, (N_EXPERTS, tm), 0).astype(F32)
    vals, idxs = [], []
    l = logits
    for _ in range(TOP_K):
        m = jnp.max(l, axis=0, keepdims=True)
        i = jnp.min(jnp.where(l == m, e_i, float(N_EXPERTS)), axis=0, keepdims=True)
        vals.append(m)
        idxs.append(i)
        l = jnp.where(e_i == i, -jnp.inf, l)
    ex = [jnp.exp(v - vals[0]) for v in vals]
    den = ex[0] + ex[1] + ex[2] + ex[3]
    onehots = [e_i == idxs[k] for k in range(TOP_K)]
    stacked = jnp.concatenate([o.astype(BF16) for o in onehots], axis=0)
    earlier = jnp.dot(stacked, before_ref[...], preferred_element_type=F32)
    run = run_ref[...]
    for k in range(TOP_K):
        pref = earlier[k * N_EXPERTS:(k + 1) * N_EXPERTS, :] + run[:, 0:1]
        rank_ref[k:k + 1, :] = jnp.sum(jnp.where(onehots[k], pref, 0.0), axis=0, keepdims=True).astype(I32)
        run = run + jnp.sum(onehots[k].astype(F32), axis=1, keepdims=True)
        idx_ref[k:k + 1, :] = idxs[k].astype(I32)
        w_ref[k:k + 1, :] = ex[k] / den
    run_ref[...] = run
    cnt_ref[...] = run.astype(I32)


def _outproj(merged, x2d, mod, w_out, norm2_g, rw_hi, rw_lo, router_b):
    tm = 512
    per_batch = SEQ // tm
    body = functools.partial(_outproj_body, tm=tm)
    return pl.pallas_call(
        body,
        grid=(TOKENS // tm,),
        in_specs=[pl.BlockSpec((tm, D_MODEL), lambda i: (i, 0)),
                  pl.BlockSpec((tm, D_MODEL), lambda i: (i, 0)),
                  pl.BlockSpec((None, 6, D_MODEL), lambda i: (i // per_batch, 0, 0)),
                  pl.BlockSpec((D_MODEL, D_MODEL), lambda i: (0, 0)),
                  pl.BlockSpec((1, D_MODEL), lambda i: (0, 0)),
                  pl.BlockSpec((N_EXPERTS, D_MODEL), lambda i: (0, 0)),
                  pl.BlockSpec((N_EXPERTS, D_MODEL), lambda i: (0, 0)),
                  pl.BlockSpec((N_EXPERTS, 1), lambda i: (0, 0))],
        out_specs=[pl.BlockSpec((tm, D_MODEL), lambda i: (i, 0)),
                   pl.BlockSpec((tm, D_MODEL), lambda i: (i, 0)),
                   pl.BlockSpec((TOP_K, tm), lambda i: (0, i)),
                   pl.BlockSpec((TOP_K, tm), lambda i: (0, i)),
                   pl.BlockSpec((TOP_K, tm), lambda i: (0, i)),
                   pl.BlockSpec((N_EXPERTS, LANES), lambda i: (0, 0))],
        out_shape=[jax.ShapeDtypeStruct((TOKENS, D_MODEL), F32),
                   jax.ShapeDtypeStruct((TOKENS, D_MODEL), F32),
                   jax.ShapeDtypeStruct((TOP_K, TOKENS), I32),
                   jax.ShapeDtypeStruct((TOP_K, TOKENS), F32),
                   jax.ShapeDtypeStruct((TOP_K, TOKENS), I32),
                   jax.ShapeDtypeStruct((N_EXPERTS, LANES), I32)],
        scratch_shapes=[pltpu.VMEM((N_EXPERTS, LANES), F32), pltpu.VMEM((tm, tm), BF16)],
        compiler_params=_cparams("arbitrary"),
        name="outproj",
    )(merged, x2d, mod, w_out, norm2_g, rw_hi, rw_lo, router_b)


def _dispatch_body(dest_ref, pend_ref, count_ref, h_ref, xs_hbm, zero_ref, zsem, sem, *, tm):
    i = pl.program_id(0)

    @pl.when(i == 0)
    def _():
        zero_ref[...] = jnp.zeros_like(zero_ref)
        _dispatch_pad(pend_ref, count_ref, xs_hbm, zero_ref, zsem, start=True)

    def group(g, carry):
        base = pl.multiple_of(g * ROW_DMA_GROUP, ROW_DMA_GROUP)
        for u in range(ROW_DMA_GROUP):
            src = h_ref.at[pl.ds(base + u, 1)]
            for k in range(TOP_K):
                d = dest_ref[(i * tm + base + u) * TOP_K + k]
                pltpu.make_async_copy(src, xs_hbm.at[pl.ds(d, 1)], sem).start(priority=(u * TOP_K + k) % 2)
        return carry
    lax.fori_loop(0, tm // ROW_DMA_GROUP, group, 0)
    for k in range(TOP_K):
        pltpu.make_async_copy(h_ref, xs_hbm.at[pl.ds(0, tm)], sem).wait()

    @pl.when(i == pl.num_programs(0) - 1)
    def _():
        _dispatch_pad(pend_ref, count_ref, xs_hbm, zero_ref, zsem, start=False)


def _dispatch_pad(pend_ref, count_ref, xs_hbm, zero_ref, zsem, start):
    sub = 8

    def go(cp):
        if start:
            cp.start()
        else:
            cp.wait()

    def rows_copy(r, n):
        return pltpu.make_async_copy(zero_ref.at[0:n], xs_hbm.at[pl.ds(r, n)], zsem)

    def expert(e, c):
        hi = pend_ref[e]
        cnt = count_ref[e]
        lo = hi - (cnt + ROW_TILE - 1) // ROW_TILE * ROW_TILE + cnt
        lo_sub = jnp.minimum((lo + sub - 1) // sub * sub, hi)

        def single(r, c2):
            go(rows_copy(r, 1))
            return c2
        lax.fori_loop(lo, lo_sub, single, 0)

        def block(b, c2):
            go(rows_copy(pl.multiple_of(b * sub, sub), sub))
            return c2
        lax.fori_loop(lo_sub // sub, hi // sub, block, 0)
        return c
    lax.fori_loop(0, N_EXPERTS, expert, 0)

    def tail(t, c):
        go(rows_copy(pl.multiple_of(t * ROW_TILE, ROW_TILE), ROW_TILE))
        return c
    lax.fori_loop(pend_ref[N_EXPERTS - 1] // ROW_TILE, N_ROW_TILES, tail, 0)


def _dispatch(dest_flat, padded_end, counts, h_rows):
    tm = 1024
    return pl.pallas_call(
        functools.partial(_dispatch_body, tm=tm),
        grid_spec=pltpu.PrefetchScalarGridSpec(
            num_scalar_prefetch=3,
            grid=(TOKENS // tm,),
            in_specs=[pl.BlockSpec((tm, D_MODEL), lambda i, d, pe, pd: (i, 0))],
            out_specs=pl.BlockSpec(memory_space=pl.ANY),
            scratch_shapes=[pltpu.VMEM((ROW_TILE, D_MODEL), F32),
                            pltpu.SemaphoreType.DMA(()),
                            pltpu.SemaphoreType.DMA(())]),
        out_shape=jax.ShapeDtypeStruct((PADDED_ROWS, D_MODEL), F32),
        compiler_params=_cparams("arbitrary"),
        name="dispatch",
    )(dest_flat, padded_end, counts, h_rows)


def _expert_body(se_ref, srow_ref, snsub_ref, xs_hbm, wg_hbm, wu_hbm, wd_hbm, bg_ref, bu_ref, bd_ref,
                 y_hbm, xstage, xb, acc, wg_f, wu_f, wd_f, wg_c, wu_c, wd_c, sem_in, sem_out, sem_w):
    s = pl.program_id(0)
    f = pl.program_id(1)
    nsub = snsub_ref[s]
    row0 = srow_ref[s]
    expert = se_ref[s]
    ff_chunk = pl.ds(pl.multiple_of(f * FF_TILE, FF_TILE), FF_TILE)

    def weight_copies(s_, f_):
        e = se_ref[s_]
        chunk = pl.ds(pl.multiple_of(f_ * FF_TILE, FF_TILE), FF_TILE)
        return (pltpu.make_async_copy(wg_hbm.at[e, :, chunk], wg_f, sem_w.at[0]),
                pltpu.make_async_copy(wu_hbm.at[e, :, chunk], wu_f, sem_w.at[1]),
                pltpu.make_async_copy(wd_hbm.at[e, chunk, :], wd_f, sem_w.at[2]))

    wrap = f == N_FF - 1
    s_next = jnp.minimum(jnp.where(wrap, s + 1, s), MAX_SUPER - 1)
    f_next = jnp.where(wrap, 0, f + 1)
    has_next = jnp.logical_not(wrap & (s == MAX_SUPER - 1)) & (snsub_ref[s_next] > 0)

    last_f = f == N_FF - 1
    n_pair = nsub // 2
    odd = nsub - 2 * n_pair
    pair_rows = 2 * ROW_TILE

    next_super = (s + 1 < MAX_SUPER) & (snsub_ref[jnp.minimum(s + 1, MAX_SUPER - 1)] > 0)

    def in_copy(row_base, t):
        src = xs_hbm.at[pl.ds(pl.multiple_of(row_base + t * ROW_TILE, ROW_TILE), ROW_TILE)]
        return pltpu.make_async_copy(src, xstage.at[t], sem_in.at[t])

    def start_rows(row_base, count):
        def body(t, c):
            in_copy(row_base, t).start()
            return c
        lax.fori_loop(0, count, body, 0)

    def out_copy(off, m):
        dst = y_hbm.at[pl.ds(pl.multiple_of(row0 + off, ROW_TILE), m)]
        return pltpu.make_async_copy(acc.at[pl.ds(off, m)], dst, sem_out)

    def drain_outputs(tiles):
        def body(p, c):
            out_copy(0, pair_rows).wait()
            return c
        lax.fori_loop(0, tiles // 2, body, 0)

        @pl.when(tiles % 2 == 1)
        def _():
            out_copy(0, ROW_TILE).wait()

    def gate(x):
        return jnp.minimum(jnp.dot(x, wg_c[...], preferred_element_type=F32) + bg_ref[pl.ds(expert, 1), ff_chunk],
                           SWIGLU_LIMIT)

    def up(x):
        return jnp.clip(jnp.dot(x, wu_c[...], preferred_element_type=F32) + bu_ref[pl.ds(expert, 1), ff_chunk],
                        -SWIGLU_LIMIT, SWIGLU_LIMIT)

    def down(rows, g, u):
        a = ((u + 1.0) * (g * jax.nn.sigmoid(SWIGLU_ALPHA * g))).astype(BF16)
        acc[rows, :] += jnp.dot(a, wd_c[...], preferred_element_type=F32)

    def finish(off, rows_n):
        @pl.when(last_f)
        def _():
            out_copy(off, rows_n).start()

    def compute(off, rows_n):
        rows = pl.ds(off, rows_n)
        x = xb[rows, :]
        down(rows, gate(x), up(x))
        finish(off, rows_n)

    def first_compute(rows_n, nxt):
        rows = pl.ds(0, rows_n)
        x = xb[rows, :]
        g = gate(x)
        wu_c[...] = wu_f[...].astype(BF16)

        @pl.when(has_next)
        def _():
            nxt[1].start()
        u = up(x)
        wd_c[...] = wd_f[...].astype(BF16)

        @pl.when(has_next)
        def _():
            nxt[2].start()
        down(rows, g, u)
        finish(0, rows_n)

    odd_off = pl.multiple_of((nsub - 1) * ROW_TILE, ROW_TILE)

    @pl.when(nsub > 0)
    def _():
        @pl.when((s == 0) & (f == 0))
        def _():
            for cp in weight_copies(s, f):
                cp.start()
            start_rows(row0, nsub)

        nxt = weight_copies(s_next, f_next)
        for cp in weight_copies(s, f):
            cp.wait()
        wg_c[...] = wg_f[...].astype(BF16)

        @pl.when(has_next)
        def _():
            nxt[0].start()

        @pl.when(f == 0)
        def _():
            @pl.when(s > 0)
            def _():
                drain_outputs(snsub_ref[jnp.maximum(s - 1, 0)])

            def load(t, c):
                in_copy(row0, t).wait()
                rows = pl.ds(pl.multiple_of(t * ROW_TILE, ROW_TILE), ROW_TILE)
                xb[rows, :] = xstage[t].astype(BF16)
                acc[rows, :] = jnp.broadcast_to(bd_ref[pl.ds(expert, 1), :], (ROW_TILE, D_MODEL))
                return c
            lax.fori_loop(0, nsub, load, 0)

        @pl.when(last_f & next_super)
        def _():
            s1 = jnp.minimum(s + 1, MAX_SUPER - 1)
            start_rows(srow_ref[s1], snsub_ref[s1])

        @pl.when(n_pair > 0)
        def _():
            first_compute(pair_rows, nxt)

        @pl.when(n_pair == 0)
        def _():
            first_compute(ROW_TILE, nxt)

        def pair(p, c):
            compute(pl.multiple_of(p * pair_rows, pair_rows), pair_rows)
            return c
        lax.fori_loop(1, n_pair, pair, 0)

        @pl.when((odd == 1) & (n_pair > 0))
        def _():
            compute(odd_off, ROW_TILE)

        @pl.when(last_f & jnp.logical_not(next_super))
        def _():
            drain_outputs(nsub)

    @pl.when((s == MAX_SUPER - 1) & last_f)
    def _():
        acc[0:ROW_TILE, :] = jnp.zeros((ROW_TILE, D_MODEL), F32)

        def tail_copy(t):
            dst = y_hbm.at[pl.ds(pl.multiple_of(t * ROW_TILE, ROW_TILE), ROW_TILE)]
            return pltpu.make_async_copy(acc.at[0:ROW_TILE], dst, sem_out)

        def tstart(t, c):
            tail_copy(t).start()
            return c

        def twait(t, c):
            tail_copy(t).wait()
            return c

        first_tail = snsub_ref[MAX_SUPER]
        lax.fori_loop(first_tail, N_ROW_TILES, tstart, 0)
        lax.fori_loop(first_tail, N_ROW_TILES, twait, 0)


def _experts(se, srow, snsub, xs, w_gate, b_gate, w_up, b_up, w_down, b_down):
    def whole(shape):
        return pl.BlockSpec(shape, lambda s, f, se, sr, sn: (0, 0))

    return pl.pallas_call(
        _expert_body,
        grid_spec=pltpu.PrefetchScalarGridSpec(
            num_scalar_prefetch=3,
            grid=(MAX_SUPER, N_FF),
            in_specs=[pl.BlockSpec(memory_space=pl.ANY),
                      pl.BlockSpec(memory_space=pl.ANY),
                      pl.BlockSpec(memory_space=pl.ANY),
                      pl.BlockSpec(memory_space=pl.ANY),
                      whole((N_EXPERTS, D_FF)),
                      whole((N_EXPERTS, D_FF)),
                      whole((N_EXPERTS, D_MODEL))],
            out_specs=pl.BlockSpec(memory_space=pl.ANY),
            scratch_shapes=[pltpu.VMEM((SUPER_TILES, ROW_TILE, D_MODEL), F32),
                            pltpu.VMEM((SUPER_ROWS, D_MODEL), BF16),
                            pltpu.VMEM((SUPER_ROWS, D_MODEL), F32),
                            pltpu.VMEM((D_MODEL, FF_TILE), F32),
                            pltpu.VMEM((D_MODEL, FF_TILE), F32),
                            pltpu.VMEM((FF_TILE, D_MODEL), F32),
                            pltpu.VMEM((D_MODEL, FF_TILE), BF16),
                            pltpu.VMEM((D_MODEL, FF_TILE), BF16),
                            pltpu.VMEM((FF_TILE, D_MODEL), BF16),
                            pltpu.SemaphoreType.DMA((SUPER_TILES,)),
                            pltpu.SemaphoreType.DMA(()),
                            pltpu.SemaphoreType.DMA((3,))]),
        out_shape=jax.ShapeDtypeStruct((PADDED_ROWS, D_MODEL), F32),
        compiler_params=_cparams("arbitrary", "arbitrary"),
        name="experts",
    )(se, srow, snsub, xs, w_gate, w_up, w_down,
      b_gate, b_up, b_down)


def _combine_body(dest_ref, y_hbm, x1_ref, w_ref, mod_ref, fg_ref, o_ref, buf, sem, *, tm):
    i = pl.program_id(0)
    slot = i % 2

    def slot_cols(k):
        return pl.ds(k * D_MODEL, D_MODEL)

    def gather(tile, dst_slot):
        def group(g, carry):
            base = pl.multiple_of(g * ROW_DMA_GROUP, ROW_DMA_GROUP)
            for u in range(ROW_DMA_GROUP):
                for k in range(TOP_K):
                    d = dest_ref[(tile * tm + base + u) * TOP_K + k]
                    pltpu.make_async_copy(y_hbm.at[pl.ds(d, 1)],
                                          buf.at[dst_slot, pl.ds(base + u, 1), slot_cols(k)],
                                          sem.at[dst_slot]).start(priority=(u * TOP_K + k) % 2)
            return carry
        lax.fori_loop(0, tm // ROW_DMA_GROUP, group, 0)

    @pl.when(i == 0)
    def _():
        gather(i, slot)

    @pl.when(i + 1 < pl.num_programs(0))
    def _():
        gather(i + 1, 1 - slot)

    for k in range(TOP_K):
        pltpu.make_async_copy(y_hbm.at[pl.ds(0, tm)], buf.at[slot, :, slot_cols(k)], sem.at[slot]).wait()
    moe = buf[slot, :, 0:D_MODEL] * w_ref[:, 0:1]
    for k in range(1, TOP_K):
        moe = moe + buf[slot, :, k * D_MODEL:(k + 1) * D_MODEL] * w_ref[:, k:k + 1]
    x2 = x1_ref[...] + mod_ref[5:6, :] * moe
    o_ref[...] = _rms(x2) * fg_ref[...]


def _combine(dest_flat, y, x1, w_t, mod, final_g):
    tm = 256
    per_batch = SEQ // tm
    return pl.pallas_call(
        functools.partial(_combine_body, tm=tm),
        grid_spec=pltpu.PrefetchScalarGridSpec(
            num_scalar_prefetch=1,
            grid=(TOKENS // tm,),
            in_specs=[pl.BlockSpec(memory_space=pl.ANY),
                      pl.BlockSpec((tm, D_MODEL), lambda i, d: (i, 0)),
                      pl.BlockSpec((tm, TOP_K), lambda i, d: (i, 0)),
                      pl.BlockSpec((None, 6, D_MODEL), lambda i, d: (i // per_batch, 0, 0)),
                      pl.BlockSpec((1, D_MODEL), lambda i, d: (0, 0))],
            out_specs=pl.BlockSpec((tm, D_MODEL), lambda i, d: (i, 0)),
            scratch_shapes=[pltpu.VMEM((2, tm, TOP_K * D_MODEL), F32),
                            pltpu.SemaphoreType.DMA((2,))]),
        out_shape=jax.ShapeDtypeStruct((TOKENS, D_MODEL), F32),
        compiler_params=_cparams("arbitrary"),
        name="combine",
    )(dest_flat, y, x1, w_t, mod, final_g)


def _routing_tables(idx, rank, counts):
    padded = (counts + ROW_TILE - 1) // ROW_TILE * ROW_TILE
    padded_end = jnp.cumsum(padded)
    padded_start = padded_end - padded
    e_i = jnp.arange(N_EXPERTS, dtype=I32)[:, None, None]
    start_of = jnp.sum(jnp.where(idx[None] == e_i, padded_start[:, None, None], 0), axis=0)
    dest = (start_of + rank).T.reshape(N_ROWS).astype(I32)

    tiles = padded // ROW_TILE
    n_super = (tiles + SUPER_TILES - 1) // SUPER_TILES
    super_end = jnp.cumsum(n_super)
    super_start = super_end - n_super
    s_i = jnp.arange(MAX_SUPER, dtype=I32)
    total = super_end[-1]
    valid = s_i < total
    e_of = jnp.minimum(jnp.sum((s_i[:, None] >= super_end[None, :]).astype(I32), axis=1), N_EXPERTS - 1)
    last_e = jnp.minimum(jnp.sum((total - 1 >= super_end).astype(I32)), N_EXPERTS - 1)
    local = s_i - super_start[e_of]
    srow = jnp.where(valid, padded_start[e_of] + local * SUPER_ROWS, 0).astype(I32)
    snsub = jnp.where(valid, jnp.minimum(tiles[e_of] - local * SUPER_TILES, SUPER_TILES), 0).astype(I32)
    snsub = jnp.concatenate([snsub, (padded_end[-1:] // ROW_TILE).astype(I32)])
    se = jnp.where(valid, e_of, last_e).astype(I32)
    return dest, padded_end.astype(I32), se, srow, snsub


def kernel(x, c, ada_w, ada_b, norm1_g, w_in, gla_gate_w2, gla_gate_b, sgu_ln_g, sgu_ln_b, sgu_w, sgu_b,
           gla_norm_g, w_branch_a, w_branch_b, w_out, norm2_g, router_w, router_b, exp_w_gate, exp_b_gate,
           exp_w_up, exp_b_up, exp_w_down, exp_b_down, final_g):
    x2d = x.reshape(TOKENS, D_MODEL)
    mod = _ada(c, ada_w[0], ada_b[0])

    w_in0 = w_in[0]
    w_in_t = w_in0.T
    w_glr = jnp.zeros((LANES, D_MODEL), BF16).at[:GLA_GATE_RANK].set(
        w_in_t[GLR_SRC:GLR_SRC + GLA_GATE_RANK].astype(BF16))
    h, glr = _norm1(x2d, mod, norm1_g, w_glr)
    proj = _inproj(h, w_in_t)

    y_a = _sgu(proj, sgu_ln_g, sgu_ln_b, sgu_w[0], sgu_b[0].T)

    w2p = jnp.zeros((LANES, GLA_DK), BF16).at[:GLA_GATE_RANK].set(gla_gate_w2[0].astype(BF16))
    y_b = _gla(proj, glr, w2p, gla_gate_b, gla_norm_g)

    merged = _merge(y_a, y_b, w_branch_a[0], w_branch_b[0], proj)

    rw_t = router_w[0].T
    rw_hi = rw_t.astype(BF16)
    rw_lo = (rw_t - rw_hi.astype(F32)).astype(BF16)
    x1, h_rows, idx, top_w, rank, cnt = _outproj(
        merged, x2d, mod, w_out[0].astype(BF16), norm2_g, rw_hi, rw_lo, router_b[0].reshape(N_EXPERTS, 1))

    counts = cnt[:, 0]
    dest, padded_end, se, srow, snsub = _routing_tables(idx, rank, counts)
    xs = _dispatch(dest, padded_end, counts, h_rows)
    y = _experts(se, srow, snsub, xs, exp_w_gate[0], exp_b_gate[0], exp_w_up[0], exp_b_up[0],
                 exp_w_down[0], exp_b_down[0])
    out = _combine(dest, y, x1, top_w.T, mod, final_g.reshape(1, D_MODEL))
    return out.reshape(BATCH, SEQ, D_MODEL)
```

```python
import functools

import jax
import jax.numpy as jnp
from jax import lax
from jax.experimental import pallas as pl
from jax.experimental.pallas import tpu as pltpu

F32 = jnp.float32
BF16 = jnp.bfloat16
I32 = jnp.int32

D_MODEL = 2048
BATCH = 4
SEQ = 2048
TOKENS = BATCH * SEQ
CHUNK = 64
SPATIAL_BLOCK = 128
A_GROUPS = 8
A_GROUP_DIM = D_MODEL // A_GROUPS
GLA_HEADS = 4
GLA_DK = D_MODEL // 2
GLA_HEAD_K = GLA_DK // GLA_HEADS
GLA_HEAD_V = D_MODEL // GLA_HEADS
GLA_GATE_RANK = 16
GLA_GATE_TAU = 16.0
N_EXPERTS = 32
TOP_K = 4
D_FF = D_MODEL
SWIGLU_LIMIT = 7.0
SWIGLU_ALPHA = 1.702
NORM_EPS = 1e-6

LANES = 128
VMEM_LIMIT = 56 * 1024 * 1024

COL_U, COL_V, COL_Q, COL_K, COL_VV, COL_R = 0, 2048, 4096, 5120, 6144, 8192
COL_GA, COL_GB = 10240, 12288
PROJ_W = 14336
GLR_SRC = 10240

ROW_TILE = 256
SUPER_TILES = 6
SUPER_ROWS = ROW_TILE * SUPER_TILES
N_ROWS = TOKENS * TOP_K
N_ROW_TILES = (N_ROWS + N_EXPERTS * (ROW_TILE - 1)) // ROW_TILE
PADDED_ROWS = N_ROW_TILES * ROW_TILE
MAX_SUPER = -(-(N_ROW_TILES + N_EXPERTS * (SUPER_TILES - 1)) // SUPER_TILES)
FF_TILE = 512
N_FF = D_FF // FF_TILE
ROW_DMA_GROUP = 8


def _cparams(*sem):
    return pltpu.CompilerParams(dimension_semantics=sem, vmem_limit_bytes=VMEM_LIMIT)


def _rms(x):
    return x * lax.rsqrt(jnp.mean(x * x, axis=-1, keepdims=True) + NORM_EPS)


def _ada_body(c_ref, w_ref, b_ref, o_ref):
    c = c_ref[...]
    cond = c * jax.nn.sigmoid(c)
    o_ref[...] = jnp.dot(cond.astype(BF16), w_ref[...].astype(BF16),
                         preferred_element_type=F32) + b_ref[...]


def _ada(c, ada_w, ada_b):
    tn = 1024
    cp = jnp.zeros((8, D_MODEL), F32).at[:BATCH].set(c)
    out = pl.pallas_call(
        _ada_body,
        grid=(6 * D_MODEL // tn,),
        in_specs=[pl.BlockSpec((8, D_MODEL), lambda j: (0, 0)),
                  pl.BlockSpec((D_MODEL, tn), lambda j: (0, j)),
                  pl.BlockSpec((1, tn), lambda j: (0, j))],
        out_specs=pl.BlockSpec((8, tn), lambda j: (0, j)),
        out_shape=jax.ShapeDtypeStruct((8, 6 * D_MODEL), F32),
        compiler_params=_cparams("arbitrary"),
        name="ada",
    )(cp, ada_w, ada_b.reshape(1, 6 * D_MODEL))
    return out[:BATCH].reshape(BATCH, 6, D_MODEL)


def _norm1_body(x_ref, mod_ref, g_ref, wglr_ref, h_ref, glr_ref):
    h = _rms(x_ref[...]) * g_ref[...] * (1.0 + mod_ref[1:2, :]) + mod_ref[0:1, :]
    hb = h.astype(BF16)
    h_ref[...] = hb
    glr_ref[...] = _dot_nt(hb, wglr_ref[...])


def _norm1(x2d, mod, norm1_g, w_glr):
    tm = 512
    per_batch = SEQ // tm
    return pl.pallas_call(
        _norm1_body,
        grid=(TOKENS // tm,),
        in_specs=[pl.BlockSpec((tm, D_MODEL), lambda i: (i, 0)),
                  pl.BlockSpec((None, 6, D_MODEL), lambda i: (i // per_batch, 0, 0)),
                  pl.BlockSpec((1, D_MODEL), lambda i: (0, 0)),
                  pl.BlockSpec((LANES, D_MODEL), lambda i: (0, 0))],
        out_specs=[pl.BlockSpec((tm, D_MODEL), lambda i: (i, 0)),
                   pl.BlockSpec((tm, LANES), lambda i: (i, 0))],
        out_shape=[jax.ShapeDtypeStruct((TOKENS, D_MODEL), BF16),
                   jax.ShapeDtypeStruct((TOKENS, LANES), F32)],
        compiler_params=_cparams("arbitrary"),
        name="norm1",
    )(x2d, mod, norm1_g, w_glr)


INPROJ_TN = 1024
INPROJ_MAIN_TILES = COL_GA // INPROJ_TN
MXU_N = 256


def _inproj_body(h_ref, wt_ref, proj_ref, wb_ref):
    j = pl.program_id(0)

    @pl.when(pl.program_id(1) == 0)
    def _():
        wb_ref[...] = wt_ref[...].astype(BF16)

    def run(act):
        for n in range(INPROJ_TN // MXU_N):
            cols = slice(n * MXU_N, (n + 1) * MXU_N)
            acc = _dot_nt(h_ref[...], wb_ref[cols, :])
            proj_ref[:, cols] = act(acc).astype(BF16)

    col = j * INPROJ_TN

    @pl.when(col < COL_Q)
    def _():
        run(jax.nn.gelu)

    @pl.when((col >= COL_Q) & (col < COL_R))
    def _():
        run(lambda a: a)

    @pl.when((col >= COL_R) & (col < COL_GA))
    def _():
        run(lambda a: a * jax.nn.sigmoid(a))

    @pl.when(col >= COL_GA)
    def _():
        run(jax.nn.sigmoid)


def _inproj(h, w_in_t):
    tm, tn = 1024, INPROJ_TN

    def w_row(j, i):
        return (pl.multiple_of(j * tn + jnp.where(j >= INPROJ_MAIN_TILES, GLA_GATE_RANK, 0), GLA_GATE_RANK), 0)

    return pl.pallas_call(
        _inproj_body,
        grid=(PROJ_W // tn, TOKENS // tm),
        in_specs=[pl.BlockSpec((tm, D_MODEL), lambda j, i: (i, 0)),
                  pl.BlockSpec((pl.Element(tn), pl.Element(D_MODEL)), w_row)],
        out_specs=pl.BlockSpec((tm, tn), lambda j, i: (i, j)),
        out_shape=jax.ShapeDtypeStruct((TOKENS, PROJ_W), BF16),
        scratch_shapes=[pltpu.VMEM((tn, D_MODEL), BF16)],
        compiler_params=_cparams("arbitrary", "arbitrary"),
        name="inproj",
    )(h, w_in_t)


SGU_BLOCKS = 8


def _sgu_body(u_ref, v_ref, lg_ref, lb_ref, ws_ref, bs_ref, o_ref):
    t_chunk = lax.broadcasted_iota(I32, (SPATIAL_BLOCK, SPATIAL_BLOCK), 0) // CHUNK
    s_chunk = lax.broadcasted_iota(I32, (SPATIAL_BLOCK, SPATIAL_BLOCK), 1) // CHUNK
    mask = s_chunk <= t_chunk
    w = [jnp.where(mask, ws_ref[g], 0.0).astype(BF16) for g in range(A_GROUPS)]
    for blk in range(SGU_BLOCKS):
        rows = slice(blk * SPATIAL_BLOCK, (blk + 1) * SPATIAL_BLOCK)
        v = v_ref[rows, :].astype(F32)
        mu = jnp.mean(v, axis=-1, keepdims=True)
        xc = v - mu
        var = jnp.mean(xc * xc, axis=-1, keepdims=True)
        vn = (xc * lax.rsqrt(var + NORM_EPS) * lg_ref[...] + lb_ref[...]).astype(BF16)
        for g in range(A_GROUPS):
            cols = slice(g * A_GROUP_DIM, (g + 1) * A_GROUP_DIM)
            mixed = jnp.dot(w[g], vn[:, cols], preferred_element_type=F32) + bs_ref[:, g:g + 1]
            o_ref[rows, cols] = (u_ref[rows, cols].astype(F32) * mixed).astype(BF16)


def _sgu(proj, ln_g, ln_b, w_s, b_s_t):
    rows_blk = SGU_BLOCKS * SPATIAL_BLOCK
    nblk = TOKENS // rows_blk
    wb = D_MODEL
    return pl.pallas_call(
        _sgu_body,
        grid=(nblk,),
        in_specs=[pl.BlockSpec((rows_blk, wb), lambda i: (i, COL_U // wb)),
                  pl.BlockSpec((rows_blk, wb), lambda i: (i, COL_V // wb)),
                  pl.BlockSpec((1, wb), lambda i: (0, 0)),
                  pl.BlockSpec((1, wb), lambda i: (0, 0)),
                  pl.BlockSpec((A_GROUPS, SPATIAL_BLOCK, SPATIAL_BLOCK), lambda i: (0, 0, 0)),
                  pl.BlockSpec((SPATIAL_BLOCK, A_GROUPS), lambda i: (0, 0))],
        out_specs=pl.BlockSpec((rows_blk, wb), lambda i: (i, 0)),
        out_shape=jax.ShapeDtypeStruct((TOKENS, wb), BF16),
        compiler_params=_cparams("arbitrary"),
        name="sgu",
    )(proj, proj, ln_g, ln_b, w_s, b_s_t)


def _dot_nt(a, b):
    return lax.dot_general(a, b, (((1,), (1,)), ((), ())), preferred_element_type=F32)


def _dot_tn(a, b):
    return lax.dot_general(a, b, (((0,), (0,)), ((), ())), preferred_element_type=F32)


def _gla_body(q_ref, k_ref, v_ref, r_ref, glr_ref, w2_ref, gb_ref, ng_ref, o_ref,
              st_ref, la_ref, qd_ref, ks_ref, dec_ref, oi_ref, *, rows_blk):
    @pl.when(pl.program_id(1) == 0)
    def _():
        st_ref[...] = jnp.zeros_like(st_ref)

    z = jnp.dot(glr_ref[...].astype(BF16), w2_ref[...], preferred_element_type=F32) + gb_ref[...]
    la_ref[...] = jax.nn.log_sigmoid(z) / GLA_GATE_TAU

    r_i = lax.broadcasted_iota(I32, (CHUNK, CHUNK), 0)
    c_i = lax.broadcasted_iota(I32, (CHUNK, CHUNK), 1)
    causal = c_i <= r_i
    tril = causal.astype(BF16)
    scale = GLA_HEAD_K ** -0.5

    def head_cols(h):
        return (slice(h * GLA_HEAD_K, (h + 1) * GLA_HEAD_K), slice(h * GLA_HEAD_V, (h + 1) * GLA_HEAD_V))

    def intra(c, carry):
        rows = pl.ds(pl.multiple_of(c * CHUNK, CHUNK), CHUNK)
        for h in range(GLA_HEADS):
            kc, vc = head_cols(h)
            la = la_ref[rows, kc]
            hi = la.astype(BF16)
            r1 = la - hi.astype(F32)
            mid = r1.astype(BF16)
            lo = (r1 - mid.astype(F32)).astype(BF16)
            b = (jnp.dot(tril, hi, preferred_element_type=F32)
                 + jnp.dot(tril, mid, preferred_element_type=F32)
                 + jnp.dot(tril, lo, preferred_element_type=F32))
            b_last = b[CHUNK - 1:CHUNK, :]
            q = q_ref[rows, kc].astype(F32) * scale
            k = k_ref[rows, kc].astype(F32)
            q_dec = (q * jnp.exp(b)).astype(BF16)
            k_intra = (k * jnp.exp(-b)).astype(BF16)
            qd_ref[rows, kc] = q_dec
            ks_ref[rows, kc] = (k * jnp.exp(b_last - b)).astype(BF16)
            dec_ref[c, :, kc] = jnp.exp(b_last)
            att = jnp.where(causal, _dot_nt(q_dec, k_intra), 0.0).astype(BF16)
            oi_ref[rows, vc] = jnp.dot(att, v_ref[rows, vc], preferred_element_type=F32)
        return carry

    def inter(c, carry):
        rows = pl.ds(pl.multiple_of(c * CHUNK, CHUNK), CHUNK)
        for h in range(GLA_HEADS):
            kc, vc = head_cols(h)
            st = st_ref[h]
            o = oi_ref[rows, vc] + _dot_nt(qd_ref[rows, kc], st.astype(BF16))
            st_ref[h] = st * dec_ref[c, :, kc] + _dot_tn(v_ref[rows, vc], ks_ref[rows, kc])
            on = _rms(o) * ng_ref[...]
            o_ref[rows, vc] = (on * r_ref[rows, vc].astype(F32)).astype(BF16)
        return carry

    n_chunks = rows_blk // CHUNK
    lax.fori_loop(0, n_chunks, intra, 0, unroll=True)
    lax.fori_loop(0, n_chunks, inter, 0, unroll=True)


def _gla(proj, glr, w2p, gate_b, norm_g):
    rows_blk = 512
    nblk = SEQ // rows_blk
    dk, dv = GLA_DK, D_MODEL

    def row(b, n):
        return b * nblk + n

    return pl.pallas_call(
        functools.partial(_gla_body, rows_blk=rows_blk),
        grid=(BATCH, nblk),
        in_specs=[pl.BlockSpec((rows_blk, dk), lambda b, n: (row(b, n), COL_Q // dk)),
                  pl.BlockSpec((rows_blk, dk), lambda b, n: (row(b, n), COL_K // dk)),
                  pl.BlockSpec((rows_blk, dv), lambda b, n: (row(b, n), COL_VV // dv)),
                  pl.BlockSpec((rows_blk, dv), lambda b, n: (row(b, n), COL_R // dv)),
                  pl.BlockSpec((rows_blk, LANES), lambda b, n: (row(b, n), 0)),
                  pl.BlockSpec((LANES, dk), lambda b, n: (0, 0)),
                  pl.BlockSpec((1, dk), lambda b, n: (0, 0)),
                  pl.BlockSpec((1, GLA_HEAD_V), lambda b, n: (0, 0))],
        out_specs=pl.BlockSpec((rows_blk, dv), lambda b, n: (row(b, n), 0)),
        out_shape=jax.ShapeDtypeStruct((TOKENS, D_MODEL), BF16),
        scratch_shapes=[pltpu.VMEM((GLA_HEADS, GLA_HEAD_V, GLA_HEAD_K), F32),
                        pltpu.VMEM((rows_blk, dk), F32),
                        pltpu.VMEM((rows_blk, dk), BF16),
                        pltpu.VMEM((rows_blk, dk), BF16),
                        pltpu.VMEM((rows_blk // CHUNK, 1, dk), F32),
                        pltpu.VMEM((rows_blk, dv), F32)],
        compiler_params=_cparams("arbitrary", "arbitrary"),
        name="gla",
    )(proj, proj, proj, proj, glr, w2p, gate_b, norm_g)


def _merge_body(ya_ref, yb_ref, wa_ref, wb_ref, ga_ref, gb_ref, o_ref, wa_c, wb_c):
    @pl.when(pl.program_id(1) == 0)
    def _():
        wa_c[...] = wa_ref[...].astype(BF16)
        wb_c[...] = wb_ref[...].astype(BF16)

    a = jnp.dot(ya_ref[...], wa_c[...], preferred_element_type=F32)
    b = jnp.dot(yb_ref[...], wb_c[...], preferred_element_type=F32)
    o_ref[...] = (ga_ref[...].astype(F32) * a + gb_ref[...].astype(F32) * b).astype(BF16)


def _merge(y_a, y_b, wa, wb, proj):
    tm, tn = 1024, 512
    return pl.pallas_call(
        _merge_body,
        grid=(D_MODEL // tn, TOKENS // tm),
        in_specs=[pl.BlockSpec((tm, D_MODEL), lambda j, i: (i, 0)),
                  pl.BlockSpec((tm, D_MODEL), lambda j, i: (i, 0)),
                  pl.BlockSpec((D_MODEL, tn), lambda j, i: (0, j)),
                  pl.BlockSpec((D_MODEL, tn), lambda j, i: (0, j)),
                  pl.BlockSpec((tm, tn), lambda j, i: (i, COL_GA // tn + j)),
                  pl.BlockSpec((tm, tn), lambda j, i: (i, COL_GB // tn + j))],
        out_specs=pl.BlockSpec((tm, tn), lambda j, i: (i, j)),
        out_shape=jax.ShapeDtypeStruct((TOKENS, D_MODEL), BF16),
        scratch_shapes=[pltpu.VMEM((D_MODEL, tn), BF16), pltpu.VMEM((D_MODEL, tn), BF16)],
        compiler_params=_cparams("arbitrary", "arbitrary"),
        name="merge",
    )(y_a, y_b, wa, wb, proj, proj)


def _outproj_body(m_ref, x_ref, mod_ref, wo_ref, g2_ref, rwh_ref, rwl_ref, rb_ref,
                  x1_ref, hp_ref, idx_ref, w_ref, rank_ref, cnt_ref, run_ref, before_ref, *, tm):
    @pl.when(pl.program_id(0) == 0)
    def _():
        run_ref[...] = jnp.zeros_like(run_ref)
        before_ref[...] = (lax.broadcasted_iota(I32, (tm, tm), 0)
                           < lax.broadcasted_iota(I32, (tm, tm), 1)).astype(BF16)

    y = jnp.dot(m_ref[...], wo_ref[...], preferred_element_type=F32)
    x1 = x_ref[...] + mod_ref[2:3, :] * y
    x1_ref[...] = x1
    h2 = _rms(x1) * g2_ref[...] * (1.0 + mod_ref[4:5, :]) + mod_ref[3:4, :]
    hp_ref[...] = h2
    hb = h2.astype(BF16)
    hf = hb.astype(F32)

    h_lo = (h2 - hf).astype(BF16)
    logits = (_dot_nt(rwh_ref[...], hb) + _dot_nt(rwh_ref[...], h_lo) + _dot_nt(rwl_ref[...], hb)
              + rb_ref[...])
    e_i = lax.broadcasted_iota(I32, (N_EXPERTS, tm), 0).astype(F32)
    vals, idxs = [], []
    l = logits
    for _ in range(TOP_K):
        m = jnp.max(l, axis=0, keepdims=True)
        i = jnp.min(jnp.where(l == m, e_i, float(N_EXPERTS)), axis=0, keepdims=True)
        vals.append(m)
        idxs.append(i)
        l = jnp.where(e_i == i, -jnp.inf, l)
    ex = [jnp.exp(v - vals[0]) for v in vals]
    den = ex[0] + ex[1] + ex[2] + ex[3]
    onehots = [e_i == idxs[k] for k in range(TOP_K)]
    stacked = jnp.concatenate([o.astype(BF16) for o in onehots], axis=0)
    earlier = jnp.dot(stacked, before_ref[...], preferred_element_type=F32)
    run = run_ref[...]
    for k in range(TOP_K):
        pref = earlier[k * N_EXPERTS:(k + 1) * N_EXPERTS, :] + run[:, 0:1]
        rank_ref[k:k + 1, :] = jnp.sum(jnp.where(onehots[k], pref, 0.0), axis=0, keepdims=True).astype(I32)
        run = run + jnp.sum(onehots[k].astype(F32), axis=1, keepdims=True)
        idx_ref[k:k + 1, :] = idxs[k].astype(I32)
        w_ref[k:k + 1, :] = ex[k] / den
    run_ref[...] = run
    cnt_ref[...] = run.astype(I32)


def _outproj(merged, x2d, mod, w_out, norm2_g, rw_hi, rw_lo, router_b):
    tm = 512
    per_batch = SEQ // tm
    body = functools.partial(_outproj_body, tm=tm)
    return pl.pallas_call(
        body,
        grid=(TOKENS // tm,),
        in_specs=[pl.BlockSpec((tm, D_MODEL), lambda i: (i, 0)),
                  pl.BlockSpec((tm, D_MODEL), lambda i: (i, 0)),
                  pl.BlockSpec((None, 6, D_MODEL), lambda i: (i // per_batch, 0, 0)),
                  pl.BlockSpec((D_MODEL, D_MODEL), lambda i: (0, 0)),
                  pl.BlockSpec((1, D_MODEL), lambda i: (0, 0)),
                  pl.BlockSpec((N_EXPERTS, D_MODEL), lambda i: (0, 0)),
                  pl.BlockSpec((N_EXPERTS, D_MODEL), lambda i: (0, 0)),
                  pl.BlockSpec((N_EXPERTS, 1), lambda i: (0, 0))],
        out_specs=[pl.BlockSpec((tm, D_MODEL), lambda i: (i, 0)),
                   pl.BlockSpec((tm, D_MODEL), lambda i: (i, 0)),
                   pl.BlockSpec((TOP_K, tm), lambda i: (0, i)),
                   pl.BlockSpec((TOP_K, tm), lambda i: (0, i)),
                   pl.BlockSpec((TOP_K, tm), lambda i: (0, i)),
                   pl.BlockSpec((N_EXPERTS, LANES), lambda i: (0, 0))],
        out_shape=[jax.ShapeDtypeStruct((TOKENS, D_MODEL), F32),
                   jax.ShapeDtypeStruct((TOKENS, D_MODEL), F32),
                   jax.ShapeDtypeStruct((TOP_K, TOKENS), I32),
                   jax.ShapeDtypeStruct((TOP_K, TOKENS), F32),
                   jax.ShapeDtypeStruct((TOP_K, TOKENS), I32),
                   jax.ShapeDtypeStruct((N_EXPERTS, LANES), I32)],
        scratch_shapes=[pltpu.VMEM((N_EXPERTS, LANES), F32), pltpu.VMEM((tm, tm), BF16)],
        compiler_params=_cparams("arbitrary"),
        name="outproj",
    )(merged, x2d, mod, w_out, norm2_g, rw_hi, rw_lo, router_b)


def _dispatch_body(dest_ref, pend_ref, count_ref, h_ref, xs_hbm, zero_ref, zsem, sem, *, tm):
    i = pl.program_id(0)

    @pl.when(i == 0)
    def _():
        zero_ref[...] = jnp.zeros_like(zero_ref)
        _dispatch_pad(pend_ref, count_ref, xs_hbm, zero_ref, zsem, start=True)

    def group(g, carry):
        base = pl.multiple_of(g * ROW_DMA_GROUP, ROW_DMA_GROUP)
        for u in range(ROW_DMA_GROUP):
            src = h_ref.at[pl.ds(base + u, 1)]
            for k in range(TOP_K):
                d = dest_ref[(i * tm + base + u) * TOP_K + k]
                pltpu.make_async_copy(src, xs_hbm.at[pl.ds(d, 1)], sem).start(priority=(u * TOP_K + k) % 2)
        return carry
    lax.fori_loop(0, tm // ROW_DMA_GROUP, group, 0)
    for k in range(TOP_K):
        pltpu.make_async_copy(h_ref, xs_hbm.at[pl.ds(0, tm)], sem).wait()

    @pl.when(i == pl.num_programs(0) - 1)
    def _():
        _dispatch_pad(pend_ref, count_ref, xs_hbm, zero_ref, zsem, start=False)


def _dispatch_pad(pend_ref, count_ref, xs_hbm, zero_ref, zsem, start):
    sub = 8

    def go(cp):
        if start:
            cp.start()
        else:
            cp.wait()

    def rows_copy(r, n):
        return pltpu.make_async_copy(zero_ref.at[0:n], xs_hbm.at[pl.ds(r, n)], zsem)

    def expert(e, c):
        hi = pend_ref[e]
        cnt = count_ref[e]
        lo = hi - (cnt + ROW_TILE - 1) // ROW_TILE * ROW_TILE + cnt
        lo_sub = jnp.minimum((lo + sub - 1) // sub * sub, hi)

        def single(r, c2):
            go(rows_copy(r, 1))
            return c2
        lax.fori_loop(lo, lo_sub, single, 0)

        def block(b, c2):
            go(rows_copy(pl.multiple_of(b * sub, sub), sub))
            return c2
        lax.fori_loop(lo_sub // sub, hi // sub, block, 0)
        return c
    lax.fori_loop(0, N_EXPERTS, expert, 0)

    def tail(t, c):
        go(rows_copy(pl.multiple_of(t * ROW_TILE, ROW_TILE), ROW_TILE))
        return c
    lax.fori_loop(pend_ref[N_EXPERTS - 1] // ROW_TILE, N_ROW_TILES, tail, 0)


def _dispatch(dest_flat, padded_end, counts, h_rows):
    tm = 1024
    return pl.pallas_call(
        functools.partial(_dispatch_body, tm=tm),
        grid_spec=pltpu.PrefetchScalarGridSpec(
            num_scalar_prefetch=3,
            grid=(TOKENS // tm,),
            in_specs=[pl.BlockSpec((tm, D_MODEL), lambda i, d, pe, pd: (i, 0))],
            out_specs=pl.BlockSpec(memory_space=pl.ANY),
            scratch_shapes=[pltpu.VMEM((ROW_TILE, D_MODEL), F32),
                            pltpu.SemaphoreType.DMA(()),
                            pltpu.SemaphoreType.DMA(())]),
        out_shape=jax.ShapeDtypeStruct((PADDED_ROWS, D_MODEL), F32),
        compiler_params=_cparams("arbitrary"),
        name="dispatch",
    )(dest_flat, padded_end, counts, h_rows)


def _expert_body(se_ref, srow_ref, snsub_ref, xs_hbm, wg_hbm, wu_hbm, wd_hbm, bg_ref, bu_ref, bd_ref,
                 y_hbm, xstage, xb, acc, wg_f, wu_f, wd_f, wg_c, wu_c, wd_c, sem_in, sem_out, sem_w):
    s = pl.program_id(0)
    f = pl.program_id(1)
    nsub = snsub_ref[s]
    row0 = srow_ref[s]
    expert = se_ref[s]
    ff_chunk = pl.ds(pl.multiple_of(f * FF_TILE, FF_TILE), FF_TILE)

    def weight_copies(s_, f_):
        e = se_ref[s_]
        chunk = pl.ds(pl.multiple_of(f_ * FF_TILE, FF_TILE), FF_TILE)
        return (pltpu.make_async_copy(wg_hbm.at[e, :, chunk], wg_f, sem_w.at[0]),
                pltpu.make_async_copy(wu_hbm.at[e, :, chunk], wu_f, sem_w.at[1]),
                pltpu.make_async_copy(wd_hbm.at[e, chunk, :], wd_f, sem_w.at[2]))

    wrap = f == N_FF - 1
    s_next = jnp.minimum(jnp.where(wrap, s + 1, s), MAX_SUPER - 1)
    f_next = jnp.where(wrap, 0, f + 1)
    has_next = jnp.logical_not(wrap & (s == MAX_SUPER - 1)) & (snsub_ref[s_next] > 0)

    last_f = f == N_FF - 1
    n_pair = nsub // 2
    odd = nsub - 2 * n_pair
    pair_rows = 2 * ROW_TILE

    next_super = (s + 1 < MAX_SUPER) & (snsub_ref[jnp.minimum(s + 1, MAX_SUPER - 1)] > 0)

    def in_copy(row_base, t):
        src = xs_hbm.at[pl.ds(pl.multiple_of(row_base + t * ROW_TILE, ROW_TILE), ROW_TILE)]
        return pltpu.make_async_copy(src, xstage.at[t], sem_in.at[t])

    def start_rows(row_base, count):
        def body(t, c):
            in_copy(row_base, t).start()
            return c
        lax.fori_loop(0, count, body, 0)

    def out_copy(off, m):
        dst = y_hbm.at[pl.ds(pl.multiple_of(row0 + off, ROW_TILE), m)]
        return pltpu.make_async_copy(acc.at[pl.ds(off, m)], dst, sem_out)

    def drain_outputs(tiles):
        def body(p, c):
            out_copy(0, pair_rows).wait()
            return c
        lax.fori_loop(0, tiles // 2, body, 0)

        @pl.when(tiles % 2 == 1)
        def _():
            out_copy(0, ROW_TILE).wait()

    def gate(x):
        return jnp.minimum(jnp.dot(x, wg_c[...], preferred_element_type=F32) + bg_ref[pl.ds(expert, 1), ff_chunk],
                           SWIGLU_LIMIT)

    def up(x):
        return jnp.clip(jnp.dot(x, wu_c[...], preferred_element_type=F32) + bu_ref[pl.ds(expert, 1), ff_chunk],
                        -SWIGLU_LIMIT, SWIGLU_LIMIT)

    def down(rows, g, u):
        a = ((u + 1.0) * (g * jax.nn.sigmoid(SWIGLU_ALPHA * g))).astype(BF16)
        acc[rows, :] += jnp.dot(a, wd_c[...], preferred_element_type=F32)

    def finish(off, rows_n):
        @pl.when(last_f)
        def _():
            out_copy(off, rows_n).start()

    def compute(off, rows_n):
        rows = pl.ds(off, rows_n)
        x = xb[rows, :]
        down(rows, gate(x), up(x))
        finish(off, rows_n)

    def first_compute(rows_n, nxt):
        rows = pl.ds(0, rows_n)
        x = xb[rows, :]
        g = gate(x)
        wu_c[...] = wu_f[...].astype(BF16)

        @pl.when(has_next)
        def _():
            nxt[1].start()
        u = up(x)
        wd_c[...] = wd_f[...].astype(BF16)

        @pl.when(has_next)
        def _():
            nxt[2].start()
        down(rows, g, u)
        finish(0, rows_n)

    odd_off = pl.multiple_of((nsub - 1) * ROW_TILE, ROW_TILE)

    @pl.when(nsub > 0)
    def _():
        @pl.when((s == 0) & (f == 0))
        def _():
            for cp in weight_copies(s, f):
                cp.start()
            start_rows(row0, nsub)

        nxt = weight_copies(s_next, f_next)
        for cp in weight_copies(s, f):
            cp.wait()
        wg_c[...] = wg_f[...].astype(BF16)

        @pl.when(has_next)
        def _():
            nxt[0].start()

        @pl.when(f == 0)
        def _():
            @pl.when(s > 0)
            def _():
                drain_outputs(snsub_ref[jnp.maximum(s - 1, 0)])

            def load(t, c):
                in_copy(row0, t).wait()
                rows = pl.ds(pl.multiple_of(t * ROW_TILE, ROW_TILE), ROW_TILE)
                xb[rows, :] = xstage[t].astype(BF16)
                acc[rows, :] = jnp.broadcast_to(bd_ref[pl.ds(expert, 1), :], (ROW_TILE, D_MODEL))
                return c
            lax.fori_loop(0, nsub, load, 0)

        @pl.when(last_f & next_super)
        def _():
            s1 = jnp.minimum(s + 1, MAX_SUPER - 1)
            start_rows(srow_ref[s1], snsub_ref[s1])

        @pl.when(n_pair > 0)
        def _():
            first_compute(pair_rows, nxt)

        @pl.when(n_pair == 0)
        def _():
            first_compute(ROW_TILE, nxt)

        def pair(p, c):
            compute(pl.multiple_of(p * pair_rows, pair_rows), pair_rows)
            return c
        lax.fori_loop(1, n_pair, pair, 0)

        @pl.when((odd == 1) & (n_pair > 0))
        def _():
            compute(odd_off, ROW_TILE)

        @pl.when(last_f & jnp.logical_not(next_super))
        def _():
            drain_outputs(nsub)

    @pl.when((s == MAX_SUPER - 1) & last_f)
    def _():
        acc[0:ROW_TILE, :] = jnp.zeros((ROW_TILE, D_MODEL), F32)

        def tail_copy(t):
            dst = y_hbm.at[pl.ds(pl.multiple_of(t * ROW_TILE, ROW_TILE), ROW_TILE)]
            return pltpu.make_async_copy(acc.at[0:ROW_TILE], dst, sem_out)

        def tstart(t, c):
            tail_copy(t).start()
            return c

        def twait(t, c):
            tail_copy(t).wait()
            return c

        first_tail = snsub_ref[MAX_SUPER]
        lax.fori_loop(first_tail, N_ROW_TILES, tstart, 0)
        lax.fori_loop(first_tail, N_ROW_TILES, twait, 0)


def _experts(se, srow, snsub, xs, w_gate, b_gate, w_up, b_up, w_down, b_down):
    def whole(shape):
        return pl.BlockSpec(shape, lambda s, f, se, sr, sn: (0, 0))

    return pl.pallas_call(
        _expert_body,
        grid_spec=pltpu.PrefetchScalarGridSpec(
            num_scalar_prefetch=3,
            grid=(MAX_SUPER, N_FF),
            in_specs=[pl.BlockSpec(memory_space=pl.ANY),
                      pl.BlockSpec(memory_space=pl.ANY),
                      pl.BlockSpec(memory_space=pl.ANY),
                      pl.BlockSpec(memory_space=pl.ANY),
                      whole((N_EXPERTS, D_FF)),
                      whole((N_EXPERTS, D_FF)),
                      whole((N_EXPERTS, D_MODEL))],
            out_specs=pl.BlockSpec(memory_space=pl.ANY),
            scratch_shapes=[pltpu.VMEM((SUPER_TILES, ROW_TILE, D_MODEL), F32),
                            pltpu.VMEM((SUPER_ROWS, D_MODEL), BF16),
                            pltpu.VMEM((SUPER_ROWS, D_MODEL), F32),
                            pltpu.VMEM((D_MODEL, FF_TILE), F32),
                            pltpu.VMEM((D_MODEL, FF_TILE), F32),
                            pltpu.VMEM((FF_TILE, D_MODEL), F32),
                            pltpu.VMEM((D_MODEL, FF_TILE), BF16),
                            pltpu.VMEM((D_MODEL, FF_TILE), BF16),
                            pltpu.VMEM((FF_TILE, D_MODEL), BF16),
                            pltpu.SemaphoreType.DMA((SUPER_TILES,)),
                            pltpu.SemaphoreType.DMA(()),
                            pltpu.SemaphoreType.DMA((3,))]),
        out_shape=jax.ShapeDtypeStruct((PADDED_ROWS, D_MODEL), F32),
        compiler_params=_cparams("arbitrary", "arbitrary"),
        name="experts",
    )(se, srow, snsub, xs, w_gate, w_up, w_down,
      b_gate, b_up, b_down)


def _combine_body(dest_ref, y_hbm, x1_ref, w_ref, mod_ref, fg_ref, o_ref, buf, sem, *, tm):
    i = pl.program_id(0)

    def slot_cols(k):
        return pl.ds(k * D_MODEL, D_MODEL)

    def gather(tile, dst_slot):
        def group(g, carry):
            base = pl.multiple_of(g * ROW_DMA_GROUP, ROW_DMA_GROUP)
            for u in range(ROW_DMA_GROUP):
                for k in range(TOP_K):
                    d = dest_ref[(tile * tm + base + u) * TOP_K + k]
                    pltpu.make_async_copy(y_hbm.at[pl.ds(d, 1)],
                                          buf.at[dst_slot, pl.ds(base + u, 1), slot_cols(k)],
                                          sem.at[dst_slot]).start(priority=(u * TOP_K + k) % 2)
            return carry
        lax.fori_loop(0, tm // ROW_DMA_GROUP, group, 0)

    def step(slot):
        @pl.when(i == 0)
        def _():
            gather(i, slot)

        @pl.when(i + 1 < pl.num_programs(0))
        def _():
            gather(i + 1, 1 - slot)

        for k in range(TOP_K):
            pltpu.make_async_copy(y_hbm.at[pl.ds(0, tm)], buf.at[slot, :, slot_cols(k)], sem.at[slot]).wait()
        moe = buf[slot, :, 0:D_MODEL] * w_ref[:, 0:1]
        for k in range(1, TOP_K):
            moe = moe + buf[slot, :, k * D_MODEL:(k + 1) * D_MODEL] * w_ref[:, k:k + 1]
        x2 = x1_ref[...] + mod_ref[5:6, :] * moe
        o_ref[...] = _rms(x2) * fg_ref[...]

    @pl.when(i % 2 == 0)
    def _():
        step(0)

    @pl.when(i % 2 == 1)
    def _():
        step(1)


def _combine(dest_flat, y, x1, w_t, mod, final_g):
    tm = 256
    per_batch = SEQ // tm
    return pl.pallas_call(
        functools.partial(_combine_body, tm=tm),
        grid_spec=pltpu.PrefetchScalarGridSpec(
            num_scalar_prefetch=1,
            grid=(TOKENS // tm,),
            in_specs=[pl.BlockSpec(memory_space=pl.ANY),
                      pl.BlockSpec((tm, D_MODEL), lambda i, d: (i, 0)),
                      pl.BlockSpec((tm, TOP_K), lambda i, d: (i, 0)),
                      pl.BlockSpec((None, 6, D_MODEL), lambda i, d: (i // per_batch, 0, 0)),
                      pl.BlockSpec((1, D_MODEL), lambda i, d: (0, 0))],
            out_specs=pl.BlockSpec((tm, D_MODEL), lambda i, d: (i, 0)),
            scratch_shapes=[pltpu.VMEM((2, tm, TOP_K * D_MODEL), F32),
                            pltpu.SemaphoreType.DMA((2,))]),
        out_shape=jax.ShapeDtypeStruct((TOKENS, D_MODEL), F32),
        compiler_params=_cparams("arbitrary"),
        name="combine",
    )(dest_flat, y, x1, w_t, mod, final_g)


def _routing_tables(idx, rank, counts):
    padded = (counts + ROW_TILE - 1) // ROW_TILE * ROW_TILE
    padded_end = jnp.cumsum(padded)
    padded_start = padded_end - padded
    e_i = jnp.arange(N_EXPERTS, dtype=I32)[:, None, None]
    start_of = jnp.sum(jnp.where(idx[None] == e_i, padded_start[:, None, None], 0), axis=0)
    dest = (start_of + rank).T.reshape(N_ROWS).astype(I32)

    tiles = padded // ROW_TILE
    n_super = (tiles + SUPER_TILES - 1) // SUPER_TILES
    super_end = jnp.cumsum(n_super)
    super_start = super_end - n_super
    s_i = jnp.arange(MAX_SUPER, dtype=I32)
    total = super_end[-1]
    valid = s_i < total
    e_of = jnp.minimum(jnp.sum((s_i[:, None] >= super_end[None, :]).astype(I32), axis=1), N_EXPERTS - 1)
    last_e = jnp.minimum(jnp.sum((total - 1 >= super_end).astype(I32)), N_EXPERTS - 1)
    local = s_i - super_start[e_of]
    srow = jnp.where(valid, padded_start[e_of] + local * SUPER_ROWS, 0).astype(I32)
    snsub = jnp.where(valid, jnp.minimum(tiles[e_of] - local * SUPER_TILES, SUPER_TILES), 0).astype(I32)
    snsub = jnp.concatenate([snsub, (padded_end[-1:] // ROW_TILE).astype(I32)])
    se = jnp.where(valid, e_of, last_e).astype(I32)
    return dest, padded_end.astype(I32), se, srow, snsub


def kernel(x, c, ada_w, ada_b, norm1_g, w_in, gla_gate_w2, gla_gate_b, sgu_ln_g, sgu_ln_b, sgu_w, sgu_b,
           gla_norm_g, w_branch_a, w_branch_b, w_out, norm2_g, router_w, router_b, exp_w_gate, exp_b_gate,
           exp_w_up, exp_b_up, exp_w_down, exp_b_down, final_g):
    x2d = x.reshape(TOKENS, D_MODEL)
    mod = _ada(c, ada_w[0], ada_b[0])

    w_in0 = w_in[0]
    w_in_t = w_in0.T
    w_glr = jnp.zeros((LANES, D_MODEL), BF16).at[:GLA_GATE_RANK].set(
        w_in_t[GLR_SRC:GLR_SRC + GLA_GATE_RANK].astype(BF16))
    h, glr = _norm1(x2d, mod, norm1_g, w_glr)
    proj = _inproj(h, w_in_t)

    y_a = _sgu(proj, sgu_ln_g, sgu_ln_b, sgu_w[0], sgu_b[0].T)

    w2p = jnp.zeros((LANES, GLA_DK), BF16).at[:GLA_GATE_RANK].set(gla_gate_w2[0].astype(BF16))
    y_b = _gla(proj, glr, w2p, gla_gate_b, gla_norm_g)

    merged = _merge(y_a, y_b, w_branch_a[0], w_branch_b[0], proj)

    rw_t = router_w[0].T
    rw_hi = rw_t.astype(BF16)
    rw_lo = (rw_t - rw_hi.astype(F32)).astype(BF16)
    x1, h_rows, idx, top_w, rank, cnt = _outproj(
        merged, x2d, mod, w_out[0].astype(BF16), norm2_g, rw_hi, rw_lo, router_b[0].reshape(N_EXPERTS, 1))

    counts = cnt[:, 0]
    dest, padded_end, se, srow, snsub = _routing_tables(idx, rank, counts)
    xs = _dispatch(dest, padded_end, counts, h_rows)
    y = _experts(se, srow, snsub, xs, exp_w_gate[0], exp_b_gate[0], exp_w_up[0], exp_b_up[0],
                 exp_w_down[0], exp_b_down[0])
    out = _combine(dest, y, x1, top_w.T, mod, final_g.reshape(1, D_MODEL))
    return out.reshape(BATCH, SEQ, D_MODEL)
```
